```python
import math
import jax
import jax.numpy as jnp
from jax import lax
import numpy as np

D_MODEL = 1024
BATCH = 8
SEQ = 4096
DEPTH = 4
DEC_BATCH = 16
DEC_SEQ = 32
PAST_LEN = 1024

CHUNK = 64
N_PAIRS = DEPTH // 2
EPS = 1e-6
NEG_INF = -1e30
ROPE_THETA = 500000.0
HEAD_DIM = 64
ROT_DIM = HEAD_DIM // 4
Q_BLOCK = 128

POOL_WINDOWS = (2, 4, 8, 16)
POOL_GROUP = D_MODEL // 8
D_A = len(POOL_WINDOWS) * POOL_GROUP
POOL_STATE = max(POOL_WINDOWS) - 1

H_B = D_MODEL // 128
HKV_B = 2
GQA_B = H_B // HKV_B
WINDOW = 128
WIN_CHUNKS = WINDOW // CHUNK
D_B = H_B * HEAD_DIM
D_EVEN_IN = D_A + D_B + 2 * HKV_B * HEAD_DIM

H_C = 4
DK_C = 64
DV_C = 128
GATE_RANK = 16
GATE_NORM = 16.0
D_CQK = H_C * DK_C
D_C = H_C * DV_C

H_D = 4
HD_D = 64
D_D = H_D * 2 * HD_D
D_ODD_IN = 2 * D_CQK + 2 * D_C + GATE_RANK + 3 * D_D

D_FF = 2816
CONV_W = 3

kernel_name = 'hybrid_streaming_encoder_step'


def rms_norm(x, w):
    xf = x.astype(jnp.float32)
    y = xf * lax.rsqrt(jnp.mean(xf * xf, axis=-1, keepdims=True) + EPS)
    return (y * w.astype(jnp.float32)).astype(x.dtype)


def rope_partial(x, pos):
    half = ROT_DIM // 2
    inv = jnp.exp(-math.log(ROPE_THETA) * jnp.arange(half, dtype=jnp.float32) * (2.0 / ROT_DIM))
    ang = pos.astype(jnp.float32)[:, None] * inv[None, :]
    shape = (1, pos.shape[0]) + (1,) * (x.ndim - 3) + (half,)
    cos = jnp.cos(ang).reshape(shape)
    sin = jnp.sin(ang).reshape(shape)
    xr = x[..., :ROT_DIM].astype(jnp.float32)
    x1, x2 = xr[..., :half], xr[..., half:]
    rot = jnp.concatenate([x1 * cos - x2 * sin, x2 * cos + x1 * sin], axis=-1)
    return jnp.concatenate([rot.astype(x.dtype), x[..., ROT_DIM:]], axis=-1)


def sink_softmax(s, sink):
    sink = sink.astype(jnp.float32)
    m = jnp.maximum(jnp.max(s, axis=-1, keepdims=True), sink)
    e = jnp.exp(s - m)
    return e / (jnp.sum(e, axis=-1, keepdims=True) + jnp.exp(sink - m))


def pool_mixer(u, prev, pos, w_pool, scale):
    b, l, _ = u.shape
    ext = jnp.concatenate([prev.astype(u.dtype), u], axis=1).astype(jnp.float32)
    csum = jnp.concatenate([jnp.zeros((b, 1, D_A), jnp.float32), jnp.cumsum(ext, axis=1)], axis=1)
    base = POOL_STATE + 1
    outs = []
    for g, w in enumerate(POOL_WINDOWS):
        lo, hi = g * POOL_GROUP, (g + 1) * POOL_GROUP
        total = csum[:, base:base + l, lo:hi] - csum[:, base - w:base - w + l, lo:hi]
        count = jnp.minimum(w, pos + 1).astype(jnp.float32)[None, :, None]
        diff = total / count - ext[:, POOL_STATE:, lo:hi]
        outs.append(jnp.einsum('blc,cd->bld', diff, w_pool[g].astype(jnp.float32)))
    y = jnp.concatenate(outs, axis=-1) * scale.astype(jnp.float32)
    return y.astype(u.dtype), ext[:, -POOL_STATE:].astype(u.dtype)


def swa_prompt(q, k, v, sinks):
    b, s = q.shape[:2]
    nc = s // CHUNK
    scale = HEAD_DIM ** -0.5
    qc = q.reshape(b, nc, CHUNK, HKV_B, GQA_B, HEAD_DIM)

    def banded(t):
        tc = t.reshape(b, nc, CHUNK, HKV_B, HEAD_DIM)
        tp = jnp.concatenate([jnp.zeros((b, WIN_CHUNKS) + tc.shape[2:], t.dtype), tc], axis=1)
        return jnp.concatenate([tp[:, j:j + nc] for j in range(WIN_CHUNKS + 1)], axis=2)

    kb, vb = banded(k), banded(v)
    src_chunk = jnp.arange(nc)[:, None] - WIN_CHUNKS + jnp.repeat(jnp.arange(WIN_CHUNKS + 1), CHUNK)[None, :]
    valid = src_chunk >= 0
    sc = jnp.einsum('bnqhgd,bnkhd->bnhgqk', qc, kb).astype(jnp.float32) * scale
    sc = jnp.where(valid[None, :, None, None, None, :], sc, NEG_INF)
    p = sink_softmax(sc, sinks.reshape(HKV_B, GQA_B)[None, None, :, :, None, None])
    o = jnp.einsum('bnhgqk,bnkhd->bnqhgd', p.astype(v.dtype), vb)
    return o.reshape(b, s, D_B)


def swa_sample(q, k, v, k_cache, v_cache, sinks):
    b, l = q.shape[:2]
    scale = HEAD_DIM ** -0.5
    k_all = jnp.concatenate([k_cache.astype(k.dtype), k], axis=1)
    v_all = jnp.concatenate([v_cache.astype(v.dtype), v], axis=1)
    qg = q.reshape(b, l, HKV_B, GQA_B, HEAD_DIM)
    sc = jnp.einsum('bqhgd,bkhd->bhgqk', qg, k_all).astype(jnp.float32) * scale
    p = sink_softmax(sc, sinks.reshape(HKV_B, GQA_B)[None, :, :, None, None])
    o = jnp.einsum('bhgqk,bkhd->bqhgd', p.astype(v.dtype), v_all).reshape(b, l, D_B)
    rows = k_cache.shape[1]
    return o, k_all[:, -rows:], v_all[:, -rows:]


def even_mixer(h, pos, prompt, pool_prev, swa_k_c, swa_v_c, w_in, w_pool, pool_scale, sinks, w_out):
    b, l, _ = h.shape
    z = h @ w_in
    u, q, k, v = jnp.split(z, [D_A, D_A + D_B, D_A + D_B + HKV_B * HEAD_DIM], axis=-1)
    ya, pool_new = pool_mixer(u, pool_prev, pos, w_pool, pool_scale)
    q = rope_partial(q.reshape(b, l, H_B, HEAD_DIM), pos)
    k = rope_partial(k.reshape(b, l, HKV_B, HEAD_DIM), pos)
    v = v.reshape(b, l, HKV_B, HEAD_DIM)
    if prompt:
        yb = swa_prompt(q, k, v, sinks)
        k_new, v_new = k[:, -WINDOW:], v[:, -WINDOW:]
    else:
        yb, k_new, v_new = swa_sample(q, k, v, swa_k_c, swa_v_c, sinks)
    y = jnp.concatenate([ya, yb], axis=-1) @ w_out
    return y, pool_new, k_new, v_new


def gla_chunked(q, k, v, log_a, s0):
    b, l, h, dk = q.shape
    dv = v.shape[-1]
    blk = min(CHUNK, l)
    n = l // blk

    def split(t):
        return t.reshape(b, n, blk, h, t.shape[-1]).transpose(1, 0, 3, 2, 4)

    causal = jnp.tril(jnp.ones((blk, blk), dtype=bool))

    def step(state, inp):
        qb, kb, vb, gb = inp
        cum = jnp.cumsum(gb, axis=2)
        q_dec = qb * jnp.exp(cum)
        k_dec = kb * jnp.exp(-cum)
        att = jnp.where(causal, jnp.einsum('bhqd,bhkd->bhqk', q_dec, k_dec), 0.0)
        out = jnp.einsum('bhqk,bhkv->bhqv', att, vb) + jnp.einsum('bhqd,bhdv->bhqv', q_dec, state)
        last = cum[:, :, -1:, :]
        new_state = (jnp.exp(last[:, :, 0, :])[..., None] * state
                     + jnp.einsum('bhkd,bhkv->bhdv', kb * jnp.exp(last - cum), vb))
        return new_state, out

    s_fin, o = lax.scan(step, s0, (split(q), split(k), split(v), split(log_a)))
    return o.transpose(1, 0, 3, 2, 4).reshape(b, l, h, dv), s_fin


def diff_prompt(q, k, v, lam):
    b, s = q.shape[:2]
    nb = s // Q_BLOCK
    scale = HD_D ** -0.5
    qb = q.reshape(b, nb, Q_BLOCK, H_D, 2, HD_D).transpose(1, 0, 2, 3, 4, 5)
    key_chunk = jnp.arange(s) // CHUNK

    def one_block(args):
        qi, i = args
        sc = jnp.einsum('bqhcd,bkhcd->bhcqk', qi, k).astype(jnp.float32) * scale
        q_chunk = (i * Q_BLOCK + jnp.arange(Q_BLOCK)) // CHUNK
        mask = key_chunk[None, :] <= q_chunk[:, None]
        p = jax.nn.softmax(jnp.where(mask, sc, NEG_INF), axis=-1)
        a = p[:, :, 0] - lam * p[:, :, 1]
        return jnp.einsum('bhqk,bkhv->bqhv', a.astype(v.dtype), v)

    o = lax.map(one_block, (qb, jnp.arange(nb)))
    return o.transpose(1, 0, 2, 3, 4).reshape(b, s, H_D, 2 * HD_D)


def diff_sample(q, k, v, k_cache, v_cache, lam):
    scale = HD_D ** -0.5
    k_all = jnp.concatenate([k_cache.astype(k.dtype), k], axis=1)
    v_all = jnp.concatenate([v_cache.astype(v.dtype), v], axis=1)
    sc = jnp.einsum('bqhcd,bkhcd->bhcqk', q, k_all).astype(jnp.float32) * scale
    p = jax.nn.softmax(sc, axis=-1)
    a = p[:, :, 0] - lam * p[:, :, 1]
    return jnp.einsum('bhqk,bkhv->bqhv', a.astype(v.dtype), v_all)


def odd_mixer(h, pos, prompt, s_prev, diff_k_c, diff_v_c, w_in, w_gate2, b_gate, gla_norm_w,
              lam_params, diff_norm_w, w_out, lambda_init):
    b, l, _ = h.shape
    f32 = jnp.float32
    z = h @ w_in
    i1 = D_CQK
    i2 = i1 + D_CQK
    i3 = i2 + D_C
    i4 = i3 + D_C
    i5 = i4 + GATE_RANK
    i6 = i5 + D_D
    i7 = i6 + D_D
    cq, ck, cv, cg, cr, dq, dk, dvv = jnp.split(z, [i1, i2, i3, i4, i5, i6, i7], axis=-1)
    log_a = jax.nn.log_sigmoid((cr @ w_gate2 + b_gate).astype(f32)) / GATE_NORM
    qg = cq.reshape(b, l, H_C, DK_C).astype(f32) * (DK_C ** -0.5)
    kg = ck.reshape(b, l, H_C, DK_C).astype(f32)
    vg = cv.reshape(b, l, H_C, DV_C).astype(f32)
    og, s_new = gla_chunked(qg, kg, vg, log_a.reshape(b, l, H_C, DK_C), s_prev.astype(f32))
    og = rms_norm(og, gla_norm_w)
    yc = (og.reshape(b, l, D_C) * jax.nn.silu(cg.astype(f32))).astype(h.dtype)
    q = rope_partial(dq.reshape(b, l, H_D, 2, HD_D), pos)
    k = rope_partial(dk.reshape(b, l, H_D, 2, HD_D), pos)
    v = dvv.reshape(b, l, H_D, 2 * HD_D)
    lp = lam_params.astype(f32)
    lam = jnp.exp(jnp.sum(lp[0] * lp[1])) - jnp.exp(jnp.sum(lp[2] * lp[3])) + lambda_init
    if prompt:
        od = diff_prompt(q, k, v, lam)
    else:
        od = diff_sample(q, k, v, diff_k_c, diff_v_c, lam)
    od = rms_norm(od, diff_norm_w) * (1.0 - lambda_init)
    yd = od.reshape(b, l, D_D).astype(h.dtype)
    y = jnp.concatenate([yc, yd], axis=-1) @ w_out
    return y, s_new.astype(h.dtype), k, v


def conv_ffn(h, conv_prev, w_up, conv_w, conv_b, w_down):
    b, l, _ = h.shape
    gate, val = jnp.split(h @ w_up, 2, axis=-1)
    ext = jnp.concatenate([conv_prev.astype(h.dtype), gate], axis=1)
    conv = conv_b
    for j in range(CONV_W):
        conv = conv + ext[:, j:j + l] * conv_w[j]
    y = (jax.nn.gelu(conv) * val) @ w_down
    return y, ext[:, -(CONV_W - 1):]


def trunk(x, pos, prompt, cache_pool, cache_swa_k, cache_swa_v, state_gla, cache_diff_k, cache_diff_v,
          cache_ffn_conv, norm_mix_w, norm_ffn_w, final_norm_w, even_w_in, pool_w, pool_scale, swa_sinks,
          even_w_out, odd_w_in, gla_w_gate2, gla_b_gate, gla_norm_w, diff_lambda, diff_norm_w, odd_w_out,
          ffn_w_up, ffn_conv_w, ffn_conv_b, ffn_w_down):
    b = x.shape[0]
    pools, swa_ks, swa_vs, glas, diff_ks, diff_vs, ffns = [], [], [], [], [], [], []
    for layer in range(DEPTH):
        p = layer // 2
        h = rms_norm(x, norm_mix_w[layer])
        if layer % 2 == 0:
            pool_prev = jnp.zeros((b, POOL_STATE, D_A), x.dtype) if prompt else cache_pool[p]
            y, pool_new, k_new, v_new = even_mixer(
                h, pos, prompt, pool_prev,
                None if prompt else cache_swa_k[p], None if prompt else cache_swa_v[p],
                even_w_in[p], pool_w[p], pool_scale[p], swa_sinks[p], even_w_out[p])
            pools.append(pool_new)
            swa_ks.append(k_new)
            swa_vs.append(v_new)
        else:
            s_prev = jnp.zeros((b, H_C, DK_C, DV_C), jnp.float32) if prompt else state_gla[p]
            lambda_init = 0.8 - 0.6 * math.exp(-0.3 * layer)
            y, s_new, k_new, v_new = odd_mixer(
                h, pos, prompt, s_prev,
                None if prompt else cache_diff_k[p], None if prompt else cache_diff_v[p],
                odd_w_in[p], gla_w_gate2[p], gla_b_gate[p], gla_norm_w[p], diff_lambda[p],
                diff_norm_w[p], odd_w_out[p], lambda_init)
            glas.append(s_new)
            diff_ks.append(k_new)
            diff_vs.append(v_new)
        x = x + y
        h = rms_norm(x, norm_ffn_w[layer])
        conv_prev = jnp.zeros((b, CONV_W - 1, D_FF), x.dtype) if prompt else cache_ffn_conv[layer]
        y, conv_new = conv_ffn(h, conv_prev, ffn_w_up[layer], ffn_conv_w[layer], ffn_conv_b[layer], ffn_w_down[layer])
        ffns.append(conv_new)
        x = x + y
    return (rms_norm(x, final_norm_w), jnp.stack(pools), jnp.stack(swa_ks), jnp.stack(swa_vs),
            jnp.stack(glas), jnp.stack(diff_ks), jnp.stack(diff_vs), jnp.stack(ffns))


def setup_inputs(seed: int = 0) -> dict:
    key = jax.random.key(seed)
    ks = iter(list(jax.random.split(key, 32)))

    def nrm(shape, scale):
        return scale * jax.random.normal(next(ks), shape, jnp.float32)

    def gain(shape):
        return 1.0 + 0.05 * jax.random.normal(next(ks), shape, jnp.float32)

    swa_rows = min(WINDOW, PAST_LEN)
    return {
        'x_prompt': nrm((BATCH, SEQ, D_MODEL), 1.0),
        'x_sample': nrm((DEC_BATCH, DEC_SEQ, D_MODEL), 1.0),
        'cache_pool': nrm((N_PAIRS, DEC_BATCH, POOL_STATE, D_A), 1.0),
        'cache_swa_k': nrm((N_PAIRS, DEC_BATCH, swa_rows, HKV_B, HEAD_DIM), 1.0),
        'cache_swa_v': nrm((N_PAIRS, DEC_BATCH, swa_rows, HKV_B, HEAD_DIM), 1.0),
        'state_gla': nrm((N_PAIRS, DEC_BATCH, H_C, DK_C, DV_C), 0.5),
        'cache_diff_k': nrm((N_PAIRS, DEC_BATCH, PAST_LEN, H_D, 2, HD_D), 1.0),
        'cache_diff_v': nrm((N_PAIRS, DEC_BATCH, PAST_LEN, H_D, 2 * HD_D), 1.0),
        'cache_ffn_conv': nrm((DEPTH, DEC_BATCH, CONV_W - 1, D_FF), 1.0),
        'norm_mix_w': gain((DEPTH, D_MODEL)),
        'norm_ffn_w': gain((DEPTH, D_MODEL)),
        'final_norm_w': gain((D_MODEL,)),
        'even_w_in': nrm((N_PAIRS, D_MODEL, D_EVEN_IN), D_MODEL ** -0.5),
        'pool_w': nrm((N_PAIRS, len(POOL_WINDOWS), POOL_GROUP, POOL_GROUP), POOL_GROUP ** -0.5),
        'pool_scale': gain((N_PAIRS, D_A)),
        'swa_sinks': nrm((N_PAIRS, H_B), 1.0),
        'even_w_out': nrm((N_PAIRS, D_A + D_B, D_MODEL), (D_A + D_B) ** -0.5),
        'odd_w_in': nrm((N_PAIRS, D_MODEL, D_ODD_IN), D_MODEL ** -0.5),
        'gla_w_gate2': nrm((N_PAIRS, GATE_RANK, D_CQK), GATE_RANK ** -0.5),
        'gla_b_gate': nrm((N_PAIRS, D_CQK), 0.1),
        'gla_norm_w': gain((N_PAIRS, DV_C)),
        'diff_lambda': nrm((N_PAIRS, 4, HD_D), 0.1),
        'diff_norm_w': gain((N_PAIRS, 2 * HD_D)),
        'odd_w_out': nrm((N_PAIRS, D_C + D_D, D_MODEL), (D_C + D_D) ** -0.5),
        'ffn_w_up': nrm((DEPTH, D_MODEL, 2 * D_FF), D_MODEL ** -0.5),
        'ffn_conv_w': nrm((DEPTH, CONV_W, D_FF), 0.5),
        'ffn_conv_b': nrm((DEPTH, D_FF), 0.02),
        'ffn_w_down': nrm((DEPTH, D_FF, D_MODEL), D_FF ** -0.5),
    }


def reference(x_prompt, x_sample, cache_pool, cache_swa_k, cache_swa_v, state_gla, cache_diff_k, cache_diff_v,
              cache_ffn_conv, norm_mix_w, norm_ffn_w, final_norm_w, even_w_in, pool_w, pool_scale, swa_sinks,
              even_w_out, odd_w_in, gla_w_gate2, gla_b_gate, gla_norm_w, diff_lambda, diff_norm_w, odd_w_out,
              ffn_w_up, ffn_conv_w, ffn_conv_b, ffn_w_down):
    weights = (norm_mix_w, norm_ffn_w, final_norm_w, even_w_in, pool_w, pool_scale, swa_sinks, even_w_out,
               odd_w_in, gla_w_gate2, gla_b_gate, gla_norm_w, diff_lambda, diff_norm_w, odd_w_out,
               ffn_w_up, ffn_conv_w, ffn_conv_b, ffn_w_down)
    pos_p = jnp.arange(x_prompt.shape[1])
    pos_s = PAST_LEN + jnp.arange(x_sample.shape[1])
    (y_prompt, new_pool_p, new_swa_k_p, new_swa_v_p, new_gla_p, new_diff_k_p, new_diff_v_p,
     new_ffn_p) = trunk(x_prompt, pos_p, True, None, None, None, None, None, None, None, *weights)
    (y_sample, new_pool_s, new_swa_k_s, new_swa_v_s, new_gla_s, new_diff_k_s, new_diff_v_s,
     new_ffn_s) = trunk(x_sample, pos_s, False, cache_pool, cache_swa_k, cache_swa_v, state_gla,
                        cache_diff_k, cache_diff_v, cache_ffn_conv, *weights)
    return (y_prompt, y_sample,
            new_pool_p, new_swa_k_p, new_swa_v_p, new_gla_p, new_diff_k_p, new_diff_v_p, new_ffn_p,
            new_pool_s, new_swa_k_s, new_swa_v_s, new_gla_s, new_diff_k_s, new_diff_v_s, new_ffn_s)
```

```python
import functools
import math

import jax
import jax.numpy as jnp
from jax import lax
from jax.experimental import pallas as pl
from jax.experimental.pallas import tpu as pltpu

F32 = jnp.float32
BF16 = jnp.bfloat16

D_MODEL = 1024
CHUNK = 64
EPS = 1e-6
NEG_INF = -1e30
ROPE_THETA = 500000.0
HEAD_DIM = 64
ROT_DIM = HEAD_DIM // 4
ROT_HALF = ROT_DIM // 2
LANES = 128

POOL_WINDOWS = (2, 4, 8, 16)
POOL_GROUP = D_MODEL // 8
D_A = len(POOL_WINDOWS) * POOL_GROUP
POOL_STATE = max(POOL_WINDOWS) - 1
POOL_HALO = POOL_STATE + 1

H_B = D_MODEL // 128
HKV_B = 2
GQA_B = H_B // HKV_B
WINDOW = 128
D_B = H_B * HEAD_DIM
D_KV = HKV_B * HEAD_DIM

H_C = 4
DK_C = 64
DV_C = 128
GATE_RANK = 16
GATE_NORM = 16.0
D_CQK = H_C * DK_C
D_C = H_C * DV_C

H_D = 4
HD_D = 64
D_D = H_D * 2 * HD_D

D_FF = 2816
CONV_W = 3
CONV_TAIL = 8

ROW_TILE = 512
ATTN_TILE = 512
FFN_COLS = 256
VMEM_LIMIT = 60 * 1024 * 1024


def _dot(a, b):
    return jnp.dot(a, b, preferred_element_type=F32)


def _dot_nt(a, b):
    return lax.dot_general(a, b, (((1,), (1,)), ((), ())), preferred_element_type=F32)


def _dot_tn(a, b):
    return lax.dot_general(a, b, (((0,), (0,)), ((), ())), preferred_element_type=F32)


def _rms(x, w):
    return x * lax.rsqrt(jnp.mean(x * x, axis=-1, keepdims=True) + EPS) * w


def _rope(z, cos_t, sin_a, sin_b):
    outs = []
    for s in range(z.shape[1] // LANES):
        x = z[:, s * LANES:(s + 1) * LANES]
        outs.append(x * cos_t + pltpu.roll(x, LANES - ROT_HALF, 1) * sin_a + pltpu.roll(x, ROT_HALF, 1) * sin_b)
    return outs[0] if len(outs) == 1 else jnp.concatenate(outs, axis=-1)


def _rope_tables(pos):
    t = pos.shape[0]
    inv = jnp.exp(-math.log(ROPE_THETA) * jnp.arange(ROT_HALF, dtype=F32) * (2.0 / ROT_DIM))
    ang = pos.astype(F32)[:, None] * inv[None, :]
    cos, sin = jnp.cos(ang), jnp.sin(ang)
    rest = HEAD_DIM - ROT_DIM
    cos_t = jnp.concatenate([cos, cos, jnp.ones((t, rest), F32)], axis=-1)
    sin_a = jnp.concatenate([-sin, jnp.zeros((t, HEAD_DIM - ROT_HALF), F32)], axis=-1)
    sin_b = jnp.concatenate([jnp.zeros((t, ROT_HALF), F32), sin, jnp.zeros((t, rest), F32)], axis=-1)
    rep = LANES // HEAD_DIM
    return tuple(jnp.tile(a, (1, rep)) for a in (cos_t, sin_a, sin_b))


def _params(n_axes=1):
    return pltpu.CompilerParams(dimension_semantics=("arbitrary",) * n_axes, vmem_limit_bytes=VMEM_LIMIT)


def _full(shape):
    nd = len(shape)
    return pl.BlockSpec(shape, lambda *_: (0,) * nd)


def _rows(tm, cols):
    return pl.BlockSpec((tm, cols), lambda i: (i, 0))


def _even_in_kernel(x_ref, nw_ref, w_ref, cos_ref, sa_ref, sb_ref, u_ref, q_ref, k_ref, v_ref):
    h = _rms(x_ref[...], nw_ref[...]).astype(BF16)
    cos_t, sin_a, sin_b = cos_ref[...], sa_ref[...], sb_ref[...]
    u_ref[...] = _dot(h, w_ref[:, 0:D_A])
    q = _dot(h, w_ref[:, D_A:D_A + D_B])
    q_ref[...] = (_rope(q, cos_t, sin_a, sin_b) * (HEAD_DIM ** -0.5)).astype(BF16)
    k = _dot(h, w_ref[:, D_A + D_B:D_A + D_B + D_KV])
    k_ref[...] = _rope(k, cos_t, sin_a, sin_b)
    v_ref[...] = _dot(h, w_ref[:, D_A + D_B + D_KV:D_A + D_B + 2 * D_KV])


def _even_in(x, nw, w, tabs, tm):
    r = x.shape[0]
    t_tiles = tabs[0].shape[0] // tm
    tab_spec = pl.BlockSpec((tm, LANES), lambda i: (i % t_tiles, 0))
    n_in = D_A + D_B + 2 * D_KV
    return pl.pallas_call(
        _even_in_kernel,
        grid=(r // tm,),
        in_specs=[_rows(tm, D_MODEL), _full((1, D_MODEL)), _full((D_MODEL, n_in)), tab_spec, tab_spec, tab_spec],
        out_specs=[_rows(tm, D_A), _rows(tm, D_B), _rows(tm, D_KV), _rows(tm, D_KV)],
        out_shape=[jax.ShapeDtypeStruct((r, D_A), F32), jax.ShapeDtypeStruct((r, D_B), BF16),
                   jax.ShapeDtypeStruct((r, D_KV), F32), jax.ShapeDtypeStruct((r, D_KV), F32)],
        compiler_params=_params(),
        name="even_in",
    )(x, nw, w, *tabs)


def _pool_mix(ext, u, pos, wp_ref, ps, ycat_ref, row0, rows):
    for g, w in enumerate(POOL_WINDOWS):
        lo, hi = g * POOL_GROUP, (g + 1) * POOL_GROUP
        s = ext[:, lo:hi]
        sh = 1
        while sh < w:
            s = s + pltpu.roll(s, sh, 0)
            sh *= 2
        count = jnp.minimum(w, pos + 1).astype(F32)
        diff = s[POOL_HALO:] / count - u[:, lo:hi]
        ya = _dot(diff.astype(BF16), wp_ref[g]) * ps[:, lo:hi]
        ycat_ref[row0:row0 + rows, lo:hi] = ya.astype(BF16)


def _sink_column(sink_ref, hk, rows_per_head):
    n = GQA_B * rows_per_head
    grp = lax.broadcasted_iota(jnp.int32, (n, 1), 0) // rows_per_head
    col = jnp.full((n, 1), sink_ref[hk * GQA_B], F32)
    for g in range(1, GQA_B):
        col = jnp.where(grp == g, sink_ref[hk * GQA_B + g], col)
    return col


def _sink_attend(s, sink_col, vb):
    m = jnp.maximum(jnp.max(s, axis=-1, keepdims=True), sink_col)
    e = jnp.exp(s - m)
    den = jnp.sum(e, axis=-1, keepdims=True) + jnp.exp(sink_col - m)
    return _dot(e.astype(BF16), vb) / den


def _store_heads(ycat_ref, o, row0, rows, hk):
    for pair in range(GQA_B // 2):
        both = jnp.concatenate([o[(2 * pair) * rows:(2 * pair + 1) * rows],
                                o[(2 * pair + 1) * rows:(2 * pair + 2) * rows]], axis=-1)
        c0 = D_A + (hk * GQA_B + 2 * pair) * HEAD_DIM
        ycat_ref[row0:row0 + rows, c0:c0 + 2 * HEAD_DIM] = both.astype(BF16)


def _even_mix_prompt_kernel(sink_ref, u_ref, q_ref, k_ref, v_ref, x_ref, wp_ref, ps_ref, wo_ref, o_ref,
                            ucar, kcar, vcar, ycat, *, tm, tiles_per_seq):
    i = pl.program_id(0)
    ti = i % tiles_per_seq
    first_key_chunk = jnp.where(ti == 0, WINDOW // CHUNK, 0)

    @pl.when(ti == 0)
    def _():
        ucar[...] = jnp.zeros_like(ucar)
        kcar[...] = jnp.zeros_like(kcar)
        vcar[...] = jnp.zeros_like(vcar)

    u = u_ref[...]
    ext = jnp.concatenate([ucar[...], u], axis=0)
    pos = ti * tm + lax.broadcasted_iota(jnp.int32, (tm, 1), 0)
    _pool_mix(ext, u, pos, wp_ref, ps_ref[...], ycat, 0, tm)
    ucar[...] = u[tm - POOL_HALO:]

    kext = jnp.concatenate([kcar[...], k_ref[...].astype(BF16)], axis=0)
    vext = jnp.concatenate([vcar[...], v_ref[...].astype(BF16)], axis=0)
    qb = 2 * CHUNK
    kb = qb + WINDOW
    qc = (lax.broadcasted_iota(jnp.int32, (GQA_B * qb, kb), 0) % qb) // CHUNK
    kc = lax.broadcasted_iota(jnp.int32, (GQA_B * qb, kb), 1) // CHUNK
    band = (kc >= qc) & (kc <= qc + WINDOW // CHUNK)
    for a in range(tm // qb):
        mask = band
        if a == 0:
            mask = band & (kc >= first_key_chunk)
        for hk in range(HKV_B):
            ks = kext[a * qb:a * qb + kb, hk * HEAD_DIM:(hk + 1) * HEAD_DIM]
            vs = vext[a * qb:a * qb + kb, hk * HEAD_DIM:(hk + 1) * HEAD_DIM]
            qs = jnp.concatenate(
                [q_ref[a * qb:(a + 1) * qb, (hk * GQA_B + g) * HEAD_DIM:(hk * GQA_B + g + 1) * HEAD_DIM]
                 for g in range(GQA_B)], axis=0)
            s = jnp.where(mask, _dot_nt(qs, ks), NEG_INF)
            o = _sink_attend(s, _sink_column(sink_ref, hk, qb), vs)
            _store_heads(ycat, o, a * qb, qb, hk)
    kcar[...] = kext[tm:]
    vcar[...] = vext[tm:]
    o_ref[...] = x_ref[...] + _dot(ycat[...], wo_ref[...])


def _even_mix_prompt(sinks, u, q, k, v, x, wp, ps, wo, seq, tm):
    r = x.shape[0]
    kern = functools.partial(_even_mix_prompt_kernel, tm=tm, tiles_per_seq=seq // tm)
    return pl.pallas_call(
        kern,
        grid=(r // tm,),
        in_specs=[pl.BlockSpec(memory_space=pltpu.SMEM),
                  _rows(tm, D_A), _rows(tm, D_B), _rows(tm, D_KV), _rows(tm, D_KV), _rows(tm, D_MODEL),
                  _full(wp.shape), _full((1, D_A)), _full((D_A + D_B, D_MODEL))],
        out_specs=_rows(tm, D_MODEL),
        out_shape=jax.ShapeDtypeStruct((r, D_MODEL), F32),
        scratch_shapes=[pltpu.VMEM((POOL_HALO, D_A), F32), pltpu.VMEM((WINDOW, D_KV), BF16),
                        pltpu.VMEM((WINDOW, D_KV), BF16), pltpu.VMEM((tm, D_A + D_B), BF16)],
        compiler_params=_params(),
        name="even_mix_prompt",
    )(sinks, u, q, k, v, x, wp, ps, wo)


def _even_mix_sample_kernel(sink_ref, u_ref, q_ref, k_ref, v_ref, x_ref, pp_ref, kc_ref, vc_ref, wp_ref, ps_ref,
                            wo_ref, o_ref, ycat, *, l, pos0):
    u = u_ref[...]
    ext = jnp.concatenate([pp_ref[0], u], axis=0)
    pos = pos0 + lax.broadcasted_iota(jnp.int32, (l, 1), 0)
    _pool_mix(ext, u, pos, wp_ref, ps_ref[...], ycat, 0, l)

    pad = jnp.zeros((LANES - l, D_KV), BF16)
    kall = jnp.concatenate([kc_ref[0].astype(BF16), k_ref[...].astype(BF16), pad], axis=0)
    vall = jnp.concatenate([vc_ref[0].astype(BF16), v_ref[...].astype(BF16), pad], axis=0)
    nk = kall.shape[0]
    valid = lax.broadcasted_iota(jnp.int32, (GQA_B * l, nk), 1) < nk - (LANES - l)
    for hk in range(HKV_B):
        ks = kall[:, hk * HEAD_DIM:(hk + 1) * HEAD_DIM]
        vs = vall[:, hk * HEAD_DIM:(hk + 1) * HEAD_DIM]
        qs = jnp.concatenate(
            [q_ref[:, (hk * GQA_B + g) * HEAD_DIM:(hk * GQA_B + g + 1) * HEAD_DIM] for g in range(GQA_B)], axis=0)
        s = jnp.where(valid, _dot_nt(qs, ks), NEG_INF)
        o = _sink_attend(s, _sink_column(sink_ref, hk, l), vs)
        _store_heads(ycat, o, 0, l, hk)
    o_ref[...] = x_ref[...] + _dot(ycat[...], wo_ref[...])


def _even_mix_sample(sinks, u, q, k, v, x, pool_prev, kc, vc, wp, ps, wo, l, pos0):
    r = x.shape[0]
    rows_c = kc.shape[1]
    kern = functools.partial(_even_mix_sample_kernel, l=l, pos0=pos0)
    cache_spec = pl.BlockSpec((1, rows_c, D_KV), lambda i: (i, 0, 0))
    return pl.pallas_call(
        kern,
        grid=(r // l,),
        in_specs=[pl.BlockSpec(memory_space=pltpu.SMEM),
                  _rows(l, D_A), _rows(l, D_B), _rows(l, D_KV), _rows(l, D_KV), _rows(l, D_MODEL),
                  pl.BlockSpec((1, POOL_HALO, D_A), lambda i: (i, 0, 0)), cache_spec, cache_spec,
                  _full(wp.shape), _full((1, D_A)), _full((D_A + D_B, D_MODEL))],
        out_specs=_rows(l, D_MODEL),
        out_shape=jax.ShapeDtypeStruct((r, D_MODEL), F32),
        scratch_shapes=[pltpu.VMEM((l, D_A + D_B), BF16)],
        compiler_params=_params(),
        name="even_mix_sample",
    )(sinks, u, q, k, v, x, pool_prev, kc, vc, wp, ps, wo)


ODD_CR0 = 2 * D_CQK + 2 * D_C
ODD_DQ0 = ODD_CR0 + LANES
ODD_IN_PADDED = ODD_DQ0 + 3 * D_D


def _odd_in_kernel(x_ref, nw_ref, w_ref, wg_ref, bg_ref, cos_ref, sa_ref, sb_ref,
                   qd_ref, kd_ref, el_ref, cv_ref, cg_ref, dq_ref, dk_ref, dv_ref, cum_scr, *, tm, blk):
    h = _rms(x_ref[...], nw_ref[...]).astype(BF16)
    cos_t, sin_a, sin_b = cos_ref[...], sa_ref[...], sb_ref[...]
    cq = _dot(h, w_ref[:, 0:D_CQK])
    ck = _dot(h, w_ref[:, D_CQK:2 * D_CQK])
    cv_ref[...] = _dot(h, w_ref[:, 2 * D_CQK:2 * D_CQK + D_C]).astype(BF16)
    cg_ref[...] = _dot(h, w_ref[:, 2 * D_CQK + D_C:ODD_CR0])
    cr = _dot(h, w_ref[:, ODD_CR0:ODD_DQ0]).astype(BF16)
    gate = _dot(cr, wg_ref[...]) + bg_ref[...]
    cum = (jnp.minimum(gate, 0.0) - jnp.log1p(jnp.exp(-jnp.abs(gate)))) * (1.0 / GATE_NORM)
    rowb = lax.broadcasted_iota(jnp.int32, (tm, 1), 0) % blk
    sh = 1
    while sh < blk:
        cum = cum + jnp.where(rowb >= sh, pltpu.roll(cum, sh, 0), 0.0)
        sh *= 2
    for s in range(D_CQK // LANES):
        cum_scr[s] = cum[:, s * LANES:(s + 1) * LANES]
        el_ref[:, s * LANES:(s + 1) * LANES] = jnp.exp(cum_scr[s, pl.ds(blk - 1, tm // blk, stride=blk), :])
    qd_ref[...] = (cq * (DK_C ** -0.5) * jnp.exp(cum)).astype(BF16)
    kd_ref[...] = (ck * jnp.exp(-cum)).astype(BF16)
    dq = _dot(h, w_ref[:, ODD_DQ0:ODD_DQ0 + D_D])
    dq_ref[...] = (_rope(dq, cos_t, sin_a, sin_b) * (HD_D ** -0.5)).astype(BF16)
    dk = _dot(h, w_ref[:, ODD_DQ0 + D_D:ODD_DQ0 + 2 * D_D])
    dk_ref[...] = _rope(dk, cos_t, sin_a, sin_b)
    dv_ref[...] = _dot(h, w_ref[:, ODD_DQ0 + 2 * D_D:ODD_DQ0 + 3 * D_D])


def _odd_in(x, nw, w, wg, bg, tabs, tm, blk):
    r = x.shape[0]
    t_tiles = tabs[0].shape[0] // tm
    tab_spec = pl.BlockSpec((tm, LANES), lambda i: (i % t_tiles, 0))
    nb = tm // blk
    kern = functools.partial(_odd_in_kernel, tm=tm, blk=blk)
    sds = jax.ShapeDtypeStruct
    return pl.pallas_call(
        kern,
        grid=(r // tm,),
        in_specs=[_rows(tm, D_MODEL), _full((1, D_MODEL)), _full((D_MODEL, ODD_IN_PADDED)),
                  _full((LANES, D_CQK)), _full((1, D_CQK)), tab_spec, tab_spec, tab_spec],
        out_specs=[_rows(tm, D_CQK), _rows(tm, D_CQK), _rows(nb, D_CQK), _rows(tm, D_C), _rows(tm, D_C),
                   _rows(tm, D_D), _rows(tm, D_D), _rows(tm, D_D)],
        out_shape=[sds((r, D_CQK), BF16), sds((r, D_CQK), BF16), sds((r // blk, D_CQK), F32),
                   sds((r, D_C), BF16), sds((r, D_C), F32),
                   sds((r, D_D), BF16), sds((r, D_D), F32), sds((r, D_D), F32)],
        scratch_shapes=[pltpu.VMEM((D_CQK // LANES, tm, LANES), F32)],
        compiler_params=_params(),
        name="odd_in",
    )(x, nw, w, wg, bg, *tabs)


def _lambda_value(lp_ref, lambda_init):
    lp = lp_ref[...]
    s1 = jnp.sum(lp[0:1] * lp[1:2], axis=-1, keepdims=True)
    s2 = jnp.sum(lp[2:3] * lp[3:4], axis=-1, keepdims=True)
    return jnp.exp(s1) - jnp.exp(s2) + lambda_init


def _diff_finish(acc, den, lam, nw, lambda_init):
    o = acc[0] / den[0] - lam * (acc[1] / den[1])
    return (_rms(o, nw) * (1.0 - lambda_init)).astype(BF16)


def _diff_prompt_kernel(lp_ref, nw_ref, q_ref, k_ref, v_ref, o_ref, kbf, vbf, *, tq, lambda_init):
    qi = pl.program_id(2)

    @pl.when(qi == 0)
    def _():
        kbf[...] = k_ref[...].astype(BF16)
        vbf[...] = v_ref[...].astype(BF16)

    qs = [q_ref[:, c * HD_D:(c + 1) * HD_D] for c in range(2)]
    qc = lax.broadcasted_iota(jnp.int32, (tq, tq), 0) // CHUNK
    kc = lax.broadcasted_iota(jnp.int32, (tq, tq), 1) // CHUNK
    causal = kc <= qc

    def step(j, carry, masked):
        start = pl.multiple_of(j * tq, tq)
        ks = kbf[pl.ds(start, tq), :]
        vs = vbf[pl.ds(start, tq), :]
        out = []
        for c in range(2):
            m, l, acc = carry[c]
            s = _dot_nt(qs[c], ks[:, c * HD_D:(c + 1) * HD_D])
            if masked:
                s = jnp.where(causal, s, NEG_INF)
            m_new = jnp.maximum(m, jnp.max(s, axis=-1, keepdims=True))
            alpha = jnp.exp(m - m_new)
            p = jnp.exp(s - m_new)
            l = alpha * l + jnp.sum(p, axis=-1, keepdims=True)
            acc = alpha * acc + _dot(p.astype(BF16), vs)
            out.append((m_new, l, acc))
        return tuple(out)

    init = tuple((jnp.full((tq, 1), NEG_INF, F32), jnp.zeros((tq, 1), F32), jnp.zeros((tq, 2 * HD_D), F32))
                 for _ in range(2))
    carry = lax.fori_loop(0, qi, lambda j, c: step(j, c, False), init)
    carry = step(qi, carry, True)
    lam = _lambda_value(lp_ref, lambda_init)
    o_ref[...] = _diff_finish([carry[0][2], carry[1][2]], [carry[0][1], carry[1][1]], lam, nw_ref[...], lambda_init)


def _diff_prompt(lp, nw, q, k, v, batch, seq, lambda_init, tq):
    r = q.shape[0]
    nq = seq // tq
    kern = functools.partial(_diff_prompt_kernel, tq=tq, lambda_init=lambda_init)
    kv_spec = pl.BlockSpec((seq, 2 * HD_D), lambda b, h, i: (b, h))
    q_spec = pl.BlockSpec((tq, 2 * HD_D), lambda b, h, i: (b * nq + i, h))
    return pl.pallas_call(
        kern,
        grid=(batch, H_D, nq),
        in_specs=[pl.BlockSpec((4, HD_D), lambda b, h, i: (0, 0)), pl.BlockSpec((1, 2 * HD_D), lambda b, h, i: (0, 0)),
                  q_spec, kv_spec, kv_spec],
        out_specs=q_spec,
        out_shape=jax.ShapeDtypeStruct((r, D_D), BF16),
        scratch_shapes=[pltpu.VMEM((seq, 2 * HD_D), BF16), pltpu.VMEM((seq, 2 * HD_D), BF16)],
        compiler_params=_params(3),
        name="diff_prompt",
    )(lp, nw, q, k, v)


def _diff_sample_kernel(lp_ref, nw_ref, q_ref, k_ref, v_ref, kc_ref, vc_ref, o_ref, *, l, lambda_init):
    pad = jnp.zeros((LANES - l, 2 * HD_D), BF16)
    kall = jnp.concatenate([kc_ref[...].astype(BF16), k_ref[...].astype(BF16), pad], axis=0)
    vall = jnp.concatenate([vc_ref[...].astype(BF16), v_ref[...].astype(BF16), pad], axis=0)
    nk = kall.shape[0]
    valid = lax.broadcasted_iota(jnp.int32, (l, nk), 1) < nk - (LANES - l)
    acc, den = [], []
    for c in range(2):
        s = _dot_nt(q_ref[:, c * HD_D:(c + 1) * HD_D], kall[:, c * HD_D:(c + 1) * HD_D])
        s = jnp.where(valid, s, NEG_INF)
        p = jnp.exp(s - jnp.max(s, axis=-1, keepdims=True))
        den.append(jnp.sum(p, axis=-1, keepdims=True))
        acc.append(_dot(p.astype(BF16), vall))
    lam = _lambda_value(lp_ref, lambda_init)
    o_ref[...] = _diff_finish(acc, den, lam, nw_ref[...], lambda_init)


def _diff_sample(lp, nw, q, k, v, kc, vc, batch, l, lambda_init):
    r = q.shape[0]
    past = kc.shape[0] // batch
    kern = functools.partial(_diff_sample_kernel, l=l, lambda_init=lambda_init)
    new_spec = pl.BlockSpec((l, 2 * HD_D), lambda b, h: (b, h))
    cache_spec = pl.BlockSpec((past, 2 * HD_D), lambda b, h: (b, h))
    return pl.pallas_call(
        kern,
        grid=(batch, H_D),
        in_specs=[pl.BlockSpec((4, HD_D), lambda b, h: (0, 0)), pl.BlockSpec((1, 2 * HD_D), lambda b, h: (0, 0)),
                  new_spec, new_spec, new_spec, cache_spec, cache_spec],
        out_specs=new_spec,
        out_shape=jax.ShapeDtypeStruct((r, D_D), BF16),
        compiler_params=_params(2),
        name="diff_sample",
    )(lp, nw, q, k, v, kc, vc)


def _odd_mix_kernel(qd_ref, kd_ref, el_ref, cv_ref, cg_ref, yd_ref, x_ref, s0_ref, gw_ref, wo_ref,
                    o_ref, s_ref, state, og, ycat, *, tm, blk, tiles_per_seq, per_block_state):
    i = pl.program_id(0)
    if not per_block_state:
        @pl.when(i % tiles_per_seq == 0)
        def _():
            state[...] = jnp.zeros_like(state)

    tri = lax.broadcasted_iota(jnp.int32, (blk, blk), 1) <= lax.broadcasted_iota(jnp.int32, (blk, blk), 0)
    for n in range(tm // blk):
        r0 = n * blk
        el = el_ref[n:n + 1, :]
        for h in range(H_C):
            if per_block_state:
                st = s0_ref[n, h]
            else:
                st = state[h]
            qd = qd_ref[r0:r0 + blk, h * DK_C:(h + 1) * DK_C]
            kd = kd_ref[r0:r0 + blk, h * DK_C:(h + 1) * DK_C]
            vh = cv_ref[r0:r0 + blk, h * DV_C:(h + 1) * DV_C]
            att = jnp.where(tri, _dot_nt(qd, kd), 0.0).astype(BF16)
            og[r0:r0 + blk, h * DV_C:(h + 1) * DV_C] = _dot(att, vh) + _dot_nt(qd, st.astype(BF16))
            st = el[:, h * DK_C:(h + 1) * DK_C] * (st + _dot_tn(vh, kd))
            if per_block_state:
                s_ref[n, h] = st
            else:
                state[h] = st
    if not per_block_state:
        s_ref[0] = state[...]

    gw = gw_ref[...]
    for h in range(H_C):
        lo, hi = h * DV_C, (h + 1) * DV_C
        g = cg_ref[:, lo:hi]
        y = _rms(og[:, lo:hi], gw) * (g * (1.0 / (1.0 + jnp.exp(-g))))
        ycat[:, lo:hi] = y.astype(BF16)
    ycat[:, D_C:] = yd_ref[...]
    o_ref[...] = x_ref[...] + _dot(ycat[...], wo_ref[...])


def _odd_mix(qd, kd, el, cv, cg, yd, x, s0, gw, wo, tm, blk, seq, per_block_state):
    r = x.shape[0]
    nb = tm // blk
    tiles_per_seq = max(seq // tm, 1)
    kern = functools.partial(_odd_mix_kernel, tm=tm, blk=blk, tiles_per_seq=tiles_per_seq,
                             per_block_state=per_block_state)
    if per_block_state:
        n_state = r // blk
        state_spec = pl.BlockSpec((nb, H_C, DV_C, DK_C), lambda i: (i, 0, 0, 0))
    else:
        n_state = r // seq
        state_spec = pl.BlockSpec((1, H_C, DV_C, DK_C), lambda i: (i // tiles_per_seq, 0, 0, 0))
    return pl.pallas_call(
        kern,
        grid=(r // tm,),
        in_specs=[_rows(tm, D_CQK), _rows(tm, D_CQK), _rows(nb, D_CQK), _rows(tm, D_C), _rows(tm, D_C),
                  _rows(tm, D_D), _rows(tm, D_MODEL), state_spec, _full((1, DV_C)), _full((D_C + D_D, D_MODEL))],
        out_specs=[_rows(tm, D_MODEL), state_spec],
        out_shape=[jax.ShapeDtypeStruct((r, D_MODEL), F32),
                   jax.ShapeDtypeStruct((n_state, H_C, DV_C, DK_C), F32)],
        scratch_shapes=[pltpu.VMEM((H_C, DV_C, DK_C), F32), pltpu.VMEM((tm, D_C), F32),
                        pltpu.VMEM((tm, D_C + D_D), BF16)],
        compiler_params=_params(),
        name="odd_mix",
    )(qd, kd, el, cv, cg, yd, x, s0, gw, wo)


def _gelu_tanh(x):
    return 0.5 * x * (1.0 + jnp.tanh(math.sqrt(2.0 / math.pi) * (x + 0.044715 * (x * x * x))))


def _ffn_kernel(x_ref, nw_ref, wup_ref, cw_ref, cb_ref, wdn_ref, prev_ref, fnw_ref, o_ref, tail_ref, carry,
                *, tm, seg, tiles_per_seq, prev_from_input, final_norm):
    i = pl.program_id(0)
    x = x_ref[...]
    h = _rms(x, nw_ref[...]).astype(BF16)
    nseg = tm // seg
    row = lax.broadcasted_iota(jnp.int32, (seg, 1), 0)
    if not prev_from_input:
        @pl.when(i % tiles_per_seq == 0)
        def _():
            carry[...] = jnp.zeros_like(carry)

    acc = jnp.zeros((tm, D_MODEL), F32)
    for c0 in range(0, D_FF, FFN_COLS):
        c1 = c0 + FFN_COLS
        gate = _dot(h, wup_ref[:, c0:c1])
        val = _dot(h, wup_ref[:, D_FF + c0:D_FF + c1])
        w0, w1, w2 = cw_ref[0:1, c0:c1], cw_ref[1:2, c0:c1], cw_ref[2:3, c0:c1]
        convs = []
        for s in range(nseg):
            g = gate[s * seg:(s + 1) * seg]
            if prev_from_input:
                p0, p1 = prev_ref[s, 0:1, c0:c1], prev_ref[s, 1:2, c0:c1]
            else:
                p0 = carry[CONV_TAIL - 2:CONV_TAIL - 1, c0:c1]
                p1 = carry[CONV_TAIL - 1:CONV_TAIL, c0:c1]
            g1 = jnp.where(row == 0, p1, pltpu.roll(g, 1, 0))
            g2 = jnp.where(row == 0, p0, jnp.where(row == 1, p1, pltpu.roll(g, 2, 0)))
            convs.append(cb_ref[:, c0:c1] + g2 * w0 + g1 * w1 + g * w2)
            tail_ref[s, :, c0:c1] = g[seg - CONV_TAIL:]
        conv = convs[0] if nseg == 1 else jnp.concatenate(convs, axis=0)
        if not prev_from_input:
            carry[:, c0:c1] = gate[tm - CONV_TAIL:]
        act = (_gelu_tanh(conv) * val).astype(BF16)
        acc = acc + _dot(act, wdn_ref[c0:c1, :])
    y = x + acc
    if final_norm:
        y = _rms(y, fnw_ref[...])
    o_ref[...] = y


def _ffn(x, nw, wup, cw, cb, wdn, prev, fnw, tm, seg, seq, prev_from_input, final_norm):
    r = x.shape[0]
    nseg = tm // seg
    kern = functools.partial(_ffn_kernel, tm=tm, seg=seg, tiles_per_seq=max(seq // tm, 1),
                             prev_from_input=prev_from_input, final_norm=final_norm)
    prev_spec = pl.BlockSpec((nseg, CONV_W - 1, D_FF), lambda i: (i, 0, 0))
    tail_spec = pl.BlockSpec((nseg, CONV_TAIL, D_FF), lambda i: (i, 0, 0))
    return pl.pallas_call(
        kern,
        grid=(r // tm,),
        in_specs=[_rows(tm, D_MODEL), _full((1, D_MODEL)), _full((D_MODEL, 2 * D_FF)), _full((CONV_W, D_FF)),
                  _full((1, D_FF)), _full((D_FF, D_MODEL)), prev_spec, _full((1, D_MODEL))],
        out_specs=[_rows(tm, D_MODEL), tail_spec],
        out_shape=[jax.ShapeDtypeStruct((r, D_MODEL), F32),
                   jax.ShapeDtypeStruct((r // seg, CONV_TAIL, D_FF), F32)],
        scratch_shapes=[pltpu.VMEM((CONV_TAIL, D_FF), F32)],
        compiler_params=_params(),
        name="ffn",
    )(x, nw, wup, cw, cb, wdn, prev, fnw)


def _pad_odd_w_in(w):
    cr = w[:, ODD_CR0:ODD_CR0 + GATE_RANK]
    return jnp.concatenate([w[:, :ODD_CR0], cr, jnp.zeros((D_MODEL, LANES - GATE_RANK), w.dtype),
                            w[:, ODD_CR0 + GATE_RANK:]], axis=1).astype(BF16)


def _trunk(x3, pos, prompt, caches, weights):
    (norm_mix_w, norm_ffn_w, final_norm_w, even_w_in, pool_w, pool_scale, swa_sinks, even_w_out, odd_w_in,
     gla_w_gate2, gla_b_gate, gla_norm_w, diff_lambda, diff_norm_w, odd_w_out, ffn_w_up, ffn_conv_w, ffn_conv_b,
     ffn_w_down) = weights
    cache_pool, cache_swa_k, cache_swa_v, state_gla, cache_diff_k, cache_diff_v, cache_ffn_conv = caches
    b, l, _ = x3.shape
    r = b * l
    depth = norm_mix_w.shape[0]
    x = x3.reshape(r, D_MODEL)
    if prompt:
        tm = min(ROW_TILE, l)
        blk = min(CHUNK, l)
        tabs = _rope_tables(pos)
    else:
        tm = r
        blk = min(CHUNK, l)
        tabs = tuple(jnp.tile(t, (b, 1)) for t in _rope_tables(pos))
    pos0 = 0 if prompt else cache_diff_k.shape[2]

    pools, swa_ks, swa_vs, glas, diff_ks, diff_vs, ffns = [], [], [], [], [], [], []
    for layer in range(depth):
        p = layer // 2
        nw = norm_mix_w[layer].reshape(1, D_MODEL)
        if layer % 2 == 0:
            u, q, k, v = _even_in(x, nw, even_w_in[p].astype(BF16), tabs, tm)
            wp = pool_w[p].astype(BF16)
            ps = pool_scale[p].reshape(1, D_A)
            wo = even_w_out[p].astype(BF16)
            if prompt:
                x = _even_mix_prompt(swa_sinks[p], u, q, k, v, x, wp, ps, wo, l, tm)
                pools.append(u.reshape(b, l, D_A)[:, l - POOL_STATE:])
                swa_ks.append(k.reshape(b, l, HKV_B, HEAD_DIM)[:, l - WINDOW:])
                swa_vs.append(v.reshape(b, l, HKV_B, HEAD_DIM)[:, l - WINDOW:])
            else:
                rows_c = cache_swa_k.shape[2]
                pool_prev = jnp.pad(cache_pool[p], ((0, 0), (1, 0), (0, 0)))
                kc = cache_swa_k[p].reshape(b, rows_c, D_KV)
                vc = cache_swa_v[p].reshape(b, rows_c, D_KV)
                x = _even_mix_sample(swa_sinks[p], u, q, k, v, x, pool_prev, kc, vc, wp, ps, wo, l, pos0)
                pools.append(jnp.concatenate([cache_pool[p], u.reshape(b, l, D_A)], axis=1)[:, -POOL_STATE:])
                swa_ks.append(jnp.concatenate([cache_swa_k[p], k.reshape(b, l, HKV_B, HEAD_DIM)], axis=1)[:, -rows_c:])
                swa_vs.append(jnp.concatenate([cache_swa_v[p], v.reshape(b, l, HKV_B, HEAD_DIM)], axis=1)[:, -rows_c:])
        else:
            lambda_init = 0.8 - 0.6 * math.exp(-0.3 * layer)
            wg = jnp.pad(gla_w_gate2[p], ((0, LANES - GATE_RANK), (0, 0))).astype(BF16)
            bg = gla_b_gate[p].reshape(1, D_CQK)
            qd, kd, el, cv, cg, dq, dk, dv = _odd_in(x, nw, _pad_odd_w_in(odd_w_in[p]), wg, bg, tabs, tm, blk)
            lp = diff_lambda[p]
            dnw = diff_norm_w[p].reshape(1, 2 * HD_D)
            if prompt:
                yd = _diff_prompt(lp, dnw, dq, dk, dv, b, l, lambda_init, min(ATTN_TILE, l))
                s0 = jnp.zeros((b, H_C, DV_C, DK_C), F32)
            else:
                past = cache_diff_k.shape[2]
                kc = cache_diff_k[p].reshape(b * past, D_D)
                vc = cache_diff_v[p].reshape(b * past, D_D)
                yd = _diff_sample(lp, dnw, dq, dk, dv, kc, vc, b, l, lambda_init)
                s0 = jnp.swapaxes(state_gla[p], -1, -2).astype(F32)
            x, s_new = _odd_mix(qd, kd, el, cv, cg, yd, x, s0, gla_norm_w[p].reshape(1, DV_C),
                                odd_w_out[p].astype(BF16), tm, blk, l, not prompt)
            glas.append(jnp.swapaxes(s_new, -1, -2))
            diff_ks.append(dk.reshape(b, l, H_D, 2, HD_D))
            diff_vs.append(dv.reshape(b, l, H_D, 2 * HD_D))
        if prompt:
            seg = tm
            prev = jnp.zeros((r // tm, CONV_W - 1, D_FF), F32)
        else:
            seg = l
            prev = cache_ffn_conv[layer]
        x, tails = _ffn(x, norm_ffn_w[layer].reshape(1, D_MODEL), ffn_w_up[layer].astype(BF16), ffn_conv_w[layer],
                        ffn_conv_b[layer].reshape(1, D_FF), ffn_w_down[layer].astype(BF16), prev,
                        final_norm_w.reshape(1, D_MODEL), tm, seg, l, not prompt, layer == depth - 1)
        tails = tails.reshape(b, -1, CONV_TAIL, D_FF)
        ffns.append(tails[:, -1, CONV_TAIL - (CONV_W - 1):])
    return (x.reshape(b, l, D_MODEL), jnp.stack(pools), jnp.stack(swa_ks), jnp.stack(swa_vs), jnp.stack(glas),
            jnp.stack(diff_ks), jnp.stack(diff_vs), jnp.stack(ffns))


def kernel(x_prompt, x_sample, cache_pool, cache_swa_k, cache_swa_v, state_gla, cache_diff_k, cache_diff_v, cache_ffn_conv, norm_mix_w, norm_ffn_w, final_norm_w, even_w_in, pool_w, pool_scale, swa_sinks, even_w_out, odd_w_in, gla_w_gate2, gla_b_gate, gla_norm_w, diff_lambda, diff_norm_w, odd_w_out, ffn_w_up, ffn_conv_w, ffn_conv_b, ffn_w_down):
    weights = (norm_mix_w, norm_ffn_w, final_norm_w, even_w_in, pool_w, pool_scale, swa_sinks, even_w_out, odd_w_in,
               gla_w_gate2, gla_b_gate, gla_norm_w, diff_lambda, diff_norm_w, odd_w_out, ffn_w_up, ffn_conv_w,
               ffn_conv_b, ffn_w_down)
    caches = (cache_pool, cache_swa_k, cache_swa_v, state_gla, cache_diff_k, cache_diff_v, cache_ffn_conv)
    past = cache_diff_k.shape[2]
    pos_p = jnp.arange(x_prompt.shape[1])
    pos_s = past + jnp.arange(x_sample.shape[1])
    outs_p = _trunk(x_prompt, pos_p, True, caches, weights)
    outs_s = _trunk(x_sample, pos_s, False, caches, weights)
    return (outs_p[0], outs_s[0]) + tuple(outs_p[1:]) + tuple(outs_s[1:])
```

```python
import functools
import math

import jax
import jax.numpy as jnp
from jax import lax
from jax.experimental import pallas as pl
from jax.experimental.pallas import tpu as pltpu

F32 = jnp.float32
BF16 = jnp.bfloat16

D_MODEL = 1024
CHUNK = 64
EPS = 1e-6
NEG_INF = -1e30
LOG2E = 1.0 / math.log(2.0)
ROPE_THETA = 500000.0
HEAD_DIM = 64
ROT_DIM = HEAD_DIM // 4
ROT_HALF = ROT_DIM // 2
LANES = 128

POOL_WINDOWS = (2, 4, 8, 16)
POOL_GROUP = D_MODEL // 8
D_A = len(POOL_WINDOWS) * POOL_GROUP
POOL_STATE = max(POOL_WINDOWS) - 1
POOL_HALO = POOL_STATE + 1

H_B = D_MODEL // 128
HKV_B = 2
GQA_B = H_B // HKV_B
WINDOW = 128
D_B = H_B * HEAD_DIM
D_KV = HKV_B * HEAD_DIM

H_C = 4
DK_C = 64
DV_C = 128
GATE_RANK = 16
GATE_NORM = 16.0
D_CQK = H_C * DK_C
D_C = H_C * DV_C

H_D = 4
HD_D = 64
D_D = H_D * 2 * HD_D

D_FF = 2816
CONV_W = 3
CONV_TAIL = 8

ROW_TILE = 512
ATTN_TILE = 512
FFN_COLS = 256
VMEM_LIMIT = 60 * 1024 * 1024


def _dot(a, b):
    return jnp.dot(a, b, preferred_element_type=F32)


def _dot_nt(a, b):
    return lax.dot_general(a, b, (((1,), (1,)), ((), ())), preferred_element_type=F32)


def _dot_tn(a, b):
    return lax.dot_general(a, b, (((0,), (0,)), ((), ())), preferred_element_type=F32)


def _rms(x, w):
    return x * lax.rsqrt(jnp.mean(x * x, axis=-1, keepdims=True) + EPS) * w


def _rope(z, cos_t, sin_a, sin_b):
    outs = []
    for s in range(z.shape[1] // LANES):
        x = z[:, s * LANES:(s + 1) * LANES]
        outs.append(x * cos_t + pltpu.roll(x, LANES - ROT_HALF, 1) * sin_a + pltpu.roll(x, ROT_HALF, 1) * sin_b)
    return outs[0] if len(outs) == 1 else jnp.concatenate(outs, axis=-1)


def _rope_tables(pos):
    t = pos.shape[0]
    inv = jnp.exp(-math.log(ROPE_THETA) * jnp.arange(ROT_HALF, dtype=F32) * (2.0 / ROT_DIM))
    ang = pos.astype(F32)[:, None] * inv[None, :]
    cos, sin = jnp.cos(ang), jnp.sin(ang)
    rest = HEAD_DIM - ROT_DIM
    cos_t = jnp.concatenate([cos, cos, jnp.ones((t, rest), F32)], axis=-1)
    sin_a = jnp.concatenate([-sin, jnp.zeros((t, HEAD_DIM - ROT_HALF), F32)], axis=-1)
    sin_b = jnp.concatenate([jnp.zeros((t, ROT_HALF), F32), sin, jnp.zeros((t, rest), F32)], axis=-1)
    rep = LANES // HEAD_DIM
    return tuple(jnp.tile(a, (1, rep)) for a in (cos_t, sin_a, sin_b))


def _params(n_axes=1):
    return pltpu.CompilerParams(dimension_semantics=("arbitrary",) * n_axes, vmem_limit_bytes=VMEM_LIMIT)


def _full(shape):
    nd = len(shape)
    return pl.BlockSpec(shape, lambda *_: (0,) * nd)


def _rows(tm, cols):
    return pl.BlockSpec((tm, cols), lambda i: (i, 0))


def _even_in_kernel(x_ref, nw_ref, w_ref, cos_ref, sa_ref, sb_ref, u_ref, q_ref, k_ref, v_ref):
    h = _rms(x_ref[...], nw_ref[...]).astype(BF16)
    cos_t, sin_a, sin_b = cos_ref[...], sa_ref[...], sb_ref[...]
    u_ref[...] = _dot(h, w_ref[:, 0:D_A])
    q = _dot(h, w_ref[:, D_A:D_A + D_B])
    q_ref[...] = (_rope(q, cos_t, sin_a, sin_b) * (HEAD_DIM ** -0.5)).astype(BF16)
    k = _dot(h, w_ref[:, D_A + D_B:D_A + D_B + D_KV])
    k_ref[...] = _rope(k, cos_t, sin_a, sin_b)
    v_ref[...] = _dot(h, w_ref[:, D_A + D_B + D_KV:D_A + D_B + 2 * D_KV])


def _even_in(x, nw, w, tabs, tm):
    r = x.shape[0]
    t_tiles = tabs[0].shape[0] // tm
    tab_spec = pl.BlockSpec((tm, LANES), lambda i: (i % t_tiles, 0))
    n_in = D_A + D_B + 2 * D_KV
    return pl.pallas_call(
        _even_in_kernel,
        grid=(r // tm,),
        in_specs=[_rows(tm, D_MODEL), _full((1, D_MODEL)), _full((D_MODEL, n_in)), tab_spec, tab_spec, tab_spec],
        out_specs=[_rows(tm, D_A), _rows(tm, D_B), _rows(tm, D_KV), _rows(tm, D_KV)],
        out_shape=[jax.ShapeDtypeStruct((r, D_A), F32), jax.ShapeDtypeStruct((r, D_B), BF16),
                   jax.ShapeDtypeStruct((r, D_KV), F32), jax.ShapeDtypeStruct((r, D_KV), F32)],
        compiler_params=_params(),
        name="even_in",
    )(x, nw, w, *tabs)


def _pool_mix(ext, u, pos, wp_ref, ps, ycat_ref, row0, rows):
    for g, w in enumerate(POOL_WINDOWS):
        lo, hi = g * POOL_GROUP, (g + 1) * POOL_GROUP
        s = ext[:, lo:hi]
        sh = 1
        while sh < w:
            s = s + pltpu.roll(s, sh, 0)
            sh *= 2
        count = jnp.minimum(w, pos + 1).astype(F32)
        diff = s[POOL_HALO:] / count - u[:, lo:hi]
        ya = _dot(diff.astype(BF16), wp_ref[g]) * ps[:, lo:hi]
        ycat_ref[row0:row0 + rows, lo:hi] = ya.astype(BF16)


def _sink_column(sink_ref, hk, rows_per_head):
    n = GQA_B * rows_per_head
    grp = lax.broadcasted_iota(jnp.int32, (n, 1), 0) // rows_per_head
    col = jnp.full((n, 1), sink_ref[hk * GQA_B], F32)
    for g in range(1, GQA_B):
        col = jnp.where(grp == g, sink_ref[hk * GQA_B + g], col)
    return col


def _sink_attend(s, sink_col, vb):
    m = jnp.maximum(jnp.max(s, axis=-1, keepdims=True), sink_col)
    e = jnp.exp(s - m)
    den = jnp.sum(e, axis=-1, keepdims=True) + jnp.exp(sink_col - m)
    return _dot(e.astype(BF16), vb) / den


def _store_heads(ycat_ref, o, row0, rows, hk):
    for pair in range(GQA_B // 2):
        both = jnp.concatenate([o[(2 * pair) * rows:(2 * pair + 1) * rows],
                                o[(2 * pair + 1) * rows:(2 * pair + 2) * rows]], axis=-1)
        c0 = D_A + (hk * GQA_B + 2 * pair) * HEAD_DIM
        ycat_ref[row0:row0 + rows, c0:c0 + 2 * HEAD_DIM] = both.astype(BF16)


def _even_mix_prompt_kernel(sink_ref, u_ref, q_ref, k_ref, v_ref, x_ref, wp_ref, ps_ref, wo_ref, o_ref,
                            ucar, kcar, vcar, ycat, *, tm, tiles_per_seq):
    i = pl.program_id(0)
    ti = i % tiles_per_seq
    first_key_chunk = jnp.where(ti == 0, WINDOW // CHUNK, 0)

    @pl.when(ti == 0)
    def _():
        ucar[...] = jnp.zeros_like(ucar)
        kcar[...] = jnp.zeros_like(kcar)
        vcar[...] = jnp.zeros_like(vcar)

    u = u_ref[...]
    ext = jnp.concatenate([ucar[...], u], axis=0)
    pos = ti * tm + lax.broadcasted_iota(jnp.int32, (tm, 1), 0)
    _pool_mix(ext, u, pos, wp_ref, ps_ref[...], ycat, 0, tm)
    ucar[...] = u[tm - POOL_HALO:]

    kext = jnp.concatenate([kcar[...], k_ref[...].astype(BF16)], axis=0)
    vext = jnp.concatenate([vcar[...], v_ref[...].astype(BF16)], axis=0)
    qb = 2 * CHUNK
    kb = qb + WINDOW
    qc = (lax.broadcasted_iota(jnp.int32, (GQA_B * qb, kb), 0) % qb) // CHUNK
    kc = lax.broadcasted_iota(jnp.int32, (GQA_B * qb, kb), 1) // CHUNK
    band = (kc >= qc) & (kc <= qc + WINDOW // CHUNK)
    for a in range(tm // qb):
        mask = band
        if a == 0:
            mask = band & (kc >= first_key_chunk)
        for hk in range(HKV_B):
            ks = kext[a * qb:a * qb + kb, hk * HEAD_DIM:(hk + 1) * HEAD_DIM]
            vs = vext[a * qb:a * qb + kb, hk * HEAD_DIM:(hk + 1) * HEAD_DIM]
            qs = jnp.concatenate(
                [q_ref[a * qb:(a + 1) * qb, (hk * GQA_B + g) * HEAD_DIM:(hk * GQA_B + g + 1) * HEAD_DIM]
                 for g in range(GQA_B)], axis=0)
            s = jnp.where(mask, _dot_nt(qs, ks), NEG_INF)
            o = _sink_attend(s, _sink_column(sink_ref, hk, qb), vs)
            _store_heads(ycat, o, a * qb, qb, hk)
    kcar[...] = kext[tm:]
    vcar[...] = vext[tm:]
    o_ref[...] = x_ref[...] + _dot(ycat[...], wo_ref[...])


def _even_mix_prompt(sinks, u, q, k, v, x, wp, ps, wo, seq, tm):
    r = x.shape[0]
    kern = functools.partial(_even_mix_prompt_kernel, tm=tm, tiles_per_seq=seq // tm)
    return pl.pallas_call(
        kern,
        grid=(r // tm,),
        in_specs=[pl.BlockSpec(memory_space=pltpu.SMEM),
                  _rows(tm, D_A), _rows(tm, D_B), _rows(tm, D_KV), _rows(tm, D_KV), _rows(tm, D_MODEL),
                  _full(wp.shape), _full((1, D_A)), _full((D_A + D_B, D_MODEL))],
        out_specs=_rows(tm, D_MODEL),
        out_shape=jax.ShapeDtypeStruct((r, D_MODEL), F32),
        scratch_shapes=[pltpu.VMEM((POOL_HALO, D_A), F32), pltpu.VMEM((WINDOW, D_KV), BF16),
                        pltpu.VMEM((WINDOW, D_KV), BF16), pltpu.VMEM((tm, D_A + D_B), BF16)],
        compiler_params=_params(),
        name="even_mix_prompt",
    )(sinks, u, q, k, v, x, wp, ps, wo)


def _even_mix_sample_kernel(sink_ref, u_ref, q_ref, k_ref, v_ref, x_ref, pp_ref, kc_ref, vc_ref, wp_ref, ps_ref,
                            wo_ref, o_ref, ycat, *, l, pos0):
    u = u_ref[...]
    ext = jnp.concatenate([pp_ref[0], u], axis=0)
    pos = pos0 + lax.broadcasted_iota(jnp.int32, (l, 1), 0)
    _pool_mix(ext, u, pos, wp_ref, ps_ref[...], ycat, 0, l)

    pad = jnp.zeros((LANES - l, D_KV), BF16)
    kall = jnp.concatenate([kc_ref[0].astype(BF16), k_ref[...].astype(BF16), pad], axis=0)
    vall = jnp.concatenate([vc_ref[0].astype(BF16), v_ref[...].astype(BF16), pad], axis=0)
    nk = kall.shape[0]
    valid = lax.broadcasted_iota(jnp.int32, (GQA_B * l, nk), 1) < nk - (LANES - l)
    for hk in range(HKV_B):
        ks = kall[:, hk * HEAD_DIM:(hk + 1) * HEAD_DIM]
        vs = vall[:, hk * HEAD_DIM:(hk + 1) * HEAD_DIM]
        qs = jnp.concatenate(
            [q_ref[:, (hk * GQA_B + g) * HEAD_DIM:(hk * GQA_B + g + 1) * HEAD_DIM] for g in range(GQA_B)], axis=0)
        s = jnp.where(valid, _dot_nt(qs, ks), NEG_INF)
        o = _sink_attend(s, _sink_column(sink_ref, hk, l), vs)
        _store_heads(ycat, o, 0, l, hk)
    o_ref[...] = x_ref[...] + _dot(ycat[...], wo_ref[...])


def _even_mix_sample(sinks, u, q, k, v, x, pool_prev, kc, vc, wp, ps, wo, l, pos0):
    r = x.shape[0]
    rows_c = kc.shape[1]
    kern = functools.partial(_even_mix_sample_kernel, l=l, pos0=pos0)
    cache_spec = pl.BlockSpec((1, rows_c, D_KV), lambda i: (i, 0, 0))
    return pl.pallas_call(
        kern,
        grid=(r // l,),
        in_specs=[pl.BlockSpec(memory_space=pltpu.SMEM),
                  _rows(l, D_A), _rows(l, D_B), _rows(l, D_KV), _rows(l, D_KV), _rows(l, D_MODEL),
                  pl.BlockSpec((1, POOL_HALO, D_A), lambda i: (i, 0, 0)), cache_spec, cache_spec,
                  _full(wp.shape), _full((1, D_A)), _full((D_A + D_B, D_MODEL))],
        out_specs=_rows(l, D_MODEL),
        out_shape=jax.ShapeDtypeStruct((r, D_MODEL), F32),
        scratch_shapes=[pltpu.VMEM((l, D_A + D_B), BF16)],
        compiler_params=_params(),
        name="even_mix_sample",
    )(sinks, u, q, k, v, x, pool_prev, kc, vc, wp, ps, wo)


ODD_CR0 = 2 * D_CQK + 2 * D_C
ODD_DQ0 = ODD_CR0 + LANES
ODD_IN_PADDED = ODD_DQ0 + 3 * D_D


def _odd_in_kernel(x_ref, nw_ref, w_ref, wg_ref, bg_ref, cos_ref, sa_ref, sb_ref,
                   qd_ref, kd_ref, el_ref, cv_ref, cg_ref, dq_ref, dk_ref, dv_ref, cum_scr, *, tm, blk):
    h = _rms(x_ref[...], nw_ref[...]).astype(BF16)
    cos_t, sin_a, sin_b = cos_ref[...], sa_ref[...], sb_ref[...]
    cq = _dot(h, w_ref[:, 0:D_CQK])
    ck = _dot(h, w_ref[:, D_CQK:2 * D_CQK])
    cv_ref[...] = _dot(h, w_ref[:, 2 * D_CQK:2 * D_CQK + D_C]).astype(BF16)
    cg_ref[...] = _dot(h, w_ref[:, 2 * D_CQK + D_C:ODD_CR0])
    cr = _dot(h, w_ref[:, ODD_CR0:ODD_DQ0]).astype(BF16)
    gate = _dot(cr, wg_ref[...]) + bg_ref[...]
    cum = (jnp.minimum(gate, 0.0) - jnp.log1p(jnp.exp(-jnp.abs(gate)))) * (1.0 / GATE_NORM)
    rowb = lax.broadcasted_iota(jnp.int32, (tm, 1), 0) % blk
    sh = 1
    while sh < blk:
        cum = cum + jnp.where(rowb >= sh, pltpu.roll(cum, sh, 0), 0.0)
        sh *= 2
    for s in range(D_CQK // LANES):
        cum_scr[s] = cum[:, s * LANES:(s + 1) * LANES]
        el_ref[:, s * LANES:(s + 1) * LANES] = jnp.exp(cum_scr[s, pl.ds(blk - 1, tm // blk, stride=blk), :])
    qd_ref[...] = (cq * (DK_C ** -0.5) * jnp.exp(cum)).astype(BF16)
    kd_ref[...] = (ck * jnp.exp(-cum)).astype(BF16)
    dq = _dot(h, w_ref[:, ODD_DQ0:ODD_DQ0 + D_D])
    dq_ref[...] = (_rope(dq, cos_t, sin_a, sin_b) * (HD_D ** -0.5 * LOG2E)).astype(BF16)
    dk = _dot(h, w_ref[:, ODD_DQ0 + D_D:ODD_DQ0 + 2 * D_D])
    dk_ref[...] = _rope(dk, cos_t, sin_a, sin_b)
    dv_ref[...] = _dot(h, w_ref[:, ODD_DQ0 + 2 * D_D:ODD_DQ0 + 3 * D_D])


def _odd_in(x, nw, w, wg, bg, tabs, tm, blk):
    r = x.shape[0]
    t_tiles = tabs[0].shape[0] // tm
    tab_spec = pl.BlockSpec((tm, LANES), lambda i: (i % t_tiles, 0))
    nb = tm // blk
    kern = functools.partial(_odd_in_kernel, tm=tm, blk=blk)
    sds = jax.ShapeDtypeStruct
    return pl.pallas_call(
        kern,
        grid=(r // tm,),
        in_specs=[_rows(tm, D_MODEL), _full((1, D_MODEL)), _full((D_MODEL, ODD_IN_PADDED)),
                  _full((LANES, D_CQK)), _full((1, D_CQK)), tab_spec, tab_spec, tab_spec],
        out_specs=[_rows(tm, D_CQK), _rows(tm, D_CQK), _rows(nb, D_CQK), _rows(tm, D_C), _rows(tm, D_C),
                   _rows(tm, D_D), _rows(tm, D_D), _rows(tm, D_D)],
        out_shape=[sds((r, D_CQK), BF16), sds((r, D_CQK), BF16), sds((r // blk, D_CQK), F32),
                   sds((r, D_C), BF16), sds((r, D_C), F32),
                   sds((r, D_D), BF16), sds((r, D_D), F32), sds((r, D_D), F32)],
        scratch_shapes=[pltpu.VMEM((D_CQK // LANES, tm, LANES), F32)],
        compiler_params=_params(),
        name="odd_in",
    )(x, nw, w, wg, bg, *tabs)


def _lambda_value(lp_ref, lambda_init):
    lp = lp_ref[...]
    s1 = jnp.sum(lp[0:1] * lp[1:2], axis=-1, keepdims=True)
    s2 = jnp.sum(lp[2:3] * lp[3:4], axis=-1, keepdims=True)
    return jnp.exp(s1) - jnp.exp(s2) + lambda_init


def _diff_finish(acc, den, lam, nw, lambda_init):
    o = acc[0] / den[0] - lam * (acc[1] / den[1])
    return (_rms(o, nw) * (1.0 - lambda_init)).astype(BF16)


def _diff_prompt_kernel(lp_ref, nw_ref, q_ref, k_ref, v_ref, o_ref, kbf, vbf, m_scr, acc_scr, s_a, s_b,
                        *, tq, lambda_init):
    qi = pl.program_id(2)
    dv = 2 * HD_D

    @pl.when(qi == 0)
    def _():
        k = k_ref[...]
        for c in range(2):
            kbf[c] = k[:, c * HD_D:(c + 1) * HD_D].astype(BF16)
        vbf[:, :dv] = v_ref[...].astype(BF16)
        vbf[:, dv:] = jnp.ones((vbf.shape[0], dv), BF16)

    m_scr[...] = jnp.full(m_scr.shape, NEG_INF, F32)
    acc_scr[...] = jnp.zeros(acc_scr.shape, F32)
    qs = [q_ref[:, c * HD_D:(c + 1) * HD_D] for c in range(2)]

    def scores(t, s_buf):
        start = pl.multiple_of(t * tq, tq)
        for c in range(2):
            s_buf[c] = _dot_nt(qs[c], kbf[c, pl.ds(start, tq), :])

    def attend(t, s_buf, masked):
        vs = vbf[pl.ds(pl.multiple_of(t * tq, tq), tq), :]
        for c in range(2):
            sc = s_buf[c]
            if masked:
                qc = lax.broadcasted_iota(jnp.int32, (tq, tq), 0) // CHUNK
                kc = lax.broadcasted_iota(jnp.int32, (tq, tq), 1) // CHUNK
                sc = jnp.where(kc <= qc, sc, NEG_INF)
            m_old = m_scr[c]
            m_new = jnp.maximum(m_old, jnp.max(sc, axis=-1, keepdims=True))
            p = jnp.exp2(sc - pltpu.repeat(m_new, tq // LANES, axis=1)).astype(BF16)
            alpha = pltpu.repeat(jnp.exp2(m_old - m_new), 2 * dv // LANES, axis=1)
            acc_scr[c] = alpha * acc_scr[c] + _dot(p, vs)
            m_scr[c] = m_new

    scores(0, s_a)

    def pair(j, _):
        t = 2 * j
        scores(t + 1, s_b)
        attend(t, s_a, False)
        scores(t + 2, s_a)
        attend(t + 1, s_b, False)
        return 0

    lax.fori_loop(0, qi // 2, pair, 0)

    @pl.when(qi % 2 == 0)
    def _():
        attend(qi, s_a, True)

    @pl.when(qi % 2 == 1)
    def _():
        scores(qi, s_b)
        attend(qi - 1, s_a, False)
        attend(qi, s_b, True)

    lam = _lambda_value(lp_ref, lambda_init)
    acc = [acc_scr[c] for c in range(2)]
    o_ref[...] = _diff_finish([a[:, :dv] for a in acc], [a[:, dv:] for a in acc], lam, nw_ref[...], lambda_init)


def _diff_prompt(lp, nw, q, k, v, batch, seq, lambda_init, tq):
    r = q.shape[0]
    nq = seq // tq
    kern = functools.partial(_diff_prompt_kernel, tq=tq, lambda_init=lambda_init)
    kv_spec = pl.BlockSpec((seq, 2 * HD_D), lambda b, h, i: (b, h))
    q_spec = pl.BlockSpec((tq, 2 * HD_D), lambda b, h, i: (b * nq + i, h))
    return pl.pallas_call(
        kern,
        grid=(batch, H_D, nq),
        in_specs=[pl.BlockSpec((4, HD_D), lambda b, h, i: (0, 0)), pl.BlockSpec((1, 2 * HD_D), lambda b, h, i: (0, 0)),
                  q_spec, kv_spec, kv_spec],
        out_specs=q_spec,
        out_shape=jax.ShapeDtypeStruct((r, D_D), BF16),
        scratch_shapes=[pltpu.VMEM((2, seq, HD_D), BF16), pltpu.VMEM((seq, 4 * HD_D), BF16),
                        pltpu.VMEM((2, tq, LANES), F32), pltpu.VMEM((2, tq, 4 * HD_D), F32),
                        pltpu.VMEM((2, tq, tq), F32), pltpu.VMEM((2, tq, tq), F32)],
        compiler_params=_params(3),
        name="diff_prompt",
    )(lp, nw, q, k, v)


def _diff_sample_kernel(lp_ref, nw_ref, q_ref, k_ref, v_ref, kc_ref, vc_ref, o_ref, *, l, lambda_init):
    pad = jnp.zeros((LANES - l, 2 * HD_D), BF16)
    kall = jnp.concatenate([kc_ref[...].astype(BF16), k_ref[...].astype(BF16), pad], axis=0)
    vall = jnp.concatenate([vc_ref[...].astype(BF16), v_ref[...].astype(BF16), pad], axis=0)
    nk = kall.shape[0]
    valid = lax.broadcasted_iota(jnp.int32, (l, nk), 1) < nk - (LANES - l)
    acc, den = [], []
    for c in range(2):
        s = _dot_nt(q_ref[:, c * HD_D:(c + 1) * HD_D], kall[:, c * HD_D:(c + 1) * HD_D])
        s = jnp.where(valid, s, NEG_INF)
        p = jnp.exp2(s - jnp.max(s, axis=-1, keepdims=True))
        den.append(jnp.sum(p, axis=-1, keepdims=True))
        acc.append(_dot(p.astype(BF16), vall))
    lam = _lambda_value(lp_ref, lambda_init)
    o_ref[...] = _diff_finish(acc, den, lam, nw_ref[...], lambda_init)


def _diff_sample(lp, nw, q, k, v, kc, vc, batch, l, lambda_init):
    r = q.shape[0]
    past = kc.shape[0] // batch
    kern = functools.partial(_diff_sample_kernel, l=l, lambda_init=lambda_init)
    new_spec = pl.BlockSpec((l, 2 * HD_D), lambda b, h: (b, h))
    cache_spec = pl.BlockSpec((past, 2 * HD_D), lambda b, h: (b, h))
    return pl.pallas_call(
        kern,
        grid=(batch, H_D),
        in_specs=[pl.BlockSpec((4, HD_D), lambda b, h: (0, 0)), pl.BlockSpec((1, 2 * HD_D), lambda b, h: (0, 0)),
                  new_spec, new_spec, new_spec, cache_spec, cache_spec],
        out_specs=new_spec,
        out_shape=jax.ShapeDtypeStruct((r, D_D), BF16),
        compiler_params=_params(2),
        name="diff_sample",
    )(lp, nw, q, k, v, kc, vc)


def _odd_mix_kernel(qd_ref, kd_ref, el_ref, cv_ref, cg_ref, yd_ref, x_ref, s0_ref, gw_ref, wo_ref,
                    o_ref, s_ref, state, og, ycat, *, tm, blk, tiles_per_seq, per_block_state):
    i = pl.program_id(0)
    if not per_block_state:
        @pl.when(i % tiles_per_seq == 0)
        def _():
            state[...] = jnp.zeros_like(state)

    tri = lax.broadcasted_iota(jnp.int32, (blk, blk), 1) <= lax.broadcasted_iota(jnp.int32, (blk, blk), 0)
    for n in range(tm // blk):
        r0 = n * blk
        el = el_ref[n:n + 1, :]
        for h in range(H_C):
            if per_block_state:
                st = s0_ref[n, h]
            else:
                st = state[h]
            qd = qd_ref[r0:r0 + blk, h * DK_C:(h + 1) * DK_C]
            kd = kd_ref[r0:r0 + blk, h * DK_C:(h + 1) * DK_C]
            vh = cv_ref[r0:r0 + blk, h * DV_C:(h + 1) * DV_C]
            att = jnp.where(tri, _dot_nt(qd, kd), 0.0).astype(BF16)
            og[r0:r0 + blk, h * DV_C:(h + 1) * DV_C] = _dot(att, vh) + _dot_nt(qd, st.astype(BF16))
            st = el[:, h * DK_C:(h + 1) * DK_C] * (st + _dot_tn(vh, kd))
            if per_block_state:
                s_ref[n, h] = st
            else:
                state[h] = st
    if not per_block_state:
        s_ref[0] = state[...]

    gw = gw_ref[...]
    for h in range(H_C):
        lo, hi = h * DV_C, (h + 1) * DV_C
        g = cg_ref[:, lo:hi]
        y = _rms(og[:, lo:hi], gw) * (g * (1.0 / (1.0 + jnp.exp(-g))))
        ycat[:, lo:hi] = y.astype(BF16)
    ycat[:, D_C:] = yd_ref[...]
    o_ref[...] = x_ref[...] + _dot(ycat[...], wo_ref[...])


def _odd_mix(qd, kd, el, cv, cg, yd, x, s0, gw, wo, tm, blk, seq, per_block_state):
    r = x.shape[0]
    nb = tm // blk
    tiles_per_seq = max(seq // tm, 1)
    kern = functools.partial(_odd_mix_kernel, tm=tm, blk=blk, tiles_per_seq=tiles_per_seq,
                             per_block_state=per_block_state)
    if per_block_state:
        n_state = r // blk
        state_spec = pl.BlockSpec((nb, H_C, DV_C, DK_C), lambda i: (i, 0, 0, 0))
    else:
        n_state = r // seq
        state_spec = pl.BlockSpec((1, H_C, DV_C, DK_C), lambda i: (i // tiles_per_seq, 0, 0, 0))
    return pl.pallas_call(
        kern,
        grid=(r // tm,),
        in_specs=[_rows(tm, D_CQK), _rows(tm, D_CQK), _rows(nb, D_CQK), _rows(tm, D_C), _rows(tm, D_C),
                  _rows(tm, D_D), _rows(tm, D_MODEL), state_spec, _full((1, DV_C)), _full((D_C + D_D, D_MODEL))],
        out_specs=[_rows(tm, D_MODEL), state_spec],
        out_shape=[jax.ShapeDtypeStruct((r, D_MODEL), F32),
                   jax.ShapeDtypeStruct((n_state, H_C, DV_C, DK_C), F32)],
        scratch_shapes=[pltpu.VMEM((H_C, DV_C, DK_C), F32), pltpu.VMEM((tm, D_C), F32),
                        pltpu.VMEM((tm, D_C + D_D), BF16)],
        compiler_params=_params(),
        name="odd_mix",
    )(qd, kd, el, cv, cg, yd, x, s0, gw, wo)


def _gelu_tanh(x):
    return 0.5 * x * (1.0 + jnp.tanh(math.sqrt(2.0 / math.pi) * (x + 0.044715 * (x * x * x))))


def _ffn_kernel(x_ref, nw_ref, wup_ref, cw_ref, cb_ref, wdn_ref, prev_ref, fnw_ref, o_ref, tail_ref, carry, act,
                *, tm, seg, tiles_per_seq, prev_from_input, final_norm):
    i = pl.program_id(0)
    x = x_ref[...]
    h = _rms(x, nw_ref[...]).astype(BF16)
    nseg = tm // seg
    row = lax.broadcasted_iota(jnp.int32, (seg, 1), 0)
    if not prev_from_input:
        @pl.when(i % tiles_per_seq == 0)
        def _():
            carry[...] = jnp.zeros_like(carry)

    def up(c0):
        return _dot(h, wup_ref[:, c0:c0 + FFN_COLS]), _dot(h, wup_ref[:, D_FF + c0:D_FF + c0 + FFN_COLS])

    nxt = up(0)
    for c0 in range(0, D_FF, FFN_COLS):
        c1 = c0 + FFN_COLS
        gate, val = nxt
        if c1 < D_FF:
            nxt = up(c1)
        w0, w1, w2 = cw_ref[0:1, c0:c1], cw_ref[1:2, c0:c1], cw_ref[2:3, c0:c1]
        convs = []
        for s in range(nseg):
            g = gate[s * seg:(s + 1) * seg]
            if prev_from_input:
                p0, p1 = prev_ref[s, 0:1, c0:c1], prev_ref[s, 1:2, c0:c1]
            else:
                p0 = carry[CONV_TAIL - 2:CONV_TAIL - 1, c0:c1]
                p1 = carry[CONV_TAIL - 1:CONV_TAIL, c0:c1]
            g1 = jnp.where(row == 0, p1, pltpu.roll(g, 1, 0))
            g2 = jnp.where(row == 0, p0, jnp.where(row == 1, p1, pltpu.roll(g, 2, 0)))
            convs.append(cb_ref[:, c0:c1] + g2 * w0 + g1 * w1 + g * w2)
            tail_ref[s, :, c0:c1] = g[seg - CONV_TAIL:]
        conv = convs[0] if nseg == 1 else jnp.concatenate(convs, axis=0)
        if not prev_from_input:
            carry[:, c0:c1] = gate[tm - CONV_TAIL:]
        act[:, c0:c1] = (_gelu_tanh(conv) * val).astype(BF16)
    y = x + _dot(act[...], wdn_ref[...])
    if final_norm:
        y = _rms(y, fnw_ref[...])
    o_ref[...] = y


def _ffn(x, nw, wup, cw, cb, wdn, prev, fnw, tm, seg, seq, prev_from_input, final_norm):
    r = x.shape[0]
    nseg = tm // seg
    kern = functools.partial(_ffn_kernel, tm=tm, seg=seg, tiles_per_seq=max(seq // tm, 1),
                             prev_from_input=prev_from_input, final_norm=final_norm)
    prev_spec = pl.BlockSpec((nseg, CONV_W - 1, D_FF), lambda i: (i, 0, 0))
    tail_spec = pl.BlockSpec((nseg, CONV_TAIL, D_FF), lambda i: (i, 0, 0))
    return pl.pallas_call(
        kern,
        grid=(r // tm,),
        in_specs=[_rows(tm, D_MODEL), _full((1, D_MODEL)), _full((D_MODEL, 2 * D_FF)), _full((CONV_W, D_FF)),
                  _full((1, D_FF)), _full((D_FF, D_MODEL)), prev_spec, _full((1, D_MODEL))],
        out_specs=[_rows(tm, D_MODEL), tail_spec],
        out_shape=[jax.ShapeDtypeStruct((r, D_MODEL), F32),
                   jax.ShapeDtypeStruct((r // seg, CONV_TAIL, D_FF), F32)],
        scratch_shapes=[pltpu.VMEM((CONV_TAIL, D_FF), F32), pltpu.VMEM((tm, D_FF), BF16)],
        compiler_params=_params(),
        name="ffn",
    )(x, nw, wup, cw, cb, wdn, prev, fnw)


def _pad_odd_w_in(w):
    cr = w[:, ODD_CR0:ODD_CR0 + GATE_RANK]
    return jnp.concatenate([w[:, :ODD_CR0], cr, jnp.zeros((D_MODEL, LANES - GATE_RANK), w.dtype),
                            w[:, ODD_CR0 + GATE_RANK:]], axis=1).astype(BF16)


def _trunk(x3, pos, prompt, caches, weights):
    (norm_mix_w, norm_ffn_w, final_norm_w, even_w_in, pool_w, pool_scale, swa_sinks, even_w_out, odd_w_in,
     gla_w_gate2, gla_b_gate, gla_norm_w, diff_lambda, diff_norm_w, odd_w_out, ffn_w_up, ffn_conv_w, ffn_conv_b,
     ffn_w_down) = weights
    cache_pool, cache_swa_k, cache_swa_v, state_gla, cache_diff_k, cache_diff_v, cache_ffn_conv = caches
    b, l, _ = x3.shape
    r = b * l
    depth = norm_mix_w.shape[0]
    x = x3.reshape(r, D_MODEL)
    if prompt:
        tm = min(ROW_TILE, l)
        blk = min(CHUNK, l)
        tabs = _rope_tables(pos)
    else:
        tm = r
        blk = min(CHUNK, l)
        tabs = tuple(jnp.tile(t, (b, 1)) for t in _rope_tables(pos))
    pos0 = 0 if prompt else cache_diff_k.shape[2]

    pools, swa_ks, swa_vs, glas, diff_ks, diff_vs, ffns = [], [], [], [], [], [], []
    for layer in range(depth):
        p = layer // 2
        nw = norm_mix_w[layer].reshape(1, D_MODEL)
        if layer % 2 == 0:
            u, q, k, v = _even_in(x, nw, even_w_in[p].astype(BF16), tabs, tm)
            wp = pool_w[p].astype(BF16)
            ps = pool_scale[p].reshape(1, D_A)
            wo = even_w_out[p].astype(BF16)
            if prompt:
                x = _even_mix_prompt(swa_sinks[p], u, q, k, v, x, wp, ps, wo, l, tm)
                pools.append(u.reshape(b, l, D_A)[:, l - POOL_STATE:])
                swa_ks.append(k.reshape(b, l, HKV_B, HEAD_DIM)[:, l - WINDOW:])
                swa_vs.append(v.reshape(b, l, HKV_B, HEAD_DIM)[:, l - WINDOW:])
            else:
                rows_c = cache_swa_k.shape[2]
                pool_prev = jnp.pad(cache_pool[p], ((0, 0), (1, 0), (0, 0)))
                kc = cache_swa_k[p].reshape(b, rows_c, D_KV)
                vc = cache_swa_v[p].reshape(b, rows_c, D_KV)
                x = _even_mix_sample(swa_sinks[p], u, q, k, v, x, pool_prev, kc, vc, wp, ps, wo, l, pos0)
                pools.append(jnp.concatenate([cache_pool[p], u.reshape(b, l, D_A)], axis=1)[:, -POOL_STATE:])
                swa_ks.append(jnp.concatenate([cache_swa_k[p], k.reshape(b, l, HKV_B, HEAD_DIM)], axis=1)[:, -rows_c:])
                swa_vs.append(jnp.concatenate([cache_swa_v[p], v.reshape(b, l, HKV_B, HEAD_DIM)], axis=1)[:, -rows_c:])
        else:
            lambda_init = 0.8 - 0.6 * math.exp(-0.3 * layer)
            wg = jnp.pad(gla_w_gate2[p], ((0, LANES - GATE_RANK), (0, 0))).astype(BF16)
            bg = gla_b_gate[p].reshape(1, D_CQK)
            qd, kd, el, cv, cg, dq, dk, dv = _odd_in(x, nw, _pad_odd_w_in(odd_w_in[p]), wg, bg, tabs, tm, blk)
            lp = diff_lambda[p]
            dnw = diff_norm_w[p].reshape(1, 2 * HD_D)
            if prompt:
                yd = _diff_prompt(lp, dnw, dq, dk, dv, b, l, lambda_init, min(ATTN_TILE, l))
                s0 = jnp.zeros((b, H_C, DV_C, DK_C), F32)
            else:
                past = cache_diff_k.shape[2]
                kc = cache_diff_k[p].reshape(b * past, D_D)
                vc = cache_diff_v[p].reshape(b * past, D_D)
                yd = _diff_sample(lp, dnw, dq, dk, dv, kc, vc, b, l, lambda_init)
                s0 = jnp.swapaxes(state_gla[p], -1, -2).astype(F32)
            x, s_new = _odd_mix(qd, kd, el, cv, cg, yd, x, s0, gla_norm_w[p].reshape(1, DV_C),
                                odd_w_out[p].astype(BF16), tm, blk, l, not prompt)
            glas.append(jnp.swapaxes(s_new, -1, -2))
            diff_ks.append(dk.reshape(b, l, H_D, 2, HD_D))
            diff_vs.append(dv.reshape(b, l, H_D, 2 * HD_D))
        if prompt:
            seg = tm
            prev = jnp.zeros((r // tm, CONV_W - 1, D_FF), F32)
        else:
            seg = l
            prev = cache_ffn_conv[layer]
        x, tails = _ffn(x, norm_ffn_w[layer].reshape(1, D_MODEL), ffn_w_up[layer].astype(BF16), ffn_conv_w[layer],
                        ffn_conv_b[layer].reshape(1, D_FF), ffn_w_down[layer].astype(BF16), prev,
                        final_norm_w.reshape(1, D_MODEL), tm, seg, l, not prompt, layer == depth - 1)
        tails = tails.reshape(b, -1, CONV_TAIL, D_FF)
        ffns.append(tails[:, -1, CONV_TAIL - (CONV_W - 1):])
    return (x.reshape(b, l, D_MODEL), jnp.stack(pools), jnp.stack(swa_ks), jnp.stack(swa_vs), jnp.stack(glas),
            jnp.stack(diff_ks), jnp.stack(diff_vs), jnp.stack(ffns))


def kernel(x_prompt, x_sample, cache_pool, cache_swa_k, cache_swa_v, state_gla, cache_diff_k, cache_diff_v, cache_ffn_conv, norm_mix_w, norm_ffn_w, final_norm_w, even_w_in, pool_w, pool_scale, swa_sinks, even_w_out, odd_w_in, gla_w_gate2, gla_b_gate, gla_norm_w, diff_lambda, diff_norm_w, odd_w_out, ffn_w_up, ffn_conv_w, ffn_conv_b, ffn_w_down):
    weights = (norm_mix_w, norm_ffn_w, final_norm_w, even_w_in, pool_w, pool_scale, swa_sinks, even_w_out, odd_w_in,
               gla_w_gate2, gla_b_gate, gla_norm_w, diff_lambda, diff_norm_w, odd_w_out, ffn_w_up, ffn_conv_w,
               ffn_conv_b, ffn_w_down)
    caches = (cache_pool, cache_swa_k, cache_swa_v, state_gla, cache_diff_k, cache_diff_v, cache_ffn_conv)
    past = cache_diff_k.shape[2]
    pos_p = jnp.arange(x_prompt.shape[1])
    pos_s = past + jnp.arange(x_sample.shape[1])
    outs_p = _trunk(x_prompt, pos_p, True, caches, weights)
    outs_s = _trunk(x_sample, pos_s, False, caches, weights)
    return (outs_p[0], outs_s[0]) + tuple(outs_p[1:]) + tuple(outs_s[1:])
```

```python
import functools
import math

import jax
import jax.numpy as jnp
from jax import lax
from jax.experimental import pallas as pl
from jax.experimental.pallas import tpu as pltpu

F32 = jnp.float32
BF16 = jnp.bfloat16

D_MODEL = 1024
CHUNK = 64
EPS = 1e-6
NEG_INF = -1e30
LOG2E = 1.0 / math.log(2.0)
ROPE_THETA = 500000.0
HEAD_DIM = 64
ROT_DIM = HEAD_DIM // 4
ROT_HALF = ROT_DIM // 2
LANES = 128

POOL_WINDOWS = (2, 4, 8, 16)
POOL_GROUP = D_MODEL // 8
D_A = len(POOL_WINDOWS) * POOL_GROUP
POOL_STATE = max(POOL_WINDOWS) - 1
POOL_HALO = POOL_STATE + 1

H_B = D_MODEL // 128
HKV_B = 2
GQA_B = H_B // HKV_B
WINDOW = 128
D_B = H_B * HEAD_DIM
D_KV = HKV_B * HEAD_DIM

H_C = 4
DK_C = 64
DV_C = 128
GATE_RANK = 16
GATE_NORM = 16.0
D_CQK = H_C * DK_C
D_C = H_C * DV_C

H_D = 4
HD_D = 64
D_D = H_D * 2 * HD_D

D_FF = 2816
CONV_W = 3
CONV_TAIL = 8

ROW_TILE = 512
ATTN_TILE = 1024
FFN_COLS = 256
VMEM_LIMIT = 60 * 1024 * 1024


def _dot(a, b):
    return jnp.dot(a, b, preferred_element_type=F32)


def _dot_nt(a, b):
    return lax.dot_general(a, b, (((1,), (1,)), ((), ())), preferred_element_type=F32)


def _dot_tn(a, b):
    return lax.dot_general(a, b, (((0,), (0,)), ((), ())), preferred_element_type=F32)


def _rms(x, w):
    return x * lax.rsqrt(jnp.mean(x * x, axis=-1, keepdims=True) + EPS) * w


def _rope(z, cos_t, sin_a, sin_b):
    outs = []
    for s in range(z.shape[1] // LANES):
        x = z[:, s * LANES:(s + 1) * LANES]
        outs.append(x * cos_t + pltpu.roll(x, LANES - ROT_HALF, 1) * sin_a + pltpu.roll(x, ROT_HALF, 1) * sin_b)
    return outs[0] if len(outs) == 1 else jnp.concatenate(outs, axis=-1)


def _rope_tables(pos):
    t = pos.shape[0]
    inv = jnp.exp(-math.log(ROPE_THETA) * jnp.arange(ROT_HALF, dtype=F32) * (2.0 / ROT_DIM))
    ang = pos.astype(F32)[:, None] * inv[None, :]
    cos, sin = jnp.cos(ang), jnp.sin(ang)
    rest = HEAD_DIM - ROT_DIM
    cos_t = jnp.concatenate([cos, cos, jnp.ones((t, rest), F32)], axis=-1)
    sin_a = jnp.concatenate([-sin, jnp.zeros((t, HEAD_DIM - ROT_HALF), F32)], axis=-1)
    sin_b = jnp.concatenate([jnp.zeros((t, ROT_HALF), F32), sin, jnp.zeros((t, rest), F32)], axis=-1)
    rep = LANES // HEAD_DIM
    return tuple(jnp.tile(a, (1, rep)) for a in (cos_t, sin_a, sin_b))


def _params(n_axes=1):
    return pltpu.CompilerParams(dimension_semantics=("arbitrary",) * n_axes, vmem_limit_bytes=VMEM_LIMIT)


def _full(shape):
    nd = len(shape)
    return pl.BlockSpec(shape, lambda *_: (0,) * nd)


def _rows(tm, cols):
    return pl.BlockSpec((tm, cols), lambda i: (i, 0))


def _even_in_kernel(x_ref, nw_ref, w_ref, cos_ref, sa_ref, sb_ref, u_ref, q_ref, k_ref, v_ref):
    h = _rms(x_ref[...], nw_ref[...]).astype(BF16)
    cos_t, sin_a, sin_b = cos_ref[...], sa_ref[...], sb_ref[...]
    u_ref[...] = _dot(h, w_ref[:, 0:D_A])
    q = _dot(h, w_ref[:, D_A:D_A + D_B])
    q_ref[...] = (_rope(q, cos_t, sin_a, sin_b) * (HEAD_DIM ** -0.5)).astype(BF16)
    k = _dot(h, w_ref[:, D_A + D_B:D_A + D_B + D_KV])
    k_ref[...] = _rope(k, cos_t, sin_a, sin_b)
    v_ref[...] = _dot(h, w_ref[:, D_A + D_B + D_KV:D_A + D_B + 2 * D_KV])


def _even_in(x, nw, w, tabs, tm):
    r = x.shape[0]
    t_tiles = tabs[0].shape[0] // tm
    tab_spec = pl.BlockSpec((tm, LANES), lambda i: (i % t_tiles, 0))
    n_in = D_A + D_B + 2 * D_KV
    return pl.pallas_call(
        _even_in_kernel,
        grid=(r // tm,),
        in_specs=[_rows(tm, D_MODEL), _full((1, D_MODEL)), _full((D_MODEL, n_in)), tab_spec, tab_spec, tab_spec],
        out_specs=[_rows(tm, D_A), _rows(tm, D_B), _rows(tm, D_KV), _rows(tm, D_KV)],
        out_shape=[jax.ShapeDtypeStruct((r, D_A), F32), jax.ShapeDtypeStruct((r, D_B), BF16),
                   jax.ShapeDtypeStruct((r, D_KV), F32), jax.ShapeDtypeStruct((r, D_KV), F32)],
        compiler_params=_params(),
        name="even_in",
    )(x, nw, w, *tabs)


def _pool_mix(ext, u, pos, wp_ref, ps, ycat_ref, row0, rows):
    for g, w in enumerate(POOL_WINDOWS):
        lo, hi = g * POOL_GROUP, (g + 1) * POOL_GROUP
        s = ext[:, lo:hi]
        sh = 1
        while sh < w:
            s = s + pltpu.roll(s, sh, 0)
            sh *= 2
        count = jnp.minimum(w, pos + 1).astype(F32)
        diff = s[POOL_HALO:] / count - u[:, lo:hi]
        ya = _dot(diff.astype(BF16), wp_ref[g]) * ps[:, lo:hi]
        ycat_ref[row0:row0 + rows, lo:hi] = ya.astype(BF16)


def _sink_column(sink_ref, hk, rows_per_head):
    n = GQA_B * rows_per_head
    grp = lax.broadcasted_iota(jnp.int32, (n, 1), 0) // rows_per_head
    col = jnp.full((n, 1), sink_ref[hk * GQA_B], F32)
    for g in range(1, GQA_B):
        col = jnp.where(grp == g, sink_ref[hk * GQA_B + g], col)
    return col


def _sink_attend(s, sink_col, vb):
    m = jnp.maximum(jnp.max(s, axis=-1, keepdims=True), sink_col)
    e = jnp.exp(s - m)
    den = jnp.sum(e, axis=-1, keepdims=True) + jnp.exp(sink_col - m)
    return _dot(e.astype(BF16), vb) / den


def _store_heads(ycat_ref, o, row0, rows, hk):
    for pair in range(GQA_B // 2):
        both = jnp.concatenate([o[(2 * pair) * rows:(2 * pair + 1) * rows],
                                o[(2 * pair + 1) * rows:(2 * pair + 2) * rows]], axis=-1)
        c0 = D_A + (hk * GQA_B + 2 * pair) * HEAD_DIM
        ycat_ref[row0:row0 + rows, c0:c0 + 2 * HEAD_DIM] = both.astype(BF16)


def _even_mix_prompt_kernel(sink_ref, u_ref, q_ref, k_ref, v_ref, x_ref, wp_ref, ps_ref, wo_ref, o_ref,
                            ucar, kcar, vcar, ycat, *, tm, tiles_per_seq):
    i = pl.program_id(0)
    ti = i % tiles_per_seq
    first_key_chunk = jnp.where(ti == 0, WINDOW // CHUNK, 0)

    @pl.when(ti == 0)
    def _():
        ucar[...] = jnp.zeros_like(ucar)
        kcar[...] = jnp.zeros_like(kcar)
        vcar[...] = jnp.zeros_like(vcar)

    u = u_ref[...]
    ext = jnp.concatenate([ucar[...], u], axis=0)
    pos = ti * tm + lax.broadcasted_iota(jnp.int32, (tm, 1), 0)
    _pool_mix(ext, u, pos, wp_ref, ps_ref[...], ycat, 0, tm)
    ucar[...] = u[tm - POOL_HALO:]

    kext = jnp.concatenate([kcar[...], k_ref[...].astype(BF16)], axis=0)
    vext = jnp.concatenate([vcar[...], v_ref[...].astype(BF16)], axis=0)
    qb = 2 * CHUNK
    kb = qb + WINDOW
    qc = (lax.broadcasted_iota(jnp.int32, (GQA_B * qb, kb), 0) % qb) // CHUNK
    kc = lax.broadcasted_iota(jnp.int32, (GQA_B * qb, kb), 1) // CHUNK
    band = (kc >= qc) & (kc <= qc + WINDOW // CHUNK)
    for a in range(tm // qb):
        mask = band
        if a == 0:
            mask = band & (kc >= first_key_chunk)
        for hk in range(HKV_B):
            ks = kext[a * qb:a * qb + kb, hk * HEAD_DIM:(hk + 1) * HEAD_DIM]
            vs = vext[a * qb:a * qb + kb, hk * HEAD_DIM:(hk + 1) * HEAD_DIM]
            qs = jnp.concatenate(
                [q_ref[a * qb:(a + 1) * qb, (hk * GQA_B + g) * HEAD_DIM:(hk * GQA_B + g + 1) * HEAD_DIM]
                 for g in range(GQA_B)], axis=0)
            s = jnp.where(mask, _dot_nt(qs, ks), NEG_INF)
            o = _sink_attend(s, _sink_column(sink_ref, hk, qb), vs)
            _store_heads(ycat, o, a * qb, qb, hk)
    kcar[...] = kext[tm:]
    vcar[...] = vext[tm:]
    o_ref[...] = x_ref[...] + _dot(ycat[...], wo_ref[...])


def _even_mix_prompt(sinks, u, q, k, v, x, wp, ps, wo, seq, tm):
    r = x.shape[0]
    kern = functools.partial(_even_mix_prompt_kernel, tm=tm, tiles_per_seq=seq // tm)
    return pl.pallas_call(
        kern,
        grid=(r // tm,),
        in_specs=[pl.BlockSpec(memory_space=pltpu.SMEM),
                  _rows(tm, D_A), _rows(tm, D_B), _rows(tm, D_KV), _rows(tm, D_KV), _rows(tm, D_MODEL),
                  _full(wp.shape), _full((1, D_A)), _full((D_A + D_B, D_MODEL))],
        out_specs=_rows(tm, D_MODEL),
        out_shape=jax.ShapeDtypeStruct((r, D_MODEL), F32),
        scratch_shapes=[pltpu.VMEM((POOL_HALO, D_A), F32), pltpu.VMEM((WINDOW, D_KV), BF16),
                        pltpu.VMEM((WINDOW, D_KV), BF16), pltpu.VMEM((tm, D_A + D_B), BF16)],
        compiler_params=_params(),
        name="even_mix_prompt",
    )(sinks, u, q, k, v, x, wp, ps, wo)


def _even_mix_sample_kernel(sink_ref, u_ref, q_ref, k_ref, v_ref, x_ref, pp_ref, kc_ref, vc_ref, wp_ref, ps_ref,
                            wo_ref, o_ref, ycat, *, l, pos0):
    u = u_ref[...]
    ext = jnp.concatenate([pp_ref[0], u], axis=0)
    pos = pos0 + lax.broadcasted_iota(jnp.int32, (l, 1), 0)
    _pool_mix(ext, u, pos, wp_ref, ps_ref[...], ycat, 0, l)

    pad = jnp.zeros((LANES - l, D_KV), BF16)
    kall = jnp.concatenate([kc_ref[0].astype(BF16), k_ref[...].astype(BF16), pad], axis=0)
    vall = jnp.concatenate([vc_ref[0].astype(BF16), v_ref[...].astype(BF16), pad], axis=0)
    nk = kall.shape[0]
    valid = lax.broadcasted_iota(jnp.int32, (GQA_B * l, nk), 1) < nk - (LANES - l)
    for hk in range(HKV_B):
        ks = kall[:, hk * HEAD_DIM:(hk + 1) * HEAD_DIM]
        vs = vall[:, hk * HEAD_DIM:(hk + 1) * HEAD_DIM]
        qs = jnp.concatenate(
            [q_ref[:, (hk * GQA_B + g) * HEAD_DIM:(hk * GQA_B + g + 1) * HEAD_DIM] for g in range(GQA_B)], axis=0)
        s = jnp.where(valid, _dot_nt(qs, ks), NEG_INF)
        o = _sink_attend(s, _sink_column(sink_ref, hk, l), vs)
        _store_heads(ycat, o, 0, l, hk)
    o_ref[...] = x_ref[...] + _dot(ycat[...], wo_ref[...])


def _even_mix_sample(sinks, u, q, k, v, x, pool_prev, kc, vc, wp, ps, wo, l, pos0):
    r = x.shape[0]
    rows_c = kc.shape[1]
    kern = functools.partial(_even_mix_sample_kernel, l=l, pos0=pos0)
    cache_spec = pl.BlockSpec((1, rows_c, D_KV), lambda i: (i, 0, 0))
    return pl.pallas_call(
        kern,
        grid=(r // l,),
        in_specs=[pl.BlockSpec(memory_space=pltpu.SMEM),
                  _rows(l, D_A), _rows(l, D_B), _rows(l, D_KV), _rows(l, D_KV), _rows(l, D_MODEL),
                  pl.BlockSpec((1, POOL_HALO, D_A), lambda i: (i, 0, 0)), cache_spec, cache_spec,
                  _full(wp.shape), _full((1, D_A)), _full((D_A + D_B, D_MODEL))],
        out_specs=_rows(l, D_MODEL),
        out_shape=jax.ShapeDtypeStruct((r, D_MODEL), F32),
        scratch_shapes=[pltpu.VMEM((l, D_A + D_B), BF16)],
        compiler_params=_params(),
        name="even_mix_sample",
    )(sinks, u, q, k, v, x, pool_prev, kc, vc, wp, ps, wo)


ODD_CR0 = 2 * D_CQK + 2 * D_C
ODD_DQ0 = ODD_CR0 + LANES
ODD_IN_PADDED = ODD_DQ0 + 3 * D_D


def _odd_in_kernel(x_ref, nw_ref, w_ref, wg_ref, bg_ref, cos_ref, sa_ref, sb_ref,
                   qd_ref, kd_ref, el_ref, cv_ref, cg_ref, dq_ref, dk_ref, dv_ref, cum_scr, *, tm, blk):
    h = _rms(x_ref[...], nw_ref[...]).astype(BF16)
    cos_t, sin_a, sin_b = cos_ref[...], sa_ref[...], sb_ref[...]
    cq = _dot(h, w_ref[:, 0:D_CQK])
    ck = _dot(h, w_ref[:, D_CQK:2 * D_CQK])
    cv_ref[...] = _dot(h, w_ref[:, 2 * D_CQK:2 * D_CQK + D_C]).astype(BF16)
    cg_ref[...] = _dot(h, w_ref[:, 2 * D_CQK + D_C:ODD_CR0])
    cr = _dot(h, w_ref[:, ODD_CR0:ODD_DQ0]).astype(BF16)
    gate = _dot(cr, wg_ref[...]) + bg_ref[...]
    cum = (jnp.minimum(gate, 0.0) - jnp.log1p(jnp.exp(-jnp.abs(gate)))) * (1.0 / GATE_NORM)
    rowb = lax.broadcasted_iota(jnp.int32, (tm, 1), 0) % blk
    sh = 1
    while sh < blk:
        cum = cum + jnp.where(rowb >= sh, pltpu.roll(cum, sh, 0), 0.0)
        sh *= 2
    for s in range(D_CQK // LANES):
        cum_scr[s] = cum[:, s * LANES:(s + 1) * LANES]
        el_ref[:, s * LANES:(s + 1) * LANES] = jnp.exp(cum_scr[s, pl.ds(blk - 1, tm // blk, stride=blk), :])
    qd_ref[...] = (cq * (DK_C ** -0.5) * jnp.exp(cum)).astype(BF16)
    kd_ref[...] = (ck * jnp.exp(-cum)).astype(BF16)
    dq = _dot(h, w_ref[:, ODD_DQ0:ODD_DQ0 + D_D])
    dq_ref[...] = (_rope(dq, cos_t, sin_a, sin_b) * (HD_D ** -0.5 * LOG2E)).astype(BF16)
    dk = _dot(h, w_ref[:, ODD_DQ0 + D_D:ODD_DQ0 + 2 * D_D])
    dk_ref[...] = _rope(dk, cos_t, sin_a, sin_b)
    dv_ref[...] = _dot(h, w_ref[:, ODD_DQ0 + 2 * D_D:ODD_DQ0 + 3 * D_D])


def _odd_in(x, nw, w, wg, bg, tabs, tm, blk):
    r = x.shape[0]
    t_tiles = tabs[0].shape[0] // tm
    tab_spec = pl.BlockSpec((tm, LANES), lambda i: (i % t_tiles, 0))
    nb = tm // blk
    kern = functools.partial(_odd_in_kernel, tm=tm, blk=blk)
    sds = jax.ShapeDtypeStruct
    return pl.pallas_call(
        kern,
        grid=(r // tm,),
        in_specs=[_rows(tm, D_MODEL), _full((1, D_MODEL)), _full((D_MODEL, ODD_IN_PADDED)),
                  _full((LANES, D_CQK)), _full((1, D_CQK)), tab_spec, tab_spec, tab_spec],
        out_specs=[_rows(tm, D_CQK), _rows(tm, D_CQK), _rows(nb, D_CQK), _rows(tm, D_C), _rows(tm, D_C),
                   _rows(tm, D_D), _rows(tm, D_D), _rows(tm, D_D)],
        out_shape=[sds((r, D_CQK), BF16), sds((r, D_CQK), BF16), sds((r // blk, D_CQK), F32),
                   sds((r, D_C), BF16), sds((r, D_C), F32),
                   sds((r, D_D), BF16), sds((r, D_D), F32), sds((r, D_D), F32)],
        scratch_shapes=[pltpu.VMEM((D_CQK // LANES, tm, LANES), F32)],
        compiler_params=_params(),
        name="odd_in",
    )(x, nw, w, wg, bg, *tabs)


def _lambda_value(lp_ref, lambda_init):
    lp = lp_ref[...]
    s1 = jnp.sum(lp[0:1] * lp[1:2], axis=-1, keepdims=True)
    s2 = jnp.sum(lp[2:3] * lp[3:4], axis=-1, keepdims=True)
    return jnp.exp(s1) - jnp.exp(s2) + lambda_init


def _diff_finish(acc, den, lam, nw, lambda_init):
    o = acc[0] / den[0] - lam * (acc[1] / den[1])
    return (_rms(o, nw) * (1.0 - lambda_init)).astype(BF16)


def _diff_prompt_kernel(lp_ref, nw_ref, q_ref, k_ref, v_ref, o_ref, kbf, vbf, m_scr, acc_scr, s_a, s_b, bias,
                        *, tq, lambda_init):
    qi = pl.program_id(2)
    dv = 2 * HD_D

    @pl.when(qi == 0)
    def _():
        k = k_ref[...]
        for c in range(2):
            kbf[c] = k[:, c * HD_D:(c + 1) * HD_D].astype(BF16)
        vbf[:, :dv] = v_ref[...].astype(BF16)
        vbf[:, dv:] = jnp.ones((vbf.shape[0], dv), BF16)

    tk = tq // 2

    @pl.when((pl.program_id(0) == 0) & (pl.program_id(1) == 0) & (qi == 0))
    def _():
        qc = lax.broadcasted_iota(jnp.int32, (tk, tk), 0) // CHUNK
        kc = lax.broadcasted_iota(jnp.int32, (tk, tk), 1) // CHUNK
        bias[...] = jnp.where(kc <= qc, 0.0, NEG_INF)

    m_scr[...] = jnp.full(m_scr.shape, NEG_INF, F32)
    acc_scr[...] = jnp.zeros(acc_scr.shape, F32)

    def scores(t, s_buf, row0=0):
        start = pl.multiple_of(t * tk, tk)
        for c in range(2):
            s_buf[c, row0:, :] = _dot_nt(q_ref[row0:, c * HD_D:(c + 1) * HD_D], kbf[c, pl.ds(start, tk), :])

    def attend(t, s_buf, row0=0, rows=tq, masked=False):
        vs = vbf[pl.ds(pl.multiple_of(t * tk, tk), tk), :]
        for c in range(2):
            sc = s_buf[c, row0:row0 + rows, :]
            if masked:
                sc = sc + bias[...]
            m_old = m_scr[c, row0:row0 + rows, :]
            m_new = jnp.maximum(m_old, jnp.max(sc, axis=-1, keepdims=True))
            p = jnp.exp2(sc - pltpu.repeat(m_new, tk // LANES, axis=1)).astype(BF16)
            alpha = pltpu.repeat(jnp.exp2(m_old - m_new), 2 * dv // LANES, axis=1)
            acc_scr[c, row0:row0 + rows, :] = alpha * acc_scr[c, row0:row0 + rows, :] + _dot(p, vs)
            m_scr[c, row0:row0 + rows, :] = m_new

    scores(0, s_a)

    def pair(j, _):
        t = 2 * j
        scores(t + 1, s_b)
        attend(t, s_a)
        scores(t + 2, s_a)
        attend(t + 1, s_b)
        return 0

    lax.fori_loop(0, qi, pair, 0)
    scores(2 * qi + 1, s_b, row0=tk)
    attend(2 * qi, s_a, row0=0, rows=tk, masked=True)
    attend(2 * qi, s_a, row0=tk, rows=tk)
    attend(2 * qi + 1, s_b, row0=tk, rows=tk, masked=True)

    lam = _lambda_value(lp_ref, lambda_init)
    acc = [acc_scr[c] for c in range(2)]
    o_ref[...] = _diff_finish([a[:, :dv] for a in acc], [a[:, dv:] for a in acc], lam, nw_ref[...], lambda_init)


def _diff_prompt(lp, nw, q, k, v, batch, seq, lambda_init, tq):
    r = q.shape[0]
    nq = seq // tq
    kern = functools.partial(_diff_prompt_kernel, tq=tq, lambda_init=lambda_init)
    kv_spec = pl.BlockSpec((seq, 2 * HD_D), lambda b, h, i: (b, h))
    q_spec = pl.BlockSpec((tq, 2 * HD_D), lambda b, h, i: (b * nq + i, h))
    return pl.pallas_call(
        kern,
        grid=(batch, H_D, nq),
        in_specs=[pl.BlockSpec((4, HD_D), lambda b, h, i: (0, 0)), pl.BlockSpec((1, 2 * HD_D), lambda b, h, i: (0, 0)),
                  q_spec, kv_spec, kv_spec],
        out_specs=q_spec,
        out_shape=jax.ShapeDtypeStruct((r, D_D), BF16),
        scratch_shapes=[pltpu.VMEM((2, seq, HD_D), BF16), pltpu.VMEM((seq, 4 * HD_D), BF16),
                        pltpu.VMEM((2, tq, LANES), F32), pltpu.VMEM((2, tq, 4 * HD_D), F32),
                        pltpu.VMEM((2, tq, tq // 2), F32), pltpu.VMEM((2, tq, tq // 2), F32),
                        pltpu.VMEM((tq // 2, tq // 2), F32)],
        compiler_params=_params(3),
        name="diff_prompt",
    )(lp, nw, q, k, v)


def _diff_sample_kernel(lp_ref, nw_ref, q_ref, k_ref, v_ref, kc_ref, vc_ref, o_ref, *, l, lambda_init):
    pad = jnp.zeros((LANES - l, 2 * HD_D), BF16)
    kall = jnp.concatenate([kc_ref[...].astype(BF16), k_ref[...].astype(BF16), pad], axis=0)
    vall = jnp.concatenate([vc_ref[...].astype(BF16), v_ref[...].astype(BF16), pad], axis=0)
    nk = kall.shape[0]
    valid = lax.broadcasted_iota(jnp.int32, (l, nk), 1) < nk - (LANES - l)
    acc, den = [], []
    for c in range(2):
        s = _dot_nt(q_ref[:, c * HD_D:(c + 1) * HD_D], kall[:, c * HD_D:(c + 1) * HD_D])
        s = jnp.where(valid, s, NEG_INF)
        p = jnp.exp2(s - jnp.max(s, axis=-1, keepdims=True))
        den.append(jnp.sum(p, axis=-1, keepdims=True))
        acc.append(_dot(p.astype(BF16), vall))
    lam = _lambda_value(lp_ref, lambda_init)
    o_ref[...] = _diff_finish(acc, den, lam, nw_ref[...], lambda_init)


def _diff_sample(lp, nw, q, k, v, kc, vc, batch, l, lambda_init):
    r = q.shape[0]
    past = kc.shape[0] // batch
    kern = functools.partial(_diff_sample_kernel, l=l, lambda_init=lambda_init)
    new_spec = pl.BlockSpec((l, 2 * HD_D), lambda b, h: (b, h))
    cache_spec = pl.BlockSpec((past, 2 * HD_D), lambda b, h: (b, h))
    return pl.pallas_call(
        kern,
        grid=(batch, H_D),
        in_specs=[pl.BlockSpec((4, HD_D), lambda b, h: (0, 0)), pl.BlockSpec((1, 2 * HD_D), lambda b, h: (0, 0)),
                  new_spec, new_spec, new_spec, cache_spec, cache_spec],
        out_specs=new_spec,
        out_shape=jax.ShapeDtypeStruct((r, D_D), BF16),
        compiler_params=_params(2),
        name="diff_sample",
    )(lp, nw, q, k, v, kc, vc)


def _odd_mix_kernel(qd_ref, kd_ref, el_ref, cv_ref, cg_ref, yd_ref, x_ref, s0_ref, gw_ref, wo_ref,
                    o_ref, s_ref, state, og, ycat, *, tm, blk, tiles_per_seq, per_block_state):
    i = pl.program_id(0)
    if not per_block_state:
        @pl.when(i % tiles_per_seq == 0)
        def _():
            state[...] = jnp.zeros_like(state)

    tri = lax.broadcasted_iota(jnp.int32, (blk, blk), 1) <= lax.broadcasted_iota(jnp.int32, (blk, blk), 0)
    for n in range(tm // blk):
        r0 = n * blk
        el = el_ref[n:n + 1, :]
        for h in range(H_C):
            if per_block_state:
                st = s0_ref[n, h]
            else:
                st = state[h]
            qd = qd_ref[r0:r0 + blk, h * DK_C:(h + 1) * DK_C]
            kd = kd_ref[r0:r0 + blk, h * DK_C:(h + 1) * DK_C]
            vh = cv_ref[r0:r0 + blk, h * DV_C:(h + 1) * DV_C]
            att = jnp.where(tri, _dot_nt(qd, kd), 0.0).astype(BF16)
            og[r0:r0 + blk, h * DV_C:(h + 1) * DV_C] = _dot(att, vh) + _dot_nt(qd, st.astype(BF16))
            st = el[:, h * DK_C:(h + 1) * DK_C] * (st + _dot_tn(vh, kd))
            if per_block_state:
                s_ref[n, h] = st
            else:
                state[h] = st
    if not per_block_state:
        s_ref[0] = state[...]

    gw = gw_ref[...]
    for h in range(H_C):
        lo, hi = h * DV_C, (h + 1) * DV_C
        g = cg_ref[:, lo:hi]
        y = _rms(og[:, lo:hi], gw) * (g * (1.0 / (1.0 + jnp.exp(-g))))
        ycat[:, lo:hi] = y.astype(BF16)
    ycat[:, D_C:] = yd_ref[...]
    o_ref[...] = x_ref[...] + _dot(ycat[...], wo_ref[...])


def _odd_mix(qd, kd, el, cv, cg, yd, x, s0, gw, wo, tm, blk, seq, per_block_state):
    r = x.shape[0]
    nb = tm // blk
    tiles_per_seq = max(seq // tm, 1)
    kern = functools.partial(_odd_mix_kernel, tm=tm, blk=blk, tiles_per_seq=tiles_per_seq,
                             per_block_state=per_block_state)
    if per_block_state:
        n_state = r // blk
        state_spec = pl.BlockSpec((nb, H_C, DV_C, DK_C), lambda i: (i, 0, 0, 0))
    else:
        n_state = r // seq
        state_spec = pl.BlockSpec((1, H_C, DV_C, DK_C), lambda i: (i // tiles_per_seq, 0, 0, 0))
    return pl.pallas_call(
        kern,
        grid=(r // tm,),
        in_specs=[_rows(tm, D_CQK), _rows(tm, D_CQK), _rows(nb, D_CQK), _rows(tm, D_C), _rows(tm, D_C),
                  _rows(tm, D_D), _rows(tm, D_MODEL), state_spec, _full((1, DV_C)), _full((D_C + D_D, D_MODEL))],
        out_specs=[_rows(tm, D_MODEL), state_spec],
        out_shape=[jax.ShapeDtypeStruct((r, D_MODEL), F32),
                   jax.ShapeDtypeStruct((n_state, H_C, DV_C, DK_C), F32)],
        scratch_shapes=[pltpu.VMEM((H_C, DV_C, DK_C), F32), pltpu.VMEM((tm, D_C), F32),
                        pltpu.VMEM((tm, D_C + D_D), BF16)],
        compiler_params=_params(),
        name="odd_mix",
    )(qd, kd, el, cv, cg, yd, x, s0, gw, wo)


def _gelu_tanh(x):
    return 0.5 * x * (1.0 + jnp.tanh(math.sqrt(2.0 / math.pi) * (x + 0.044715 * (x * x * x))))


def _ffn_kernel(x_ref, nw_ref, wup_ref, cw_ref, cb_ref, wdn_ref, prev_ref, fnw_ref, o_ref, tail_ref, carry, act,
                *, tm, seg, tiles_per_seq, prev_from_input, final_norm):
    i = pl.program_id(0)
    x = x_ref[...]
    h = _rms(x, nw_ref[...]).astype(BF16)
    nseg = tm // seg
    row = lax.broadcasted_iota(jnp.int32, (seg, 1), 0)
    if not prev_from_input:
        @pl.when(i % tiles_per_seq == 0)
        def _():
            carry[...] = jnp.zeros_like(carry)

    def up(c0):
        return _dot(h, wup_ref[:, c0:c0 + FFN_COLS]), _dot(h, wup_ref[:, D_FF + c0:D_FF + c0 + FFN_COLS])

    nxt = up(0)
    for c0 in range(0, D_FF, FFN_COLS):
        c1 = c0 + FFN_COLS
        gate, val = nxt
        if c1 < D_FF:
            nxt = up(c1)
        w0, w1, w2 = cw_ref[0:1, c0:c1], cw_ref[1:2, c0:c1], cw_ref[2:3, c0:c1]
        convs = []
        for s in range(nseg):
            g = gate[s * seg:(s + 1) * seg]
            if prev_from_input:
                p0, p1 = prev_ref[s, 0:1, c0:c1], prev_ref[s, 1:2, c0:c1]
            else:
                p0 = carry[CONV_TAIL - 2:CONV_TAIL - 1, c0:c1]
                p1 = carry[CONV_TAIL - 1:CONV_TAIL, c0:c1]
            g1 = jnp.where(row == 0, p1, pltpu.roll(g, 1, 0))
            g2 = jnp.where(row == 0, p0, jnp.where(row == 1, p1, pltpu.roll(g, 2, 0)))
            convs.append(cb_ref[:, c0:c1] + g2 * w0 + g1 * w1 + g * w2)
            tail_ref[s, :, c0:c1] = g[seg - CONV_TAIL:]
        conv = convs[0] if nseg == 1 else jnp.concatenate(convs, axis=0)
        if not prev_from_input:
            carry[:, c0:c1] = gate[tm - CONV_TAIL:]
        act[:, c0:c1] = (_gelu_tanh(conv) * val).astype(BF16)
    y = x + _dot(act[...], wdn_ref[...])
    if final_norm:
        y = _rms(y, fnw_ref[...])
    o_ref[...] = y


def _ffn(x, nw, wup, cw, cb, wdn, prev, fnw, tm, seg, seq, prev_from_input, final_norm):
    r = x.shape[0]
    nseg = tm // seg
    kern = functools.partial(_ffn_kernel, tm=tm, seg=seg, tiles_per_seq=max(seq // tm, 1),
                             prev_from_input=prev_from_input, final_norm=final_norm)
    prev_spec = pl.BlockSpec((nseg, CONV_W - 1, D_FF), lambda i: (i, 0, 0))
    tail_spec = pl.BlockSpec((nseg, CONV_TAIL, D_FF), lambda i: (i, 0, 0))
    return pl.pallas_call(
        kern,
        grid=(r // tm,),
        in_specs=[_rows(tm, D_MODEL), _full((1, D_MODEL)), _full((D_MODEL, 2 * D_FF)), _full((CONV_W, D_FF)),
                  _full((1, D_FF)), _full((D_FF, D_MODEL)), prev_spec, _full((1, D_MODEL))],
        out_specs=[_rows(tm, D_MODEL), tail_spec],
        out_shape=[jax.ShapeDtypeStruct((r, D_MODEL), F32),
                   jax.ShapeDtypeStruct((r // seg, CONV_TAIL, D_FF), F32)],
        scratch_shapes=[pltpu.VMEM((CONV_TAIL, D_FF), F32), pltpu.VMEM((tm, D_FF), BF16)],
        compiler_params=_params(),
        name="ffn",
    )(x, nw, wup, cw, cb, wdn, prev, fnw)


def _pad_odd_w_in(w):
    cr = w[:, ODD_CR0:ODD_CR0 + GATE_RANK]
    return jnp.concatenate([w[:, :ODD_CR0], cr, jnp.zeros((D_MODEL, LANES - GATE_RANK), w.dtype),
                            w[:, ODD_CR0 + GATE_RANK:]], axis=1).astype(BF16)


def _trunk(x3, pos, prompt, caches, weights):
    (norm_mix_w, norm_ffn_w, final_norm_w, even_w_in, pool_w, pool_scale, swa_sinks, even_w_out, odd_w_in,
     gla_w_gate2, gla_b_gate, gla_norm_w, diff_lambda, diff_norm_w, odd_w_out, ffn_w_up, ffn_conv_w, ffn_conv_b,
     ffn_w_down) = weights
    cache_pool, cache_swa_k, cache_swa_v, state_gla, cache_diff_k, cache_diff_v, cache_ffn_conv = caches
    b, l, _ = x3.shape
    r = b * l
    depth = norm_mix_w.shape[0]
    x = x3.reshape(r, D_MODEL)
    if prompt:
        tm = min(ROW_TILE, l)
        blk = min(CHUNK, l)
        tabs = _rope_tables(pos)
    else:
        tm = r
        blk = min(CHUNK, l)
        tabs = tuple(jnp.tile(t, (b, 1)) for t in _rope_tables(pos))
    pos0 = 0 if prompt else cache_diff_k.shape[2]

    pools, swa_ks, swa_vs, glas, diff_ks, diff_vs, ffns = [], [], [], [], [], [], []
    for layer in range(depth):
        p = layer // 2
        nw = norm_mix_w[layer].reshape(1, D_MODEL)
        if layer % 2 == 0:
            u, q, k, v = _even_in(x, nw, even_w_in[p].astype(BF16), tabs, tm)
            wp = pool_w[p].astype(BF16)
            ps = pool_scale[p].reshape(1, D_A)
            wo = even_w_out[p].astype(BF16)
            if prompt:
                x = _even_mix_prompt(swa_sinks[p], u, q, k, v, x, wp, ps, wo, l, tm)
                pools.append(u.reshape(b, l, D_A)[:, l - POOL_STATE:])
                swa_ks.append(k.reshape(b, l, HKV_B, HEAD_DIM)[:, l - WINDOW:])
                swa_vs.append(v.reshape(b, l, HKV_B, HEAD_DIM)[:, l - WINDOW:])
            else:
                rows_c = cache_swa_k.shape[2]
                pool_prev = jnp.pad(cache_pool[p], ((0, 0), (1, 0), (0, 0)))
                kc = cache_swa_k[p].reshape(b, rows_c, D_KV)
                vc = cache_swa_v[p].reshape(b, rows_c, D_KV)
                x = _even_mix_sample(swa_sinks[p], u, q, k, v, x, pool_prev, kc, vc, wp, ps, wo, l, pos0)
                pools.append(jnp.concatenate([cache_pool[p], u.reshape(b, l, D_A)], axis=1)[:, -POOL_STATE:])
                swa_ks.append(jnp.concatenate([cache_swa_k[p], k.reshape(b, l, HKV_B, HEAD_DIM)], axis=1)[:, -rows_c:])
                swa_vs.append(jnp.concatenate([cache_swa_v[p], v.reshape(b, l, HKV_B, HEAD_DIM)], axis=1)[:, -rows_c:])
        else:
            lambda_init = 0.8 - 0.6 * math.exp(-0.3 * layer)
            wg = jnp.pad(gla_w_gate2[p], ((0, LANES - GATE_RANK), (0, 0))).astype(BF16)
            bg = gla_b_gate[p].reshape(1, D_CQK)
            qd, kd, el, cv, cg, dq, dk, dv = _odd_in(x, nw, _pad_odd_w_in(odd_w_in[p]), wg, bg, tabs, tm, blk)
            lp = diff_lambda[p]
            dnw = diff_norm_w[p].reshape(1, 2 * HD_D)
            if prompt:
                yd = _diff_prompt(lp, dnw, dq, dk, dv, b, l, lambda_init, min(ATTN_TILE, l))
                s0 = jnp.zeros((b, H_C, DV_C, DK_C), F32)
            else:
                past = cache_diff_k.shape[2]
                kc = cache_diff_k[p].reshape(b * past, D_D)
                vc = cache_diff_v[p].reshape(b * past, D_D)
                yd = _diff_sample(lp, dnw, dq, dk, dv, kc, vc, b, l, lambda_init)
                s0 = jnp.swapaxes(state_gla[p], -1, -2).astype(F32)
            x, s_new = _odd_mix(qd, kd, el, cv, cg, yd, x, s0, gla_norm_w[p].reshape(1, DV_C),
                                odd_w_out[p].astype(BF16), tm, blk, l, not prompt)
            glas.append(jnp.swapaxes(s_new, -1, -2))
            diff_ks.append(dk.reshape(b, l, H_D, 2, HD_D))
            diff_vs.append(dv.reshape(b, l, H_D, 2 * HD_D))
        if prompt:
            seg = tm
            prev = jnp.zeros((r // tm, CONV_W - 1, D_FF), F32)
        else:
            seg = l
            prev = cache_ffn_conv[layer]
        x, tails = _ffn(x, norm_ffn_w[layer].reshape(1, D_MODEL), ffn_w_up[layer].astype(BF16), ffn_conv_w[layer],
                        ffn_conv_b[layer].reshape(1, D_FF), ffn_w_down[layer].astype(BF16), prev,
                        final_norm_w.reshape(1, D_MODEL), tm, seg, l, not prompt, layer == depth - 1)
        tails = tails.reshape(b, -1, CONV_TAIL, D_FF)
        ffns.append(tails[:, -1, CONV_TAIL - (CONV_W - 1):])
    return (x.reshape(b, l, D_MODEL), jnp.stack(pools), jnp.stack(swa_ks), jnp.stack(swa_vs), jnp.stack(glas),
            jnp.stack(diff_ks), jnp.stack(diff_vs), jnp.stack(ffns))


def kernel(x_prompt, x_sample, cache_pool, cache_swa_k, cache_swa_v, state_gla, cache_diff_k, cache_diff_v, cache_ffn_conv, norm_mix_w, norm_ffn_w, final_norm_w, even_w_in, pool_w, pool_scale, swa_sinks, even_w_out, odd_w_in, gla_w_gate2, gla_b_gate, gla_norm_w, diff_lambda, diff_norm_w, odd_w_out, ffn_w_up, ffn_conv_w, ffn_conv_b, ffn_w_down):
    weights = (norm_mix_w, norm_ffn_w, final_norm_w, even_w_in, pool_w, pool_scale, swa_sinks, even_w_out, odd_w_in,
               gla_w_gate2, gla_b_gate, gla_norm_w, diff_lambda, diff_norm_w, odd_w_out, ffn_w_up, ffn_conv_w,
               ffn_conv_b, ffn_w_down)
    caches = (cache_pool, cache_swa_k, cache_swa_v, state_gla, cache_diff_k, cache_diff_v, cache_ffn_conv)
    past = cache_diff_k.shape[2]
    pos_p = jnp.arange(x_prompt.shape[1])
    pos_s = past + jnp.arange(x_sample.shape[1])
    outs_p = _trunk(x_prompt, pos_p, True, caches, weights)
    outs_s = _trunk(x_sample, pos_s, False, caches, weights)
    return (outs_p[0], outs_s[0]) + tuple(outs_p[1:]) + tuple(outs_s[1:])
```

```python
import functools
import math

import jax
import jax.numpy as jnp
from jax import lax
from jax.experimental import pallas as pl
from jax.experimental.pallas import tpu as pltpu

F32 = jnp.float32
BF16 = jnp.bfloat16

D_MODEL = 1024
CHUNK = 64
EPS = 1e-6
NEG_INF = -1e30
LOG2E = 1.0 / math.log(2.0)
ROPE_THETA = 500000.0
HEAD_DIM = 64
ROT_DIM = HEAD_DIM // 4
ROT_HALF = ROT_DIM // 2
LANES = 128

POOL_WINDOWS = (2, 4, 8, 16)
POOL_GROUP = D_MODEL // 8
D_A = len(POOL_WINDOWS) * POOL_GROUP
POOL_STATE = max(POOL_WINDOWS) - 1
POOL_HALO = POOL_STATE + 1

H_B = D_MODEL // 128
HKV_B = 2
GQA_B = H_B // HKV_B
WINDOW = 128
D_B = H_B * HEAD_DIM
D_KV = HKV_B * HEAD_DIM

H_C = 4
DK_C = 64
DV_C = 128
GATE_RANK = 16
GATE_NORM = 16.0
D_CQK = H_C * DK_C
D_C = H_C * DV_C

H_D = 4
HD_D = 64
D_D = H_D * 2 * HD_D

D_FF = 2816
CONV_W = 3
CONV_TAIL = 8

ROW_TILE = 512
ATTN_TILE = 1024
FFN_COLS = 256
VMEM_LIMIT = 60 * 1024 * 1024


def _dot(a, b):
    return jnp.dot(a, b, preferred_element_type=F32)


def _dot_nt(a, b):
    return lax.dot_general(a, b, (((1,), (1,)), ((), ())), preferred_element_type=F32)


def _dot_tn(a, b):
    return lax.dot_general(a, b, (((0,), (0,)), ((), ())), preferred_element_type=F32)


def _rms(x, w):
    return x * lax.rsqrt(jnp.mean(x * x, axis=-1, keepdims=True) + EPS) * w


def _rope(z, cos_t, sin_a, sin_b):
    outs = []
    for s in range(z.shape[1] // LANES):
        x = z[:, s * LANES:(s + 1) * LANES]
        outs.append(x * cos_t + pltpu.roll(x, LANES - ROT_HALF, 1) * sin_a + pltpu.roll(x, ROT_HALF, 1) * sin_b)
    return outs[0] if len(outs) == 1 else jnp.concatenate(outs, axis=-1)


def _rope_tables(pos):
    t = pos.shape[0]
    inv = jnp.exp(-math.log(ROPE_THETA) * jnp.arange(ROT_HALF, dtype=F32) * (2.0 / ROT_DIM))
    ang = pos.astype(F32)[:, None] * inv[None, :]
    cos, sin = jnp.cos(ang), jnp.sin(ang)
    rest = HEAD_DIM - ROT_DIM
    cos_t = jnp.concatenate([cos, cos, jnp.ones((t, rest), F32)], axis=-1)
    sin_a = jnp.concatenate([-sin, jnp.zeros((t, HEAD_DIM - ROT_HALF), F32)], axis=-1)
    sin_b = jnp.concatenate([jnp.zeros((t, ROT_HALF), F32), sin, jnp.zeros((t, rest), F32)], axis=-1)
    rep = LANES // HEAD_DIM
    return tuple(jnp.tile(a, (1, rep)) for a in (cos_t, sin_a, sin_b))


def _params(n_axes=1):
    return pltpu.CompilerParams(dimension_semantics=("arbitrary",) * n_axes, vmem_limit_bytes=VMEM_LIMIT)


def _full(shape):
    nd = len(shape)
    return pl.BlockSpec(shape, lambda *_: (0,) * nd)


def _rows(tm, cols):
    return pl.BlockSpec((tm, cols), lambda i: (i, 0))


def _even_in_kernel(x_ref, nw_ref, w_ref, cos_ref, sa_ref, sb_ref, u_ref, q_ref, k_ref, v_ref):
    h = _rms(x_ref[...], nw_ref[...]).astype(BF16)
    cos_t, sin_a, sin_b = cos_ref[...], sa_ref[...], sb_ref[...]
    u_ref[...] = _dot(h, w_ref[:, 0:D_A])
    q = _dot(h, w_ref[:, D_A:D_A + D_B])
    q_ref[...] = (_rope(q, cos_t, sin_a, sin_b) * (HEAD_DIM ** -0.5)).astype(BF16)
    k = _dot(h, w_ref[:, D_A + D_B:D_A + D_B + D_KV])
    k_ref[...] = _rope(k, cos_t, sin_a, sin_b)
    v_ref[...] = _dot(h, w_ref[:, D_A + D_B + D_KV:D_A + D_B + 2 * D_KV])


def _even_in(x, nw, w, tabs, tm):
    r = x.shape[0]
    t_tiles = tabs[0].shape[0] // tm
    tab_spec = pl.BlockSpec((tm, LANES), lambda i: (i % t_tiles, 0))
    n_in = D_A + D_B + 2 * D_KV
    return pl.pallas_call(
        _even_in_kernel,
        grid=(r // tm,),
        in_specs=[_rows(tm, D_MODEL), _full((1, D_MODEL)), _full((D_MODEL, n_in)), tab_spec, tab_spec, tab_spec],
        out_specs=[_rows(tm, D_A), _rows(tm, D_B), _rows(tm, D_KV), _rows(tm, D_KV)],
        out_shape=[jax.ShapeDtypeStruct((r, D_A), F32), jax.ShapeDtypeStruct((r, D_B), BF16),
                   jax.ShapeDtypeStruct((r, D_KV), F32), jax.ShapeDtypeStruct((r, D_KV), F32)],
        compiler_params=_params(),
        name="even_in",
    )(x, nw, w, *tabs)


def _pool_mix(ext, u, pos, wp_ref, ps, ycat_ref, row0, rows):
    for g, w in enumerate(POOL_WINDOWS):
        lo, hi = g * POOL_GROUP, (g + 1) * POOL_GROUP
        s = ext[:, lo:hi]
        sh = 1
        while sh < w:
            s = s + pltpu.roll(s, sh, 0)
            sh *= 2
        count = jnp.minimum(w, pos + 1).astype(F32)
        diff = s[POOL_HALO:] / count - u[:, lo:hi]
        ya = _dot(diff.astype(BF16), wp_ref[g]) * ps[:, lo:hi]
        ycat_ref[row0:row0 + rows, lo:hi] = ya.astype(BF16)


def _sink_column(sink_ref, hk, rows_per_head):
    n = GQA_B * rows_per_head
    grp = lax.broadcasted_iota(jnp.int32, (n, 1), 0) // rows_per_head
    col = jnp.full((n, 1), sink_ref[hk * GQA_B], F32)
    for g in range(1, GQA_B):
        col = jnp.where(grp == g, sink_ref[hk * GQA_B + g], col)
    return col


def _sink_attend(s, sink_col, vb):
    m = jnp.maximum(jnp.max(s, axis=-1, keepdims=True), sink_col)
    e = jnp.exp(s - m)
    den = jnp.sum(e, axis=-1, keepdims=True) + jnp.exp(sink_col - m)
    return _dot(e.astype(BF16), vb) / den


def _store_heads(ycat_ref, o, row0, rows, hk):
    for pair in range(GQA_B // 2):
        both = jnp.concatenate([o[(2 * pair) * rows:(2 * pair + 1) * rows],
                                o[(2 * pair + 1) * rows:(2 * pair + 2) * rows]], axis=-1)
        c0 = D_A + (hk * GQA_B + 2 * pair) * HEAD_DIM
        ycat_ref[row0:row0 + rows, c0:c0 + 2 * HEAD_DIM] = both.astype(BF16)


def _even_mix_prompt_kernel(sink_ref, u_ref, q_ref, k_ref, v_ref, x_ref, wp_ref, ps_ref, wo_ref, o_ref,
                            ucar, kcar, vcar, ycat, *, tm, tiles_per_seq):
    i = pl.program_id(0)
    ti = i % tiles_per_seq
    first_key_chunk = jnp.where(ti == 0, WINDOW // CHUNK, 0)

    @pl.when(ti == 0)
    def _():
        ucar[...] = jnp.zeros_like(ucar)
        kcar[...] = jnp.zeros_like(kcar)
        vcar[...] = jnp.zeros_like(vcar)

    u = u_ref[...]
    ext = jnp.concatenate([ucar[...], u], axis=0)
    pos = ti * tm + lax.broadcasted_iota(jnp.int32, (tm, 1), 0)
    _pool_mix(ext, u, pos, wp_ref, ps_ref[...], ycat, 0, tm)
    ucar[...] = u[tm - POOL_HALO:]

    kext = jnp.concatenate([kcar[...], k_ref[...].astype(BF16)], axis=0)
    vext = jnp.concatenate([vcar[...], v_ref[...].astype(BF16)], axis=0)
    qb = 2 * CHUNK
    kb = qb + WINDOW
    qc = (lax.broadcasted_iota(jnp.int32, (GQA_B * qb, kb), 0) % qb) // CHUNK
    kc = lax.broadcasted_iota(jnp.int32, (GQA_B * qb, kb), 1) // CHUNK
    band = (kc >= qc) & (kc <= qc + WINDOW // CHUNK)
    for a in range(tm // qb):
        mask = band
        if a == 0:
            mask = band & (kc >= first_key_chunk)
        for hk in range(HKV_B):
            ks = kext[a * qb:a * qb + kb, hk * HEAD_DIM:(hk + 1) * HEAD_DIM]
            vs = vext[a * qb:a * qb + kb, hk * HEAD_DIM:(hk + 1) * HEAD_DIM]
            qs = jnp.concatenate(
                [q_ref[a * qb:(a + 1) * qb, (hk * GQA_B + g) * HEAD_DIM:(hk * GQA_B + g + 1) * HEAD_DIM]
                 for g in range(GQA_B)], axis=0)
            s = jnp.where(mask, _dot_nt(qs, ks), NEG_INF)
            o = _sink_attend(s, _sink_column(sink_ref, hk, qb), vs)
            _store_heads(ycat, o, a * qb, qb, hk)
    kcar[...] = kext[tm:]
    vcar[...] = vext[tm:]
    o_ref[...] = x_ref[...] + _dot(ycat[...], wo_ref[...])


def _even_mix_prompt(sinks, u, q, k, v, x, wp, ps, wo, seq, tm):
    r = x.shape[0]
    kern = functools.partial(_even_mix_prompt_kernel, tm=tm, tiles_per_seq=seq // tm)
    return pl.pallas_call(
        kern,
        grid=(r // tm,),
        in_specs=[pl.BlockSpec(memory_space=pltpu.SMEM),
                  _rows(tm, D_A), _rows(tm, D_B), _rows(tm, D_KV), _rows(tm, D_KV), _rows(tm, D_MODEL),
                  _full(wp.shape), _full((1, D_A)), _full((D_A + D_B, D_MODEL))],
        out_specs=_rows(tm, D_MODEL),
        out_shape=jax.ShapeDtypeStruct((r, D_MODEL), F32),
        scratch_shapes=[pltpu.VMEM((POOL_HALO, D_A), F32), pltpu.VMEM((WINDOW, D_KV), BF16),
                        pltpu.VMEM((WINDOW, D_KV), BF16), pltpu.VMEM((tm, D_A + D_B), BF16)],
        compiler_params=_params(),
        name="even_mix_prompt",
    )(sinks, u, q, k, v, x, wp, ps, wo)


def _even_mix_sample_kernel(sink_ref, u_ref, q_ref, k_ref, v_ref, x_ref, pp_ref, kc_ref, vc_ref, wp_ref, ps_ref,
                            wo_ref, o_ref, ycat, *, l, pos0):
    u = u_ref[...]
    ext = jnp.concatenate([pp_ref[0], u], axis=0)
    pos = pos0 + lax.broadcasted_iota(jnp.int32, (l, 1), 0)
    _pool_mix(ext, u, pos, wp_ref, ps_ref[...], ycat, 0, l)

    pad = jnp.zeros((LANES - l, D_KV), BF16)
    kall = jnp.concatenate([kc_ref[0].astype(BF16), k_ref[...].astype(BF16), pad], axis=0)
    vall = jnp.concatenate([vc_ref[0].astype(BF16), v_ref[...].astype(BF16), pad], axis=0)
    nk = kall.shape[0]
    valid = lax.broadcasted_iota(jnp.int32, (GQA_B * l, nk), 1) < nk - (LANES - l)
    for hk in range(HKV_B):
        ks = kall[:, hk * HEAD_DIM:(hk + 1) * HEAD_DIM]
        vs = vall[:, hk * HEAD_DIM:(hk + 1) * HEAD_DIM]
        qs = jnp.concatenate(
            [q_ref[:, (hk * GQA_B + g) * HEAD_DIM:(hk * GQA_B + g + 1) * HEAD_DIM] for g in range(GQA_B)], axis=0)
        s = jnp.where(valid, _dot_nt(qs, ks), NEG_INF)
        o = _sink_attend(s, _sink_column(sink_ref, hk, l), vs)
        _store_heads(ycat, o, 0, l, hk)
    o_ref[...] = x_ref[...] + _dot(ycat[...], wo_ref[...])


def _even_mix_sample(sinks, u, q, k, v, x, pool_prev, kc, vc, wp, ps, wo, l, pos0):
    r = x.shape[0]
    rows_c = kc.shape[1]
    kern = functools.partial(_even_mix_sample_kernel, l=l, pos0=pos0)
    cache_spec = pl.BlockSpec((1, rows_c, D_KV), lambda i: (i, 0, 0))
    return pl.pallas_call(
        kern,
        grid=(r // l,),
        in_specs=[pl.BlockSpec(memory_space=pltpu.SMEM),
                  _rows(l, D_A), _rows(l, D_B), _rows(l, D_KV), _rows(l, D_KV), _rows(l, D_MODEL),
                  pl.BlockSpec((1, POOL_HALO, D_A), lambda i: (i, 0, 0)), cache_spec, cache_spec,
                  _full(wp.shape), _full((1, D_A)), _full((D_A + D_B, D_MODEL))],
        out_specs=_rows(l, D_MODEL),
        out_shape=jax.ShapeDtypeStruct((r, D_MODEL), F32),
        scratch_shapes=[pltpu.VMEM((l, D_A + D_B), BF16)],
        compiler_params=_params(),
        name="even_mix_sample",
    )(sinks, u, q, k, v, x, pool_prev, kc, vc, wp, ps, wo)


ODD_CR0 = 2 * D_CQK + 2 * D_C
ODD_DQ0 = ODD_CR0 + LANES
ODD_IN_PADDED = ODD_DQ0 + 3 * D_D


def _odd_in_kernel(x_ref, nw_ref, w_ref, wg_ref, bg_ref, cos_ref, sa_ref, sb_ref, *rest, tm, blk, cache_layout,
                   aliased):
    if aliased:
        rest = rest[2:]
    qd_ref, kd_ref, el_ref, cv_ref, cg_ref, dq_ref, dk_ref, dv_ref, cum_scr = rest
    h = _rms(x_ref[...], nw_ref[...]).astype(BF16)
    cos_t, sin_a, sin_b = cos_ref[...], sa_ref[...], sb_ref[...]
    cq = _dot(h, w_ref[:, 0:D_CQK])
    ck = _dot(h, w_ref[:, D_CQK:2 * D_CQK])
    cv_ref[...] = _dot(h, w_ref[:, 2 * D_CQK:2 * D_CQK + D_C]).astype(BF16)
    cg_ref[...] = _dot(h, w_ref[:, 2 * D_CQK + D_C:ODD_CR0])
    cr = _dot(h, w_ref[:, ODD_CR0:ODD_DQ0]).astype(BF16)
    gate = _dot(cr, wg_ref[...]) + bg_ref[...]
    cum = (jnp.minimum(gate, 0.0) - jnp.log1p(jnp.exp(-jnp.abs(gate)))) * (1.0 / GATE_NORM)
    rowb = lax.broadcasted_iota(jnp.int32, (tm, 1), 0) % blk
    sh = 1
    while sh < blk:
        cum = cum + jnp.where(rowb >= sh, pltpu.roll(cum, sh, 0), 0.0)
        sh *= 2
    for s in range(D_CQK // LANES):
        cum_scr[s] = cum[:, s * LANES:(s + 1) * LANES]
        el_ref[:, s * LANES:(s + 1) * LANES] = jnp.exp(cum_scr[s, pl.ds(blk - 1, tm // blk, stride=blk), :])
    qd_ref[...] = (cq * (DK_C ** -0.5) * jnp.exp(cum)).astype(BF16)
    kd_ref[...] = (ck * jnp.exp(-cum)).astype(BF16)
    dq = _dot(h, w_ref[:, ODD_DQ0:ODD_DQ0 + D_D])
    dq_ref[...] = (_rope(dq, cos_t, sin_a, sin_b) * (HD_D ** -0.5 * LOG2E)).astype(BF16)
    dk = _rope(_dot(h, w_ref[:, ODD_DQ0 + D_D:ODD_DQ0 + 2 * D_D]), cos_t, sin_a, sin_b)
    dv = _dot(h, w_ref[:, ODD_DQ0 + 2 * D_D:ODD_DQ0 + 3 * D_D])
    if cache_layout:
        for hd in range(H_D):
            lo, hi = hd * 2 * HD_D, (hd + 1) * 2 * HD_D
            dk_ref[0, 0, hd] = dk[:, lo:hi].T
            dv_ref[0, pl.ds(hd, tm, stride=H_D), :] = dv[:, lo:hi]
    else:
        dk_ref[...] = dk
        dv_ref[...] = dv


def _odd_in(x, nw, w, wg, bg, tabs, tm, blk, pair=None):
    r = x.shape[0]
    t_tiles = tabs[0].shape[0] // tm
    tab_spec = pl.BlockSpec((tm, LANES), lambda i: (i % t_tiles, 0))
    nb = tm // blk
    sds = jax.ShapeDtypeStruct
    in_specs = [_rows(tm, D_MODEL), _full((1, D_MODEL)), _full((D_MODEL, ODD_IN_PADDED)),
                _full((LANES, D_CQK)), _full((1, D_CQK)), tab_spec, tab_spec, tab_spec]
    args = [x, nw, w, wg, bg, *tabs]
    aliases = {}
    if pair is None:
        kv_specs = [_rows(tm, D_D), _rows(tm, D_D)]
        kv_shapes = [sds((r, D_D), F32), sds((r, D_D), F32)]
    else:
        p, n_pairs, batch, seq, bufs = pair
        tps = seq // tm
        kv_specs = [pl.BlockSpec((1, 1, H_D, 2 * HD_D, tm), lambda i: (p, i // tps, 0, 0, i % tps)),
                    pl.BlockSpec((1, tm * H_D, 2 * HD_D), lambda i: (p, i, 0))]
        kv_shapes = [sds((n_pairs, batch, H_D, 2 * HD_D, seq), F32), sds((n_pairs, r * H_D, 2 * HD_D), F32)]
        if bufs is not None:
            in_specs += [pl.BlockSpec(memory_space=pl.ANY)] * 2
            args += list(bufs)
            aliases = {len(args) - 2: 6, len(args) - 1: 7}
    kern = functools.partial(_odd_in_kernel, tm=tm, blk=blk, cache_layout=pair is not None, aliased=bool(aliases))
    return pl.pallas_call(
        kern,
        grid=(r // tm,),
        in_specs=in_specs,
        out_specs=[_rows(tm, D_CQK), _rows(tm, D_CQK), _rows(nb, D_CQK), _rows(tm, D_C), _rows(tm, D_C),
                   _rows(tm, D_D)] + kv_specs,
        out_shape=[sds((r, D_CQK), BF16), sds((r, D_CQK), BF16), sds((r // blk, D_CQK), F32),
                   sds((r, D_C), BF16), sds((r, D_C), F32), sds((r, D_D), BF16)] + kv_shapes,
        scratch_shapes=[pltpu.VMEM((D_CQK // LANES, tm, LANES), F32)],
        input_output_aliases=aliases,
        compiler_params=_params(),
        name="odd_in",
    )(*args)


def _lambda_value(lp_ref, lambda_init):
    lp = lp_ref[...]
    s1 = jnp.sum(lp[0:1] * lp[1:2], axis=-1, keepdims=True)
    s2 = jnp.sum(lp[2:3] * lp[3:4], axis=-1, keepdims=True)
    return jnp.exp(s1) - jnp.exp(s2) + lambda_init


def _diff_finish(acc, den, lam, nw, lambda_init):
    o = acc[0] / den[0] - lam * (acc[1] / den[1])
    return (_rms(o, nw) * (1.0 - lambda_init)).astype(BF16)


def _lane_tile(x, n):
    return x if n == 1 else jnp.concatenate([x] * n, axis=1)


def _diff_prompt_kernel(lp_ref, nw_ref, q_ref, k_ref, v_ref, o_ref, kbf, vbf, m_scr, acc_scr, s_a, s_b, bias,
                        *, tq, lambda_init):
    hd = pl.program_id(1)
    qi = pl.program_id(2)
    dv = 2 * HD_D
    seq = vbf.shape[0]

    @pl.when(qi == 0)
    def _():
        kbf[...] = k_ref[0, 0, 0].astype(BF16)
        vbf[:, :dv] = v_ref[0, pl.ds(hd, seq, stride=H_D), :].astype(BF16)
        vbf[:, dv:] = jnp.ones((seq, dv), BF16)

    tk = tq // 2

    @pl.when((pl.program_id(0) == 0) & (pl.program_id(1) == 0) & (qi == 0))
    def _():
        qc = lax.broadcasted_iota(jnp.int32, (tk, tk), 0) // CHUNK
        kc = lax.broadcasted_iota(jnp.int32, (tk, tk), 1) // CHUNK
        bias[...] = jnp.where(kc <= qc, 0.0, NEG_INF)

    m_scr[...] = jnp.full(m_scr.shape, NEG_INF, F32)
    acc_scr[...] = jnp.zeros(acc_scr.shape, F32)

    def scores(t, s_buf, row0=0):
        start = pl.multiple_of(t * tk, tk)
        for c in range(2):
            s_buf[c, row0:, :] = _dot(q_ref[row0:, c * HD_D:(c + 1) * HD_D],
                                      kbf[c * HD_D:(c + 1) * HD_D, pl.ds(start, tk)])

    def attend(t, s_buf, row0=0, rows=tq, masked=False):
        vs = vbf[pl.ds(pl.multiple_of(t * tk, tk), tk), :]
        for c in range(2):
            sc = s_buf[c, row0:row0 + rows, :]
            if masked:
                sc = sc + bias[...]
            m_old = m_scr[c, row0:row0 + rows, :]
            m_new = jnp.maximum(m_old, jnp.max(sc, axis=-1, keepdims=True))
            p = jnp.exp2(sc - _lane_tile(m_new, tk // LANES)).astype(BF16)
            alpha = _lane_tile(jnp.exp2(m_old - m_new), 2 * dv // LANES)
            acc_scr[c, row0:row0 + rows, :] = alpha * acc_scr[c, row0:row0 + rows, :] + _dot(p, vs)
            m_scr[c, row0:row0 + rows, :] = m_new

    scores(0, s_a)

    def pair(j, _):
        t = 2 * j
        scores(t + 1, s_b)
        attend(t, s_a)
        scores(t + 2, s_a)
        attend(t + 1, s_b)
        return 0

    lax.fori_loop(0, qi, pair, 0)
    scores(2 * qi + 1, s_b, row0=tk)
    attend(2 * qi, s_a, row0=0, rows=tk, masked=True)
    attend(2 * qi, s_a, row0=tk, rows=tk)
    attend(2 * qi + 1, s_b, row0=tk, rows=tk, masked=True)

    lam = _lambda_value(lp_ref, lambda_init)
    acc = [acc_scr[c] for c in range(2)]
    o_ref[...] = _diff_finish([a[:, :dv] for a in acc], [a[:, dv:] for a in acc], lam, nw_ref[...], lambda_init)


def _diff_prompt(lp, nw, q, kt, v, p, batch, seq, lambda_init, tq):
    r = q.shape[0]
    nq = seq // tq
    kern = functools.partial(_diff_prompt_kernel, tq=tq, lambda_init=lambda_init)
    k_spec = pl.BlockSpec((1, 1, 1, 2 * HD_D, seq), lambda b, h, i: (p, b, h, 0, 0))
    v_spec = pl.BlockSpec((1, seq * H_D, 2 * HD_D), lambda b, h, i: (p, b, 0))
    q_spec = pl.BlockSpec((tq, 2 * HD_D), lambda b, h, i: (b * nq + i, h))
    return pl.pallas_call(
        kern,
        grid=(batch, H_D, nq),
        in_specs=[pl.BlockSpec((4, HD_D), lambda b, h, i: (0, 0)), pl.BlockSpec((1, 2 * HD_D), lambda b, h, i: (0, 0)),
                  q_spec, k_spec, v_spec],
        out_specs=q_spec,
        out_shape=jax.ShapeDtypeStruct((r, D_D), BF16),
        scratch_shapes=[pltpu.VMEM((2 * HD_D, seq), BF16), pltpu.VMEM((seq, 4 * HD_D), BF16),
                        pltpu.VMEM((2, tq, LANES), F32), pltpu.VMEM((2, tq, 4 * HD_D), F32),
                        pltpu.VMEM((2, tq, tq // 2), F32), pltpu.VMEM((2, tq, tq // 2), F32),
                        pltpu.VMEM((tq // 2, tq // 2), F32)],
        compiler_params=_params(3),
        name="diff_prompt",
    )(lp, nw, q, kt, v)


def _diff_sample_kernel(lp_ref, nw_ref, q_ref, k_ref, v_ref, kc_ref, vc_ref, o_ref, *, l, past, lambda_init):
    hd = pl.program_id(1)
    pad = jnp.zeros((LANES - l, 2 * HD_D), BF16)
    knew = jnp.concatenate([k_ref[...].astype(BF16), pad], axis=0)
    vnew = jnp.concatenate([v_ref[...].astype(BF16), pad], axis=0)
    kct = kc_ref[0, 0, 0].astype(BF16)
    vcache = vc_ref[0, pl.ds(hd, past, stride=H_D), :].astype(BF16)
    valid = lax.broadcasted_iota(jnp.int32, (l, LANES), 1) < l
    acc, den = [], []
    for c in range(2):
        qc = q_ref[:, c * HD_D:(c + 1) * HD_D]
        s_old = _dot(qc, kct[c * HD_D:(c + 1) * HD_D, :])
        s_new = jnp.where(valid, _dot_nt(qc, knew[:, c * HD_D:(c + 1) * HD_D]), NEG_INF)
        m = jnp.maximum(jnp.max(s_old, axis=-1, keepdims=True), jnp.max(s_new, axis=-1, keepdims=True))
        p_old = jnp.exp2(s_old - m)
        p_new = jnp.exp2(s_new - m)
        den.append(jnp.sum(p_old, axis=-1, keepdims=True) + jnp.sum(p_new, axis=-1, keepdims=True))
        acc.append(_dot(p_old.astype(BF16), vcache) + _dot(p_new.astype(BF16), vnew))
    lam = _lambda_value(lp_ref, lambda_init)
    o_ref[...] = _diff_finish(acc, den, lam, nw_ref[...], lambda_init)


def _diff_sample(lp, nw, q, k, v, kct, vc, p, batch, l, lambda_init):
    r = q.shape[0]
    past = kct.shape[-1]
    kern = functools.partial(_diff_sample_kernel, l=l, past=past, lambda_init=lambda_init)
    new_spec = pl.BlockSpec((l, 2 * HD_D), lambda b, h: (b, h))
    cache_k_spec = pl.BlockSpec((1, 1, 1, 2 * HD_D, past), lambda b, h: (p, b, h, 0, 0))
    cache_v_spec = pl.BlockSpec((1, past * H_D, 2 * HD_D), lambda b, h: (p, b, 0))
    return pl.pallas_call(
        kern,
        grid=(batch, H_D),
        in_specs=[pl.BlockSpec((4, HD_D), lambda b, h: (0, 0)), pl.BlockSpec((1, 2 * HD_D), lambda b, h: (0, 0)),
                  new_spec, new_spec, new_spec, cache_k_spec, cache_v_spec],
        out_specs=new_spec,
        out_shape=jax.ShapeDtypeStruct((r, D_D), BF16),
        compiler_params=_params(2),
        name="diff_sample",
    )(lp, nw, q, k, v, kct, vc)


def _odd_mix_kernel(qd_ref, kd_ref, el_ref, cv_ref, cg_ref, yd_ref, x_ref, s0_ref, gw_ref, wo_ref,
                    o_ref, s_ref, state, og, ycat, *, tm, blk, tiles_per_seq, per_block_state):
    i = pl.program_id(0)
    if not per_block_state:
        @pl.when(i % tiles_per_seq == 0)
        def _():
            state[...] = jnp.zeros_like(state)

    tri = lax.broadcasted_iota(jnp.int32, (blk, blk), 1) <= lax.broadcasted_iota(jnp.int32, (blk, blk), 0)
    for n in range(tm // blk):
        r0 = n * blk
        el = el_ref[n:n + 1, :]
        for h in range(H_C):
            if per_block_state:
                st = s0_ref[n, h]
            else:
                st = state[h]
            qd = qd_ref[r0:r0 + blk, h * DK_C:(h + 1) * DK_C]
            kd = kd_ref[r0:r0 + blk, h * DK_C:(h + 1) * DK_C]
            vh = cv_ref[r0:r0 + blk, h * DV_C:(h + 1) * DV_C]
            att = jnp.where(tri, _dot_nt(qd, kd), 0.0).astype(BF16)
            og[r0:r0 + blk, h * DV_C:(h + 1) * DV_C] = _dot(att, vh) + _dot_nt(qd, st.astype(BF16))
            st = el[:, h * DK_C:(h + 1) * DK_C] * (st + _dot_tn(vh, kd))
            if per_block_state:
                s_ref[n, h] = st
            else:
                state[h] = st
    if not per_block_state:
        s_ref[0] = state[...]

    gw = gw_ref[...]
    for h in range(H_C):
        lo, hi = h * DV_C, (h + 1) * DV_C
        g = cg_ref[:, lo:hi]
        y = _rms(og[:, lo:hi], gw) * (g * (1.0 / (1.0 + jnp.exp(-g))))
        ycat[:, lo:hi] = y.astype(BF16)
    ycat[:, D_C:] = yd_ref[...]
    o_ref[...] = x_ref[...] + _dot(ycat[...], wo_ref[...])


def _odd_mix(qd, kd, el, cv, cg, yd, x, s0, gw, wo, tm, blk, seq, per_block_state):
    r = x.shape[0]
    nb = tm // blk
    tiles_per_seq = max(seq // tm, 1)
    kern = functools.partial(_odd_mix_kernel, tm=tm, blk=blk, tiles_per_seq=tiles_per_seq,
                             per_block_state=per_block_state)
    if per_block_state:
        n_state = r // blk
        state_spec = pl.BlockSpec((nb, H_C, DV_C, DK_C), lambda i: (i, 0, 0, 0))
    else:
        n_state = r // seq
        state_spec = pl.BlockSpec((1, H_C, DV_C, DK_C), lambda i: (i // tiles_per_seq, 0, 0, 0))
    return pl.pallas_call(
        kern,
        grid=(r // tm,),
        in_specs=[_rows(tm, D_CQK), _rows(tm, D_CQK), _rows(nb, D_CQK), _rows(tm, D_C), _rows(tm, D_C),
                  _rows(tm, D_D), _rows(tm, D_MODEL), state_spec, _full((1, DV_C)), _full((D_C + D_D, D_MODEL))],
        out_specs=[_rows(tm, D_MODEL), state_spec],
        out_shape=[jax.ShapeDtypeStruct((r, D_MODEL), F32),
                   jax.ShapeDtypeStruct((n_state, H_C, DV_C, DK_C), F32)],
        scratch_shapes=[pltpu.VMEM((H_C, DV_C, DK_C), F32), pltpu.VMEM((tm, D_C), F32),
                        pltpu.VMEM((tm, D_C + D_D), BF16)],
        compiler_params=_params(),
        name="odd_mix",
    )(qd, kd, el, cv, cg, yd, x, s0, gw, wo)


def _gelu_tanh(x):
    return 0.5 * x * (1.0 + jnp.tanh(math.sqrt(2.0 / math.pi) * (x + 0.044715 * (x * x * x))))


def _ffn_kernel(x_ref, nw_ref, wup_ref, cw_ref, cb_ref, wdn_ref, prev_ref, fnw_ref, o_ref, tail_ref, carry, act,
                *, tm, seg, tiles_per_seq, prev_from_input, final_norm):
    i = pl.program_id(0)
    x = x_ref[...]
    h = _rms(x, nw_ref[...]).astype(BF16)
    nseg = tm // seg
    row = lax.broadcasted_iota(jnp.int32, (seg, 1), 0)
    if not prev_from_input:
        @pl.when(i % tiles_per_seq == 0)
        def _():
            carry[...] = jnp.zeros_like(carry)

    def up(c0):
        return _dot(h, wup_ref[:, c0:c0 + FFN_COLS]), _dot(h, wup_ref[:, D_FF + c0:D_FF + c0 + FFN_COLS])

    nxt = up(0)
    for c0 in range(0, D_FF, FFN_COLS):
        c1 = c0 + FFN_COLS
        gate, val = nxt
        if c1 < D_FF:
            nxt = up(c1)
        w0, w1, w2 = cw_ref[0:1, c0:c1], cw_ref[1:2, c0:c1], cw_ref[2:3, c0:c1]
        convs = []
        for s in range(nseg):
            g = gate[s * seg:(s + 1) * seg]
            if prev_from_input:
                p0, p1 = prev_ref[s, 0:1, c0:c1], prev_ref[s, 1:2, c0:c1]
            else:
                p0 = carry[CONV_TAIL - 2:CONV_TAIL - 1, c0:c1]
                p1 = carry[CONV_TAIL - 1:CONV_TAIL, c0:c1]
            g1 = jnp.where(row == 0, p1, pltpu.roll(g, 1, 0))
            g2 = jnp.where(row == 0, p0, jnp.where(row == 1, p1, pltpu.roll(g, 2, 0)))
            convs.append(cb_ref[:, c0:c1] + g2 * w0 + g1 * w1 + g * w2)
            tail_ref[s, :, c0:c1] = g[seg - CONV_TAIL:]
        conv = convs[0] if nseg == 1 else jnp.concatenate(convs, axis=0)
        if not prev_from_input:
            carry[:, c0:c1] = gate[tm - CONV_TAIL:]
        act[:, c0:c1] = (_gelu_tanh(conv) * val).astype(BF16)
    y = x + _dot(act[...], wdn_ref[...])
    if final_norm:
        y = _rms(y, fnw_ref[...])
    o_ref[...] = y


def _ffn(x, nw, wup, cw, cb, wdn, prev, fnw, tm, seg, seq, prev_from_input, final_norm):
    r = x.shape[0]
    nseg = tm // seg
    kern = functools.partial(_ffn_kernel, tm=tm, seg=seg, tiles_per_seq=max(seq // tm, 1),
                             prev_from_input=prev_from_input, final_norm=final_norm)
    prev_spec = pl.BlockSpec((nseg, CONV_W - 1, D_FF), lambda i: (i, 0, 0))
    tail_spec = pl.BlockSpec((nseg, CONV_TAIL, D_FF), lambda i: (i, 0, 0))
    return pl.pallas_call(
        kern,
        grid=(r // tm,),
        in_specs=[_rows(tm, D_MODEL), _full((1, D_MODEL)), _full((D_MODEL, 2 * D_FF)), _full((CONV_W, D_FF)),
                  _full((1, D_FF)), _full((D_FF, D_MODEL)), prev_spec, _full((1, D_MODEL))],
        out_specs=[_rows(tm, D_MODEL), tail_spec],
        out_shape=[jax.ShapeDtypeStruct((r, D_MODEL), F32),
                   jax.ShapeDtypeStruct((r // seg, CONV_TAIL, D_FF), F32)],
        scratch_shapes=[pltpu.VMEM((CONV_TAIL, D_FF), F32), pltpu.VMEM((tm, D_FF), BF16)],
        compiler_params=_params(),
        name="ffn",
    )(x, nw, wup, cw, cb, wdn, prev, fnw)


def _pad_odd_w_in(w):
    cr = w[:, ODD_CR0:ODD_CR0 + GATE_RANK]
    return jnp.concatenate([w[:, :ODD_CR0], cr, jnp.zeros((D_MODEL, LANES - GATE_RANK), w.dtype),
                            w[:, ODD_CR0 + GATE_RANK:]], axis=1).astype(BF16)


def _trunk(x3, pos, prompt, caches, weights):
    (norm_mix_w, norm_ffn_w, final_norm_w, even_w_in, pool_w, pool_scale, swa_sinks, even_w_out, odd_w_in,
     gla_w_gate2, gla_b_gate, gla_norm_w, diff_lambda, diff_norm_w, odd_w_out, ffn_w_up, ffn_conv_w, ffn_conv_b,
     ffn_w_down) = weights
    cache_pool, cache_swa_k, cache_swa_v, state_gla, cache_diff_k, cache_diff_v, cache_ffn_conv = caches
    b, l, _ = x3.shape
    r = b * l
    depth = norm_mix_w.shape[0]
    x = x3.reshape(r, D_MODEL)
    if prompt:
        tm = min(ROW_TILE, l)
        blk = min(CHUNK, l)
        tabs = _rope_tables(pos)
    else:
        tm = r
        blk = min(CHUNK, l)
        tabs = tuple(jnp.tile(t, (b, 1)) for t in _rope_tables(pos))
    pos0 = 0 if prompt else cache_diff_k.shape[2]

    pools, swa_ks, swa_vs, glas, diff_ks, diff_vs, ffns = [], [], [], [], [], [], []
    kv_p = None
    for layer in range(depth):
        p = layer // 2
        nw = norm_mix_w[layer].reshape(1, D_MODEL)
        if layer % 2 == 0:
            u, q, k, v = _even_in(x, nw, even_w_in[p].astype(BF16), tabs, tm)
            wp = pool_w[p].astype(BF16)
            ps = pool_scale[p].reshape(1, D_A)
            wo = even_w_out[p].astype(BF16)
            if prompt:
                x = _even_mix_prompt(swa_sinks[p], u, q, k, v, x, wp, ps, wo, l, tm)
                pools.append(u.reshape(b, l, D_A)[:, l - POOL_STATE:])
                swa_ks.append(k.reshape(b, l, HKV_B, HEAD_DIM)[:, l - WINDOW:])
                swa_vs.append(v.reshape(b, l, HKV_B, HEAD_DIM)[:, l - WINDOW:])
            else:
                rows_c = cache_swa_k.shape[2]
                pool_prev = jnp.pad(cache_pool[p], ((0, 0), (1, 0), (0, 0)))
                kc = cache_swa_k[p].reshape(b, rows_c, D_KV)
                vc = cache_swa_v[p].reshape(b, rows_c, D_KV)
                x = _even_mix_sample(swa_sinks[p], u, q, k, v, x, pool_prev, kc, vc, wp, ps, wo, l, pos0)
                pools.append(jnp.concatenate([cache_pool[p], u.reshape(b, l, D_A)], axis=1)[:, -POOL_STATE:])
                swa_ks.append(jnp.concatenate([cache_swa_k[p], k.reshape(b, l, HKV_B, HEAD_DIM)], axis=1)[:, -rows_c:])
                swa_vs.append(jnp.concatenate([cache_swa_v[p], v.reshape(b, l, HKV_B, HEAD_DIM)], axis=1)[:, -rows_c:])
        else:
            lambda_init = 0.8 - 0.6 * math.exp(-0.3 * layer)
            wg = jnp.pad(gla_w_gate2[p], ((0, LANES - GATE_RANK), (0, 0))).astype(BF16)
            bg = gla_b_gate[p].reshape(1, D_CQK)
            lp = diff_lambda[p]
            dnw = diff_norm_w[p].reshape(1, 2 * HD_D)
            w_in = _pad_odd_w_in(odd_w_in[p])
            if prompt:
                qd, kd, el, cv, cg, dq, *kv_p = _odd_in(x, nw, w_in, wg, bg, tabs, tm, blk,
                                                        pair=(p, depth // 2, b, l, kv_p))
                yd = _diff_prompt(lp, dnw, dq, kv_p[0], kv_p[1], p, b, l, lambda_init, min(ATTN_TILE, l))
                s0 = jnp.zeros((b, H_C, DV_C, DK_C), F32)
            else:
                qd, kd, el, cv, cg, dq, dk, dv = _odd_in(x, nw, w_in, wg, bg, tabs, tm, blk)
                past = cache_diff_k.shape[2]
                kct = jnp.transpose(cache_diff_k, (0, 1, 3, 4, 5, 2)).reshape(depth // 2, b, H_D, 2 * HD_D, past)
                vc = cache_diff_v.reshape(depth // 2, b * past * H_D, 2 * HD_D)
                yd = _diff_sample(lp, dnw, dq, dk, dv, kct, vc, p, b, l, lambda_init)
                s0 = jnp.swapaxes(state_gla[p], -1, -2).astype(F32)
                diff_ks.append(dk.reshape(b, l, H_D, 2, HD_D))
                diff_vs.append(dv.reshape(b, l, H_D, 2 * HD_D))
            x, s_new = _odd_mix(qd, kd, el, cv, cg, yd, x, s0, gla_norm_w[p].reshape(1, DV_C),
                                odd_w_out[p].astype(BF16), tm, blk, l, not prompt)
            glas.append(jnp.swapaxes(s_new, -1, -2))
        if prompt:
            seg = tm
            prev = jnp.zeros((r // tm, CONV_W - 1, D_FF), F32)
        else:
            seg = l
            prev = cache_ffn_conv[layer]
        x, tails = _ffn(x, norm_ffn_w[layer].reshape(1, D_MODEL), ffn_w_up[layer].astype(BF16), ffn_conv_w[layer],
                        ffn_conv_b[layer].reshape(1, D_FF), ffn_w_down[layer].astype(BF16), prev,
                        final_norm_w.reshape(1, D_MODEL), tm, seg, l, not prompt, layer == depth - 1)
        tails = tails.reshape(b, -1, CONV_TAIL, D_FF)
        ffns.append(tails[:, -1, CONV_TAIL - (CONV_W - 1):])
    if prompt:
        n_pairs = depth // 2
        diff_k = jnp.transpose(kv_p[0].reshape(n_pairs, b, H_D, 2, HD_D, l), (0, 1, 5, 2, 3, 4))
        diff_v = kv_p[1].reshape(n_pairs, b, l, H_D, 2 * HD_D)
    else:
        diff_k, diff_v = jnp.stack(diff_ks), jnp.stack(diff_vs)
    return (x.reshape(b, l, D_MODEL), jnp.stack(pools), jnp.stack(swa_ks), jnp.stack(swa_vs), jnp.stack(glas),
            diff_k, diff_v, jnp.stack(ffns))


def kernel(x_prompt, x_sample, cache_pool, cache_swa_k, cache_swa_v, state_gla, cache_diff_k, cache_diff_v, cache_ffn_conv, norm_mix_w, norm_ffn_w, final_norm_w, even_w_in, pool_w, pool_scale, swa_sinks, even_w_out, odd_w_in, gla_w_gate2, gla_b_gate, gla_norm_w, diff_lambda, diff_norm_w, odd_w_out, ffn_w_up, ffn_conv_w, ffn_conv_b, ffn_w_down):
    weights = (norm_mix_w, norm_ffn_w, final_norm_w, even_w_in, pool_w, pool_scale, swa_sinks, even_w_out, odd_w_in,
               gla_w_gate2, gla_b_gate, gla_norm_w, diff_lambda, diff_norm_w, odd_w_out, ffn_w_up, ffn_conv_w,
               ffn_conv_b, ffn_w_down)
    caches = (cache_pool, cache_swa_k, cache_swa_v, state_gla, cache_diff_k, cache_diff_v, cache_ffn_conv)
    past = cache_diff_k.shape[2]
    pos_p = jnp.arange(x_prompt.shape[1])
    pos_s = past + jnp.arange(x_sample.shape[1])
    outs_p = _trunk(x_prompt, pos_p, True, caches, weights)
    outs_s = _trunk(x_sample, pos_s, False, caches, weights)
    return (outs_p[0], outs_s[0]) + tuple(outs_p[1:]) + tuple(outs_s[1:])
```

```python
import functools
import math

import jax
import jax.numpy as jnp
from jax import lax
from jax.experimental import pallas as pl
from jax.experimental.pallas import tpu as pltpu

F32 = jnp.float32
BF16 = jnp.bfloat16

D_MODEL = 1024
CHUNK = 64
EPS = 1e-6
NEG_INF = -1e30
LOG2E = 1.0 / math.log(2.0)
ROPE_THETA = 500000.0
HEAD_DIM = 64
ROT_DIM = HEAD_DIM // 4
ROT_HALF = ROT_DIM // 2
LANES = 128

POOL_WINDOWS = (2, 4, 8, 16)
POOL_GROUP = D_MODEL // 8
D_A = len(POOL_WINDOWS) * POOL_GROUP
POOL_STATE = max(POOL_WINDOWS) - 1
POOL_HALO = POOL_STATE + 1

H_B = D_MODEL // 128
HKV_B = 2
GQA_B = H_B // HKV_B
WINDOW = 128
D_B = H_B * HEAD_DIM
D_KV = HKV_B * HEAD_DIM

H_C = 4
DK_C = 64
DV_C = 128
GATE_RANK = 16
GATE_NORM = 16.0
D_CQK = H_C * DK_C
D_C = H_C * DV_C

H_D = 4
HD_D = 64
D_D = H_D * 2 * HD_D

D_FF = 2816
CONV_W = 3
CONV_TAIL = 8

ROW_TILE = 1024
FFN_ROW_TILE = 1024
ATTN_TILE = 1024
FFN_COLS = 256
VMEM_LIMIT = 60 * 1024 * 1024


def _dot(a, b):
    return jnp.dot(a, b, preferred_element_type=F32)


def _dot_nt(a, b):
    return lax.dot_general(a, b, (((1,), (1,)), ((), ())), preferred_element_type=F32)


def _dot_tn(a, b):
    return lax.dot_general(a, b, (((0,), (0,)), ((), ())), preferred_element_type=F32)


def _rms(x, w):
    return x * lax.rsqrt(jnp.mean(x * x, axis=-1, keepdims=True) + EPS) * w


def _rope(z, cos_t, sin_a, sin_b):
    outs = []
    for s in range(z.shape[1] // LANES):
        x = z[:, s * LANES:(s + 1) * LANES]
        outs.append(x * cos_t + pltpu.roll(x, LANES - ROT_HALF, 1) * sin_a + pltpu.roll(x, ROT_HALF, 1) * sin_b)
    return outs[0] if len(outs) == 1 else jnp.concatenate(outs, axis=-1)


def _rope_tables(pos):
    t = pos.shape[0]
    inv = jnp.exp(-math.log(ROPE_THETA) * jnp.arange(ROT_HALF, dtype=F32) * (2.0 / ROT_DIM))
    ang = pos.astype(F32)[:, None] * inv[None, :]
    cos, sin = jnp.cos(ang), jnp.sin(ang)
    rest = HEAD_DIM - ROT_DIM
    cos_t = jnp.concatenate([cos, cos, jnp.ones((t, rest), F32)], axis=-1)
    sin_a = jnp.concatenate([-sin, jnp.zeros((t, HEAD_DIM - ROT_HALF), F32)], axis=-1)
    sin_b = jnp.concatenate([jnp.zeros((t, ROT_HALF), F32), sin, jnp.zeros((t, rest), F32)], axis=-1)
    rep = LANES // HEAD_DIM
    return tuple(jnp.tile(a, (1, rep)) for a in (cos_t, sin_a, sin_b))


def _params(n_axes=1):
    return pltpu.CompilerParams(dimension_semantics=("arbitrary",) * n_axes, vmem_limit_bytes=VMEM_LIMIT)


def _full(shape):
    nd = len(shape)
    return pl.BlockSpec(shape, lambda *_: (0,) * nd, pipeline_mode=pl.Buffered(1))


def _rows(tm, cols):
    return pl.BlockSpec((tm, cols), lambda i: (i, 0))


def _even_in_kernel(x_ref, nw_ref, w_ref, cos_ref, sa_ref, sb_ref, u_ref, q_ref, k_ref, v_ref):
    h = _rms(x_ref[...], nw_ref[...]).astype(BF16)
    cos_t, sin_a, sin_b = cos_ref[...], sa_ref[...], sb_ref[...]
    u_ref[...] = _dot(h, w_ref[:, 0:D_A])
    q = _dot(h, w_ref[:, D_A:D_A + D_B])
    q_ref[...] = (_rope(q, cos_t, sin_a, sin_b) * (HEAD_DIM ** -0.5)).astype(BF16)
    k = _dot(h, w_ref[:, D_A + D_B:D_A + D_B + D_KV])
    k_ref[...] = _rope(k, cos_t, sin_a, sin_b)
    v_ref[...] = _dot(h, w_ref[:, D_A + D_B + D_KV:D_A + D_B + 2 * D_KV])


def _even_in(x, nw, w, tabs, tm):
    r = x.shape[0]
    t_tiles = tabs[0].shape[0] // tm
    tab_spec = pl.BlockSpec((tm, LANES), lambda i: (i % t_tiles, 0))
    n_in = D_A + D_B + 2 * D_KV
    return pl.pallas_call(
        _even_in_kernel,
        grid=(r // tm,),
        in_specs=[_rows(tm, D_MODEL), _full((1, D_MODEL)), _full((D_MODEL, n_in)), tab_spec, tab_spec, tab_spec],
        out_specs=[_rows(tm, D_A), _rows(tm, D_B), _rows(tm, D_KV), _rows(tm, D_KV)],
        out_shape=[jax.ShapeDtypeStruct((r, D_A), F32), jax.ShapeDtypeStruct((r, D_B), BF16),
                   jax.ShapeDtypeStruct((r, D_KV), F32), jax.ShapeDtypeStruct((r, D_KV), F32)],
        compiler_params=_params(),
        name="even_in",
    )(x, nw, w, *tabs)


def _pool_mix(ext, u, pos, wp_ref, ps, ycat_ref, row0, rows):
    for g, w in enumerate(POOL_WINDOWS):
        lo, hi = g * POOL_GROUP, (g + 1) * POOL_GROUP
        s = ext[:, lo:hi]
        sh = 1
        while sh < w:
            s = s + pltpu.roll(s, sh, 0)
            sh *= 2
        count = jnp.minimum(w, pos + 1).astype(F32)
        diff = s[POOL_HALO:] / count - u[:, lo:hi]
        ya = _dot(diff.astype(BF16), wp_ref[g]) * ps[:, lo:hi]
        ycat_ref[row0:row0 + rows, lo:hi] = ya.astype(BF16)


def _sink_column(sink_ref, hk, rows_per_head):
    n = GQA_B * rows_per_head
    grp = lax.broadcasted_iota(jnp.int32, (n, 1), 0) // rows_per_head
    col = jnp.full((n, 1), sink_ref[hk * GQA_B], F32)
    for g in range(1, GQA_B):
        col = jnp.where(grp == g, sink_ref[hk * GQA_B + g], col)
    return col


def _sink_attend(s, sink_col, vb):
    m = jnp.maximum(jnp.max(s, axis=-1, keepdims=True), sink_col)
    e = jnp.exp(s - m)
    den = jnp.sum(e, axis=-1, keepdims=True) + jnp.exp(sink_col - m)
    return _dot(e.astype(BF16), vb) / den


def _store_heads(ycat_ref, o, row0, rows, hk):
    for pair in range(GQA_B // 2):
        both = jnp.concatenate([o[(2 * pair) * rows:(2 * pair + 1) * rows],
                                o[(2 * pair + 1) * rows:(2 * pair + 2) * rows]], axis=-1)
        c0 = D_A + (hk * GQA_B + 2 * pair) * HEAD_DIM
        ycat_ref[row0:row0 + rows, c0:c0 + 2 * HEAD_DIM] = both.astype(BF16)


def _even_mix_prompt_kernel(sink_ref, u_ref, q_ref, k_ref, v_ref, x_ref, wp_ref, ps_ref, wo_ref, o_ref,
                            ucar, kcar, vcar, ycat, *, tm, tiles_per_seq):
    i = pl.program_id(0)
    ti = i % tiles_per_seq
    first_key_chunk = jnp.where(ti == 0, WINDOW // CHUNK, 0)

    @pl.when(ti == 0)
    def _():
        ucar[...] = jnp.zeros_like(ucar)
        kcar[...] = jnp.zeros_like(kcar)
        vcar[...] = jnp.zeros_like(vcar)

    u = u_ref[...]
    ext = jnp.concatenate([ucar[...], u], axis=0)
    pos = ti * tm + lax.broadcasted_iota(jnp.int32, (tm, POOL_GROUP), 0)
    _pool_mix(ext, u, pos, wp_ref, ps_ref[...], ycat, 0, tm)
    ucar[...] = u[tm - POOL_HALO:]

    kext = jnp.concatenate([kcar[...], k_ref[...].astype(BF16)], axis=0)
    vext = jnp.concatenate([vcar[...], v_ref[...].astype(BF16)], axis=0)
    kcar[...] = kext[tm:]
    vcar[...] = vext[tm:]
    qb = 2 * CHUNK
    kb = qb + WINDOW
    nrow = GQA_B * qb
    qc = (lax.broadcasted_iota(jnp.int32, (nrow, kb), 0) % qb) // CHUNK
    kc = lax.broadcasted_iota(jnp.int32, (nrow, kb), 1) // CHUNK
    band = (kc >= qc) & (kc <= qc + WINDOW // CHUNK)
    first = band & (kc >= first_key_chunk)
    grp = lax.broadcasted_iota(jnp.int32, (nrow, LANES), 0) // qb
    low_lanes = lax.broadcasted_iota(jnp.int32, (qb, LANES), 1) < HEAD_DIM
    ones = jnp.ones((WINDOW + tm, HEAD_DIM), BF16)
    v_aug, sink_rep = [], []
    for hk in range(HKV_B):
        v_aug.append(jnp.concatenate([vext[:, hk * HEAD_DIM:(hk + 1) * HEAD_DIM], ones], axis=-1))
        rep = jnp.full((nrow, LANES), sink_ref[hk * GQA_B], F32)
        for g in range(1, GQA_B):
            rep = jnp.where(grp == g, sink_ref[hk * GQA_B + g], rep)
        sink_rep.append(rep)

    def scores(a, hk):
        qs = jnp.concatenate(
            [q_ref[a * qb:(a + 1) * qb, (hk * GQA_B + g) * HEAD_DIM:(hk * GQA_B + g + 1) * HEAD_DIM]
             for g in range(GQA_B)], axis=0)
        return _dot_nt(qs, kext[a * qb:a * qb + kb, hk * HEAD_DIM:(hk + 1) * HEAD_DIM])

    blocks = [(a, hk) for a in range(tm // qb) for hk in range(HKV_B)]
    nxt = scores(*blocks[0])
    for n, (a, hk) in enumerate(blocks):
        s = jnp.where(first if a == 0 else band, nxt, NEG_INF)
        if n + 1 < len(blocks):
            nxt = scores(*blocks[n + 1])
        m = jnp.maximum(jnp.max(s, axis=-1, keepdims=True), sink_rep[hk])
        e = jnp.exp(s - _lane_tile(m, kb // LANES)).astype(BF16)
        oa = _dot(e, v_aug[hk][a * qb:a * qb + kb])
        inv = 1.0 / (oa + jnp.exp(sink_rep[hk] - m))
        for pair in range(GQA_B // 2):
            lo = slice(2 * pair * qb, (2 * pair + 1) * qb)
            hi = slice((2 * pair + 1) * qb, (2 * pair + 2) * qb)
            both = jnp.where(low_lanes, oa[lo] * pltpu.roll(inv[lo], HEAD_DIM, 1),
                             pltpu.roll(oa[hi], HEAD_DIM, 1) * inv[hi])
            c0 = D_A + (hk * GQA_B + 2 * pair) * HEAD_DIM
            ycat[a * qb:(a + 1) * qb, c0:c0 + 2 * HEAD_DIM] = both.astype(BF16)
    o_ref[...] = x_ref[...] + _dot(ycat[...], wo_ref[...])


def _even_mix_prompt(sinks, u, q, k, v, x, wp, ps, wo, seq, tm):
    r = x.shape[0]
    kern = functools.partial(_even_mix_prompt_kernel, tm=tm, tiles_per_seq=seq // tm)
    return pl.pallas_call(
        kern,
        grid=(r // tm,),
        in_specs=[pl.BlockSpec(memory_space=pltpu.SMEM),
                  _rows(tm, D_A), _rows(tm, D_B), _rows(tm, D_KV), _rows(tm, D_KV), _rows(tm, D_MODEL),
                  _full(wp.shape), _full((1, D_A)), _full((D_A + D_B, D_MODEL))],
        out_specs=_rows(tm, D_MODEL),
        out_shape=jax.ShapeDtypeStruct((r, D_MODEL), F32),
        scratch_shapes=[pltpu.VMEM((POOL_HALO, D_A), F32), pltpu.VMEM((WINDOW, D_KV), BF16),
                        pltpu.VMEM((WINDOW, D_KV), BF16), pltpu.VMEM((tm, D_A + D_B), BF16)],
        compiler_params=_params(),
        name="even_mix_prompt",
    )(sinks, u, q, k, v, x, wp, ps, wo)


def _even_mix_sample_kernel(sink_ref, u_ref, q_ref, k_ref, v_ref, x_ref, pp_ref, kc_ref, vc_ref, wp_ref, ps_ref,
                            wo_ref, o_ref, ycat, *, l, pos0):
    u = u_ref[...]
    ext = jnp.concatenate([pp_ref[0], u], axis=0)
    pos = pos0 + lax.broadcasted_iota(jnp.int32, (l, 1), 0)
    _pool_mix(ext, u, pos, wp_ref, ps_ref[...], ycat, 0, l)

    pad = jnp.zeros((LANES - l, D_KV), BF16)
    kall = jnp.concatenate([kc_ref[0].astype(BF16), k_ref[...].astype(BF16), pad], axis=0)
    vall = jnp.concatenate([vc_ref[0].astype(BF16), v_ref[...].astype(BF16), pad], axis=0)
    nk = kall.shape[0]
    valid = lax.broadcasted_iota(jnp.int32, (GQA_B * l, nk), 1) < nk - (LANES - l)
    for hk in range(HKV_B):
        ks = kall[:, hk * HEAD_DIM:(hk + 1) * HEAD_DIM]
        vs = vall[:, hk * HEAD_DIM:(hk + 1) * HEAD_DIM]
        qs = jnp.concatenate(
            [q_ref[:, (hk * GQA_B + g) * HEAD_DIM:(hk * GQA_B + g + 1) * HEAD_DIM] for g in range(GQA_B)], axis=0)
        s = jnp.where(valid, _dot_nt(qs, ks), NEG_INF)
        o = _sink_attend(s, _sink_column(sink_ref, hk, l), vs)
        _store_heads(ycat, o, 0, l, hk)
    o_ref[...] = x_ref[...] + _dot(ycat[...], wo_ref[...])


def _even_mix_sample(sinks, u, q, k, v, x, pool_prev, kc, vc, wp, ps, wo, l, pos0):
    r = x.shape[0]
    rows_c = kc.shape[1]
    kern = functools.partial(_even_mix_sample_kernel, l=l, pos0=pos0)
    cache_spec = pl.BlockSpec((1, rows_c, D_KV), lambda i: (i, 0, 0))
    return pl.pallas_call(
        kern,
        grid=(r // l,),
        in_specs=[pl.BlockSpec(memory_space=pltpu.SMEM),
                  _rows(l, D_A), _rows(l, D_B), _rows(l, D_KV), _rows(l, D_KV), _rows(l, D_MODEL),
                  pl.BlockSpec((1, POOL_HALO, D_A), lambda i: (i, 0, 0)), cache_spec, cache_spec,
                  _full(wp.shape), _full((1, D_A)), _full((D_A + D_B, D_MODEL))],
        out_specs=_rows(l, D_MODEL),
        out_shape=jax.ShapeDtypeStruct((r, D_MODEL), F32),
        scratch_shapes=[pltpu.VMEM((l, D_A + D_B), BF16)],
        compiler_params=_params(),
        name="even_mix_sample",
    )(sinks, u, q, k, v, x, pool_prev, kc, vc, wp, ps, wo)


ODD_CR0 = 2 * D_CQK + 2 * D_C
ODD_DQ0 = ODD_CR0 + LANES
ODD_IN_PADDED = ODD_DQ0 + 3 * D_D


def _odd_in_kernel(x_ref, nw_ref, w_ref, wg_ref, bg_ref, cos_ref, sa_ref, sb_ref, *rest, tm, blk, cache_layout,
                   aliased):
    if aliased:
        rest = rest[2:]
    qd_ref, kd_ref, el_ref, cv_ref, cg_ref, dq_ref, dk_ref, dv_ref, cum_scr = rest
    h = _rms(x_ref[...], nw_ref[...]).astype(BF16)
    cos_t, sin_a, sin_b = cos_ref[...], sa_ref[...], sb_ref[...]
    cq = _dot(h, w_ref[:, 0:D_CQK])
    ck = _dot(h, w_ref[:, D_CQK:2 * D_CQK])
    cv_ref[...] = _dot(h, w_ref[:, 2 * D_CQK:2 * D_CQK + D_C]).astype(BF16)
    cg_ref[...] = _dot(h, w_ref[:, 2 * D_CQK + D_C:ODD_CR0])
    cr = _dot(h, w_ref[:, ODD_CR0:ODD_DQ0]).astype(BF16)
    gate = _dot(cr, wg_ref[...]) + bg_ref[...]
    cum = (jnp.minimum(gate, 0.0) - jnp.log1p(jnp.exp(-jnp.abs(gate)))) * (1.0 / GATE_NORM)
    rowb = lax.broadcasted_iota(jnp.int32, (tm, 1), 0) % blk
    sh = 1
    while sh < blk:
        cum = cum + jnp.where(rowb >= sh, pltpu.roll(cum, sh, 0), 0.0)
        sh *= 2
    for s in range(D_CQK // LANES):
        cum_scr[s] = cum[:, s * LANES:(s + 1) * LANES]
        el_ref[:, s * LANES:(s + 1) * LANES] = jnp.exp(cum_scr[s, pl.ds(blk - 1, tm // blk, stride=blk), :])
    qd_ref[...] = (cq * (DK_C ** -0.5) * jnp.exp(cum)).astype(BF16)
    kd_ref[...] = (ck * jnp.exp(-cum)).astype(BF16)
    dq = _dot(h, w_ref[:, ODD_DQ0:ODD_DQ0 + D_D])
    dq_ref[...] = (_rope(dq, cos_t, sin_a, sin_b) * (HD_D ** -0.5 * LOG2E)).astype(BF16)
    dk = _rope(_dot(h, w_ref[:, ODD_DQ0 + D_D:ODD_DQ0 + 2 * D_D]), cos_t, sin_a, sin_b)
    dv = _dot(h, w_ref[:, ODD_DQ0 + 2 * D_D:ODD_DQ0 + 3 * D_D])
    if cache_layout:
        for hd in range(H_D):
            lo, hi = hd * 2 * HD_D, (hd + 1) * 2 * HD_D
            dk_ref[0, 0, hd] = dk[:, lo:hi].T
            dv_ref[0, pl.ds(hd, tm, stride=H_D), :] = dv[:, lo:hi]
    else:
        dk_ref[...] = dk
        dv_ref[...] = dv


def _odd_in(x, nw, w, wg, bg, tabs, tm, blk, pair=None):
    r = x.shape[0]
    t_tiles = tabs[0].shape[0] // tm
    tab_spec = pl.BlockSpec((tm, LANES), lambda i: (i % t_tiles, 0))
    nb = tm // blk
    sds = jax.ShapeDtypeStruct
    in_specs = [_rows(tm, D_MODEL), _full((1, D_MODEL)), _full((D_MODEL, ODD_IN_PADDED)),
                _full((LANES, D_CQK)), _full((1, D_CQK)), tab_spec, tab_spec, tab_spec]
    args = [x, nw, w, wg, bg, *tabs]
    aliases = {}
    if pair is None:
        kv_specs = [_rows(tm, D_D), _rows(tm, D_D)]
        kv_shapes = [sds((r, D_D), F32), sds((r, D_D), F32)]
    else:
        p, n_pairs, batch, seq, bufs = pair
        tps = seq // tm
        kv_specs = [pl.BlockSpec((1, 1, H_D, 2 * HD_D, tm), lambda i: (p, i // tps, 0, 0, i % tps)),
                    pl.BlockSpec((1, tm * H_D, 2 * HD_D), lambda i: (p, i, 0))]
        kv_shapes = [sds((n_pairs, batch, H_D, 2 * HD_D, seq), F32), sds((n_pairs, r * H_D, 2 * HD_D), F32)]
        if bufs is not None:
            in_specs += [pl.BlockSpec(memory_space=pl.ANY)] * 2
            args += list(bufs)
            aliases = {len(args) - 2: 6, len(args) - 1: 7}
    kern = functools.partial(_odd_in_kernel, tm=tm, blk=blk, cache_layout=pair is not None, aliased=bool(aliases))
    return pl.pallas_call(
        kern,
        grid=(r // tm,),
        in_specs=in_specs,
        out_specs=[_rows(tm, D_CQK), _rows(tm, D_CQK), _rows(nb, D_CQK), _rows(tm, D_C), _rows(tm, D_C),
                   _rows(tm, D_D)] + kv_specs,
        out_shape=[sds((r, D_CQK), BF16), sds((r, D_CQK), BF16), sds((r // blk, D_CQK), F32),
                   sds((r, D_C), BF16), sds((r, D_C), F32), sds((r, D_D), BF16)] + kv_shapes,
        scratch_shapes=[pltpu.VMEM((D_CQK // LANES, tm, LANES), F32)],
        input_output_aliases=aliases,
        compiler_params=_params(),
        name="odd_in",
    )(*args)


def _lambda_value(lp_ref, lambda_init):
    lp = lp_ref[...]
    s1 = jnp.sum(lp[0:1] * lp[1:2], axis=-1, keepdims=True)
    s2 = jnp.sum(lp[2:3] * lp[3:4], axis=-1, keepdims=True)
    return jnp.exp(s1) - jnp.exp(s2) + lambda_init


def _diff_finish(acc, den, lam, nw, lambda_init):
    o = acc[0] / den[0] - lam * (acc[1] / den[1])
    return (_rms(o, nw) * (1.0 - lambda_init)).astype(BF16)


def _lane_tile(x, n):
    return x if n == 1 else jnp.concatenate([x] * n, axis=1)


def _diff_prompt_kernel(lp_ref, nw_ref, q_ref, k_ref, v_ref, o_ref, kbf, vbf, m_scr, acc_scr, s_a, s_b, bias,
                        *, tq, lambda_init):
    hd = pl.program_id(1)
    qi = pl.program_id(2)
    dv = 2 * HD_D
    seq = vbf.shape[0]

    @pl.when(qi == 0)
    def _():
        kbf[...] = k_ref[0, 0, 0].astype(BF16)
        vbf[:, :dv] = v_ref[0, pl.ds(hd, seq, stride=H_D), :].astype(BF16)
        vbf[:, dv:] = jnp.ones((seq, dv), BF16)

    tk = tq // 2

    @pl.when((pl.program_id(0) == 0) & (pl.program_id(1) == 0) & (qi == 0))
    def _():
        qc = lax.broadcasted_iota(jnp.int32, (tk, tk), 0) // CHUNK
        kc = lax.broadcasted_iota(jnp.int32, (tk, tk), 1) // CHUNK
        bias[...] = jnp.where(kc <= qc, 0.0, NEG_INF)

    m_scr[...] = jnp.full(m_scr.shape, NEG_INF, F32)
    acc_scr[...] = jnp.zeros(acc_scr.shape, F32)

    def scores(t, s_buf, row0=0):
        start = pl.multiple_of(t * tk, tk)
        for c in range(2):
            s_buf[c, row0:, :] = _dot(q_ref[row0:, c * HD_D:(c + 1) * HD_D],
                                      kbf[c * HD_D:(c + 1) * HD_D, pl.ds(start, tk)])

    def attend(t, s_buf, row0=0, rows=tq, masked=False):
        vs = vbf[pl.ds(pl.multiple_of(t * tk, tk), tk), :]
        for c in range(2):
            sc = s_buf[c, row0:row0 + rows, :]
            if masked:
                sc = sc + bias[...]
            m_old = m_scr[c, row0:row0 + rows, :]
            m_new = jnp.maximum(m_old, jnp.max(sc, axis=-1, keepdims=True))
            p = jnp.exp2(sc - _lane_tile(m_new, tk // LANES)).astype(BF16)
            alpha = _lane_tile(jnp.exp2(m_old - m_new), 2 * dv // LANES)
            acc_scr[c, row0:row0 + rows, :] = alpha * acc_scr[c, row0:row0 + rows, :] + _dot(p, vs)
            m_scr[c, row0:row0 + rows, :] = m_new

    scores(0, s_a)

    def pair(j, _):
        t = 2 * j
        scores(t + 1, s_b)
        attend(t, s_a)
        scores(t + 2, s_a)
        attend(t + 1, s_b)
        return 0

    lax.fori_loop(0, qi, pair, 0)
    scores(2 * qi + 1, s_b, row0=tk)
    attend(2 * qi, s_a, row0=0, rows=tk, masked=True)
    attend(2 * qi, s_a, row0=tk, rows=tk)
    attend(2 * qi + 1, s_b, row0=tk, rows=tk, masked=True)

    lam = _lambda_value(lp_ref, lambda_init)
    acc = [acc_scr[c] for c in range(2)]
    o_ref[...] = _diff_finish([a[:, :dv] for a in acc], [a[:, dv:] for a in acc], lam, nw_ref[...], lambda_init)


def _diff_prompt(lp, nw, q, kt, v, p, batch, seq, lambda_init, tq):
    r = q.shape[0]
    nq = seq // tq
    kern = functools.partial(_diff_prompt_kernel, tq=tq, lambda_init=lambda_init)
    k_spec = pl.BlockSpec((1, 1, 1, 2 * HD_D, seq), lambda b, h, i: (p, b, h, 0, 0))
    v_spec = pl.BlockSpec((1, seq * H_D, 2 * HD_D), lambda b, h, i: (p, b, 0))
    q_spec = pl.BlockSpec((tq, 2 * HD_D), lambda b, h, i: (b * nq + i, h))
    return pl.pallas_call(
        kern,
        grid=(batch, H_D, nq),
        in_specs=[pl.BlockSpec((4, HD_D), lambda b, h, i: (0, 0)), pl.BlockSpec((1, 2 * HD_D), lambda b, h, i: (0, 0)),
                  q_spec, k_spec, v_spec],
        out_specs=q_spec,
        out_shape=jax.ShapeDtypeStruct((r, D_D), BF16),
        scratch_shapes=[pltpu.VMEM((2 * HD_D, seq), BF16), pltpu.VMEM((seq, 4 * HD_D), BF16),
                        pltpu.VMEM((2, tq, LANES), F32), pltpu.VMEM((2, tq, 4 * HD_D), F32),
                        pltpu.VMEM((2, tq, tq // 2), F32), pltpu.VMEM((2, tq, tq // 2), F32),
                        pltpu.VMEM((tq // 2, tq // 2), F32)],
        compiler_params=_params(3),
        name="diff_prompt",
    )(lp, nw, q, kt, v)


def _diff_sample_kernel(lp_ref, nw_ref, q_ref, k_ref, v_ref, kc_ref, vc_ref, o_ref, *, l, past, lambda_init):
    hd = pl.program_id(1)
    pad = jnp.zeros((LANES - l, 2 * HD_D), BF16)
    knew = jnp.concatenate([k_ref[...].astype(BF16), pad], axis=0)
    vnew = jnp.concatenate([v_ref[...].astype(BF16), pad], axis=0)
    kct = kc_ref[0, 0, 0].astype(BF16)
    vcache = vc_ref[0, pl.ds(hd, past, stride=H_D), :].astype(BF16)
    valid = lax.broadcasted_iota(jnp.int32, (l, LANES), 1) < l
    acc, den = [], []
    for c in range(2):
        qc = q_ref[:, c * HD_D:(c + 1) * HD_D]
        s_old = _dot(qc, kct[c * HD_D:(c + 1) * HD_D, :])
        s_new = jnp.where(valid, _dot_nt(qc, knew[:, c * HD_D:(c + 1) * HD_D]), NEG_INF)
        m = jnp.maximum(jnp.max(s_old, axis=-1, keepdims=True), jnp.max(s_new, axis=-1, keepdims=True))
        p_old = jnp.exp2(s_old - m)
        p_new = jnp.exp2(s_new - m)
        den.append(jnp.sum(p_old, axis=-1, keepdims=True) + jnp.sum(p_new, axis=-1, keepdims=True))
        acc.append(_dot(p_old.astype(BF16), vcache) + _dot(p_new.astype(BF16), vnew))
    lam = _lambda_value(lp_ref, lambda_init)
    o_ref[...] = _diff_finish(acc, den, lam, nw_ref[...], lambda_init)


def _diff_sample(lp, nw, q, k, v, kct, vc, p, batch, l, lambda_init):
    r = q.shape[0]
    past = kct.shape[-1]
    kern = functools.partial(_diff_sample_kernel, l=l, past=past, lambda_init=lambda_init)
    new_spec = pl.BlockSpec((l, 2 * HD_D), lambda b, h: (b, h))
    cache_k_spec = pl.BlockSpec((1, 1, 1, 2 * HD_D, past), lambda b, h: (p, b, h, 0, 0))
    cache_v_spec = pl.BlockSpec((1, past * H_D, 2 * HD_D), lambda b, h: (p, b, 0))
    return pl.pallas_call(
        kern,
        grid=(batch, H_D),
        in_specs=[pl.BlockSpec((4, HD_D), lambda b, h: (0, 0)), pl.BlockSpec((1, 2 * HD_D), lambda b, h: (0, 0)),
                  new_spec, new_spec, new_spec, cache_k_spec, cache_v_spec],
        out_specs=new_spec,
        out_shape=jax.ShapeDtypeStruct((r, D_D), BF16),
        compiler_params=_params(2),
        name="diff_sample",
    )(lp, nw, q, k, v, kct, vc)


def _odd_mix_kernel(qd_ref, kd_ref, el_ref, cv_ref, cg_ref, yd_ref, x_ref, s0_ref, gw_ref, wo_ref,
                    o_ref, s_ref, state, og, ycat, *, tm, blk, tiles_per_seq, per_block_state):
    i = pl.program_id(0)
    if not per_block_state:
        @pl.when(i % tiles_per_seq == 0)
        def _():
            state[...] = jnp.zeros_like(state)

    tri = lax.broadcasted_iota(jnp.int32, (blk, blk), 1) <= lax.broadcasted_iota(jnp.int32, (blk, blk), 0)
    for n in range(tm // blk):
        r0 = n * blk
        el = el_ref[n:n + 1, :]
        for h in range(H_C):
            if per_block_state:
                st = s0_ref[n, h]
            else:
                st = state[h]
            qd = qd_ref[r0:r0 + blk, h * DK_C:(h + 1) * DK_C]
            kd = kd_ref[r0:r0 + blk, h * DK_C:(h + 1) * DK_C]
            vh = cv_ref[r0:r0 + blk, h * DV_C:(h + 1) * DV_C]
            att = jnp.where(tri, _dot_nt(qd, kd), 0.0).astype(BF16)
            og[r0:r0 + blk, h * DV_C:(h + 1) * DV_C] = _dot(att, vh) + _dot_nt(qd, st.astype(BF16))
            st = el[:, h * DK_C:(h + 1) * DK_C] * (st + _dot_tn(vh, kd))
            if per_block_state:
                s_ref[n, h] = st
            else:
                state[h] = st
    if not per_block_state:
        s_ref[0] = state[...]

    gw = gw_ref[...]
    for h in range(H_C):
        lo, hi = h * DV_C, (h + 1) * DV_C
        g = cg_ref[:, lo:hi]
        y = _rms(og[:, lo:hi], gw) * (g * (1.0 / (1.0 + jnp.exp(-g))))
        ycat[:, lo:hi] = y.astype(BF16)
    ycat[:, D_C:] = yd_ref[...]
    o_ref[...] = x_ref[...] + _dot(ycat[...], wo_ref[...])


def _odd_mix(qd, kd, el, cv, cg, yd, x, s0, gw, wo, tm, blk, seq, per_block_state):
    r = x.shape[0]
    nb = tm // blk
    tiles_per_seq = max(seq // tm, 1)
    kern = functools.partial(_odd_mix_kernel, tm=tm, blk=blk, tiles_per_seq=tiles_per_seq,
                             per_block_state=per_block_state)
    if per_block_state:
        n_state = r // blk
        state_spec = pl.BlockSpec((nb, H_C, DV_C, DK_C), lambda i: (i, 0, 0, 0))
    else:
        n_state = r // seq
        state_spec = pl.BlockSpec((1, H_C, DV_C, DK_C), lambda i: (i // tiles_per_seq, 0, 0, 0))
    return pl.pallas_call(
        kern,
        grid=(r // tm,),
        in_specs=[_rows(tm, D_CQK), _rows(tm, D_CQK), _rows(nb, D_CQK), _rows(tm, D_C), _rows(tm, D_C),
                  _rows(tm, D_D), _rows(tm, D_MODEL), state_spec, _full((1, DV_C)), _full((D_C + D_D, D_MODEL))],
        out_specs=[_rows(tm, D_MODEL), state_spec],
        out_shape=[jax.ShapeDtypeStruct((r, D_MODEL), F32),
                   jax.ShapeDtypeStruct((n_state, H_C, DV_C, DK_C), F32)],
        scratch_shapes=[pltpu.VMEM((H_C, DV_C, DK_C), F32), pltpu.VMEM((tm, D_C), F32),
                        pltpu.VMEM((tm, D_C + D_D), BF16)],
        compiler_params=_params(),
        name="odd_mix",
    )(qd, kd, el, cv, cg, yd, x, s0, gw, wo)


def _gelu_tanh(x):
    return 0.5 * x * (1.0 + jnp.tanh(math.sqrt(2.0 / math.pi) * (x + 0.044715 * (x * x * x))))


def _ffn_kernel(x_ref, nw_ref, wup_ref, cw_ref, cb_ref, wdn_ref, prev_ref, fnw_ref, o_ref, tail_ref, carry, act,
                *, tm, seg, tiles_per_seq, prev_from_input, final_norm):
    i = pl.program_id(0)
    x = x_ref[...]
    h = _rms(x, nw_ref[...]).astype(BF16)
    nseg = tm // seg
    row = lax.broadcasted_iota(jnp.int32, (seg, 1), 0)
    if not prev_from_input:
        @pl.when(i % tiles_per_seq == 0)
        def _():
            carry[...] = jnp.zeros_like(carry)

    def up(c0):
        return _dot(h, wup_ref[:, c0:c0 + FFN_COLS]), _dot(h, wup_ref[:, D_FF + c0:D_FF + c0 + FFN_COLS])

    nxt = up(0)
    for c0 in range(0, D_FF, FFN_COLS):
        c1 = c0 + FFN_COLS
        gate, val = nxt
        if c1 < D_FF:
            nxt = up(c1)
        w0, w1, w2 = cw_ref[0:1, c0:c1], cw_ref[1:2, c0:c1], cw_ref[2:3, c0:c1]
        convs = []
        for s in range(nseg):
            g = gate[s * seg:(s + 1) * seg]
            if prev_from_input:
                p0, p1 = prev_ref[s, 0:1, c0:c1], prev_ref[s, 1:2, c0:c1]
            else:
                p0 = carry[CONV_TAIL - 2:CONV_TAIL - 1, c0:c1]
                p1 = carry[CONV_TAIL - 1:CONV_TAIL, c0:c1]
            g1 = jnp.where(row == 0, p1, pltpu.roll(g, 1, 0))
            g2 = jnp.where(row == 0, p0, jnp.where(row == 1, p1, pltpu.roll(g, 2, 0)))
            convs.append(cb_ref[:, c0:c1] + g2 * w0 + g1 * w1 + g * w2)
            tail_ref[s, :, c0:c1] = g[seg - CONV_TAIL:]
        conv = convs[0] if nseg == 1 else jnp.concatenate(convs, axis=0)
        if not prev_from_input:
            carry[:, c0:c1] = gate[tm - CONV_TAIL:]
        act[:, c0:c1] = (_gelu_tanh(conv) * val).astype(BF16)
    y = x + _dot(act[...], wdn_ref[...])
    if final_norm:
        y = _rms(y, fnw_ref[...])
    o_ref[...] = y


def _ffn(x, nw, wup, cw, cb, wdn, prev, fnw, tm, seg, seq, prev_from_input, final_norm):
    r = x.shape[0]
    nseg = tm // seg
    kern = functools.partial(_ffn_kernel, tm=tm, seg=seg, tiles_per_seq=max(seq // tm, 1),
                             prev_from_input=prev_from_input, final_norm=final_norm)
    prev_spec = pl.BlockSpec((nseg, CONV_W - 1, D_FF), lambda i: (i, 0, 0))
    tail_spec = pl.BlockSpec((nseg, CONV_TAIL, D_FF), lambda i: (i, 0, 0))
    return pl.pallas_call(
        kern,
        grid=(r // tm,),
        in_specs=[_rows(tm, D_MODEL), _full((1, D_MODEL)), _full((D_MODEL, 2 * D_FF)), _full((CONV_W, D_FF)),
                  _full((1, D_FF)), _full((D_FF, D_MODEL)), prev_spec, _full((1, D_MODEL))],
        out_specs=[_rows(tm, D_MODEL), tail_spec],
        out_shape=[jax.ShapeDtypeStruct((r, D_MODEL), F32),
                   jax.ShapeDtypeStruct((r // seg, CONV_TAIL, D_FF), F32)],
        scratch_shapes=[pltpu.VMEM((CONV_TAIL, D_FF), F32), pltpu.VMEM((tm, D_FF), BF16)],
        compiler_params=_params(),
        name="ffn",
    )(x, nw, wup, cw, cb, wdn, prev, fnw)


def _pad_odd_w_in(w):
    cr = w[:, ODD_CR0:ODD_CR0 + GATE_RANK]
    return jnp.concatenate([w[:, :ODD_CR0], cr, jnp.zeros((D_MODEL, LANES - GATE_RANK), w.dtype),
                            w[:, ODD_CR0 + GATE_RANK:]], axis=1).astype(BF16)


def _trunk(x3, pos, prompt, caches, weights):
    (norm_mix_w, norm_ffn_w, final_norm_w, even_w_in, pool_w, pool_scale, swa_sinks, even_w_out, odd_w_in,
     gla_w_gate2, gla_b_gate, gla_norm_w, diff_lambda, diff_norm_w, odd_w_out, ffn_w_up, ffn_conv_w, ffn_conv_b,
     ffn_w_down) = weights
    cache_pool, cache_swa_k, cache_swa_v, state_gla, cache_diff_k, cache_diff_v, cache_ffn_conv = caches
    b, l, _ = x3.shape
    r = b * l
    depth = norm_mix_w.shape[0]
    x = x3.reshape(r, D_MODEL)
    if prompt:
        tm = min(ROW_TILE, l)
        blk = min(CHUNK, l)
        tabs = _rope_tables(pos)
    else:
        tm = r
        blk = min(CHUNK, l)
        tabs = tuple(jnp.tile(t, (b, 1)) for t in _rope_tables(pos))
    pos0 = 0 if prompt else cache_diff_k.shape[2]

    pools, swa_ks, swa_vs, glas, diff_ks, diff_vs, ffns = [], [], [], [], [], [], []
    kv_p = None
    for layer in range(depth):
        p = layer // 2
        nw = norm_mix_w[layer].reshape(1, D_MODEL)
        if layer % 2 == 0:
            u, q, k, v = _even_in(x, nw, even_w_in[p].astype(BF16), tabs, tm)
            wp = pool_w[p].astype(BF16)
            ps = pool_scale[p].reshape(1, D_A)
            wo = even_w_out[p].astype(BF16)
            if prompt:
                x = _even_mix_prompt(swa_sinks[p], u, q, k, v, x, wp, ps, wo, l, tm)
                pools.append(u.reshape(b, l, D_A)[:, l - POOL_STATE:])
                swa_ks.append(k.reshape(b, l, HKV_B, HEAD_DIM)[:, l - WINDOW:])
                swa_vs.append(v.reshape(b, l, HKV_B, HEAD_DIM)[:, l - WINDOW:])
            else:
                rows_c = cache_swa_k.shape[2]
                pool_prev = jnp.pad(cache_pool[p], ((0, 0), (1, 0), (0, 0)))
                kc = cache_swa_k[p].reshape(b, rows_c, D_KV)
                vc = cache_swa_v[p].reshape(b, rows_c, D_KV)
                x = _even_mix_sample(swa_sinks[p], u, q, k, v, x, pool_prev, kc, vc, wp, ps, wo, l, pos0)
                pools.append(jnp.concatenate([cache_pool[p], u.reshape(b, l, D_A)], axis=1)[:, -POOL_STATE:])
                swa_ks.append(jnp.concatenate([cache_swa_k[p], k.reshape(b, l, HKV_B, HEAD_DIM)], axis=1)[:, -rows_c:])
                swa_vs.append(jnp.concatenate([cache_swa_v[p], v.reshape(b, l, HKV_B, HEAD_DIM)], axis=1)[:, -rows_c:])
        else:
            lambda_init = 0.8 - 0.6 * math.exp(-0.3 * layer)
            wg = jnp.pad(gla_w_gate2[p], ((0, LANES - GATE_RANK), (0, 0))).astype(BF16)
            bg = gla_b_gate[p].reshape(1, D_CQK)
            lp = diff_lambda[p]
            dnw = diff_norm_w[p].reshape(1, 2 * HD_D)
            w_in = _pad_odd_w_in(odd_w_in[p])
            if prompt:
                qd, kd, el, cv, cg, dq, *kv_p = _odd_in(x, nw, w_in, wg, bg, tabs, tm, blk,
                                                        pair=(p, depth // 2, b, l, kv_p))
                yd = _diff_prompt(lp, dnw, dq, kv_p[0], kv_p[1], p, b, l, lambda_init, min(ATTN_TILE, l))
                s0 = jnp.zeros((b, H_C, DV_C, DK_C), F32)
            else:
                qd, kd, el, cv, cg, dq, dk, dv = _odd_in(x, nw, w_in, wg, bg, tabs, tm, blk)
                past = cache_diff_k.shape[2]
                kct = jnp.transpose(cache_diff_k, (0, 1, 3, 4, 5, 2)).reshape(depth // 2, b, H_D, 2 * HD_D, past)
                vc = cache_diff_v.reshape(depth // 2, b * past * H_D, 2 * HD_D)
                yd = _diff_sample(lp, dnw, dq, dk, dv, kct, vc, p, b, l, lambda_init)
                s0 = jnp.swapaxes(state_gla[p], -1, -2).astype(F32)
                diff_ks.append(dk.reshape(b, l, H_D, 2, HD_D))
                diff_vs.append(dv.reshape(b, l, H_D, 2 * HD_D))
            x, s_new = _odd_mix(qd, kd, el, cv, cg, yd, x, s0, gla_norm_w[p].reshape(1, DV_C),
                                odd_w_out[p].astype(BF16), tm, blk, l, not prompt)
            glas.append(jnp.swapaxes(s_new, -1, -2))
        if prompt:
            tf = seg = min(FFN_ROW_TILE, l)
            prev = jnp.zeros((r // tf, CONV_W - 1, D_FF), F32)
        else:
            tf, seg = tm, l
            prev = cache_ffn_conv[layer]
        x, tails = _ffn(x, norm_ffn_w[layer].reshape(1, D_MODEL), ffn_w_up[layer].astype(BF16), ffn_conv_w[layer],
                        ffn_conv_b[layer].reshape(1, D_FF), ffn_w_down[layer].astype(BF16), prev,
                        final_norm_w.reshape(1, D_MODEL), tf, seg, l, not prompt, layer == depth - 1)
        tails = tails.reshape(b, -1, CONV_TAIL, D_FF)
        ffns.append(tails[:, -1, CONV_TAIL - (CONV_W - 1):])
    if prompt:
        n_pairs = depth // 2
        diff_k = jnp.transpose(kv_p[0].reshape(n_pairs, b, H_D, 2, HD_D, l), (0, 1, 5, 2, 3, 4))
        diff_v = kv_p[1].reshape(n_pairs, b, l, H_D, 2 * HD_D)
    else:
        diff_k, diff_v = jnp.stack(diff_ks), jnp.stack(diff_vs)
    return (x.reshape(b, l, D_MODEL), jnp.stack(pools), jnp.stack(swa_ks), jnp.stack(swa_vs), jnp.stack(glas),
            diff_k, diff_v, jnp.stack(ffns))


def kernel(x_prompt, x_sample, cache_pool, cache_swa_k, cache_swa_v, state_gla, cache_diff_k, cache_diff_v, cache_ffn_conv, norm_mix_w, norm_ffn_w, final_norm_w, even_w_in, pool_w, pool_scale, swa_sinks, even_w_out, odd_w_in, gla_w_gate2, gla_b_gate, gla_norm_w, diff_lambda, diff_norm_w, odd_w_out, ffn_w_up, ffn_conv_w, ffn_conv_b, ffn_w_down):
    weights = (norm_mix_w, norm_ffn_w, final_norm_w, even_w_in, pool_w, pool_scale, swa_sinks, even_w_out, odd_w_in,
               gla_w_gate2, gla_b_gate, gla_norm_w, diff_lambda, diff_norm_w, odd_w_out, ffn_w_up, ffn_conv_w,
               ffn_conv_b, ffn_w_down)
    caches = (cache_pool, cache_swa_k, cache_swa_v, state_gla, cache_diff_k, cache_diff_v, cache_ffn_conv)
    past = cache_diff_k.shape[2]
    pos_p = jnp.arange(x_prompt.shape[1])
    pos_s = past + jnp.arange(x_sample.shape[1])
    outs_p = _trunk(x_prompt, pos_p, True, caches, weights)
    outs_s = _trunk(x_sample, pos_s, False, caches, weights)
    return (outs_p[0], outs_s[0]) + tuple(outs_p[1:]) + tuple(outs_s[1:])
```

```python
import functools
import math

import jax
import jax.numpy as jnp
from jax import lax
from jax.experimental import pallas as pl
from jax.experimental.pallas import tpu as pltpu

F32 = jnp.float32
BF16 = jnp.bfloat16

D_MODEL = 1024
CHUNK = 64
EPS = 1e-6
NEG_INF = -1e30
LOG2E = 1.0 / math.log(2.0)
ROPE_THETA = 500000.0
HEAD_DIM = 64
ROT_DIM = HEAD_DIM // 4
ROT_HALF = ROT_DIM // 2
LANES = 128

POOL_WINDOWS = (2, 4, 8, 16)
POOL_GROUP = D_MODEL // 8
D_A = len(POOL_WINDOWS) * POOL_GROUP
POOL_STATE = max(POOL_WINDOWS) - 1
POOL_HALO = POOL_STATE + 1

H_B = D_MODEL // 128
HKV_B = 2
GQA_B = H_B // HKV_B
WINDOW = 128
D_B = H_B * HEAD_DIM
D_KV = HKV_B * HEAD_DIM

H_C = 4
DK_C = 64
DV_C = 128
GATE_RANK = 16
GATE_NORM = 16.0
D_CQK = H_C * DK_C
D_C = H_C * DV_C

H_D = 4
HD_D = 64
D_D = H_D * 2 * HD_D

D_FF = 2816
CONV_W = 3
CONV_TAIL = 8

ROW_TILE = 1024
FFN_ROW_TILE = 1024
ATTN_TILE = 1024
FFN_COLS = 256
VMEM_LIMIT = 60 * 1024 * 1024


def _dot(a, b):
    return jnp.dot(a, b, preferred_element_type=F32)


def _dot_nt(a, b):
    return lax.dot_general(a, b, (((1,), (1,)), ((), ())), preferred_element_type=F32)


def _dot_tn(a, b):
    return lax.dot_general(a, b, (((0,), (0,)), ((), ())), preferred_element_type=F32)


def _rms(x, w):
    return x * lax.rsqrt(jnp.mean(x * x, axis=-1, keepdims=True) + EPS) * w


def _rope(z, cos_t, sin_a, sin_b):
    outs = []
    for s in range(z.shape[1] // LANES):
        x = z[:, s * LANES:(s + 1) * LANES]
        outs.append(x * cos_t + pltpu.roll(x, LANES - ROT_HALF, 1) * sin_a + pltpu.roll(x, ROT_HALF, 1) * sin_b)
    return outs[0] if len(outs) == 1 else jnp.concatenate(outs, axis=-1)


def _rope_tables(pos):
    t = pos.shape[0]
    inv = jnp.exp(-math.log(ROPE_THETA) * jnp.arange(ROT_HALF, dtype=F32) * (2.0 / ROT_DIM))
    ang = pos.astype(F32)[:, None] * inv[None, :]
    cos, sin = jnp.cos(ang), jnp.sin(ang)
    rest = HEAD_DIM - ROT_DIM
    cos_t = jnp.concatenate([cos, cos, jnp.ones((t, rest), F32)], axis=-1)
    sin_a = jnp.concatenate([-sin, jnp.zeros((t, HEAD_DIM - ROT_HALF), F32)], axis=-1)
    sin_b = jnp.concatenate([jnp.zeros((t, ROT_HALF), F32), sin, jnp.zeros((t, rest), F32)], axis=-1)
    rep = LANES // HEAD_DIM
    return tuple(jnp.tile(a, (1, rep)) for a in (cos_t, sin_a, sin_b))


def _params(n_axes=1):
    return pltpu.CompilerParams(dimension_semantics=("arbitrary",) * n_axes, vmem_limit_bytes=VMEM_LIMIT)


def _full(shape):
    nd = len(shape)
    return pl.BlockSpec(shape, lambda *_: (0,) * nd, pipeline_mode=pl.Buffered(1))


def _rows(tm, cols):
    return pl.BlockSpec((tm, cols), lambda i: (i, 0))


def _even_in_kernel(x_ref, nw_ref, w_ref, cos_ref, sa_ref, sb_ref, u_ref, q_ref, k_ref, v_ref, *kt_ref):
    h = _rms(x_ref[...], nw_ref[...]).astype(BF16)
    cos_t, sin_a, sin_b = cos_ref[...], sa_ref[...], sb_ref[...]
    q = _dot(h, w_ref[:, D_A:D_A + D_B])
    k = _dot(h, w_ref[:, D_A + D_B:D_A + D_B + D_KV])
    v_ref[...] = _dot(h, w_ref[:, D_A + D_B + D_KV:D_A + D_B + 2 * D_KV])
    q_ref[...] = (_rope(q, cos_t, sin_a, sin_b) * (HEAD_DIM ** -0.5)).astype(BF16)
    k = _rope(k, cos_t, sin_a, sin_b)
    k_ref[...] = k
    if kt_ref:
        kt_ref[0][...] = k.T.astype(BF16)
    u_ref[...] = _dot(h, w_ref[:, 0:D_A])


def _even_in(x, nw, w, tabs, tm, transposed_keys):
    r = x.shape[0]
    t_tiles = tabs[0].shape[0] // tm
    tab_spec = pl.BlockSpec((tm, LANES), lambda i: (i % t_tiles, 0))
    n_in = D_A + D_B + 2 * D_KV
    out_specs = [_rows(tm, D_A), _rows(tm, D_B), _rows(tm, D_KV), _rows(tm, D_KV)]
    out_shape = [jax.ShapeDtypeStruct((r, D_A), F32), jax.ShapeDtypeStruct((r, D_B), BF16),
                 jax.ShapeDtypeStruct((r, D_KV), F32), jax.ShapeDtypeStruct((r, D_KV), F32)]
    if transposed_keys:
        out_specs.append(pl.BlockSpec((D_KV, tm), lambda i: (0, i)))
        out_shape.append(jax.ShapeDtypeStruct((D_KV, r), BF16))
    return pl.pallas_call(
        _even_in_kernel,
        grid=(r // tm,),
        in_specs=[_rows(tm, D_MODEL), _full((1, D_MODEL)), _full((D_MODEL, n_in)), tab_spec, tab_spec, tab_spec],
        out_specs=out_specs,
        out_shape=out_shape,
        compiler_params=_params(),
        name="even_in",
    )(x, nw, w, *tabs)


def _pool_mix(ext, u, pos, wp_ref, ps, ycat_ref, row0, rows):
    for g, w in enumerate(POOL_WINDOWS):
        lo, hi = g * POOL_GROUP, (g + 1) * POOL_GROUP
        s = ext[:, lo:hi]
        sh = 1
        while sh < w:
            s = s + pltpu.roll(s, sh, 0)
            sh *= 2
        count = jnp.minimum(w, pos + 1).astype(F32)
        diff = s[POOL_HALO:] / count - u[:, lo:hi]
        ya = _dot(diff.astype(BF16), wp_ref[g]) * ps[:, lo:hi]
        ycat_ref[row0:row0 + rows, lo:hi] = ya.astype(BF16)


def _sink_column(sink_ref, hk, rows_per_head):
    n = GQA_B * rows_per_head
    grp = lax.broadcasted_iota(jnp.int32, (n, 1), 0) // rows_per_head
    col = jnp.full((n, 1), sink_ref[hk * GQA_B], F32)
    for g in range(1, GQA_B):
        col = jnp.where(grp == g, sink_ref[hk * GQA_B + g], col)
    return col


def _sink_attend(s, sink_col, vb):
    m = jnp.maximum(jnp.max(s, axis=-1, keepdims=True), sink_col)
    e = jnp.exp(s - m)
    den = jnp.sum(e, axis=-1, keepdims=True) + jnp.exp(sink_col - m)
    return _dot(e.astype(BF16), vb) / den


def _store_heads(ycat_ref, o, row0, rows, hk):
    for pair in range(GQA_B // 2):
        both = jnp.concatenate([o[(2 * pair) * rows:(2 * pair + 1) * rows],
                                o[(2 * pair + 1) * rows:(2 * pair + 2) * rows]], axis=-1)
        c0 = D_A + (hk * GQA_B + 2 * pair) * HEAD_DIM
        ycat_ref[row0:row0 + rows, c0:c0 + 2 * HEAD_DIM] = both.astype(BF16)


def _even_mix_prompt_kernel(sink_ref, u_ref, q_ref, k_ref, v_ref, x_ref, wp_ref, ps_ref, wo_ref, o_ref,
                            ucar, kcar, vcar, ycat, *, tm, tiles_per_seq):
    i = pl.program_id(0)
    ti = i % tiles_per_seq
    first_key_chunk = jnp.where(ti == 0, WINDOW // CHUNK, 0)

    @pl.when(ti == 0)
    def _():
        ucar[...] = jnp.zeros_like(ucar)
        kcar[...] = jnp.zeros_like(kcar)
        vcar[...] = jnp.zeros_like(vcar)

    u = u_ref[...]
    ext = jnp.concatenate([ucar[...], u], axis=0)
    pos = ti * tm + lax.broadcasted_iota(jnp.int32, (tm, POOL_GROUP), 0)
    _pool_mix(ext, u, pos, wp_ref, ps_ref[...], ycat, 0, tm)
    ucar[...] = u[tm - POOL_HALO:]

    kext = jnp.concatenate([kcar[...], k_ref[...]], axis=1)
    vext = jnp.concatenate([vcar[...], v_ref[...].astype(BF16)], axis=0)
    kcar[...] = kext[:, tm:]
    vcar[...] = vext[tm:]
    qb = 2 * CHUNK
    kb = qb + WINDOW
    nrow = GQA_B * qb
    qc = (lax.broadcasted_iota(jnp.int32, (nrow, kb), 0) % qb) // CHUNK
    kc = lax.broadcasted_iota(jnp.int32, (nrow, kb), 1) // CHUNK
    band = (kc >= qc) & (kc <= qc + WINDOW // CHUNK)
    first = band & (kc >= first_key_chunk)
    grp = lax.broadcasted_iota(jnp.int32, (nrow, LANES), 0) // qb
    low_lanes = lax.broadcasted_iota(jnp.int32, (qb, LANES), 1) < HEAD_DIM
    ones = jnp.ones((WINDOW + tm, HEAD_DIM), BF16)
    zeros = jnp.zeros((HEAD_DIM, WINDOW + tm), BF16)
    order = [g for g in range(GQA_B) if g % 2 == 0] + [g for g in range(GQA_B) if g % 2 == 1]
    v_aug, sink_rep, k_even, k_odd = [], [], [], []
    for hk in range(HKV_B):
        v_aug.append(jnp.concatenate([vext[:, hk * HEAD_DIM:(hk + 1) * HEAD_DIM], ones], axis=-1))
        kt = kext[hk * HEAD_DIM:(hk + 1) * HEAD_DIM]
        k_even.append(jnp.concatenate([kt, zeros], axis=0))
        k_odd.append(jnp.concatenate([zeros, kt], axis=0))
        rep = jnp.full((nrow, LANES), sink_ref[hk * GQA_B + order[0]], F32)
        for r in range(1, GQA_B):
            rep = jnp.where(grp == r, sink_ref[hk * GQA_B + order[r]], rep)
        sink_rep.append(rep)

    def scores(a, hk):
        def tiles(gs):
            return jnp.concatenate(
                [q_ref[a * qb:(a + 1) * qb, (hk * GQA_B + g) // 2 * LANES:((hk * GQA_B + g) // 2 + 1) * LANES]
                 for g in gs], axis=0)
        half = GQA_B // 2
        return jnp.concatenate([_dot(tiles(order[:half]), k_even[hk][:, a * qb:a * qb + kb]),
                                _dot(tiles(order[half:]), k_odd[hk][:, a * qb:a * qb + kb])], axis=0)

    def softmax(a, hk, s):
        s = jnp.where(first if a == 0 else band, s, NEG_INF)
        m = jnp.maximum(jnp.max(s, axis=-1, keepdims=True), sink_rep[hk])
        return jnp.exp(s - _lane_tile(m, kb // LANES)).astype(BF16), jnp.exp(sink_rep[hk] - m)

    def finish(a, hk, e, sink_term):
        oa = _dot(e, v_aug[hk][a * qb:a * qb + kb])
        inv = 1.0 / (oa + sink_term)
        for pair in range(GQA_B // 2):
            lo = slice(pair * qb, (pair + 1) * qb)
            hi = slice((GQA_B // 2 + pair) * qb, (GQA_B // 2 + pair + 1) * qb)
            both = jnp.where(low_lanes, oa[lo] * pltpu.roll(inv[lo], HEAD_DIM, 1),
                             pltpu.roll(oa[hi], HEAD_DIM, 1) * inv[hi])
            c0 = D_A + (hk * GQA_B + 2 * pair) * HEAD_DIM
            ycat[a * qb:(a + 1) * qb, c0:c0 + 2 * HEAD_DIM] = both.astype(BF16)

    blocks = [(a, hk) for a in range(tm // qb) for hk in range(HKV_B)]
    nb = len(blocks)
    s_q = {0: scores(*blocks[0])}
    e_q = {}
    for n in range(nb + 2):
        if n + 1 < nb:
            s_q[n + 1] = scores(*blocks[n + 1])
        if n < nb:
            e_q[n] = softmax(*blocks[n], s_q.pop(n))
        if 1 <= n <= nb:
            finish(*blocks[n - 1], *e_q.pop(n - 1))
    o_ref[...] = x_ref[...] + _dot(ycat[...], wo_ref[...])


def _even_mix_prompt(sinks, u, q, kt, v, x, wp, ps, wo, seq, tm):
    r = x.shape[0]
    kern = functools.partial(_even_mix_prompt_kernel, tm=tm, tiles_per_seq=seq // tm)
    return pl.pallas_call(
        kern,
        grid=(r // tm,),
        in_specs=[pl.BlockSpec(memory_space=pltpu.SMEM),
                  _rows(tm, D_A), _rows(tm, D_B), pl.BlockSpec((D_KV, tm), lambda i: (0, i)), _rows(tm, D_KV),
                  _rows(tm, D_MODEL), _full(wp.shape), _full((1, D_A)), _full((D_A + D_B, D_MODEL))],
        out_specs=_rows(tm, D_MODEL),
        out_shape=jax.ShapeDtypeStruct((r, D_MODEL), F32),
        scratch_shapes=[pltpu.VMEM((POOL_HALO, D_A), F32), pltpu.VMEM((D_KV, WINDOW), BF16),
                        pltpu.VMEM((WINDOW, D_KV), BF16), pltpu.VMEM((tm, D_A + D_B), BF16)],
        compiler_params=_params(),
        name="even_mix_prompt",
    )(sinks, u, q, kt, v, x, wp, ps, wo)


def _even_mix_sample_kernel(sink_ref, u_ref, q_ref, k_ref, v_ref, x_ref, pp_ref, kc_ref, vc_ref, wp_ref, ps_ref,
                            wo_ref, o_ref, ycat, *, l, pos0):
    u = u_ref[...]
    ext = jnp.concatenate([pp_ref[0], u], axis=0)
    pos = pos0 + lax.broadcasted_iota(jnp.int32, (l, 1), 0)
    _pool_mix(ext, u, pos, wp_ref, ps_ref[...], ycat, 0, l)

    pad = jnp.zeros((LANES - l, D_KV), BF16)
    kall = jnp.concatenate([kc_ref[0].astype(BF16), k_ref[...].astype(BF16), pad], axis=0)
    vall = jnp.concatenate([vc_ref[0].astype(BF16), v_ref[...].astype(BF16), pad], axis=0)
    nk = kall.shape[0]
    valid = lax.broadcasted_iota(jnp.int32, (GQA_B * l, nk), 1) < nk - (LANES - l)
    for hk in range(HKV_B):
        ks = kall[:, hk * HEAD_DIM:(hk + 1) * HEAD_DIM]
        vs = vall[:, hk * HEAD_DIM:(hk + 1) * HEAD_DIM]
        qs = jnp.concatenate(
            [q_ref[:, (hk * GQA_B + g) * HEAD_DIM:(hk * GQA_B + g + 1) * HEAD_DIM] for g in range(GQA_B)], axis=0)
        s = jnp.where(valid, _dot_nt(qs, ks), NEG_INF)
        o = _sink_attend(s, _sink_column(sink_ref, hk, l), vs)
        _store_heads(ycat, o, 0, l, hk)
    o_ref[...] = x_ref[...] + _dot(ycat[...], wo_ref[...])


def _even_mix_sample(sinks, u, q, k, v, x, pool_prev, kc, vc, wp, ps, wo, l, pos0):
    r = x.shape[0]
    rows_c = kc.shape[1]
    kern = functools.partial(_even_mix_sample_kernel, l=l, pos0=pos0)
    cache_spec = pl.BlockSpec((1, rows_c, D_KV), lambda i: (i, 0, 0))
    return pl.pallas_call(
        kern,
        grid=(r // l,),
        in_specs=[pl.BlockSpec(memory_space=pltpu.SMEM),
                  _rows(l, D_A), _rows(l, D_B), _rows(l, D_KV), _rows(l, D_KV), _rows(l, D_MODEL),
                  pl.BlockSpec((1, POOL_HALO, D_A), lambda i: (i, 0, 0)), cache_spec, cache_spec,
                  _full(wp.shape), _full((1, D_A)), _full((D_A + D_B, D_MODEL))],
        out_specs=_rows(l, D_MODEL),
        out_shape=jax.ShapeDtypeStruct((r, D_MODEL), F32),
        scratch_shapes=[pltpu.VMEM((l, D_A + D_B), BF16)],
        compiler_params=_params(),
        name="even_mix_sample",
    )(sinks, u, q, k, v, x, pool_prev, kc, vc, wp, ps, wo)


ODD_CR0 = 2 * D_CQK + 2 * D_C
ODD_DQ0 = ODD_CR0 + LANES
ODD_IN_PADDED = ODD_DQ0 + 3 * D_D


def _odd_in_kernel(x_ref, nw_ref, w_ref, wg_ref, bg_ref, cos_ref, sa_ref, sb_ref, *rest, tm, blk, cache_layout,
                   aliased):
    if aliased:
        rest = rest[2:]
    qd_ref, kd_ref, el_ref, cv_ref, cg_ref, dq_ref, dk_ref, dv_ref, cum_scr = rest
    h = _rms(x_ref[...], nw_ref[...]).astype(BF16)
    cos_t, sin_a, sin_b = cos_ref[...], sa_ref[...], sb_ref[...]
    cr = _dot(h, w_ref[:, ODD_CR0:ODD_DQ0]).astype(BF16)
    gate = _dot(cr, wg_ref[...]) + bg_ref[...]
    cq = _dot(h, w_ref[:, 0:D_CQK])
    ck = _dot(h, w_ref[:, D_CQK:2 * D_CQK])
    dq = _dot(h, w_ref[:, ODD_DQ0:ODD_DQ0 + D_D])
    cum = (jnp.minimum(gate, 0.0) - jnp.log1p(jnp.exp(-jnp.abs(gate)))) * (1.0 / GATE_NORM)
    rowb = lax.broadcasted_iota(jnp.int32, (tm, 1), 0) % blk
    sh = 1
    while sh < blk:
        cum = cum + jnp.where(rowb >= sh, pltpu.roll(cum, sh, 0), 0.0)
        sh *= 2
    for s in range(D_CQK // LANES):
        cum_scr[s] = cum[:, s * LANES:(s + 1) * LANES]
        el_ref[:, s * LANES:(s + 1) * LANES] = jnp.exp(cum_scr[s, pl.ds(blk - 1, tm // blk, stride=blk), :])
    qd_ref[...] = (cq * (DK_C ** -0.5) * jnp.exp(cum)).astype(BF16)
    kd_ref[...] = (ck * jnp.exp(-cum)).astype(BF16)
    dk = _dot(h, w_ref[:, ODD_DQ0 + D_D:ODD_DQ0 + 2 * D_D])
    dq_ref[...] = (_rope(dq, cos_t, sin_a, sin_b) * (HD_D ** -0.5 * LOG2E)).astype(BF16)
    dv = _dot(h, w_ref[:, ODD_DQ0 + 2 * D_D:ODD_DQ0 + 3 * D_D])
    dk = _rope(dk, cos_t, sin_a, sin_b)
    cv_ref[...] = _dot(h, w_ref[:, 2 * D_CQK:2 * D_CQK + D_C]).astype(BF16)
    cg_ref[...] = _dot(h, w_ref[:, 2 * D_CQK + D_C:ODD_CR0])
    if cache_layout:
        for hd in range(H_D):
            lo, hi = hd * 2 * HD_D, (hd + 1) * 2 * HD_D
            dk_ref[0, 0, hd] = dk[:, lo:hi].T
            dv_ref[0, pl.ds(hd, tm, stride=H_D), :] = dv[:, lo:hi]
    else:
        dk_ref[...] = dk
        dv_ref[...] = dv


def _odd_in(x, nw, w, wg, bg, tabs, tm, blk, pair=None):
    r = x.shape[0]
    t_tiles = tabs[0].shape[0] // tm
    tab_spec = pl.BlockSpec((tm, LANES), lambda i: (i % t_tiles, 0))
    nb = tm // blk
    sds = jax.ShapeDtypeStruct
    in_specs = [_rows(tm, D_MODEL), _full((1, D_MODEL)), _full((D_MODEL, ODD_IN_PADDED)),
                _full((LANES, D_CQK)), _full((1, D_CQK)), tab_spec, tab_spec, tab_spec]
    args = [x, nw, w, wg, bg, *tabs]
    aliases = {}
    if pair is None:
        kv_specs = [_rows(tm, D_D), _rows(tm, D_D)]
        kv_shapes = [sds((r, D_D), F32), sds((r, D_D), F32)]
    else:
        p, n_pairs, batch, seq, bufs = pair
        tps = seq // tm
        kv_specs = [pl.BlockSpec((1, 1, H_D, 2 * HD_D, tm), lambda i: (p, i // tps, 0, 0, i % tps)),
                    pl.BlockSpec((1, tm * H_D, 2 * HD_D), lambda i: (p, i, 0))]
        kv_shapes = [sds((n_pairs, batch, H_D, 2 * HD_D, seq), F32), sds((n_pairs, r * H_D, 2 * HD_D), F32)]
        if bufs is not None:
            in_specs += [pl.BlockSpec(memory_space=pl.ANY)] * 2
            args += list(bufs)
            aliases = {len(args) - 2: 6, len(args) - 1: 7}
    kern = functools.partial(_odd_in_kernel, tm=tm, blk=blk, cache_layout=pair is not None, aliased=bool(aliases))
    return pl.pallas_call(
        kern,
        grid=(r // tm,),
        in_specs=in_specs,
        out_specs=[_rows(tm, D_CQK), _rows(tm, D_CQK), _rows(nb, D_CQK), _rows(tm, D_C), _rows(tm, D_C),
                   _rows(tm, D_D)] + kv_specs,
        out_shape=[sds((r, D_CQK), BF16), sds((r, D_CQK), BF16), sds((r // blk, D_CQK), F32),
                   sds((r, D_C), BF16), sds((r, D_C), F32), sds((r, D_D), BF16)] + kv_shapes,
        scratch_shapes=[pltpu.VMEM((D_CQK // LANES, tm, LANES), F32)],
        input_output_aliases=aliases,
        compiler_params=_params(),
        name="odd_in",
    )(*args)


def _lambda_value(lp_ref, lambda_init):
    lp = lp_ref[...]
    s1 = jnp.sum(lp[0:1] * lp[1:2], axis=-1, keepdims=True)
    s2 = jnp.sum(lp[2:3] * lp[3:4], axis=-1, keepdims=True)
    return jnp.exp(s1) - jnp.exp(s2) + lambda_init


def _diff_finish(acc, den, lam, nw, lambda_init):
    o = acc[0] / den[0] - lam * (acc[1] / den[1])
    return (_rms(o, nw) * (1.0 - lambda_init)).astype(BF16)


def _lane_tile(x, n):
    return x if n == 1 else jnp.concatenate([x] * n, axis=1)


def _diff_prompt_kernel(lp_ref, nw_ref, q_ref, k_ref, v_ref, o_ref, kbf, vbf, m_scr, acc_scr, s_a, s_b, bias,
                        *, tq, lambda_init):
    hd = pl.program_id(1)
    qi = pl.program_id(2)
    dv = 2 * HD_D
    seq = vbf.shape[0]

    @pl.when(qi == 0)
    def _():
        kbf[...] = k_ref[0, 0, 0].astype(BF16)
        vbf[:, :dv] = v_ref[0, pl.ds(hd, seq, stride=H_D), :].astype(BF16)
        vbf[:, dv:] = jnp.ones((seq, dv), BF16)

    tk = tq // 2

    @pl.when((pl.program_id(0) == 0) & (pl.program_id(1) == 0) & (qi == 0))
    def _():
        qc = lax.broadcasted_iota(jnp.int32, (tk, tk), 0) // CHUNK
        kc = lax.broadcasted_iota(jnp.int32, (tk, tk), 1) // CHUNK
        bias[...] = jnp.where(kc <= qc, 0.0, NEG_INF)

    m_scr[...] = jnp.full(m_scr.shape, NEG_INF, F32)
    acc_scr[...] = jnp.zeros(acc_scr.shape, F32)

    def scores(t, s_buf, row0=0):
        start = pl.multiple_of(t * tk, tk)
        for c in range(2):
            s_buf[c, row0:, :] = _dot(q_ref[row0:, c * HD_D:(c + 1) * HD_D],
                                      kbf[c * HD_D:(c + 1) * HD_D, pl.ds(start, tk)])

    def attend(t, s_buf, row0=0, rows=tq, masked=False):
        vs = vbf[pl.ds(pl.multiple_of(t * tk, tk), tk), :]
        for c in range(2):
            sc = s_buf[c, row0:row0 + rows, :]
            if masked:
                sc = sc + bias[...]
            m_old = m_scr[c, row0:row0 + rows, :]
            m_new = jnp.maximum(m_old, jnp.max(sc, axis=-1, keepdims=True))
            p = jnp.exp2(sc - _lane_tile(m_new, tk // LANES)).astype(BF16)
            alpha = _lane_tile(jnp.exp2(m_old - m_new), 2 * dv // LANES)
            acc_scr[c, row0:row0 + rows, :] = alpha * acc_scr[c, row0:row0 + rows, :] + _dot(p, vs)
            m_scr[c, row0:row0 + rows, :] = m_new

    scores(0, s_a)

    def pair(j, _):
        t = 2 * j
        scores(t + 1, s_b)
        attend(t, s_a)
        scores(t + 2, s_a)
        attend(t + 1, s_b)
        return 0

    lax.fori_loop(0, qi, pair, 0)
    scores(2 * qi + 1, s_b, row0=tk)
    attend(2 * qi, s_a, row0=0, rows=tk, masked=True)
    attend(2 * qi, s_a, row0=tk, rows=tk)
    attend(2 * qi + 1, s_b, row0=tk, rows=tk, masked=True)

    lam = _lambda_value(lp_ref, lambda_init)
    acc = [acc_scr[c] for c in range(2)]
    o_ref[...] = _diff_finish([a[:, :dv] for a in acc], [a[:, dv:] for a in acc], lam, nw_ref[...], lambda_init)


def _diff_prompt(lp, nw, q, kt, v, p, batch, seq, lambda_init, tq):
    r = q.shape[0]
    nq = seq // tq
    kern = functools.partial(_diff_prompt_kernel, tq=tq, lambda_init=lambda_init)
    k_spec = pl.BlockSpec((1, 1, 1, 2 * HD_D, seq), lambda b, h, i: (p, b, h, 0, 0))
    v_spec = pl.BlockSpec((1, seq * H_D, 2 * HD_D), lambda b, h, i: (p, b, 0))
    q_spec = pl.BlockSpec((tq, 2 * HD_D), lambda b, h, i: (b * nq + i, h))
    return pl.pallas_call(
        kern,
        grid=(batch, H_D, nq),
        in_specs=[pl.BlockSpec((4, HD_D), lambda b, h, i: (0, 0)), pl.BlockSpec((1, 2 * HD_D), lambda b, h, i: (0, 0)),
                  q_spec, k_spec, v_spec],
        out_specs=q_spec,
        out_shape=jax.ShapeDtypeStruct((r, D_D), BF16),
        scratch_shapes=[pltpu.VMEM((2 * HD_D, seq), BF16), pltpu.VMEM((seq, 4 * HD_D), BF16),
                        pltpu.VMEM((2, tq, LANES), F32), pltpu.VMEM((2, tq, 4 * HD_D), F32),
                        pltpu.VMEM((2, tq, tq // 2), F32), pltpu.VMEM((2, tq, tq // 2), F32),
                        pltpu.VMEM((tq // 2, tq // 2), F32)],
        compiler_params=_params(3),
        name="diff_prompt",
    )(lp, nw, q, kt, v)


def _diff_sample_kernel(lp_ref, nw_ref, q_ref, k_ref, v_ref, kc_ref, vc_ref, o_ref, *, l, past, lambda_init):
    hd = pl.program_id(1)
    pad = jnp.zeros((LANES - l, 2 * HD_D), BF16)
    knew = jnp.concatenate([k_ref[...].astype(BF16), pad], axis=0)
    vnew = jnp.concatenate([v_ref[...].astype(BF16), pad], axis=0)
    kct = kc_ref[0, 0, 0].astype(BF16)
    vcache = vc_ref[0, pl.ds(hd, past, stride=H_D), :].astype(BF16)
    valid = lax.broadcasted_iota(jnp.int32, (l, LANES), 1) < l
    acc, den = [], []
    for c in range(2):
        qc = q_ref[:, c * HD_D:(c + 1) * HD_D]
        s_old = _dot(qc, kct[c * HD_D:(c + 1) * HD_D, :])
        s_new = jnp.where(valid, _dot_nt(qc, knew[:, c * HD_D:(c + 1) * HD_D]), NEG_INF)
        m = jnp.maximum(jnp.max(s_old, axis=-1, keepdims=True), jnp.max(s_new, axis=-1, keepdims=True))
        p_old = jnp.exp2(s_old - m)
        p_new = jnp.exp2(s_new - m)
        den.append(jnp.sum(p_old, axis=-1, keepdims=True) + jnp.sum(p_new, axis=-1, keepdims=True))
        acc.append(_dot(p_old.astype(BF16), vcache) + _dot(p_new.astype(BF16), vnew))
    lam = _lambda_value(lp_ref, lambda_init)
    o_ref[...] = _diff_finish(acc, den, lam, nw_ref[...], lambda_init)


def _diff_sample(lp, nw, q, k, v, kct, vc, p, batch, l, lambda_init):
    r = q.shape[0]
    past = kct.shape[-1]
    kern = functools.partial(_diff_sample_kernel, l=l, past=past, lambda_init=lambda_init)
    new_spec = pl.BlockSpec((l, 2 * HD_D), lambda b, h: (b, h))
    cache_k_spec = pl.BlockSpec((1, 1, 1, 2 * HD_D, past), lambda b, h: (p, b, h, 0, 0))
    cache_v_spec = pl.BlockSpec((1, past * H_D, 2 * HD_D), lambda b, h: (p, b, 0))
    return pl.pallas_call(
        kern,
        grid=(batch, H_D),
        in_specs=[pl.BlockSpec((4, HD_D), lambda b, h: (0, 0)), pl.BlockSpec((1, 2 * HD_D), lambda b, h: (0, 0)),
                  new_spec, new_spec, new_spec, cache_k_spec, cache_v_spec],
        out_specs=new_spec,
        out_shape=jax.ShapeDtypeStruct((r, D_D), BF16),
        compiler_params=_params(2),
        name="diff_sample",
    )(lp, nw, q, k, v, kct, vc)


def _odd_mix_kernel(qd_ref, kd_ref, el_ref, cv_ref, cg_ref, yd_ref, x_ref, s0_ref, gw_ref, wo_ref,
                    o_ref, s_ref, state, og, ycat, *, tm, blk, tiles_per_seq, per_block_state):
    i = pl.program_id(0)
    if not per_block_state:
        @pl.when(i % tiles_per_seq == 0)
        def _():
            state[...] = jnp.zeros_like(state)

    tri = lax.broadcasted_iota(jnp.int32, (blk, blk), 1) <= lax.broadcasted_iota(jnp.int32, (blk, blk), 0)
    for n in range(tm // blk):
        r0 = n * blk
        el = el_ref[n:n + 1, :]
        for h in range(H_C):
            if per_block_state:
                st = s0_ref[n, h]
            else:
                st = state[h]
            qd = qd_ref[r0:r0 + blk, h * DK_C:(h + 1) * DK_C]
            kd = kd_ref[r0:r0 + blk, h * DK_C:(h + 1) * DK_C]
            vh = cv_ref[r0:r0 + blk, h * DV_C:(h + 1) * DV_C]
            att = jnp.where(tri, _dot_nt(qd, kd), 0.0).astype(BF16)
            og[r0:r0 + blk, h * DV_C:(h + 1) * DV_C] = _dot(att, vh) + _dot_nt(qd, st.astype(BF16))
            st = el[:, h * DK_C:(h + 1) * DK_C] * (st + _dot_tn(vh, kd))
            if per_block_state:
                s_ref[n, h] = st
            else:
                state[h] = st
    if not per_block_state:
        s_ref[0] = state[...]

    gw = gw_ref[...]
    for h in range(H_C):
        lo, hi = h * DV_C, (h + 1) * DV_C
        g = cg_ref[:, lo:hi]
        y = _rms(og[:, lo:hi], gw) * (g * (1.0 / (1.0 + jnp.exp(-g))))
        ycat[:, lo:hi] = y.astype(BF16)
    ycat[:, D_C:] = yd_ref[...]
    o_ref[...] = x_ref[...] + _dot(ycat[...], wo_ref[...])


def _odd_mix(qd, kd, el, cv, cg, yd, x, s0, gw, wo, tm, blk, seq, per_block_state):
    r = x.shape[0]
    nb = tm // blk
    tiles_per_seq = max(seq // tm, 1)
    kern = functools.partial(_odd_mix_kernel, tm=tm, blk=blk, tiles_per_seq=tiles_per_seq,
                             per_block_state=per_block_state)
    if per_block_state:
        n_state = r // blk
        state_spec = pl.BlockSpec((nb, H_C, DV_C, DK_C), lambda i: (i, 0, 0, 0))
    else:
        n_state = r // seq
        state_spec = pl.BlockSpec((1, H_C, DV_C, DK_C), lambda i: (i // tiles_per_seq, 0, 0, 0))
    return pl.pallas_call(
        kern,
        grid=(r // tm,),
        in_specs=[_rows(tm, D_CQK), _rows(tm, D_CQK), _rows(nb, D_CQK), _rows(tm, D_C), _rows(tm, D_C),
                  _rows(tm, D_D), _rows(tm, D_MODEL), state_spec, _full((1, DV_C)), _full((D_C + D_D, D_MODEL))],
        out_specs=[_rows(tm, D_MODEL), state_spec],
        out_shape=[jax.ShapeDtypeStruct((r, D_MODEL), F32),
                   jax.ShapeDtypeStruct((n_state, H_C, DV_C, DK_C), F32)],
        scratch_shapes=[pltpu.VMEM((H_C, DV_C, DK_C), F32), pltpu.VMEM((tm, D_C), F32),
                        pltpu.VMEM((tm, D_C + D_D), BF16)],
        compiler_params=_params(),
        name="odd_mix",
    )(qd, kd, el, cv, cg, yd, x, s0, gw, wo)


def _gelu_tanh(x):
    return 0.5 * x * (1.0 + jnp.tanh(math.sqrt(2.0 / math.pi) * (x + 0.044715 * (x * x * x))))


def _ffn_kernel(x_ref, nw_ref, wup_ref, cw_ref, cb_ref, wdn_ref, prev_ref, fnw_ref, o_ref, tail_ref, carry, act,
                *, tm, seg, tiles_per_seq, prev_from_input, final_norm):
    i = pl.program_id(0)
    x = x_ref[...]
    h = _rms(x, nw_ref[...]).astype(BF16)
    nseg = tm // seg
    row = lax.broadcasted_iota(jnp.int32, (seg, 1), 0)
    if not prev_from_input:
        @pl.when(i % tiles_per_seq == 0)
        def _():
            carry[...] = jnp.zeros_like(carry)

    def up(c0):
        return _dot(h, wup_ref[:, c0:c0 + FFN_COLS]), _dot(h, wup_ref[:, D_FF + c0:D_FF + c0 + FFN_COLS])

    nxt = up(0)
    for c0 in range(0, D_FF, FFN_COLS):
        c1 = c0 + FFN_COLS
        gate, val = nxt
        if c1 < D_FF:
            nxt = up(c1)
        w0, w1, w2 = cw_ref[0:1, c0:c1], cw_ref[1:2, c0:c1], cw_ref[2:3, c0:c1]
        convs = []
        for s in range(nseg):
            g = gate[s * seg:(s + 1) * seg]
            if prev_from_input:
                p0, p1 = prev_ref[s, 0:1, c0:c1], prev_ref[s, 1:2, c0:c1]
            else:
                p0 = carry[CONV_TAIL - 2:CONV_TAIL - 1, c0:c1]
                p1 = carry[CONV_TAIL - 1:CONV_TAIL, c0:c1]
            g1 = jnp.where(row == 0, p1, pltpu.roll(g, 1, 0))
            g2 = jnp.where(row == 0, p0, jnp.where(row == 1, p1, pltpu.roll(g, 2, 0)))
            convs.append(cb_ref[:, c0:c1] + g2 * w0 + g1 * w1 + g * w2)
            tail_ref[s, :, c0:c1] = g[seg - CONV_TAIL:]
        conv = convs[0] if nseg == 1 else jnp.concatenate(convs, axis=0)
        if not prev_from_input:
            carry[:, c0:c1] = gate[tm - CONV_TAIL:]
        act[:, c0:c1] = (_gelu_tanh(conv) * val).astype(BF16)
    y = x + _dot(act[...], wdn_ref[...])
    if final_norm:
        y = _rms(y, fnw_ref[...])
    o_ref[...] = y


def _ffn(x, nw, wup, cw, cb, wdn, prev, fnw, tm, seg, seq, prev_from_input, final_norm):
    r = x.shape[0]
    nseg = tm // seg
    kern = functools.partial(_ffn_kernel, tm=tm, seg=seg, tiles_per_seq=max(seq // tm, 1),
                             prev_from_input=prev_from_input, final_norm=final_norm)
    prev_spec = pl.BlockSpec((nseg, CONV_W - 1, D_FF), lambda i: (i, 0, 0))
    tail_spec = pl.BlockSpec((nseg, CONV_TAIL, D_FF), lambda i: (i, 0, 0))
    return pl.pallas_call(
        kern,
        grid=(r // tm,),
        in_specs=[_rows(tm, D_MODEL), _full((1, D_MODEL)), _full((D_MODEL, 2 * D_FF)), _full((CONV_W, D_FF)),
                  _full((1, D_FF)), _full((D_FF, D_MODEL)), prev_spec, _full((1, D_MODEL))],
        out_specs=[_rows(tm, D_MODEL), tail_spec],
        out_shape=[jax.ShapeDtypeStruct((r, D_MODEL), F32),
                   jax.ShapeDtypeStruct((r // seg, CONV_TAIL, D_FF), F32)],
        scratch_shapes=[pltpu.VMEM((CONV_TAIL, D_FF), F32), pltpu.VMEM((tm, D_FF), BF16)],
        compiler_params=_params(),
        name="ffn",
    )(x, nw, wup, cw, cb, wdn, prev, fnw)


def _pad_odd_w_in(w):
    cr = w[:, ODD_CR0:ODD_CR0 + GATE_RANK]
    return jnp.concatenate([w[:, :ODD_CR0], cr, jnp.zeros((D_MODEL, LANES - GATE_RANK), w.dtype),
                            w[:, ODD_CR0 + GATE_RANK:]], axis=1).astype(BF16)


def _trunk(x3, pos, prompt, caches, weights):
    (norm_mix_w, norm_ffn_w, final_norm_w, even_w_in, pool_w, pool_scale, swa_sinks, even_w_out, odd_w_in,
     gla_w_gate2, gla_b_gate, gla_norm_w, diff_lambda, diff_norm_w, odd_w_out, ffn_w_up, ffn_conv_w, ffn_conv_b,
     ffn_w_down) = weights
    cache_pool, cache_swa_k, cache_swa_v, state_gla, cache_diff_k, cache_diff_v, cache_ffn_conv = caches
    b, l, _ = x3.shape
    r = b * l
    depth = norm_mix_w.shape[0]
    x = x3.reshape(r, D_MODEL)
    if prompt:
        tm = min(ROW_TILE, l)
        blk = min(CHUNK, l)
        tabs = _rope_tables(pos)
    else:
        tm = r
        blk = min(CHUNK, l)
        tabs = tuple(jnp.tile(t, (b, 1)) for t in _rope_tables(pos))
    pos0 = 0 if prompt else cache_diff_k.shape[2]

    pools, swa_ks, swa_vs, glas, diff_ks, diff_vs, ffns = [], [], [], [], [], [], []
    kv_p = None
    for layer in range(depth):
        p = layer // 2
        nw = norm_mix_w[layer].reshape(1, D_MODEL)
        if layer % 2 == 0:
            u, q, k, v, *kt = _even_in(x, nw, even_w_in[p].astype(BF16), tabs, tm, prompt)
            wp = pool_w[p].astype(BF16)
            ps = pool_scale[p].reshape(1, D_A)
            wo = even_w_out[p].astype(BF16)
            if prompt:
                x = _even_mix_prompt(swa_sinks[p], u, q, kt[0], v, x, wp, ps, wo, l, tm)
                pools.append(u.reshape(b, l, D_A)[:, l - POOL_STATE:])
                swa_ks.append(k.reshape(b, l, HKV_B, HEAD_DIM)[:, l - WINDOW:])
                swa_vs.append(v.reshape(b, l, HKV_B, HEAD_DIM)[:, l - WINDOW:])
            else:
                rows_c = cache_swa_k.shape[2]
                pool_prev = jnp.pad(cache_pool[p], ((0, 0), (1, 0), (0, 0)))
                kc = cache_swa_k[p].reshape(b, rows_c, D_KV)
                vc = cache_swa_v[p].reshape(b, rows_c, D_KV)
                x = _even_mix_sample(swa_sinks[p], u, q, k, v, x, pool_prev, kc, vc, wp, ps, wo, l, pos0)
                pools.append(jnp.concatenate([cache_pool[p], u.reshape(b, l, D_A)], axis=1)[:, -POOL_STATE:])
                swa_ks.append(jnp.concatenate([cache_swa_k[p], k.reshape(b, l, HKV_B, HEAD_DIM)], axis=1)[:, -rows_c:])
                swa_vs.append(jnp.concatenate([cache_swa_v[p], v.reshape(b, l, HKV_B, HEAD_DIM)], axis=1)[:, -rows_c:])
        else:
            lambda_init = 0.8 - 0.6 * math.exp(-0.3 * layer)
            wg = jnp.pad(gla_w_gate2[p], ((0, LANES - GATE_RANK), (0, 0))).astype(BF16)
            bg = gla_b_gate[p].reshape(1, D_CQK)
            lp = diff_lambda[p]
            dnw = diff_norm_w[p].reshape(1, 2 * HD_D)
            w_in = _pad_odd_w_in(odd_w_in[p])
            if prompt:
                qd, kd, el, cv, cg, dq, *kv_p = _odd_in(x, nw, w_in, wg, bg, tabs, tm, blk,
                                                        pair=(p, depth // 2, b, l, kv_p))
                yd = _diff_prompt(lp, dnw, dq, kv_p[0], kv_p[1], p, b, l, lambda_init, min(ATTN_TILE, l))
                s0 = jnp.zeros((b, H_C, DV_C, DK_C), F32)
            else:
                qd, kd, el, cv, cg, dq, dk, dv = _odd_in(x, nw, w_in, wg, bg, tabs, tm, blk)
                past = cache_diff_k.shape[2]
                kct = jnp.transpose(cache_diff_k, (0, 1, 3, 4, 5, 2)).reshape(depth // 2, b, H_D, 2 * HD_D, past)
                vc = cache_diff_v.reshape(depth // 2, b * past * H_D, 2 * HD_D)
                yd = _diff_sample(lp, dnw, dq, dk, dv, kct, vc, p, b, l, lambda_init)
                s0 = jnp.swapaxes(state_gla[p], -1, -2).astype(F32)
                diff_ks.append(dk.reshape(b, l, H_D, 2, HD_D))
                diff_vs.append(dv.reshape(b, l, H_D, 2 * HD_D))
            x, s_new = _odd_mix(qd, kd, el, cv, cg, yd, x, s0, gla_norm_w[p].reshape(1, DV_C),
                                odd_w_out[p].astype(BF16), tm, blk, l, not prompt)
            glas.append(jnp.swapaxes(s_new, -1, -2))
        if prompt:
            tf = seg = min(FFN_ROW_TILE, l)
            prev = jnp.zeros((r // tf, CONV_W - 1, D_FF), F32)
        else:
            tf, seg = tm, l
            prev = cache_ffn_conv[layer]
        x, tails = _ffn(x, norm_ffn_w[layer].reshape(1, D_MODEL), ffn_w_up[layer].astype(BF16), ffn_conv_w[layer],
                        ffn_conv_b[layer].reshape(1, D_FF), ffn_w_down[layer].astype(BF16), prev,
                        final_norm_w.reshape(1, D_MODEL), tf, seg, l, not prompt, layer == depth - 1)
        tails = tails.reshape(b, -1, CONV_TAIL, D_FF)
        ffns.append(tails[:, -1, CONV_TAIL - (CONV_W - 1):])
    if prompt:
        n_pairs = depth // 2
        diff_k = jnp.transpose(kv_p[0].reshape(n_pairs, b, H_D, 2, HD_D, l), (0, 1, 5, 2, 3, 4))
        diff_v = kv_p[1].reshape(n_pairs, b, l, H_D, 2 * HD_D)
    else:
        diff_k, diff_v = jnp.stack(diff_ks), jnp.stack(diff_vs)
    return (x.reshape(b, l, D_MODEL), jnp.stack(pools), jnp.stack(swa_ks), jnp.stack(swa_vs), jnp.stack(glas),
            diff_k, diff_v, jnp.stack(ffns))


def kernel(x_prompt, x_sample, cache_pool, cache_swa_k, cache_swa_v, state_gla, cache_diff_k, cache_diff_v, cache_ffn_conv, norm_mix_w, norm_ffn_w, final_norm_w, even_w_in, pool_w, pool_scale, swa_sinks, even_w_out, odd_w_in, gla_w_gate2, gla_b_gate, gla_norm_w, diff_lambda, diff_norm_w, odd_w_out, ffn_w_up, ffn_conv_w, ffn_conv_b, ffn_w_down):
    weights = (norm_mix_w, norm_ffn_w, final_norm_w, even_w_in, pool_w, pool_scale, swa_sinks, even_w_out, odd_w_in,
               gla_w_gate2, gla_b_gate, gla_norm_w, diff_lambda, diff_norm_w, odd_w_out, ffn_w_up, ffn_conv_w,
               ffn_conv_b, ffn_w_down)
    caches = (cache_pool, cache_swa_k, cache_swa_v, state_gla, cache_diff_k, cache_diff_v, cache_ffn_conv)
    past = cache_diff_k.shape[2]
    pos_p = jnp.arange(x_prompt.shape[1])
    pos_s = past + jnp.arange(x_sample.shape[1])
    outs_p = _trunk(x_prompt, pos_p, True, caches, weights)
    outs_s = _trunk(x_sample, pos_s, False, caches, weights)
    return (outs_p[0], outs_s[0]) + tuple(outs_p[1:]) + tuple(outs_s[1:])
```

```python
import functools
import math

import jax
import jax.numpy as jnp
from jax import lax
from jax.experimental import pallas as pl
from jax.experimental.pallas import tpu as pltpu

F32 = jnp.float32
BF16 = jnp.bfloat16

D_MODEL = 1024
CHUNK = 64
EPS = 1e-6
NEG_INF = -1e30
LOG2E = 1.0 / math.log(2.0)
ROPE_THETA = 500000.0
HEAD_DIM = 64
ROT_DIM = HEAD_DIM // 4
ROT_HALF = ROT_DIM // 2
LANES = 128

POOL_WINDOWS = (2, 4, 8, 16)
POOL_GROUP = D_MODEL // 8
D_A = len(POOL_WINDOWS) * POOL_GROUP
POOL_STATE = max(POOL_WINDOWS) - 1
POOL_HALO = POOL_STATE + 1

H_B = D_MODEL // 128
HKV_B = 2
GQA_B = H_B // HKV_B
WINDOW = 128
D_B = H_B * HEAD_DIM
D_KV = HKV_B * HEAD_DIM

H_C = 4
DK_C = 64
DV_C = 128
GATE_RANK = 16
GATE_NORM = 16.0
D_CQK = H_C * DK_C
D_C = H_C * DV_C

H_D = 4
HD_D = 64
D_D = H_D * 2 * HD_D

D_FF = 2816
CONV_W = 3
CONV_TAIL = 8

ROW_TILE = 1024
FFN_ROW_TILE = 1024
ATTN_TILE = 1024
FFN_COLS = 256
VMEM_LIMIT = 60 * 1024 * 1024


def _dot(a, b):
    return jnp.dot(a, b, preferred_element_type=F32)


def _dot_nt(a, b):
    return lax.dot_general(a, b, (((1,), (1,)), ((), ())), preferred_element_type=F32)


def _dot_tn(a, b):
    return lax.dot_general(a, b, (((0,), (0,)), ((), ())), preferred_element_type=F32)


def _rms(x, w):
    return x * lax.rsqrt(jnp.mean(x * x, axis=-1, keepdims=True) + EPS) * w


def _rope(z, cos_t, sin_a, sin_b):
    outs = []
    for s in range(z.shape[1] // LANES):
        x = z[:, s * LANES:(s + 1) * LANES]
        outs.append(x * cos_t + pltpu.roll(x, LANES - ROT_HALF, 1) * sin_a + pltpu.roll(x, ROT_HALF, 1) * sin_b)
    return outs[0] if len(outs) == 1 else jnp.concatenate(outs, axis=-1)


def _rope_tables(pos):
    t = pos.shape[0]
    inv = jnp.exp(-math.log(ROPE_THETA) * jnp.arange(ROT_HALF, dtype=F32) * (2.0 / ROT_DIM))
    ang = pos.astype(F32)[:, None] * inv[None, :]
    cos, sin = jnp.cos(ang), jnp.sin(ang)
    rest = HEAD_DIM - ROT_DIM
    cos_t = jnp.concatenate([cos, cos, jnp.ones((t, rest), F32)], axis=-1)
    sin_a = jnp.concatenate([-sin, jnp.zeros((t, HEAD_DIM - ROT_HALF), F32)], axis=-1)
    sin_b = jnp.concatenate([jnp.zeros((t, ROT_HALF), F32), sin, jnp.zeros((t, rest), F32)], axis=-1)
    rep = LANES // HEAD_DIM
    return tuple(jnp.tile(a, (1, rep)) for a in (cos_t, sin_a, sin_b))


def _params(n_axes=1):
    return pltpu.CompilerParams(dimension_semantics=("arbitrary",) * n_axes, vmem_limit_bytes=VMEM_LIMIT)


def _full(shape):
    nd = len(shape)
    return pl.BlockSpec(shape, lambda *_: (0,) * nd, pipeline_mode=pl.Buffered(1))


def _rows(tm, cols):
    return pl.BlockSpec((tm, cols), lambda i: (i, 0))


def _even_in_kernel(x_ref, nw_ref, w_ref, cos_ref, sa_ref, sb_ref, u_ref, q_ref, k_ref, v_ref, *kt_ref):
    h = _rms(x_ref[...], nw_ref[...]).astype(BF16)
    cos_t, sin_a, sin_b = cos_ref[...], sa_ref[...], sb_ref[...]
    q = _dot(h, w_ref[:, D_A:D_A + D_B])
    k = _dot(h, w_ref[:, D_A + D_B:D_A + D_B + D_KV])
    v_ref[...] = _dot(h, w_ref[:, D_A + D_B + D_KV:D_A + D_B + 2 * D_KV])
    q_ref[...] = (_rope(q, cos_t, sin_a, sin_b) * (HEAD_DIM ** -0.5)).astype(BF16)
    k = _rope(k, cos_t, sin_a, sin_b)
    k_ref[...] = k
    if kt_ref:
        kt_ref[0][...] = k.T.astype(BF16)
    u_ref[...] = _dot(h, w_ref[:, 0:D_A])


def _even_in(x, nw, w, tabs, tm, transposed_keys):
    r = x.shape[0]
    t_tiles = tabs[0].shape[0] // tm
    tab_spec = pl.BlockSpec((tm, LANES), lambda i: (i % t_tiles, 0))
    n_in = D_A + D_B + 2 * D_KV
    out_specs = [_rows(tm, D_A), _rows(tm, D_B), _rows(tm, D_KV), _rows(tm, D_KV)]
    out_shape = [jax.ShapeDtypeStruct((r, D_A), F32), jax.ShapeDtypeStruct((r, D_B), BF16),
                 jax.ShapeDtypeStruct((r, D_KV), F32), jax.ShapeDtypeStruct((r, D_KV), F32)]
    if transposed_keys:
        out_specs.append(pl.BlockSpec((D_KV, tm), lambda i: (0, i)))
        out_shape.append(jax.ShapeDtypeStruct((D_KV, r), BF16))
    return pl.pallas_call(
        _even_in_kernel,
        grid=(r // tm,),
        in_specs=[_rows(tm, D_MODEL), _full((1, D_MODEL)), _full((D_MODEL, n_in)), tab_spec, tab_spec, tab_spec],
        out_specs=out_specs,
        out_shape=out_shape,
        compiler_params=_params(),
        name="even_in",
    )(x, nw, w, *tabs)


def _pool_mix(ext, u, pos, wp_ref, ps, ycat_ref, row0, rows):
    for g, w in enumerate(POOL_WINDOWS):
        lo, hi = g * POOL_GROUP, (g + 1) * POOL_GROUP
        s = ext[:, lo:hi]
        sh = 1
        while sh < w:
            s = s + pltpu.roll(s, sh, 0)
            sh *= 2
        count = jnp.minimum(w, pos + 1).astype(F32)
        diff = s[POOL_HALO:] / count - u[:, lo:hi]
        ya = _dot(diff.astype(BF16), wp_ref[g]) * ps[:, lo:hi]
        ycat_ref[row0:row0 + rows, lo:hi] = ya.astype(BF16)


def _sink_column(sink_ref, hk, rows_per_head):
    n = GQA_B * rows_per_head
    grp = lax.broadcasted_iota(jnp.int32, (n, 1), 0) // rows_per_head
    col = jnp.full((n, 1), sink_ref[hk * GQA_B], F32)
    for g in range(1, GQA_B):
        col = jnp.where(grp == g, sink_ref[hk * GQA_B + g], col)
    return col


def _sink_attend(s, sink_col, vb):
    m = jnp.maximum(jnp.max(s, axis=-1, keepdims=True), sink_col)
    e = jnp.exp(s - m)
    den = jnp.sum(e, axis=-1, keepdims=True) + jnp.exp(sink_col - m)
    return _dot(e.astype(BF16), vb) / den


def _store_heads(ycat_ref, o, row0, rows, hk):
    for pair in range(GQA_B // 2):
        both = jnp.concatenate([o[(2 * pair) * rows:(2 * pair + 1) * rows],
                                o[(2 * pair + 1) * rows:(2 * pair + 2) * rows]], axis=-1)
        c0 = D_A + (hk * GQA_B + 2 * pair) * HEAD_DIM
        ycat_ref[row0:row0 + rows, c0:c0 + 2 * HEAD_DIM] = both.astype(BF16)


def _even_mix_prompt_kernel(sink_ref, u_ref, q_ref, k_ref, v_ref, x_ref, wp_ref, ps_ref, wo_ref, o_ref,
                            ucar, kcar, vcar, ycat, *, tm, tiles_per_seq):
    i = pl.program_id(0)
    ti = i % tiles_per_seq
    first_key_chunk = jnp.where(ti == 0, WINDOW // CHUNK, 0)

    @pl.when(ti == 0)
    def _():
        ucar[...] = jnp.zeros_like(ucar)
        kcar[...] = jnp.zeros_like(kcar)
        vcar[...] = jnp.zeros_like(vcar)

    u = u_ref[...]
    ext = jnp.concatenate([ucar[...], u], axis=0)
    pos = ti * tm + lax.broadcasted_iota(jnp.int32, (tm, POOL_GROUP), 0)
    _pool_mix(ext, u, pos, wp_ref, ps_ref[...], ycat, 0, tm)
    ucar[...] = u[tm - POOL_HALO:]

    kext = jnp.concatenate([kcar[...], k_ref[...]], axis=1)
    vext = jnp.concatenate([vcar[...], v_ref[...].astype(BF16)], axis=0)
    kcar[...] = kext[:, tm:]
    vcar[...] = vext[tm:]
    qb = 2 * CHUNK
    kb = qb + WINDOW
    nrow = GQA_B * qb
    qc = (lax.broadcasted_iota(jnp.int32, (nrow, kb), 0) % qb) // CHUNK
    kc = lax.broadcasted_iota(jnp.int32, (nrow, kb), 1) // CHUNK
    band = (kc >= qc) & (kc <= qc + WINDOW // CHUNK)
    first = band & (kc >= first_key_chunk)
    grp = lax.broadcasted_iota(jnp.int32, (nrow, LANES), 0) // qb
    low_lanes = lax.broadcasted_iota(jnp.int32, (qb, LANES), 1) < HEAD_DIM
    ones = jnp.ones((WINDOW + tm, HEAD_DIM), BF16)
    zeros = jnp.zeros((HEAD_DIM, WINDOW + tm), BF16)
    order = [g for g in range(GQA_B) if g % 2 == 0] + [g for g in range(GQA_B) if g % 2 == 1]
    v_aug, sink_rep, k_even, k_odd = [], [], [], []
    for hk in range(HKV_B):
        v_aug.append(jnp.concatenate([vext[:, hk * HEAD_DIM:(hk + 1) * HEAD_DIM], ones], axis=-1))
        kt = kext[hk * HEAD_DIM:(hk + 1) * HEAD_DIM]
        k_even.append(jnp.concatenate([kt, zeros], axis=0))
        k_odd.append(jnp.concatenate([zeros, kt], axis=0))
        rep = jnp.full((nrow, LANES), sink_ref[hk * GQA_B + order[0]], F32)
        for r in range(1, GQA_B):
            rep = jnp.where(grp == r, sink_ref[hk * GQA_B + order[r]], rep)
        sink_rep.append(rep)

    def scores(a, hk):
        def tiles(gs):
            return jnp.concatenate(
                [q_ref[a * qb:(a + 1) * qb, (hk * GQA_B + g) // 2 * LANES:((hk * GQA_B + g) // 2 + 1) * LANES]
                 for g in gs], axis=0)
        half = GQA_B // 2
        return jnp.concatenate([_dot(tiles(order[:half]), k_even[hk][:, a * qb:a * qb + kb]),
                                _dot(tiles(order[half:]), k_odd[hk][:, a * qb:a * qb + kb])], axis=0)

    def softmax(a, hk, s):
        s = jnp.where(first if a == 0 else band, s, NEG_INF)
        m = jnp.maximum(jnp.max(s, axis=-1, keepdims=True), sink_rep[hk])
        return jnp.exp(s - _lane_tile(m, kb // LANES)).astype(BF16), jnp.exp(sink_rep[hk] - m)

    def finish(a, hk, e, sink_term):
        oa = _dot(e, v_aug[hk][a * qb:a * qb + kb])
        inv = 1.0 / (oa + sink_term)
        for pair in range(GQA_B // 2):
            lo = slice(pair * qb, (pair + 1) * qb)
            hi = slice((GQA_B // 2 + pair) * qb, (GQA_B // 2 + pair + 1) * qb)
            both = jnp.where(low_lanes, oa[lo] * pltpu.roll(inv[lo], HEAD_DIM, 1),
                             pltpu.roll(oa[hi], HEAD_DIM, 1) * inv[hi])
            c0 = D_A + (hk * GQA_B + 2 * pair) * HEAD_DIM
            ycat[a * qb:(a + 1) * qb, c0:c0 + 2 * HEAD_DIM] = both.astype(BF16)

    blocks = [(a, hk) for a in range(tm // qb) for hk in range(HKV_B)]
    nb = len(blocks)
    s_q = {0: scores(*blocks[0])}
    e_q = {}
    for n in range(nb + 2):
        if n + 1 < nb:
            s_q[n + 1] = scores(*blocks[n + 1])
        if n < nb:
            e_q[n] = softmax(*blocks[n], s_q.pop(n))
        if 1 <= n <= nb:
            finish(*blocks[n - 1], *e_q.pop(n - 1))
    o_ref[...] = x_ref[...] + _dot(ycat[...], wo_ref[...])


def _even_mix_prompt(sinks, u, q, kt, v, x, wp, ps, wo, seq, tm):
    r = x.shape[0]
    kern = functools.partial(_even_mix_prompt_kernel, tm=tm, tiles_per_seq=seq // tm)
    return pl.pallas_call(
        kern,
        grid=(r // tm,),
        in_specs=[pl.BlockSpec(memory_space=pltpu.SMEM),
                  _rows(tm, D_A), _rows(tm, D_B), pl.BlockSpec((D_KV, tm), lambda i: (0, i)), _rows(tm, D_KV),
                  _rows(tm, D_MODEL), _full(wp.shape), _full((1, D_A)), _full((D_A + D_B, D_MODEL))],
        out_specs=_rows(tm, D_MODEL),
        out_shape=jax.ShapeDtypeStruct((r, D_MODEL), F32),
        scratch_shapes=[pltpu.VMEM((POOL_HALO, D_A), F32), pltpu.VMEM((D_KV, WINDOW), BF16),
                        pltpu.VMEM((WINDOW, D_KV), BF16), pltpu.VMEM((tm, D_A + D_B), BF16)],
        compiler_params=_params(),
        name="even_mix_prompt",
    )(sinks, u, q, kt, v, x, wp, ps, wo)


def _even_mix_sample_kernel(sink_ref, u_ref, q_ref, k_ref, v_ref, x_ref, pp_ref, kc_ref, vc_ref, wp_ref, ps_ref,
                            wo_ref, o_ref, ycat, *, l, pos0):
    u = u_ref[...]
    ext = jnp.concatenate([pp_ref[0], u], axis=0)
    pos = pos0 + lax.broadcasted_iota(jnp.int32, (l, 1), 0)
    _pool_mix(ext, u, pos, wp_ref, ps_ref[...], ycat, 0, l)

    pad = jnp.zeros((LANES - l, D_KV), BF16)
    kall = jnp.concatenate([kc_ref[0].astype(BF16), k_ref[...].astype(BF16), pad], axis=0)
    vall = jnp.concatenate([vc_ref[0].astype(BF16), v_ref[...].astype(BF16), pad], axis=0)
    nk = kall.shape[0]
    valid = lax.broadcasted_iota(jnp.int32, (GQA_B * l, nk), 1) < nk - (LANES - l)
    for hk in range(HKV_B):
        ks = kall[:, hk * HEAD_DIM:(hk + 1) * HEAD_DIM]
        vs = vall[:, hk * HEAD_DIM:(hk + 1) * HEAD_DIM]
        qs = jnp.concatenate(
            [q_ref[:, (hk * GQA_B + g) * HEAD_DIM:(hk * GQA_B + g + 1) * HEAD_DIM] for g in range(GQA_B)], axis=0)
        s = jnp.where(valid, _dot_nt(qs, ks), NEG_INF)
        o = _sink_attend(s, _sink_column(sink_ref, hk, l), vs)
        _store_heads(ycat, o, 0, l, hk)
    o_ref[...] = x_ref[...] + _dot(ycat[...], wo_ref[...])


def _even_mix_sample(sinks, u, q, k, v, x, pool_prev, kc, vc, wp, ps, wo, l, pos0):
    r = x.shape[0]
    rows_c = kc.shape[1]
    kern = functools.partial(_even_mix_sample_kernel, l=l, pos0=pos0)
    cache_spec = pl.BlockSpec((1, rows_c, D_KV), lambda i: (i, 0, 0))
    return pl.pallas_call(
        kern,
        grid=(r // l,),
        in_specs=[pl.BlockSpec(memory_space=pltpu.SMEM),
                  _rows(l, D_A), _rows(l, D_B), _rows(l, D_KV), _rows(l, D_KV), _rows(l, D_MODEL),
                  pl.BlockSpec((1, POOL_HALO, D_A), lambda i: (i, 0, 0)), cache_spec, cache_spec,
                  _full(wp.shape), _full((1, D_A)), _full((D_A + D_B, D_MODEL))],
        out_specs=_rows(l, D_MODEL),
        out_shape=jax.ShapeDtypeStruct((r, D_MODEL), F32),
        scratch_shapes=[pltpu.VMEM((l, D_A + D_B), BF16)],
        compiler_params=_params(),
        name="even_mix_sample",
    )(sinks, u, q, k, v, x, pool_prev, kc, vc, wp, ps, wo)


ODD_CR0 = 2 * D_CQK + 2 * D_C
ODD_DQ0 = ODD_CR0 + LANES
ODD_IN_PADDED = ODD_DQ0 + 3 * D_D


def _odd_in_kernel(x_ref, nw_ref, w_ref, wg_ref, bg_ref, cos_ref, sa_ref, sb_ref, *rest, tm, blk, cache_layout,
                   aliased):
    if aliased:
        rest = rest[2:]
    qd_ref, kd_ref, el_ref, cv_ref, cg_ref, dq_ref, dk_ref, dv_ref, cum_scr = rest
    h = _rms(x_ref[...], nw_ref[...]).astype(BF16)
    cos_t, sin_a, sin_b = cos_ref[...], sa_ref[...], sb_ref[...]
    cr = _dot(h, w_ref[:, ODD_CR0:ODD_DQ0]).astype(BF16)
    gate = _dot(cr, wg_ref[...]) + bg_ref[...]
    cq = _dot(h, w_ref[:, 0:D_CQK])
    ck = _dot(h, w_ref[:, D_CQK:2 * D_CQK])
    dq = _dot(h, w_ref[:, ODD_DQ0:ODD_DQ0 + D_D])
    cum = (jnp.minimum(gate, 0.0) - jnp.log1p(jnp.exp(-jnp.abs(gate)))) * (1.0 / GATE_NORM)
    rowb = lax.broadcasted_iota(jnp.int32, (tm, 1), 0) % blk
    sh = 1
    while sh < blk:
        cum = cum + jnp.where(rowb >= sh, pltpu.roll(cum, sh, 0), 0.0)
        sh *= 2
    for s in range(D_CQK // LANES):
        cum_scr[s] = cum[:, s * LANES:(s + 1) * LANES]
        el_ref[:, s * LANES:(s + 1) * LANES] = jnp.exp(cum_scr[s, pl.ds(blk - 1, tm // blk, stride=blk), :])
    qd_ref[...] = (cq * (DK_C ** -0.5) * jnp.exp(cum)).astype(BF16)
    kd_ref[...] = (ck * jnp.exp(-cum)).astype(BF16)
    dk = _dot(h, w_ref[:, ODD_DQ0 + D_D:ODD_DQ0 + 2 * D_D])
    dq_ref[...] = (_rope(dq, cos_t, sin_a, sin_b) * (HD_D ** -0.5 * LOG2E)).astype(BF16)
    dv = _dot(h, w_ref[:, ODD_DQ0 + 2 * D_D:ODD_DQ0 + 3 * D_D])
    dk = _rope(dk, cos_t, sin_a, sin_b)
    cv_ref[...] = _dot(h, w_ref[:, 2 * D_CQK:2 * D_CQK + D_C]).astype(BF16)
    cg_ref[...] = _dot(h, w_ref[:, 2 * D_CQK + D_C:ODD_CR0])
    if cache_layout:
        for hd in range(H_D):
            lo, hi = hd * 2 * HD_D, (hd + 1) * 2 * HD_D
            dk_ref[0, 0, hd] = dk[:, lo:hi].T
            dv_ref[0, pl.ds(hd, tm, stride=H_D), :] = dv[:, lo:hi]
    else:
        dk_ref[...] = dk
        dv_ref[...] = dv


def _odd_in(x, nw, w, wg, bg, tabs, tm, blk, pair=None):
    r = x.shape[0]
    t_tiles = tabs[0].shape[0] // tm
    tab_spec = pl.BlockSpec((tm, LANES), lambda i: (i % t_tiles, 0))
    nb = tm // blk
    sds = jax.ShapeDtypeStruct
    in_specs = [_rows(tm, D_MODEL), _full((1, D_MODEL)), _full((D_MODEL, ODD_IN_PADDED)),
                _full((LANES, D_CQK)), _full((1, D_CQK)), tab_spec, tab_spec, tab_spec]
    args = [x, nw, w, wg, bg, *tabs]
    aliases = {}
    if pair is None:
        kv_specs = [_rows(tm, D_D), _rows(tm, D_D)]
        kv_shapes = [sds((r, D_D), F32), sds((r, D_D), F32)]
    else:
        p, n_pairs, batch, seq, bufs = pair
        tps = seq // tm
        kv_specs = [pl.BlockSpec((1, 1, H_D, 2 * HD_D, tm), lambda i: (p, i // tps, 0, 0, i % tps)),
                    pl.BlockSpec((1, tm * H_D, 2 * HD_D), lambda i: (p, i, 0))]
        kv_shapes = [sds((n_pairs, batch, H_D, 2 * HD_D, seq), F32), sds((n_pairs, r * H_D, 2 * HD_D), F32)]
        if bufs is not None:
            in_specs += [pl.BlockSpec(memory_space=pl.ANY)] * 2
            args += list(bufs)
            aliases = {len(args) - 2: 6, len(args) - 1: 7}
    kern = functools.partial(_odd_in_kernel, tm=tm, blk=blk, cache_layout=pair is not None, aliased=bool(aliases))
    return pl.pallas_call(
        kern,
        grid=(r // tm,),
        in_specs=in_specs,
        out_specs=[_rows(tm, D_CQK), _rows(tm, D_CQK), _rows(nb, D_CQK), _rows(tm, D_C), _rows(tm, D_C),
                   _rows(tm, D_D)] + kv_specs,
        out_shape=[sds((r, D_CQK), BF16), sds((r, D_CQK), BF16), sds((r // blk, D_CQK), F32),
                   sds((r, D_C), BF16), sds((r, D_C), F32), sds((r, D_D), BF16)] + kv_shapes,
        scratch_shapes=[pltpu.VMEM((D_CQK // LANES, tm, LANES), F32)],
        input_output_aliases=aliases,
        compiler_params=_params(),
        name="odd_in",
    )(*args)


def _lambda_value(lp_ref, lambda_init):
    lp = lp_ref[...]
    s1 = jnp.sum(lp[0:1] * lp[1:2], axis=-1, keepdims=True)
    s2 = jnp.sum(lp[2:3] * lp[3:4], axis=-1, keepdims=True)
    return jnp.exp(s1) - jnp.exp(s2) + lambda_init


def _diff_finish(acc, den, lam, nw, lambda_init):
    o = acc[0] / den[0] - lam * (acc[1] / den[1])
    return (_rms(o, nw) * (1.0 - lambda_init)).astype(BF16)


def _lane_tile(x, n):
    return x if n == 1 else jnp.concatenate([x] * n, axis=1)


def _diff_prompt_kernel(lp_ref, nw_ref, q_ref, k_ref, v_ref, o_ref, kbf, vbf, m_scr, acc_scr, s_a, s_b, bias,
                        *, tq, lambda_init):
    hd = pl.program_id(1)
    qi = pl.program_id(2)
    dv = 2 * HD_D
    seq = vbf.shape[0]

    @pl.when(qi == 0)
    def _():
        kbf[...] = k_ref[0, 0, 0].astype(BF16)
        vbf[:, :dv] = v_ref[0, pl.ds(hd, seq, stride=H_D), :].astype(BF16)
        vbf[:, dv:] = jnp.ones((seq, dv), BF16)

    tk = tq // 2

    @pl.when((pl.program_id(0) == 0) & (pl.program_id(1) == 0) & (qi == 0))
    def _():
        qc = lax.broadcasted_iota(jnp.int32, (tk, tk), 0) // CHUNK
        kc = lax.broadcasted_iota(jnp.int32, (tk, tk), 1) // CHUNK
        bias[...] = jnp.where(kc <= qc, 0.0, NEG_INF)

    m_scr[...] = jnp.full(m_scr.shape, NEG_INF, F32)
    acc_scr[...] = jnp.zeros(acc_scr.shape, F32)

    def scores(t, s_buf, row0=0):
        start = pl.multiple_of(t * tk, tk)
        for c in range(2):
            s_buf[c, row0:, :] = _dot(q_ref[row0:, c * HD_D:(c + 1) * HD_D],
                                      kbf[c * HD_D:(c + 1) * HD_D, pl.ds(start, tk)])

    def attend(t, s_buf, row0=0, rows=tq, masked=False):
        vs = vbf[pl.ds(pl.multiple_of(t * tk, tk), tk), :]
        for c in range(2):
            sc = s_buf[c, row0:row0 + rows, :]
            if masked:
                sc = sc + bias[...]
            m_old = m_scr[c, row0:row0 + rows, :]
            m_new = jnp.maximum(m_old, jnp.max(sc, axis=-1, keepdims=True))
            p = jnp.exp2(sc - _lane_tile(m_new, tk // LANES)).astype(BF16)
            alpha = _lane_tile(jnp.exp2(m_old - m_new), 2 * dv // LANES)
            acc_scr[c, row0:row0 + rows, :] = alpha * acc_scr[c, row0:row0 + rows, :] + _dot(p, vs)
            m_scr[c, row0:row0 + rows, :] = m_new

    scores(0, s_a)

    def pair(j, _):
        t = 2 * j
        scores(t + 1, s_b)
        attend(t, s_a)
        scores(t + 2, s_a)
        attend(t + 1, s_b)
        return 0

    lax.fori_loop(0, qi, pair, 0)
    scores(2 * qi + 1, s_b, row0=tk)
    attend(2 * qi, s_a, row0=0, rows=tk, masked=True)
    attend(2 * qi, s_a, row0=tk, rows=tk)
    attend(2 * qi + 1, s_b, row0=tk, rows=tk, masked=True)

    lam = _lambda_value(lp_ref, lambda_init)
    acc = [acc_scr[c] for c in range(2)]
    o_ref[...] = _diff_finish([a[:, :dv] for a in acc], [a[:, dv:] for a in acc], lam, nw_ref[...], lambda_init)


def _diff_prompt(lp, nw, q, kt, v, p, batch, seq, lambda_init, tq):
    r = q.shape[0]
    nq = seq // tq
    kern = functools.partial(_diff_prompt_kernel, tq=tq, lambda_init=lambda_init)
    k_spec = pl.BlockSpec((1, 1, 1, 2 * HD_D, seq), lambda b, h, i: (p, b, h, 0, 0))
    v_spec = pl.BlockSpec((1, seq * H_D, 2 * HD_D), lambda b, h, i: (p, b, 0))
    q_spec = pl.BlockSpec((tq, 2 * HD_D), lambda b, h, i: (b * nq + i, h))
    return pl.pallas_call(
        kern,
        grid=(batch, H_D, nq),
        in_specs=[pl.BlockSpec((4, HD_D), lambda b, h, i: (0, 0)), pl.BlockSpec((1, 2 * HD_D), lambda b, h, i: (0, 0)),
                  q_spec, k_spec, v_spec],
        out_specs=q_spec,
        out_shape=jax.ShapeDtypeStruct((r, D_D), BF16),
        scratch_shapes=[pltpu.VMEM((2 * HD_D, seq), BF16), pltpu.VMEM((seq, 4 * HD_D), BF16),
                        pltpu.VMEM((2, tq, LANES), F32), pltpu.VMEM((2, tq, 4 * HD_D), F32),
                        pltpu.VMEM((2, tq, tq // 2), F32), pltpu.VMEM((2, tq, tq // 2), F32),
                        pltpu.VMEM((tq // 2, tq // 2), F32)],
        compiler_params=_params(3),
        name="diff_prompt",
    )(lp, nw, q, kt, v)


def _diff_sample_kernel(lp_ref, nw_ref, q_ref, k_ref, v_ref, kc_ref, vc_ref, o_ref, *, l, past, lambda_init):
    pad = jnp.zeros((LANES - l, 2 * HD_D), BF16)
    valid = lax.broadcasted_iota(jnp.int32, (l, LANES), 1) < l
    lam = _lambda_value(lp_ref, lambda_init)
    for hd in range(H_D):
        lo, hi = hd * 2 * HD_D, (hd + 1) * 2 * HD_D
        knew = jnp.concatenate([k_ref[:, lo:hi].astype(BF16), pad], axis=0)
        vnew = jnp.concatenate([v_ref[:, lo:hi].astype(BF16), pad], axis=0)
        kct = kc_ref[0, 0, hd].astype(BF16)
        vcache = vc_ref[0, pl.ds(hd, past, stride=H_D), :].astype(BF16)
        acc, den = [], []
        for c in range(2):
            qc = q_ref[:, lo + c * HD_D:lo + (c + 1) * HD_D]
            s_old = _dot(qc, kct[c * HD_D:(c + 1) * HD_D, :])
            s_new = jnp.where(valid, _dot_nt(qc, knew[:, c * HD_D:(c + 1) * HD_D]), NEG_INF)
            m = jnp.maximum(jnp.max(s_old, axis=-1, keepdims=True), jnp.max(s_new, axis=-1, keepdims=True))
            p_old = jnp.exp2(s_old - m)
            p_new = jnp.exp2(s_new - m)
            den.append(jnp.sum(p_old, axis=-1, keepdims=True) + jnp.sum(p_new, axis=-1, keepdims=True))
            acc.append(_dot(p_old.astype(BF16), vcache) + _dot(p_new.astype(BF16), vnew))
        o_ref[:, lo:hi] = _diff_finish(acc, den, lam, nw_ref[...], lambda_init)


def _diff_sample(lp, nw, q, k, v, kct, vc, p, batch, l, lambda_init):
    r = q.shape[0]
    past = kct.shape[-1]
    kern = functools.partial(_diff_sample_kernel, l=l, past=past, lambda_init=lambda_init)
    cache_k_spec = pl.BlockSpec((1, 1, H_D, 2 * HD_D, past), lambda b: (p, b, 0, 0, 0))
    cache_v_spec = pl.BlockSpec((1, past * H_D, 2 * HD_D), lambda b: (p, b, 0))
    return pl.pallas_call(
        kern,
        grid=(batch,),
        in_specs=[_full((4, HD_D)), _full((1, 2 * HD_D)), _rows(l, D_D), _rows(l, D_D), _rows(l, D_D),
                  cache_k_spec, cache_v_spec],
        out_specs=_rows(l, D_D),
        out_shape=jax.ShapeDtypeStruct((r, D_D), BF16),
        compiler_params=_params(),
        name="diff_sample",
    )(lp, nw, q, k, v, kct, vc)


def _odd_mix_kernel(qd_ref, kd_ref, el_ref, cv_ref, cg_ref, yd_ref, x_ref, s0_ref, gw_ref, wo_ref,
                    o_ref, s_ref, state, og, ycat, *, tm, blk, tiles_per_seq, per_block_state):
    i = pl.program_id(0)
    if not per_block_state:
        @pl.when(i % tiles_per_seq == 0)
        def _():
            state[...] = jnp.zeros_like(state)

    tri = lax.broadcasted_iota(jnp.int32, (blk, blk), 1) <= lax.broadcasted_iota(jnp.int32, (blk, blk), 0)
    for n in range(tm // blk):
        r0 = n * blk
        el = el_ref[n:n + 1, :]
        for h in range(H_C):
            if per_block_state:
                st = s0_ref[n, h]
            else:
                st = state[h]
            qd = qd_ref[r0:r0 + blk, h * DK_C:(h + 1) * DK_C]
            kd = kd_ref[r0:r0 + blk, h * DK_C:(h + 1) * DK_C]
            vh = cv_ref[r0:r0 + blk, h * DV_C:(h + 1) * DV_C]
            att = jnp.where(tri, _dot_nt(qd, kd), 0.0).astype(BF16)
            og[r0:r0 + blk, h * DV_C:(h + 1) * DV_C] = _dot(att, vh) + _dot_nt(qd, st.astype(BF16))
            st = el[:, h * DK_C:(h + 1) * DK_C] * (st + _dot_tn(vh, kd))
            if per_block_state:
                s_ref[n, h] = st
            else:
                state[h] = st
    if not per_block_state:
        s_ref[0] = state[...]

    gw = gw_ref[...]
    for h in range(H_C):
        lo, hi = h * DV_C, (h + 1) * DV_C
        g = cg_ref[:, lo:hi]
        y = _rms(og[:, lo:hi], gw) * (g * (1.0 / (1.0 + jnp.exp(-g))))
        ycat[:, lo:hi] = y.astype(BF16)
    ycat[:, D_C:] = yd_ref[...]
    o_ref[...] = x_ref[...] + _dot(ycat[...], wo_ref[...])


def _odd_mix(qd, kd, el, cv, cg, yd, x, s0, gw, wo, tm, blk, seq, per_block_state):
    r = x.shape[0]
    nb = tm // blk
    tiles_per_seq = max(seq // tm, 1)
    kern = functools.partial(_odd_mix_kernel, tm=tm, blk=blk, tiles_per_seq=tiles_per_seq,
                             per_block_state=per_block_state)
    if per_block_state:
        n_state = r // blk
        state_spec = pl.BlockSpec((nb, H_C, DV_C, DK_C), lambda i: (i, 0, 0, 0))
    else:
        n_state = r // seq
        state_spec = pl.BlockSpec((1, H_C, DV_C, DK_C), lambda i: (i // tiles_per_seq, 0, 0, 0))
    return pl.pallas_call(
        kern,
        grid=(r // tm,),
        in_specs=[_rows(tm, D_CQK), _rows(tm, D_CQK), _rows(nb, D_CQK), _rows(tm, D_C), _rows(tm, D_C),
                  _rows(tm, D_D), _rows(tm, D_MODEL), state_spec, _full((1, DV_C)), _full((D_C + D_D, D_MODEL))],
        out_specs=[_rows(tm, D_MODEL), state_spec],
        out_shape=[jax.ShapeDtypeStruct((r, D_MODEL), F32),
                   jax.ShapeDtypeStruct((n_state, H_C, DV_C, DK_C), F32)],
        scratch_shapes=[pltpu.VMEM((H_C, DV_C, DK_C), F32), pltpu.VMEM((tm, D_C), F32),
                        pltpu.VMEM((tm, D_C + D_D), BF16)],
        compiler_params=_params(),
        name="odd_mix",
    )(qd, kd, el, cv, cg, yd, x, s0, gw, wo)


def _gelu_tanh(x):
    return 0.5 * x * (1.0 + jnp.tanh(math.sqrt(2.0 / math.pi) * (x + 0.044715 * (x * x * x))))


def _ffn_kernel(x_ref, nw_ref, wup_ref, cw_ref, cb_ref, wdn_ref, prev_ref, fnw_ref, o_ref, tail_ref, carry, act,
                *, tm, seg, tiles_per_seq, prev_from_input, final_norm):
    i = pl.program_id(0)
    x = x_ref[...]
    nseg = tm // seg
    row = lax.broadcasted_iota(jnp.int32, (seg, 1), 0)
    if not prev_from_input:
        @pl.when(i % tiles_per_seq == 0)
        def _():
            carry[...] = jnp.zeros_like(carry)

    def up(hh, c0):
        return _dot(hh, wup_ref[:, c0:c0 + FFN_COLS]), _dot(hh, wup_ref[:, D_FF + c0:D_FF + c0 + FFN_COLS])

    h = _rms(x, nw_ref[...]).astype(BF16)
    nxt = up(h, 0)
    for c0 in range(0, D_FF, FFN_COLS):
        c1 = c0 + FFN_COLS
        gate, val = nxt
        if c1 < D_FF:
            nxt = up(h, c1)
        w0, w1, w2 = cw_ref[0:1, c0:c1], cw_ref[1:2, c0:c1], cw_ref[2:3, c0:c1]
        convs = []
        for s in range(nseg):
            g = gate[s * seg:(s + 1) * seg]
            if prev_from_input:
                p0, p1 = prev_ref[s, 0:1, c0:c1], prev_ref[s, 1:2, c0:c1]
            else:
                p0 = carry[CONV_TAIL - 2:CONV_TAIL - 1, c0:c1]
                p1 = carry[CONV_TAIL - 1:CONV_TAIL, c0:c1]
            g1 = jnp.where(row == 0, p1, pltpu.roll(g, 1, 0))
            g2 = jnp.where(row == 0, p0, jnp.where(row == 1, p1, pltpu.roll(g, 2, 0)))
            convs.append(cb_ref[:, c0:c1] + g2 * w0 + g1 * w1 + g * w2)
            tail_ref[s, :, c0:c1] = g[seg - CONV_TAIL:]
        conv = convs[0] if nseg == 1 else jnp.concatenate(convs, axis=0)
        if not prev_from_input:
            carry[:, c0:c1] = gate[tm - CONV_TAIL:]
        act[:, c0:c1] = (_gelu_tanh(conv) * val).astype(BF16)
    if final_norm:
        o_ref[...] = _rms(x + _dot(act[...], wdn_ref[...]), fnw_ref[...])
    else:
        half = D_MODEL // 2
        for c0 in (0, half):
            o_ref[:, c0:c0 + half] = x[:, c0:c0 + half] + _dot(act[...], wdn_ref[:, c0:c0 + half])


def _ffn(x, nw, wup, cw, cb, wdn, prev, fnw, tm, seg, seq, prev_from_input, final_norm):
    r = x.shape[0]
    nseg = tm // seg
    kern = functools.partial(_ffn_kernel, tm=tm, seg=seg, tiles_per_seq=max(seq // tm, 1),
                             prev_from_input=prev_from_input, final_norm=final_norm)
    prev_spec = pl.BlockSpec((nseg, CONV_W - 1, D_FF), lambda i: (i, 0, 0))
    tail_spec = pl.BlockSpec((nseg, CONV_TAIL, D_FF), lambda i: (i, 0, 0))
    return pl.pallas_call(
        kern,
        grid=(r // tm,),
        in_specs=[_rows(tm, D_MODEL), _full((1, D_MODEL)), _full((D_MODEL, 2 * D_FF)), _full((CONV_W, D_FF)),
                  _full((1, D_FF)), _full((D_FF, D_MODEL)), prev_spec, _full((1, D_MODEL))],
        out_specs=[_rows(tm, D_MODEL), tail_spec],
        out_shape=[jax.ShapeDtypeStruct((r, D_MODEL), F32),
                   jax.ShapeDtypeStruct((r // seg, CONV_TAIL, D_FF), F32)],
        scratch_shapes=[pltpu.VMEM((CONV_TAIL, D_FF), F32), pltpu.VMEM((tm, D_FF), BF16)],
        compiler_params=_params(),
        name="ffn",
    )(x, nw, wup, cw, cb, wdn, prev, fnw)


def _pad_odd_w_in(w):
    cr = w[:, ODD_CR0:ODD_CR0 + GATE_RANK]
    return jnp.concatenate([w[:, :ODD_CR0], cr, jnp.zeros((D_MODEL, LANES - GATE_RANK), w.dtype),
                            w[:, ODD_CR0 + GATE_RANK:]], axis=1).astype(BF16)


def _trunk(x3, pos, prompt, caches, weights):
    (norm_mix_w, norm_ffn_w, final_norm_w, even_w_in, pool_w, pool_scale, swa_sinks, even_w_out, odd_w_in,
     gla_w_gate2, gla_b_gate, gla_norm_w, diff_lambda, diff_norm_w, odd_w_out, ffn_w_up, ffn_conv_w, ffn_conv_b,
     ffn_w_down) = weights
    cache_pool, cache_swa_k, cache_swa_v, state_gla, cache_diff_k, cache_diff_v, cache_ffn_conv = caches
    b, l, _ = x3.shape
    r = b * l
    depth = norm_mix_w.shape[0]
    x = x3.reshape(r, D_MODEL)
    if prompt:
        tm = min(ROW_TILE, l)
        blk = min(CHUNK, l)
        tabs = _rope_tables(pos)
    else:
        tm = r
        blk = min(CHUNK, l)
        tabs = tuple(jnp.tile(t, (b, 1)) for t in _rope_tables(pos))
    pos0 = 0 if prompt else cache_diff_k.shape[2]

    pools, swa_ks, swa_vs, glas, diff_ks, diff_vs, ffns = [], [], [], [], [], [], []
    kv_p = None
    for layer in range(depth):
        p = layer // 2
        nw = norm_mix_w[layer].reshape(1, D_MODEL)
        if layer % 2 == 0:
            u, q, k, v, *kt = _even_in(x, nw, even_w_in[p].astype(BF16), tabs, tm, prompt)
            wp = pool_w[p].astype(BF16)
            ps = pool_scale[p].reshape(1, D_A)
            wo = even_w_out[p].astype(BF16)
            if prompt:
                x = _even_mix_prompt(swa_sinks[p], u, q, kt[0], v, x, wp, ps, wo, l, tm)
                pools.append(u.reshape(b, l, D_A)[:, l - POOL_STATE:])
                swa_ks.append(k.reshape(b, l, HKV_B, HEAD_DIM)[:, l - WINDOW:])
                swa_vs.append(v.reshape(b, l, HKV_B, HEAD_DIM)[:, l - WINDOW:])
            else:
                rows_c = cache_swa_k.shape[2]
                pool_prev = jnp.pad(cache_pool[p], ((0, 0), (1, 0), (0, 0)))
                kc = cache_swa_k[p].reshape(b, rows_c, D_KV)
                vc = cache_swa_v[p].reshape(b, rows_c, D_KV)
                x = _even_mix_sample(swa_sinks[p], u, q, k, v, x, pool_prev, kc, vc, wp, ps, wo, l, pos0)
                pools.append(jnp.concatenate([cache_pool[p], u.reshape(b, l, D_A)], axis=1)[:, -POOL_STATE:])
                swa_ks.append(jnp.concatenate([cache_swa_k[p], k.reshape(b, l, HKV_B, HEAD_DIM)], axis=1)[:, -rows_c:])
                swa_vs.append(jnp.concatenate([cache_swa_v[p], v.reshape(b, l, HKV_B, HEAD_DIM)], axis=1)[:, -rows_c:])
        else:
            lambda_init = 0.8 - 0.6 * math.exp(-0.3 * layer)
            wg = jnp.pad(gla_w_gate2[p], ((0, LANES - GATE_RANK), (0, 0))).astype(BF16)
            bg = gla_b_gate[p].reshape(1, D_CQK)
            lp = diff_lambda[p]
            dnw = diff_norm_w[p].reshape(1, 2 * HD_D)
            w_in = _pad_odd_w_in(odd_w_in[p])
            if prompt:
                qd, kd, el, cv, cg, dq, *kv_p = _odd_in(x, nw, w_in, wg, bg, tabs, tm, blk,
                                                        pair=(p, depth // 2, b, l, kv_p))
                yd = _diff_prompt(lp, dnw, dq, kv_p[0], kv_p[1], p, b, l, lambda_init, min(ATTN_TILE, l))
                s0 = jnp.zeros((b, H_C, DV_C, DK_C), F32)
            else:
                qd, kd, el, cv, cg, dq, dk, dv = _odd_in(x, nw, w_in, wg, bg, tabs, tm, blk)
                past = cache_diff_k.shape[2]
                kct = jnp.transpose(cache_diff_k, (0, 1, 3, 4, 5, 2)).reshape(depth // 2, b, H_D, 2 * HD_D, past)
                vc = cache_diff_v.reshape(depth // 2, b * past * H_D, 2 * HD_D)
                yd = _diff_sample(lp, dnw, dq, dk, dv, kct, vc, p, b, l, lambda_init)
                s0 = jnp.swapaxes(state_gla[p], -1, -2).astype(F32)
                diff_ks.append(dk.reshape(b, l, H_D, 2, HD_D))
                diff_vs.append(dv.reshape(b, l, H_D, 2 * HD_D))
            x, s_new = _odd_mix(qd, kd, el, cv, cg, yd, x, s0, gla_norm_w[p].reshape(1, DV_C),
                                odd_w_out[p].astype(BF16), tm, blk, l, not prompt)
            glas.append(jnp.swapaxes(s_new, -1, -2))
        if prompt:
            tf = seg = min(FFN_ROW_TILE, l)
            prev = jnp.zeros((r // tf, CONV_W - 1, D_FF), F32)
        else:
            tf, seg = tm, l
            prev = cache_ffn_conv[layer]
        x, tails = _ffn(x, norm_ffn_w[layer].reshape(1, D_MODEL), ffn_w_up[layer].astype(BF16), ffn_conv_w[layer],
                        ffn_conv_b[layer].reshape(1, D_FF), ffn_w_down[layer].astype(BF16), prev,
                        final_norm_w.reshape(1, D_MODEL), tf, seg, l, not prompt, layer == depth - 1)
        tails = tails.reshape(b, -1, CONV_TAIL, D_FF)
        ffns.append(tails[:, -1, CONV_TAIL - (CONV_W - 1):])
    if prompt:
        n_pairs = depth // 2
        diff_k = jnp.transpose(kv_p[0].reshape(n_pairs, b, H_D, 2, HD_D, l), (0, 1, 5, 2, 3, 4))
        diff_v = kv_p[1].reshape(n_pairs, b, l, H_D, 2 * HD_D)
    else:
        diff_k, diff_v = jnp.stack(diff_ks), jnp.stack(diff_vs)
    return (x.reshape(b, l, D_MODEL), jnp.stack(pools), jnp.stack(swa_ks), jnp.stack(swa_vs), jnp.stack(glas),
            diff_k, diff_v, jnp.stack(ffns))


def kernel(x_prompt, x_sample, cache_pool, cache_swa_k, cache_swa_v, state_gla, cache_diff_k, cache_diff_v, cache_ffn_conv, norm_mix_w, norm_ffn_w, final_norm_w, even_w_in, pool_w, pool_scale, swa_sinks, even_w_out, odd_w_in, gla_w_gate2, gla_b_gate, gla_norm_w, diff_lambda, diff_norm_w, odd_w_out, ffn_w_up, ffn_conv_w, ffn_conv_b, ffn_w_down):
    weights = (norm_mix_w, norm_ffn_w, final_norm_w, even_w_in, pool_w, pool_scale, swa_sinks, even_w_out, odd_w_in,
               gla_w_gate2, gla_b_gate, gla_norm_w, diff_lambda, diff_norm_w, odd_w_out, ffn_w_up, ffn_conv_w,
               ffn_conv_b, ffn_w_down)
    caches = (cache_pool, cache_swa_k, cache_swa_v, state_gla, cache_diff_k, cache_diff_v, cache_ffn_conv)
    past = cache_diff_k.shape[2]
    pos_p = jnp.arange(x_prompt.shape[1])
    pos_s = past + jnp.arange(x_sample.shape[1])
    outs_p = _trunk(x_prompt, pos_p, True, caches, weights)
    outs_s = _trunk(x_sample, pos_s, False, caches, weights)
    return (outs_p[0], outs_s[0]) + tuple(outs_p[1:]) + tuple(outs_s[1:])
```

```python
import functools
import math

import jax
import jax.numpy as jnp
from jax import lax
from jax.experimental import pallas as pl
from jax.experimental.pallas import tpu as pltpu

F32 = jnp.float32
BF16 = jnp.bfloat16

D_MODEL = 1024
CHUNK = 64
EPS = 1e-6
NEG_INF = -1e30
LOG2E = 1.0 / math.log(2.0)
ROPE_THETA = 500000.0
HEAD_DIM = 64
ROT_DIM = HEAD_DIM // 4
ROT_HALF = ROT_DIM // 2
LANES = 128

POOL_WINDOWS = (2, 4, 8, 16)
POOL_GROUP = D_MODEL // 8
D_A = len(POOL_WINDOWS) * POOL_GROUP
POOL_STATE = max(POOL_WINDOWS) - 1
POOL_HALO = POOL_STATE + 1

H_B = D_MODEL // 128
HKV_B = 2
GQA_B = H_B // HKV_B
WINDOW = 128
D_B = H_B * HEAD_DIM
D_KV = HKV_B * HEAD_DIM

H_C = 4
DK_C = 64
DV_C = 128
GATE_RANK = 16
GATE_NORM = 16.0
D_CQK = H_C * DK_C
D_C = H_C * DV_C

H_D = 4
HD_D = 64
D_D = H_D * 2 * HD_D

D_FF = 2816
CONV_W = 3
CONV_TAIL = 8

ROW_TILE = 1024
FFN_ROW_TILE = 1024
ATTN_TILE = 1024
FFN_COLS = 256
VMEM_LIMIT = 60 * 1024 * 1024


def _dot(a, b):
    return jnp.dot(a, b, preferred_element_type=F32)


def _dot_nt(a, b):
    return lax.dot_general(a, b, (((1,), (1,)), ((), ())), preferred_element_type=F32)


def _dot_tn(a, b):
    return lax.dot_general(a, b, (((0,), (0,)), ((), ())), preferred_element_type=F32)


def _rms(x, w):
    return x * lax.rsqrt(jnp.mean(x * x, axis=-1, keepdims=True) + EPS) * w


def _rope(z, cos_t, sin_a, sin_b):
    outs = []
    for s in range(z.shape[1] // LANES):
        x = z[:, s * LANES:(s + 1) * LANES]
        outs.append(x * cos_t + pltpu.roll(x, LANES - ROT_HALF, 1) * sin_a + pltpu.roll(x, ROT_HALF, 1) * sin_b)
    return outs[0] if len(outs) == 1 else jnp.concatenate(outs, axis=-1)


def _rope_tables(pos):
    t = pos.shape[0]
    inv = jnp.exp(-math.log(ROPE_THETA) * jnp.arange(ROT_HALF, dtype=F32) * (2.0 / ROT_DIM))
    ang = pos.astype(F32)[:, None] * inv[None, :]
    cos, sin = jnp.cos(ang), jnp.sin(ang)
    rest = HEAD_DIM - ROT_DIM
    cos_t = jnp.concatenate([cos, cos, jnp.ones((t, rest), F32)], axis=-1)
    sin_a = jnp.concatenate([-sin, jnp.zeros((t, HEAD_DIM - ROT_HALF), F32)], axis=-1)
    sin_b = jnp.concatenate([jnp.zeros((t, ROT_HALF), F32), sin, jnp.zeros((t, rest), F32)], axis=-1)
    rep = LANES // HEAD_DIM
    return tuple(jnp.tile(a, (1, rep)) for a in (cos_t, sin_a, sin_b))


def _params(n_axes=1):
    return pltpu.CompilerParams(dimension_semantics=("arbitrary",) * n_axes, vmem_limit_bytes=VMEM_LIMIT)


def _full(shape):
    nd = len(shape)
    return pl.BlockSpec(shape, lambda *_: (0,) * nd, pipeline_mode=pl.Buffered(1))


def _rows(tm, cols):
    return pl.BlockSpec((tm, cols), lambda i: (i, 0))


def _even_in_kernel(x_ref, nw_ref, w_ref, cos_ref, sa_ref, sb_ref, u_ref, q_ref, k_ref, v_ref, *tails,
                    tm, prompt):
    h = _rms(x_ref[...], nw_ref[...]).astype(BF16)
    cos_t, sin_a, sin_b = cos_ref[...], sa_ref[...], sb_ref[...]
    q = _dot(h, w_ref[:, D_A:D_A + D_B])
    k = _dot(h, w_ref[:, D_A + D_B:D_A + D_B + D_KV])
    v = _dot(h, w_ref[:, D_A + D_B + D_KV:D_A + D_B + 2 * D_KV])
    q_ref[...] = (_rope(q, cos_t, sin_a, sin_b) * (HEAD_DIM ** -0.5)).astype(BF16)
    k = _rope(k, cos_t, sin_a, sin_b)
    if prompt:
        k_ref[...] = k.T.astype(BF16)
        v_ref[...] = v.astype(BF16)
        tails[0][0] = k[tm - WINDOW:]
        tails[1][0] = v[tm - WINDOW:]
    else:
        k_ref[...] = k
        v_ref[...] = v
    u_ref[...] = _dot(h, w_ref[:, 0:D_A])


def _even_in(x, nw, w, tabs, tm, seq, prompt):
    r = x.shape[0]
    t_tiles = tabs[0].shape[0] // tm
    tab_spec = pl.BlockSpec((tm, LANES), lambda i: (i % t_tiles, 0))
    n_in = D_A + D_B + 2 * D_KV
    sds = jax.ShapeDtypeStruct
    if prompt:
        tps = seq // tm
        tail_spec = pl.BlockSpec((1, WINDOW, D_KV), lambda i: (i // tps, 0, 0))
        kv_specs = [pl.BlockSpec((D_KV, tm), lambda i: (0, i)), _rows(tm, D_KV), tail_spec, tail_spec]
        kv_shapes = [sds((D_KV, r), BF16), sds((r, D_KV), BF16),
                     sds((r // seq, WINDOW, D_KV), F32), sds((r // seq, WINDOW, D_KV), F32)]
    else:
        kv_specs = [_rows(tm, D_KV), _rows(tm, D_KV)]
        kv_shapes = [sds((r, D_KV), F32), sds((r, D_KV), F32)]
    return pl.pallas_call(
        functools.partial(_even_in_kernel, tm=tm, prompt=prompt),
        grid=(r // tm,),
        in_specs=[_rows(tm, D_MODEL), _full((1, D_MODEL)), _full((D_MODEL, n_in)), tab_spec, tab_spec, tab_spec],
        out_specs=[_rows(tm, D_A), _rows(tm, D_B)] + kv_specs,
        out_shape=[sds((r, D_A), F32), sds((r, D_B), BF16)] + kv_shapes,
        compiler_params=_params(),
        name="even_in",
    )(x, nw, w, *tabs)


def _pool_mix(ext, u, pos, wp_ref, ps, ycat_ref, row0, rows):
    for g, w in enumerate(POOL_WINDOWS):
        lo, hi = g * POOL_GROUP, (g + 1) * POOL_GROUP
        s = ext[:, lo:hi]
        sh = 1
        while sh < w:
            s = s + pltpu.roll(s, sh, 0)
            sh *= 2
        count = jnp.minimum(w, pos + 1).astype(F32)
        diff = s[POOL_HALO:] / count - u[:, lo:hi]
        ya = _dot(diff.astype(BF16), wp_ref[g]) * ps[:, lo:hi]
        ycat_ref[row0:row0 + rows, lo:hi] = ya.astype(BF16)


def _sink_column(sink_ref, hk, rows_per_head):
    n = GQA_B * rows_per_head
    grp = lax.broadcasted_iota(jnp.int32, (n, 1), 0) // rows_per_head
    col = jnp.full((n, 1), sink_ref[hk * GQA_B], F32)
    for g in range(1, GQA_B):
        col = jnp.where(grp == g, sink_ref[hk * GQA_B + g], col)
    return col


def _sink_attend(s, sink_col, vb):
    m = jnp.maximum(jnp.max(s, axis=-1, keepdims=True), sink_col)
    e = jnp.exp(s - m)
    den = jnp.sum(e, axis=-1, keepdims=True) + jnp.exp(sink_col - m)
    return _dot(e.astype(BF16), vb) / den


def _store_heads(ycat_ref, o, row0, rows, hk):
    for pair in range(GQA_B // 2):
        both = jnp.concatenate([o[(2 * pair) * rows:(2 * pair + 1) * rows],
                                o[(2 * pair + 1) * rows:(2 * pair + 2) * rows]], axis=-1)
        c0 = D_A + (hk * GQA_B + 2 * pair) * HEAD_DIM
        ycat_ref[row0:row0 + rows, c0:c0 + 2 * HEAD_DIM] = both.astype(BF16)


def _even_mix_prompt_kernel(sink_ref, u_ref, q_ref, k_ref, v_ref, x_ref, wp_ref, ps_ref, wo_ref, o_ref,
                            ucar, kcar, vcar, ycat, *, tm, tiles_per_seq):
    i = pl.program_id(0)
    ti = i % tiles_per_seq
    first_key_chunk = jnp.where(ti == 0, WINDOW // CHUNK, 0)

    @pl.when(ti == 0)
    def _():
        ucar[...] = jnp.zeros_like(ucar)
        kcar[...] = jnp.zeros_like(kcar)
        vcar[...] = jnp.zeros_like(vcar)

    u = u_ref[...]
    ext = jnp.concatenate([ucar[...], u], axis=0)
    pos = ti * tm + lax.broadcasted_iota(jnp.int32, (tm, POOL_GROUP), 0)
    _pool_mix(ext, u, pos, wp_ref, ps_ref[...], ycat, 0, tm)
    ucar[...] = u[tm - POOL_HALO:]

    kext = jnp.concatenate([kcar[...], k_ref[...]], axis=1)
    vext = jnp.concatenate([vcar[...], v_ref[...]], axis=0)
    kcar[...] = kext[:, tm:]
    vcar[...] = vext[tm:]
    qb = 2 * CHUNK
    kb = qb + WINDOW
    nrow = GQA_B * qb
    qc = (lax.broadcasted_iota(jnp.int32, (nrow, kb), 0) % qb) // CHUNK
    kc = lax.broadcasted_iota(jnp.int32, (nrow, kb), 1) // CHUNK
    band = (kc >= qc) & (kc <= qc + WINDOW // CHUNK)
    first = band & (kc >= first_key_chunk)
    grp = lax.broadcasted_iota(jnp.int32, (nrow, LANES), 0) // qb
    low_lanes = lax.broadcasted_iota(jnp.int32, (qb, LANES), 1) < HEAD_DIM
    ones = jnp.ones((WINDOW + tm, HEAD_DIM), BF16)
    zeros = jnp.zeros((HEAD_DIM, WINDOW + tm), BF16)
    order = [g for g in range(GQA_B) if g % 2 == 0] + [g for g in range(GQA_B) if g % 2 == 1]
    v_aug, sink_rep, k_even, k_odd = [], [], [], []
    for hk in range(HKV_B):
        v_aug.append(jnp.concatenate([vext[:, hk * HEAD_DIM:(hk + 1) * HEAD_DIM], ones], axis=-1))
        kt = kext[hk * HEAD_DIM:(hk + 1) * HEAD_DIM]
        k_even.append(jnp.concatenate([kt, zeros], axis=0))
        k_odd.append(jnp.concatenate([zeros, kt], axis=0))
        rep = jnp.full((nrow, LANES), sink_ref[hk * GQA_B + order[0]], F32)
        for r in range(1, GQA_B):
            rep = jnp.where(grp == r, sink_ref[hk * GQA_B + order[r]], rep)
        sink_rep.append(rep)

    def scores(a, hk):
        def tiles(gs):
            return jnp.concatenate(
                [q_ref[a * qb:(a + 1) * qb, (hk * GQA_B + g) // 2 * LANES:((hk * GQA_B + g) // 2 + 1) * LANES]
                 for g in gs], axis=0)
        half = GQA_B // 2
        return jnp.concatenate([_dot(tiles(order[:half]), k_even[hk][:, a * qb:a * qb + kb]),
                                _dot(tiles(order[half:]), k_odd[hk][:, a * qb:a * qb + kb])], axis=0)

    def softmax(a, hk, s):
        s = jnp.where(first if a == 0 else band, s, NEG_INF)
        m = jnp.maximum(jnp.max(s, axis=-1, keepdims=True), sink_rep[hk])
        return jnp.exp(s - _lane_tile(m, kb // LANES)).astype(BF16), jnp.exp(sink_rep[hk] - m)

    def finish(a, hk, e, sink_term):
        oa = _dot(e, v_aug[hk][a * qb:a * qb + kb])
        inv = 1.0 / (oa + sink_term)
        for pair in range(GQA_B // 2):
            lo = slice(pair * qb, (pair + 1) * qb)
            hi = slice((GQA_B // 2 + pair) * qb, (GQA_B // 2 + pair + 1) * qb)
            both = jnp.where(low_lanes, oa[lo] * pltpu.roll(inv[lo], HEAD_DIM, 1),
                             pltpu.roll(oa[hi], HEAD_DIM, 1) * inv[hi])
            c0 = D_A + (hk * GQA_B + 2 * pair) * HEAD_DIM
            ycat[a * qb:(a + 1) * qb, c0:c0 + 2 * HEAD_DIM] = both.astype(BF16)

    blocks = [(a, hk) for a in range(tm // qb) for hk in range(HKV_B)]
    nb = len(blocks)
    s_q = {0: scores(*blocks[0])}
    e_q = {}
    for n in range(nb + 2):
        if n + 1 < nb:
            s_q[n + 1] = scores(*blocks[n + 1])
        if n < nb:
            e_q[n] = softmax(*blocks[n], s_q.pop(n))
        if 1 <= n <= nb:
            finish(*blocks[n - 1], *e_q.pop(n - 1))
    o_ref[...] = x_ref[...] + _dot(ycat[...], wo_ref[...])


def _even_mix_prompt(sinks, u, q, kt, v, x, wp, ps, wo, seq, tm):
    r = x.shape[0]
    kern = functools.partial(_even_mix_prompt_kernel, tm=tm, tiles_per_seq=seq // tm)
    return pl.pallas_call(
        kern,
        grid=(r // tm,),
        in_specs=[pl.BlockSpec(memory_space=pltpu.SMEM),
                  _rows(tm, D_A), _rows(tm, D_B), pl.BlockSpec((D_KV, tm), lambda i: (0, i)), _rows(tm, D_KV),
                  _rows(tm, D_MODEL), _full(wp.shape), _full((1, D_A)), _full((D_A + D_B, D_MODEL))],
        out_specs=_rows(tm, D_MODEL),
        out_shape=jax.ShapeDtypeStruct((r, D_MODEL), F32),
        scratch_shapes=[pltpu.VMEM((POOL_HALO, D_A), F32), pltpu.VMEM((D_KV, WINDOW), BF16),
                        pltpu.VMEM((WINDOW, D_KV), BF16), pltpu.VMEM((tm, D_A + D_B), BF16)],
        compiler_params=_params(),
        name="even_mix_prompt",
    )(sinks, u, q, kt, v, x, wp, ps, wo)


def _even_mix_sample_kernel(sink_ref, u_ref, q_ref, k_ref, v_ref, x_ref, pp_ref, kc_ref, vc_ref, wp_ref, ps_ref,
                            wo_ref, o_ref, ycat, *, l, pos0):
    u = u_ref[...]
    ext = jnp.concatenate([pp_ref[0], u], axis=0)
    pos = pos0 + lax.broadcasted_iota(jnp.int32, (l, 1), 0)
    _pool_mix(ext, u, pos, wp_ref, ps_ref[...], ycat, 0, l)

    pad = jnp.zeros((LANES - l, D_KV), BF16)
    kall = jnp.concatenate([kc_ref[0].astype(BF16), k_ref[...].astype(BF16), pad], axis=0)
    vall = jnp.concatenate([vc_ref[0].astype(BF16), v_ref[...].astype(BF16), pad], axis=0)
    nk = kall.shape[0]
    valid = lax.broadcasted_iota(jnp.int32, (GQA_B * l, nk), 1) < nk - (LANES - l)
    for hk in range(HKV_B):
        ks = kall[:, hk * HEAD_DIM:(hk + 1) * HEAD_DIM]
        vs = vall[:, hk * HEAD_DIM:(hk + 1) * HEAD_DIM]
        qs = jnp.concatenate(
            [q_ref[:, (hk * GQA_B + g) * HEAD_DIM:(hk * GQA_B + g + 1) * HEAD_DIM] for g in range(GQA_B)], axis=0)
        s = jnp.where(valid, _dot_nt(qs, ks), NEG_INF)
        o = _sink_attend(s, _sink_column(sink_ref, hk, l), vs)
        _store_heads(ycat, o, 0, l, hk)
    o_ref[...] = x_ref[...] + _dot(ycat[...], wo_ref[...])


def _even_mix_sample(sinks, u, q, k, v, x, pool_prev, kc, vc, wp, ps, wo, l, pos0):
    r = x.shape[0]
    rows_c = kc.shape[1]
    kern = functools.partial(_even_mix_sample_kernel, l=l, pos0=pos0)
    cache_spec = pl.BlockSpec((1, rows_c, D_KV), lambda i: (i, 0, 0))
    return pl.pallas_call(
        kern,
        grid=(r // l,),
        in_specs=[pl.BlockSpec(memory_space=pltpu.SMEM),
                  _rows(l, D_A), _rows(l, D_B), _rows(l, D_KV), _rows(l, D_KV), _rows(l, D_MODEL),
                  pl.BlockSpec((1, POOL_HALO, D_A), lambda i: (i, 0, 0)), cache_spec, cache_spec,
                  _full(wp.shape), _full((1, D_A)), _full((D_A + D_B, D_MODEL))],
        out_specs=_rows(l, D_MODEL),
        out_shape=jax.ShapeDtypeStruct((r, D_MODEL), F32),
        scratch_shapes=[pltpu.VMEM((l, D_A + D_B), BF16)],
        compiler_params=_params(),
        name="even_mix_sample",
    )(sinks, u, q, k, v, x, pool_prev, kc, vc, wp, ps, wo)


ODD_CR0 = 2 * D_CQK + 2 * D_C
ODD_DQ0 = ODD_CR0 + LANES
ODD_IN_PADDED = ODD_DQ0 + 3 * D_D


def _odd_in_kernel(x_ref, nw_ref, w_ref, wg_ref, bg_ref, cos_ref, sa_ref, sb_ref, *rest, tm, blk, cache_layout,
                   aliased):
    if aliased:
        rest = rest[2:]
    qd_ref, kd_ref, el_ref, cv_ref, cg_ref, dq_ref, dk_ref, dv_ref, cum_scr = rest
    h = _rms(x_ref[...], nw_ref[...]).astype(BF16)
    cos_t, sin_a, sin_b = cos_ref[...], sa_ref[...], sb_ref[...]
    cr = _dot(h, w_ref[:, ODD_CR0:ODD_DQ0]).astype(BF16)
    gate = _dot(cr, wg_ref[...]) + bg_ref[...]
    cq = _dot(h, w_ref[:, 0:D_CQK])
    ck = _dot(h, w_ref[:, D_CQK:2 * D_CQK])
    dq = _dot(h, w_ref[:, ODD_DQ0:ODD_DQ0 + D_D])
    cum = (jnp.minimum(gate, 0.0) - jnp.log1p(jnp.exp(-jnp.abs(gate)))) * (1.0 / GATE_NORM)
    rowb = lax.broadcasted_iota(jnp.int32, (tm, 1), 0) % blk
    sh = 1
    while sh < blk:
        cum = cum + jnp.where(rowb >= sh, pltpu.roll(cum, sh, 0), 0.0)
        sh *= 2
    for s in range(D_CQK // LANES):
        cum_scr[s] = cum[:, s * LANES:(s + 1) * LANES]
        el_ref[:, s * LANES:(s + 1) * LANES] = jnp.exp(cum_scr[s, pl.ds(blk - 1, tm // blk, stride=blk), :])
    qd_ref[...] = (cq * (DK_C ** -0.5) * jnp.exp(cum)).astype(BF16)
    kd_ref[...] = (ck * jnp.exp(-cum)).astype(BF16)
    dk = _dot(h, w_ref[:, ODD_DQ0 + D_D:ODD_DQ0 + 2 * D_D])
    dq_ref[...] = (_rope(dq, cos_t, sin_a, sin_b) * (HD_D ** -0.5 * LOG2E)).astype(BF16)
    dv = _dot(h, w_ref[:, ODD_DQ0 + 2 * D_D:ODD_DQ0 + 3 * D_D])
    dk = _rope(dk, cos_t, sin_a, sin_b)
    cv_ref[...] = _dot(h, w_ref[:, 2 * D_CQK:2 * D_CQK + D_C]).astype(BF16)
    cg_ref[...] = _dot(h, w_ref[:, 2 * D_CQK + D_C:ODD_CR0])
    if cache_layout:
        for hd in range(H_D):
            lo, hi = hd * 2 * HD_D, (hd + 1) * 2 * HD_D
            dk_ref[0, 0, hd] = dk[:, lo:hi].T
            dv_ref[0, pl.ds(hd, tm, stride=H_D), :] = dv[:, lo:hi]
    else:
        dk_ref[...] = dk
        dv_ref[...] = dv


def _odd_in(x, nw, w, wg, bg, tabs, tm, blk, pair=None):
    r = x.shape[0]
    t_tiles = tabs[0].shape[0] // tm
    tab_spec = pl.BlockSpec((tm, LANES), lambda i: (i % t_tiles, 0))
    nb = tm // blk
    sds = jax.ShapeDtypeStruct
    in_specs = [_rows(tm, D_MODEL), _full((1, D_MODEL)), _full((D_MODEL, ODD_IN_PADDED)),
                _full((LANES, D_CQK)), _full((1, D_CQK)), tab_spec, tab_spec, tab_spec]
    args = [x, nw, w, wg, bg, *tabs]
    aliases = {}
    if pair is None:
        kv_specs = [_rows(tm, D_D), _rows(tm, D_D)]
        kv_shapes = [sds((r, D_D), F32), sds((r, D_D), F32)]
    else:
        p, n_pairs, batch, seq, bufs = pair
        tps = seq // tm
        kv_specs = [pl.BlockSpec((1, 1, H_D, 2 * HD_D, tm), lambda i: (p, i // tps, 0, 0, i % tps)),
                    pl.BlockSpec((1, tm * H_D, 2 * HD_D), lambda i: (p, i, 0))]
        kv_shapes = [sds((n_pairs, batch, H_D, 2 * HD_D, seq), F32), sds((n_pairs, r * H_D, 2 * HD_D), F32)]
        if bufs is not None:
            in_specs += [pl.BlockSpec(memory_space=pl.ANY)] * 2
            args += list(bufs)
            aliases = {len(args) - 2: 6, len(args) - 1: 7}
    kern = functools.partial(_odd_in_kernel, tm=tm, blk=blk, cache_layout=pair is not None, aliased=bool(aliases))
    return pl.pallas_call(
        kern,
        grid=(r // tm,),
        in_specs=in_specs,
        out_specs=[_rows(tm, D_CQK), _rows(tm, D_CQK), _rows(nb, D_CQK), _rows(tm, D_C), _rows(tm, D_C),
                   _rows(tm, D_D)] + kv_specs,
        out_shape=[sds((r, D_CQK), BF16), sds((r, D_CQK), BF16), sds((r // blk, D_CQK), F32),
                   sds((r, D_C), BF16), sds((r, D_C), F32), sds((r, D_D), BF16)] + kv_shapes,
        scratch_shapes=[pltpu.VMEM((D_CQK // LANES, tm, LANES), F32)],
        input_output_aliases=aliases,
        compiler_params=_params(),
        name="odd_in",
    )(*args)


def _lambda_value(lp_ref, lambda_init):
    lp = lp_ref[...]
    s1 = jnp.sum(lp[0:1] * lp[1:2], axis=-1, keepdims=True)
    s2 = jnp.sum(lp[2:3] * lp[3:4], axis=-1, keepdims=True)
    return jnp.exp(s1) - jnp.exp(s2) + lambda_init


def _diff_finish(acc, den, lam, nw, lambda_init):
    o = acc[0] / den[0] - lam * (acc[1] / den[1])
    return (_rms(o, nw) * (1.0 - lambda_init)).astype(BF16)


def _lane_tile(x, n):
    return x if n == 1 else jnp.concatenate([x] * n, axis=1)


def _diff_prompt_kernel(lp_ref, nw_ref, q_ref, k_ref, v_ref, o_ref, kbf, vbf, m_scr, acc_scr, s_a, s_b, bias,
                        *, tq, lambda_init):
    hd = pl.program_id(1)
    qi = pl.program_id(2)
    dv = 2 * HD_D
    seq = vbf.shape[0]

    @pl.when(qi == 0)
    def _():
        kbf[...] = k_ref[0, 0, 0].astype(BF16)
        vbf[:, :dv] = v_ref[0, pl.ds(hd, seq, stride=H_D), :].astype(BF16)
        vbf[:, dv:] = jnp.ones((seq, dv), BF16)

    tk = tq // 2

    @pl.when((pl.program_id(0) == 0) & (pl.program_id(1) == 0) & (qi == 0))
    def _():
        qc = lax.broadcasted_iota(jnp.int32, (tk, tk), 0) // CHUNK
        kc = lax.broadcasted_iota(jnp.int32, (tk, tk), 1) // CHUNK
        bias[...] = jnp.where(kc <= qc, 0.0, NEG_INF)

    m_scr[...] = jnp.full(m_scr.shape, NEG_INF, F32)
    acc_scr[...] = jnp.zeros(acc_scr.shape, F32)

    def scores(t, s_buf, row0=0):
        start = pl.multiple_of(t * tk, tk)
        for c in range(2):
            s_buf[c, row0:, :] = _dot(q_ref[row0:, c * HD_D:(c + 1) * HD_D],
                                      kbf[c * HD_D:(c + 1) * HD_D, pl.ds(start, tk)])

    def attend(t, s_buf, row0=0, rows=tq, masked=False):
        vs = vbf[pl.ds(pl.multiple_of(t * tk, tk), tk), :]
        for c in range(2):
            sc = s_buf[c, row0:row0 + rows, :]
            if masked:
                sc = sc + bias[...]
            m_old = m_scr[c, row0:row0 + rows, :]
            m_new = jnp.maximum(m_old, jnp.max(sc, axis=-1, keepdims=True))
            p = jnp.exp2(sc - _lane_tile(m_new, tk // LANES)).astype(BF16)
            alpha = _lane_tile(jnp.exp2(m_old - m_new), 2 * dv // LANES)
            acc_scr[c, row0:row0 + rows, :] = alpha * acc_scr[c, row0:row0 + rows, :] + _dot(p, vs)
            m_scr[c, row0:row0 + rows, :] = m_new

    scores(0, s_a)

    def pair(j, _):
        t = 2 * j
        scores(t + 1, s_b)
        attend(t, s_a)
        scores(t + 2, s_a)
        attend(t + 1, s_b)
        return 0

    lax.fori_loop(0, qi, pair, 0)
    scores(2 * qi + 1, s_b, row0=tk)
    attend(2 * qi, s_a, row0=0, rows=tk, masked=True)
    attend(2 * qi, s_a, row0=tk, rows=tk)
    attend(2 * qi + 1, s_b, row0=tk, rows=tk, masked=True)

    lam = _lambda_value(lp_ref, lambda_init)
    acc = [acc_scr[c] for c in range(2)]
    o_ref[...] = _diff_finish([a[:, :dv] for a in acc], [a[:, dv:] for a in acc], lam, nw_ref[...], lambda_init)


def _diff_prompt(lp, nw, q, kt, v, p, batch, seq, lambda_init, tq):
    r = q.shape[0]
    nq = seq // tq
    kern = functools.partial(_diff_prompt_kernel, tq=tq, lambda_init=lambda_init)
    k_spec = pl.BlockSpec((1, 1, 1, 2 * HD_D, seq), lambda b, h, i: (p, b, h, 0, 0))
    v_spec = pl.BlockSpec((1, seq * H_D, 2 * HD_D), lambda b, h, i: (p, b, 0))
    q_spec = pl.BlockSpec((tq, 2 * HD_D), lambda b, h, i: (b * nq + i, h))
    return pl.pallas_call(
        kern,
        grid=(batch, H_D, nq),
        in_specs=[pl.BlockSpec((4, HD_D), lambda b, h, i: (0, 0)), pl.BlockSpec((1, 2 * HD_D), lambda b, h, i: (0, 0)),
                  q_spec, k_spec, v_spec],
        out_specs=q_spec,
        out_shape=jax.ShapeDtypeStruct((r, D_D), BF16),
        scratch_shapes=[pltpu.VMEM((2 * HD_D, seq), BF16), pltpu.VMEM((seq, 4 * HD_D), BF16),
                        pltpu.VMEM((2, tq, LANES), F32), pltpu.VMEM((2, tq, 4 * HD_D), F32),
                        pltpu.VMEM((2, tq, tq // 2), F32), pltpu.VMEM((2, tq, tq // 2), F32),
                        pltpu.VMEM((tq // 2, tq // 2), F32)],
        compiler_params=_params(3),
        name="diff_prompt",
    )(lp, nw, q, kt, v)


def _diff_sample_kernel(lp_ref, nw_ref, q_ref, k_ref, v_ref, kc_ref, vc_ref, o_ref, *, l, past, lambda_init):
    pad = jnp.zeros((LANES - l, 2 * HD_D), BF16)
    valid = lax.broadcasted_iota(jnp.int32, (l, LANES), 1) < l
    lam = _lambda_value(lp_ref, lambda_init)
    for hd in range(H_D):
        lo, hi = hd * 2 * HD_D, (hd + 1) * 2 * HD_D
        knew = jnp.concatenate([k_ref[:, lo:hi].astype(BF16), pad], axis=0)
        vnew = jnp.concatenate([v_ref[:, lo:hi].astype(BF16), pad], axis=0)
        kct = kc_ref[0, 0, hd].astype(BF16)
        vcache = vc_ref[0, pl.ds(hd, past, stride=H_D), :].astype(BF16)
        acc, den = [], []
        for c in range(2):
            qc = q_ref[:, lo + c * HD_D:lo + (c + 1) * HD_D]
            s_old = _dot(qc, kct[c * HD_D:(c + 1) * HD_D, :])
            s_new = jnp.where(valid, _dot_nt(qc, knew[:, c * HD_D:(c + 1) * HD_D]), NEG_INF)
            m = jnp.maximum(jnp.max(s_old, axis=-1, keepdims=True), jnp.max(s_new, axis=-1, keepdims=True))
            p_old = jnp.exp2(s_old - m)
            p_new = jnp.exp2(s_new - m)
            den.append(jnp.sum(p_old, axis=-1, keepdims=True) + jnp.sum(p_new, axis=-1, keepdims=True))
            acc.append(_dot(p_old.astype(BF16), vcache) + _dot(p_new.astype(BF16), vnew))
        o_ref[:, lo:hi] = _diff_finish(acc, den, lam, nw_ref[...], lambda_init)


def _diff_sample(lp, nw, q, k, v, kct, vc, p, batch, l, lambda_init):
    r = q.shape[0]
    past = kct.shape[-1]
    kern = functools.partial(_diff_sample_kernel, l=l, past=past, lambda_init=lambda_init)
    cache_k_spec = pl.BlockSpec((1, 1, H_D, 2 * HD_D, past), lambda b: (p, b, 0, 0, 0))
    cache_v_spec = pl.BlockSpec((1, past * H_D, 2 * HD_D), lambda b: (p, b, 0))
    return pl.pallas_call(
        kern,
        grid=(batch,),
        in_specs=[_full((4, HD_D)), _full((1, 2 * HD_D)), _rows(l, D_D), _rows(l, D_D), _rows(l, D_D),
                  cache_k_spec, cache_v_spec],
        out_specs=_rows(l, D_D),
        out_shape=jax.ShapeDtypeStruct((r, D_D), BF16),
        compiler_params=_params(),
        name="diff_sample",
    )(lp, nw, q, k, v, kct, vc)


def _odd_mix_kernel(qd_ref, kd_ref, el_ref, cv_ref, cg_ref, yd_ref, x_ref, s0_ref, gw_ref, wo_ref,
                    o_ref, s_ref, state, og, ycat, *, tm, blk, tiles_per_seq, per_block_state):
    i = pl.program_id(0)
    if not per_block_state:
        @pl.when(i % tiles_per_seq == 0)
        def _():
            state[...] = jnp.zeros_like(state)

    tri = lax.broadcasted_iota(jnp.int32, (blk, blk), 1) <= lax.broadcasted_iota(jnp.int32, (blk, blk), 0)
    for n in range(tm // blk):
        r0 = n * blk
        el = el_ref[n:n + 1, :]
        for h in range(H_C):
            if per_block_state:
                st = s0_ref[n, h]
            else:
                st = state[h]
            qd = qd_ref[r0:r0 + blk, h * DK_C:(h + 1) * DK_C]
            kd = kd_ref[r0:r0 + blk, h * DK_C:(h + 1) * DK_C]
            vh = cv_ref[r0:r0 + blk, h * DV_C:(h + 1) * DV_C]
            att = jnp.where(tri, _dot_nt(qd, kd), 0.0).astype(BF16)
            og[r0:r0 + blk, h * DV_C:(h + 1) * DV_C] = _dot(att, vh) + _dot_nt(qd, st.astype(BF16))
            st = el[:, h * DK_C:(h + 1) * DK_C] * (st + _dot_tn(vh, kd))
            if per_block_state:
                s_ref[n, h] = st
            else:
                state[h] = st
    if not per_block_state:
        s_ref[0] = state[...]

    gw = gw_ref[...]
    for h in range(H_C):
        lo, hi = h * DV_C, (h + 1) * DV_C
        g = cg_ref[:, lo:hi]
        y = _rms(og[:, lo:hi], gw) * (g * (1.0 / (1.0 + jnp.exp(-g))))
        ycat[:, lo:hi] = y.astype(BF16)
    ycat[:, D_C:] = yd_ref[...]
    o_ref[...] = x_ref[...] + _dot(ycat[...], wo_ref[...])


def _odd_mix(qd, kd, el, cv, cg, yd, x, s0, gw, wo, tm, blk, seq, per_block_state):
    r = x.shape[0]
    nb = tm // blk
    tiles_per_seq = max(seq // tm, 1)
    kern = functools.partial(_odd_mix_kernel, tm=tm, blk=blk, tiles_per_seq=tiles_per_seq,
                             per_block_state=per_block_state)
    if per_block_state:
        n_state = r // blk
        state_spec = pl.BlockSpec((nb, H_C, DV_C, DK_C), lambda i: (i, 0, 0, 0))
    else:
        n_state = r // seq
        state_spec = pl.BlockSpec((1, H_C, DV_C, DK_C), lambda i: (i // tiles_per_seq, 0, 0, 0))
    return pl.pallas_call(
        kern,
        grid=(r // tm,),
        in_specs=[_rows(tm, D_CQK), _rows(tm, D_CQK), _rows(nb, D_CQK), _rows(tm, D_C), _rows(tm, D_C),
                  _rows(tm, D_D), _rows(tm, D_MODEL), state_spec, _full((1, DV_C)), _full((D_C + D_D, D_MODEL))],
        out_specs=[_rows(tm, D_MODEL), state_spec],
        out_shape=[jax.ShapeDtypeStruct((r, D_MODEL), F32),
                   jax.ShapeDtypeStruct((n_state, H_C, DV_C, DK_C), F32)],
        scratch_shapes=[pltpu.VMEM((H_C, DV_C, DK_C), F32), pltpu.VMEM((tm, D_C), F32),
                        pltpu.VMEM((tm, D_C + D_D), BF16)],
        compiler_params=_params(),
        name="odd_mix",
    )(qd, kd, el, cv, cg, yd, x, s0, gw, wo)


def _gelu_tanh(x):
    return 0.5 * x * (1.0 + jnp.tanh(math.sqrt(2.0 / math.pi) * (x + 0.044715 * (x * x * x))))


def _ffn_kernel(x_ref, nw_ref, wup_ref, cw_ref, cb_ref, wdn_ref, prev_ref, fnw_ref, o_ref, tail_ref, carry, act,
                *, tm, seg, tiles_per_seq, prev_from_input, final_norm):
    i = pl.program_id(0)
    x = x_ref[...]
    nseg = tm // seg
    row = lax.broadcasted_iota(jnp.int32, (seg, 1), 0)
    if not prev_from_input:
        @pl.when(i % tiles_per_seq == 0)
        def _():
            carry[...] = jnp.zeros_like(carry)

    def up(hh, c0):
        return (_dot(hh, wup_ref[0, :, c0:c0 + FFN_COLS]),
                _dot(hh, wup_ref[0, :, D_FF + c0:D_FF + c0 + FFN_COLS]))

    h = _rms(x, nw_ref[...]).astype(BF16)
    nxt = up(h, 0)
    for c0 in range(0, D_FF, FFN_COLS):
        c1 = c0 + FFN_COLS
        gate, val = nxt
        if c1 < D_FF:
            nxt = up(h, c1)
        w0, w1, w2 = cw_ref[0:1, c0:c1], cw_ref[1:2, c0:c1], cw_ref[2:3, c0:c1]
        convs = []
        for s in range(nseg):
            g = gate[s * seg:(s + 1) * seg]
            if prev_from_input:
                p0, p1 = prev_ref[s, 0:1, c0:c1], prev_ref[s, 1:2, c0:c1]
            else:
                p0 = carry[CONV_TAIL - 2:CONV_TAIL - 1, c0:c1]
                p1 = carry[CONV_TAIL - 1:CONV_TAIL, c0:c1]
            g1 = jnp.where(row == 0, p1, pltpu.roll(g, 1, 0))
            g2 = jnp.where(row == 0, p0, jnp.where(row == 1, p1, pltpu.roll(g, 2, 0)))
            convs.append(cb_ref[:, c0:c1] + g2 * w0 + g1 * w1 + g * w2)
            tail_ref[s, :, c0:c1] = g[seg - CONV_TAIL:]
        conv = convs[0] if nseg == 1 else jnp.concatenate(convs, axis=0)
        if not prev_from_input:
            carry[:, c0:c1] = gate[tm - CONV_TAIL:]
        act[:, c0:c1] = (_gelu_tanh(conv) * val).astype(BF16)
    if final_norm:
        o_ref[...] = _rms(x + _dot(act[...], wdn_ref[0]), fnw_ref[...])
    else:
        half = D_MODEL // 2
        for c0 in (0, half):
            o_ref[:, c0:c0 + half] = x[:, c0:c0 + half] + _dot(act[...], wdn_ref[0, :, c0:c0 + half])


def _ffn(x, nw, wup, cw, cb, wdn, prev, fnw, layer, tm, seg, seq, prev_from_input, final_norm):
    r = x.shape[0]
    nseg = tm // seg
    kern = functools.partial(_ffn_kernel, tm=tm, seg=seg, tiles_per_seq=max(seq // tm, 1),
                             prev_from_input=prev_from_input, final_norm=final_norm)
    prev_spec = pl.BlockSpec((nseg, CONV_W - 1, D_FF), lambda i: (i, 0, 0))
    tail_spec = pl.BlockSpec((nseg, CONV_TAIL, D_FF), lambda i: (i, 0, 0))
    wup_spec = pl.BlockSpec((1, D_MODEL, 2 * D_FF), lambda i: (layer, 0, 0), pipeline_mode=pl.Buffered(1))
    wdn_spec = pl.BlockSpec((1, D_FF, D_MODEL), lambda i: (layer, 0, 0), pipeline_mode=pl.Buffered(1))
    return pl.pallas_call(
        kern,
        grid=(r // tm,),
        in_specs=[_rows(tm, D_MODEL), _full((1, D_MODEL)), wup_spec, _full((CONV_W, D_FF)),
                  _full((1, D_FF)), wdn_spec, prev_spec, _full((1, D_MODEL))],
        out_specs=[_rows(tm, D_MODEL), tail_spec],
        out_shape=[jax.ShapeDtypeStruct((r, D_MODEL), F32),
                   jax.ShapeDtypeStruct((r // seg, CONV_TAIL, D_FF), F32)],
        scratch_shapes=[pltpu.VMEM((CONV_TAIL, D_FF), F32), pltpu.VMEM((tm, D_FF), BF16)],
        compiler_params=_params(),
        name="ffn",
    )(x, nw, wup, cw, cb, wdn, prev, fnw)


def _pad_odd_w_in(w):
    cr = w[:, ODD_CR0:ODD_CR0 + GATE_RANK]
    return jnp.concatenate([w[:, :ODD_CR0], cr, jnp.zeros((D_MODEL, LANES - GATE_RANK), w.dtype),
                            w[:, ODD_CR0 + GATE_RANK:]], axis=1).astype(BF16)


def _trunk(x3, pos, prompt, caches, weights):
    (norm_mix_w, norm_ffn_w, final_norm_w, even_w_in, pool_w, pool_scale, swa_sinks, even_w_out, odd_w_in,
     gla_w_gate2, gla_b_gate, gla_norm_w, diff_lambda, diff_norm_w, odd_w_out, ffn_w_up, ffn_conv_w, ffn_conv_b,
     ffn_w_down) = weights
    cache_pool, cache_swa_k, cache_swa_v, state_gla, cache_diff_k, cache_diff_v, cache_ffn_conv = caches
    b, l, _ = x3.shape
    r = b * l
    depth = norm_mix_w.shape[0]
    x = x3.reshape(r, D_MODEL)
    if prompt:
        tm = min(ROW_TILE, l)
        blk = min(CHUNK, l)
        tabs = _rope_tables(pos)
    else:
        tm = r
        blk = min(CHUNK, l)
        tabs = tuple(jnp.tile(t, (b, 1)) for t in _rope_tables(pos))
    pos0 = 0 if prompt else cache_diff_k.shape[2]

    pools, swa_ks, swa_vs, glas, diff_ks, diff_vs, ffns = [], [], [], [], [], [], []
    kv_p = None
    for layer in range(depth):
        p = layer // 2
        nw = norm_mix_w[layer].reshape(1, D_MODEL)
        if layer % 2 == 0:
            u, q, k, v, *tails = _even_in(x, nw, even_w_in[p].astype(BF16), tabs, tm, l, prompt)
            wp = pool_w[p].astype(BF16)
            ps = pool_scale[p].reshape(1, D_A)
            wo = even_w_out[p].astype(BF16)
            if prompt:
                x = _even_mix_prompt(swa_sinks[p], u, q, k, v, x, wp, ps, wo, l, tm)
                pools.append(u.reshape(b, l, D_A)[:, l - POOL_STATE:])
                swa_ks.append(tails[0].reshape(b, WINDOW, HKV_B, HEAD_DIM))
                swa_vs.append(tails[1].reshape(b, WINDOW, HKV_B, HEAD_DIM))
            else:
                rows_c = cache_swa_k.shape[2]
                pool_prev = jnp.pad(cache_pool[p], ((0, 0), (1, 0), (0, 0)))
                kc = cache_swa_k[p].reshape(b, rows_c, D_KV)
                vc = cache_swa_v[p].reshape(b, rows_c, D_KV)
                x = _even_mix_sample(swa_sinks[p], u, q, k, v, x, pool_prev, kc, vc, wp, ps, wo, l, pos0)
                pools.append(jnp.concatenate([cache_pool[p], u.reshape(b, l, D_A)], axis=1)[:, -POOL_STATE:])
                swa_ks.append(jnp.concatenate([cache_swa_k[p], k.reshape(b, l, HKV_B, HEAD_DIM)], axis=1)[:, -rows_c:])
                swa_vs.append(jnp.concatenate([cache_swa_v[p], v.reshape(b, l, HKV_B, HEAD_DIM)], axis=1)[:, -rows_c:])
        else:
            lambda_init = 0.8 - 0.6 * math.exp(-0.3 * layer)
            wg = jnp.pad(gla_w_gate2[p], ((0, LANES - GATE_RANK), (0, 0))).astype(BF16)
            bg = gla_b_gate[p].reshape(1, D_CQK)
            lp = diff_lambda[p]
            dnw = diff_norm_w[p].reshape(1, 2 * HD_D)
            w_in = _pad_odd_w_in(odd_w_in[p])
            if prompt:
                qd, kd, el, cv, cg, dq, *kv_p = _odd_in(x, nw, w_in, wg, bg, tabs, tm, blk,
                                                        pair=(p, depth // 2, b, l, kv_p))
                yd = _diff_prompt(lp, dnw, dq, kv_p[0], kv_p[1], p, b, l, lambda_init, min(ATTN_TILE, l))
                s0 = jnp.zeros((b, H_C, DV_C, DK_C), F32)
            else:
                qd, kd, el, cv, cg, dq, dk, dv = _odd_in(x, nw, w_in, wg, bg, tabs, tm, blk)
                past = cache_diff_k.shape[2]
                kct = jnp.transpose(cache_diff_k, (0, 1, 3, 4, 5, 2)).reshape(depth // 2, b, H_D, 2 * HD_D, past)
                vc = cache_diff_v.reshape(depth // 2, b * past * H_D, 2 * HD_D)
                yd = _diff_sample(lp, dnw, dq, dk, dv, kct, vc, p, b, l, lambda_init)
                s0 = jnp.swapaxes(state_gla[p], -1, -2).astype(F32)
                diff_ks.append(dk.reshape(b, l, H_D, 2, HD_D))
                diff_vs.append(dv.reshape(b, l, H_D, 2 * HD_D))
            x, s_new = _odd_mix(qd, kd, el, cv, cg, yd, x, s0, gla_norm_w[p].reshape(1, DV_C),
                                odd_w_out[p].astype(BF16), tm, blk, l, not prompt)
            glas.append(jnp.swapaxes(s_new, -1, -2))
        if prompt:
            tf = seg = min(FFN_ROW_TILE, l)
            prev = jnp.zeros((r // tf, CONV_W - 1, D_FF), F32)
        else:
            tf, seg = tm, l
            prev = cache_ffn_conv[layer]
        x, tails = _ffn(x, norm_ffn_w[layer].reshape(1, D_MODEL), ffn_w_up.astype(BF16), ffn_conv_w[layer],
                        ffn_conv_b[layer].reshape(1, D_FF), ffn_w_down.astype(BF16), prev,
                        final_norm_w.reshape(1, D_MODEL), layer, tf, seg, l, not prompt, layer == depth - 1)
        tails = tails.reshape(b, -1, CONV_TAIL, D_FF)
        ffns.append(tails[:, -1, CONV_TAIL - (CONV_W - 1):])
    if prompt:
        n_pairs = depth // 2
        diff_k = jnp.transpose(kv_p[0].reshape(n_pairs, b, H_D, 2, HD_D, l), (0, 1, 5, 2, 3, 4))
        diff_v = kv_p[1].reshape(n_pairs, b, l, H_D, 2 * HD_D)
    else:
        diff_k, diff_v = jnp.stack(diff_ks), jnp.stack(diff_vs)
    return (x.reshape(b, l, D_MODEL), jnp.stack(pools), jnp.stack(swa_ks), jnp.stack(swa_vs), jnp.stack(glas),
            diff_k, diff_v, jnp.stack(ffns))


def kernel(x_prompt, x_sample, cache_pool, cache_swa_k, cache_swa_v, state_gla, cache_diff_k, cache_diff_v, cache_ffn_conv, norm_mix_w, norm_ffn_w, final_norm_w, even_w_in, pool_w, pool_scale, swa_sinks, even_w_out, odd_w_in, gla_w_gate2, gla_b_gate, gla_norm_w, diff_lambda, diff_norm_w, odd_w_out, ffn_w_up, ffn_conv_w, ffn_conv_b, ffn_w_down):
    weights = (norm_mix_w, norm_ffn_w, final_norm_w, even_w_in, pool_w, pool_scale, swa_sinks, even_w_out, odd_w_in,
               gla_w_gate2, gla_b_gate, gla_norm_w, diff_lambda, diff_norm_w, odd_w_out, ffn_w_up, ffn_conv_w,
               ffn_conv_b, ffn_w_down)
    caches = (cache_pool, cache_swa_k, cache_swa_v, state_gla, cache_diff_k, cache_diff_v, cache_ffn_conv)
    past = cache_diff_k.shape[2]
    pos_p = jnp.arange(x_prompt.shape[1])
    pos_s = past + jnp.arange(x_sample.shape[1])
    outs_p = _trunk(x_prompt, pos_p, True, caches, weights)
    outs_s = _trunk(x_sample, pos_s, False, caches, weights)
    return (outs_p[0], outs_s[0]) + tuple(outs_p[1:]) + tuple(outs_s[1:])
```

```python
import functools
import math

import jax
import jax.numpy as jnp
from jax import lax
from jax.experimental import pallas as pl
from jax.experimental.pallas import tpu as pltpu

F32 = jnp.float32
BF16 = jnp.bfloat16

D_MODEL = 1024
CHUNK = 64
EPS = 1e-6
NEG_INF = -1e30
LOG2E = 1.0 / math.log(2.0)
ROPE_THETA = 500000.0
HEAD_DIM = 64
ROT_DIM = HEAD_DIM // 4
ROT_HALF = ROT_DIM // 2
LANES = 128

POOL_WINDOWS = (2, 4, 8, 16)
POOL_GROUP = D_MODEL // 8
D_A = len(POOL_WINDOWS) * POOL_GROUP
POOL_STATE = max(POOL_WINDOWS) - 1
POOL_HALO = POOL_STATE + 1

H_B = D_MODEL // 128
HKV_B = 2
GQA_B = H_B // HKV_B
WINDOW = 128
D_B = H_B * HEAD_DIM
D_KV = HKV_B * HEAD_DIM

H_C = 4
DK_C = 64
DV_C = 128
GATE_RANK = 16
GATE_NORM = 16.0
D_CQK = H_C * DK_C
D_C = H_C * DV_C

H_D = 4
HD_D = 64
D_D = H_D * 2 * HD_D

D_FF = 2816
CONV_W = 3
CONV_TAIL = 8

ROW_TILE = 1024
FFN_ROW_TILE = 1024
ATTN_TILE = 1024
FFN_COLS = 256
VMEM_LIMIT = 60 * 1024 * 1024


def _dot(a, b):
    return jnp.dot(a, b, preferred_element_type=F32)


def _dot_nt(a, b):
    return lax.dot_general(a, b, (((1,), (1,)), ((), ())), preferred_element_type=F32)


def _dot_tn(a, b):
    return lax.dot_general(a, b, (((0,), (0,)), ((), ())), preferred_element_type=F32)


def _rms(x, w):
    return x * lax.rsqrt(jnp.mean(x * x, axis=-1, keepdims=True) + EPS) * w


def _rope(z, cos_t, sin_a, sin_b):
    outs = []
    for s in range(z.shape[1] // LANES):
        x = z[:, s * LANES:(s + 1) * LANES]
        outs.append(x * cos_t + pltpu.roll(x, LANES - ROT_HALF, 1) * sin_a + pltpu.roll(x, ROT_HALF, 1) * sin_b)
    return outs[0] if len(outs) == 1 else jnp.concatenate(outs, axis=-1)


def _rope_tables(pos):
    t = pos.shape[0]
    inv = jnp.exp(-math.log(ROPE_THETA) * jnp.arange(ROT_HALF, dtype=F32) * (2.0 / ROT_DIM))
    ang = pos.astype(F32)[:, None] * inv[None, :]
    cos, sin = jnp.cos(ang), jnp.sin(ang)
    rest = HEAD_DIM - ROT_DIM
    cos_t = jnp.concatenate([cos, cos, jnp.ones((t, rest), F32)], axis=-1)
    sin_a = jnp.concatenate([-sin, jnp.zeros((t, HEAD_DIM - ROT_HALF), F32)], axis=-1)
    sin_b = jnp.concatenate([jnp.zeros((t, ROT_HALF), F32), sin, jnp.zeros((t, rest), F32)], axis=-1)
    rep = LANES // HEAD_DIM
    return tuple(jnp.tile(a, (1, rep)) for a in (cos_t, sin_a, sin_b))


def _params(n_axes=1):
    return pltpu.CompilerParams(dimension_semantics=("arbitrary",) * n_axes, vmem_limit_bytes=VMEM_LIMIT)


def _full(shape):
    nd = len(shape)
    return pl.BlockSpec(shape, lambda *_: (0,) * nd, pipeline_mode=pl.Buffered(1))


def _rows(tm, cols):
    return pl.BlockSpec((tm, cols), lambda i: (i, 0))


def _even_in_kernel(x_ref, nw_ref, w_ref, cos_ref, sa_ref, sb_ref, u_ref, q_ref, k_ref, v_ref, *tails,
                    tm, prompt):
    h = _rms(x_ref[...], nw_ref[...]).astype(BF16)
    cos_t, sin_a, sin_b = cos_ref[...], sa_ref[...], sb_ref[...]
    q = _dot(h, w_ref[:, D_A:D_A + D_B])
    k = _dot(h, w_ref[:, D_A + D_B:D_A + D_B + D_KV])
    v = _dot(h, w_ref[:, D_A + D_B + D_KV:D_A + D_B + 2 * D_KV])
    q_ref[...] = (_rope(q, cos_t, sin_a, sin_b) * (HEAD_DIM ** -0.5)).astype(BF16)
    k = _rope(k, cos_t, sin_a, sin_b)
    if prompt:
        k_ref[...] = k.T.astype(BF16)
        v_ref[...] = v.astype(BF16)
        tails[0][0] = k[tm - WINDOW:]
        tails[1][0] = v[tm - WINDOW:]
    else:
        k_ref[...] = k
        v_ref[...] = v
    u_ref[...] = _dot(h, w_ref[:, 0:D_A])


def _even_in(x, nw, w, tabs, tm, seq, prompt):
    r = x.shape[0]
    t_tiles = tabs[0].shape[0] // tm
    tab_spec = pl.BlockSpec((tm, LANES), lambda i: (i % t_tiles, 0))
    n_in = D_A + D_B + 2 * D_KV
    sds = jax.ShapeDtypeStruct
    if prompt:
        tps = seq // tm
        tail_spec = pl.BlockSpec((1, WINDOW, D_KV), lambda i: (i // tps, 0, 0))
        kv_specs = [pl.BlockSpec((D_KV, tm), lambda i: (0, i)), _rows(tm, D_KV), tail_spec, tail_spec]
        kv_shapes = [sds((D_KV, r), BF16), sds((r, D_KV), BF16),
                     sds((r // seq, WINDOW, D_KV), F32), sds((r // seq, WINDOW, D_KV), F32)]
    else:
        kv_specs = [_rows(tm, D_KV), _rows(tm, D_KV)]
        kv_shapes = [sds((r, D_KV), F32), sds((r, D_KV), F32)]
    return pl.pallas_call(
        functools.partial(_even_in_kernel, tm=tm, prompt=prompt),
        grid=(r // tm,),
        in_specs=[_rows(tm, D_MODEL), _full((1, D_MODEL)), _full((D_MODEL, n_in)), tab_spec, tab_spec, tab_spec],
        out_specs=[_rows(tm, D_A), _rows(tm, D_B)] + kv_specs,
        out_shape=[sds((r, D_A), F32), sds((r, D_B), BF16)] + kv_shapes,
        compiler_params=_params(),
        name="even_in",
    )(x, nw, w, *tabs)


def _pool_mix(ext, u, pos, wp_ref, ps, ycat_ref, row0, rows):
    for g, w in enumerate(POOL_WINDOWS):
        lo, hi = g * POOL_GROUP, (g + 1) * POOL_GROUP
        s = ext[:, lo:hi]
        sh = 1
        while sh < w:
            s = s + pltpu.roll(s, sh, 0)
            sh *= 2
        count = jnp.minimum(w, pos + 1).astype(F32)
        diff = s[POOL_HALO:] / count - u[:, lo:hi]
        ya = _dot(diff.astype(BF16), wp_ref[g]) * ps[:, lo:hi]
        ycat_ref[row0:row0 + rows, lo:hi] = ya.astype(BF16)


def _sink_column(sink_ref, hk, rows_per_head):
    n = GQA_B * rows_per_head
    grp = lax.broadcasted_iota(jnp.int32, (n, 1), 0) // rows_per_head
    col = jnp.full((n, 1), sink_ref[hk * GQA_B], F32)
    for g in range(1, GQA_B):
        col = jnp.where(grp == g, sink_ref[hk * GQA_B + g], col)
    return col


def _sink_attend(s, sink_col, vb):
    m = jnp.maximum(jnp.max(s, axis=-1, keepdims=True), sink_col)
    e = jnp.exp(s - m)
    den = jnp.sum(e, axis=-1, keepdims=True) + jnp.exp(sink_col - m)
    return _dot(e.astype(BF16), vb) / den


def _store_heads(ycat_ref, o, row0, rows, hk):
    for pair in range(GQA_B // 2):
        both = jnp.concatenate([o[(2 * pair) * rows:(2 * pair + 1) * rows],
                                o[(2 * pair + 1) * rows:(2 * pair + 2) * rows]], axis=-1)
        c0 = D_A + (hk * GQA_B + 2 * pair) * HEAD_DIM
        ycat_ref[row0:row0 + rows, c0:c0 + 2 * HEAD_DIM] = both.astype(BF16)


def _even_mix_prompt_kernel(sink_ref, u_ref, q_ref, k_ref, v_ref, x_ref, wp_ref, ps_ref, wo_ref, o_ref,
                            ucar, kcar, vcar, ycat, *, tm, tiles_per_seq):
    i = pl.program_id(0)
    ti = i % tiles_per_seq
    first_key_chunk = jnp.where(ti == 0, WINDOW // CHUNK, 0)

    @pl.when(ti == 0)
    def _():
        ucar[...] = jnp.zeros_like(ucar)
        kcar[...] = jnp.zeros_like(kcar)
        vcar[...] = jnp.zeros_like(vcar)

    u = u_ref[...]
    ext = jnp.concatenate([ucar[...], u], axis=0)
    pos = ti * tm + lax.broadcasted_iota(jnp.int32, (tm, POOL_GROUP), 0)
    _pool_mix(ext, u, pos, wp_ref, ps_ref[...], ycat, 0, tm)
    ucar[...] = u[tm - POOL_HALO:]

    kext = jnp.concatenate([kcar[...], k_ref[...]], axis=1)
    vext = jnp.concatenate([vcar[...], v_ref[...]], axis=0)
    kcar[...] = kext[:, tm:]
    vcar[...] = vext[tm:]
    qb = 2 * CHUNK
    kb = qb + WINDOW
    nrow = GQA_B * qb
    qc = (lax.broadcasted_iota(jnp.int32, (nrow, kb), 0) % qb) // CHUNK
    kc = lax.broadcasted_iota(jnp.int32, (nrow, kb), 1) // CHUNK
    band = (kc >= qc) & (kc <= qc + WINDOW // CHUNK)
    first = band & (kc >= first_key_chunk)
    grp = lax.broadcasted_iota(jnp.int32, (nrow, LANES), 0) // qb
    low_lanes = lax.broadcasted_iota(jnp.int32, (qb, LANES), 1) < HEAD_DIM
    ones = jnp.ones((WINDOW + tm, HEAD_DIM), BF16)
    zeros = jnp.zeros((HEAD_DIM, WINDOW + tm), BF16)
    order = [g for g in range(GQA_B) if g % 2 == 0] + [g for g in range(GQA_B) if g % 2 == 1]
    v_aug, sink_rep, k_even, k_odd = [], [], [], []
    for hk in range(HKV_B):
        v_aug.append(jnp.concatenate([vext[:, hk * HEAD_DIM:(hk + 1) * HEAD_DIM], ones], axis=-1))
        kt = kext[hk * HEAD_DIM:(hk + 1) * HEAD_DIM]
        k_even.append(jnp.concatenate([kt, zeros], axis=0))
        k_odd.append(jnp.concatenate([zeros, kt], axis=0))
        rep = jnp.full((nrow, LANES), sink_ref[hk * GQA_B + order[0]], F32)
        for r in range(1, GQA_B):
            rep = jnp.where(grp == r, sink_ref[hk * GQA_B + order[r]], rep)
        sink_rep.append(rep)

    def scores(a, hk):
        def tiles(gs):
            return jnp.concatenate(
                [q_ref[a * qb:(a + 1) * qb, (hk * GQA_B + g) // 2 * LANES:((hk * GQA_B + g) // 2 + 1) * LANES]
                 for g in gs], axis=0)
        half = GQA_B // 2
        return jnp.concatenate([_dot(tiles(order[:half]), k_even[hk][:, a * qb:a * qb + kb]),
                                _dot(tiles(order[half:]), k_odd[hk][:, a * qb:a * qb + kb])], axis=0)

    def softmax(a, hk, s):
        s = jnp.where(first if a == 0 else band, s, NEG_INF)
        m = jnp.maximum(jnp.max(s, axis=-1, keepdims=True), sink_rep[hk])
        return jnp.exp(s - _lane_tile(m, kb // LANES)).astype(BF16), jnp.exp(sink_rep[hk] - m)

    def finish(a, hk, e, sink_term):
        oa = _dot(e, v_aug[hk][a * qb:a * qb + kb])
        inv = 1.0 / (oa + sink_term)
        for pair in range(GQA_B // 2):
            lo = slice(pair * qb, (pair + 1) * qb)
            hi = slice((GQA_B // 2 + pair) * qb, (GQA_B // 2 + pair + 1) * qb)
            both = jnp.where(low_lanes, oa[lo] * pltpu.roll(inv[lo], HEAD_DIM, 1),
                             pltpu.roll(oa[hi], HEAD_DIM, 1) * inv[hi])
            c0 = D_A + (hk * GQA_B + 2 * pair) * HEAD_DIM
            ycat[a * qb:(a + 1) * qb, c0:c0 + 2 * HEAD_DIM] = both.astype(BF16)

    blocks = [(a, hk) for a in range(tm // qb) for hk in range(HKV_B)]
    nb = len(blocks)
    s_q = {0: scores(*blocks[0])}
    e_q = {}
    for n in range(nb + 2):
        if n + 1 < nb:
            s_q[n + 1] = scores(*blocks[n + 1])
        if n < nb:
            e_q[n] = softmax(*blocks[n], s_q.pop(n))
        if 1 <= n <= nb:
            finish(*blocks[n - 1], *e_q.pop(n - 1))
    o_ref[...] = x_ref[...] + _dot(ycat[...], wo_ref[...])


def _even_mix_prompt(sinks, u, q, kt, v, x, wp, ps, wo, seq, tm):
    r = x.shape[0]
    kern = functools.partial(_even_mix_prompt_kernel, tm=tm, tiles_per_seq=seq // tm)
    return pl.pallas_call(
        kern,
        grid=(r // tm,),
        in_specs=[pl.BlockSpec(memory_space=pltpu.SMEM),
                  _rows(tm, D_A), _rows(tm, D_B), pl.BlockSpec((D_KV, tm), lambda i: (0, i)), _rows(tm, D_KV),
                  _rows(tm, D_MODEL), _full(wp.shape), _full((1, D_A)), _full((D_A + D_B, D_MODEL))],
        out_specs=_rows(tm, D_MODEL),
        out_shape=jax.ShapeDtypeStruct((r, D_MODEL), F32),
        scratch_shapes=[pltpu.VMEM((POOL_HALO, D_A), F32), pltpu.VMEM((D_KV, WINDOW), BF16),
                        pltpu.VMEM((WINDOW, D_KV), BF16), pltpu.VMEM((tm, D_A + D_B), BF16)],
        compiler_params=_params(),
        name="even_mix_prompt",
    )(sinks, u, q, kt, v, x, wp, ps, wo)


def _even_mix_sample_kernel(sink_ref, u_ref, q_ref, k_ref, v_ref, x_ref, pp_ref, kc_ref, vc_ref, wp_ref, ps_ref,
                            wo_ref, o_ref, ycat, *, l, pos0):
    u = u_ref[...]
    ext = jnp.concatenate([pp_ref[0], u], axis=0)
    pos = pos0 + lax.broadcasted_iota(jnp.int32, (l, 1), 0)
    _pool_mix(ext, u, pos, wp_ref, ps_ref[...], ycat, 0, l)

    pad = jnp.zeros((LANES - l, D_KV), BF16)
    kall = jnp.concatenate([kc_ref[0].astype(BF16), k_ref[...].astype(BF16), pad], axis=0)
    vall = jnp.concatenate([vc_ref[0].astype(BF16), v_ref[...].astype(BF16), pad], axis=0)
    nk = kall.shape[0]
    valid = lax.broadcasted_iota(jnp.int32, (GQA_B * l, nk), 1) < nk - (LANES - l)
    for hk in range(HKV_B):
        ks = kall[:, hk * HEAD_DIM:(hk + 1) * HEAD_DIM]
        vs = vall[:, hk * HEAD_DIM:(hk + 1) * HEAD_DIM]
        qs = jnp.concatenate(
            [q_ref[:, (hk * GQA_B + g) * HEAD_DIM:(hk * GQA_B + g + 1) * HEAD_DIM] for g in range(GQA_B)], axis=0)
        s = jnp.where(valid, _dot_nt(qs, ks), NEG_INF)
        o = _sink_attend(s, _sink_column(sink_ref, hk, l), vs)
        _store_heads(ycat, o, 0, l, hk)
    o_ref[...] = x_ref[...] + _dot(ycat[...], wo_ref[...])


def _even_mix_sample(sinks, u, q, k, v, x, pool_prev, kc, vc, wp, ps, wo, l, pos0):
    r = x.shape[0]
    rows_c = kc.shape[1]
    kern = functools.partial(_even_mix_sample_kernel, l=l, pos0=pos0)
    cache_spec = pl.BlockSpec((1, rows_c, D_KV), lambda i: (i, 0, 0))
    return pl.pallas_call(
        kern,
        grid=(r // l,),
        in_specs=[pl.BlockSpec(memory_space=pltpu.SMEM),
                  _rows(l, D_A), _rows(l, D_B), _rows(l, D_KV), _rows(l, D_KV), _rows(l, D_MODEL),
                  pl.BlockSpec((1, POOL_HALO, D_A), lambda i: (i, 0, 0)), cache_spec, cache_spec,
                  _full(wp.shape), _full((1, D_A)), _full((D_A + D_B, D_MODEL))],
        out_specs=_rows(l, D_MODEL),
        out_shape=jax.ShapeDtypeStruct((r, D_MODEL), F32),
        scratch_shapes=[pltpu.VMEM((l, D_A + D_B), BF16)],
        compiler_params=_params(),
        name="even_mix_sample",
    )(sinks, u, q, k, v, x, pool_prev, kc, vc, wp, ps, wo)


ODD_CR0 = 2 * D_CQK + 2 * D_C
ODD_DQ0 = ODD_CR0 + LANES
ODD_IN_PADDED = ODD_DQ0 + 3 * D_D


def _odd_in_kernel(x_ref, nw_ref, w_ref, wg_ref, bg_ref, cos_ref, sa_ref, sb_ref, *rest, tm, blk, cache_layout,
                   aliased):
    if aliased:
        rest = rest[2:]
    qd_ref, kd_ref, el_ref, cv_ref, cg_ref, dq_ref, dk_ref, dv_ref, cum_scr = rest
    h = _rms(x_ref[...], nw_ref[...]).astype(BF16)
    cos_t, sin_a, sin_b = cos_ref[...], sa_ref[...], sb_ref[...]
    cr = _dot(h, w_ref[:, ODD_CR0:ODD_DQ0]).astype(BF16)
    gate = _dot(cr, wg_ref[...]) + bg_ref[...]
    cq = _dot(h, w_ref[:, 0:D_CQK])
    ck = _dot(h, w_ref[:, D_CQK:2 * D_CQK])
    dq = _dot(h, w_ref[:, ODD_DQ0:ODD_DQ0 + D_D])
    cum = (jnp.minimum(gate, 0.0) - jnp.log1p(jnp.exp(-jnp.abs(gate)))) * (1.0 / GATE_NORM)
    rowb = lax.broadcasted_iota(jnp.int32, (tm, 1), 0) % blk
    sh = 1
    while sh < blk:
        cum = cum + jnp.where(rowb >= sh, pltpu.roll(cum, sh, 0), 0.0)
        sh *= 2
    nb = tm // blk
    for s in range(D_CQK // LANES):
        cum_scr[s] = cum[:, s * LANES:(s + 1) * LANES]
        last = cum_scr[s, pl.ds(blk - 1, nb, stride=blk), :]
        last = jnp.concatenate([last, jnp.zeros((LANES - nb, LANES), F32)], axis=0)
        el_ref[0, s * LANES:(s + 1) * LANES, :] = jnp.exp(last.T)
    qd_ref[...] = (cq * (DK_C ** -0.5) * jnp.exp(cum)).astype(BF16)
    kd_ref[...] = (ck * jnp.exp(-cum)).T.astype(BF16)
    dk = _dot(h, w_ref[:, ODD_DQ0 + D_D:ODD_DQ0 + 2 * D_D])
    dq_ref[...] = (_rope(dq, cos_t, sin_a, sin_b) * (HD_D ** -0.5 * LOG2E)).astype(BF16)
    dv = _dot(h, w_ref[:, ODD_DQ0 + 2 * D_D:ODD_DQ0 + 3 * D_D])
    dk = _rope(dk, cos_t, sin_a, sin_b)
    cv_ref[...] = _dot(h, w_ref[:, 2 * D_CQK:2 * D_CQK + D_C]).astype(BF16)
    cg_ref[...] = _dot(h, w_ref[:, 2 * D_CQK + D_C:ODD_CR0])
    if cache_layout:
        for hd in range(H_D):
            lo, hi = hd * 2 * HD_D, (hd + 1) * 2 * HD_D
            dk_ref[0, 0, hd] = dk[:, lo:hi].T
            dv_ref[0, pl.ds(hd, tm, stride=H_D), :] = dv[:, lo:hi]
    else:
        dk_ref[...] = dk
        dv_ref[...] = dv


def _odd_in(x, nw, w, wg, bg, tabs, tm, blk, pair=None):
    r = x.shape[0]
    t_tiles = tabs[0].shape[0] // tm
    tab_spec = pl.BlockSpec((tm, LANES), lambda i: (i % t_tiles, 0))
    nb = tm // blk
    sds = jax.ShapeDtypeStruct
    in_specs = [_rows(tm, D_MODEL), _full((1, D_MODEL)), _full((D_MODEL, ODD_IN_PADDED)),
                _full((LANES, D_CQK)), _full((1, D_CQK)), tab_spec, tab_spec, tab_spec]
    args = [x, nw, w, wg, bg, *tabs]
    aliases = {}
    if pair is None:
        kv_specs = [_rows(tm, D_D), _rows(tm, D_D)]
        kv_shapes = [sds((r, D_D), F32), sds((r, D_D), F32)]
    else:
        p, n_pairs, batch, seq, bufs = pair
        tps = seq // tm
        kv_specs = [pl.BlockSpec((1, 1, H_D, 2 * HD_D, tm), lambda i: (p, i // tps, 0, 0, i % tps)),
                    pl.BlockSpec((1, tm * H_D, 2 * HD_D), lambda i: (p, i, 0))]
        kv_shapes = [sds((n_pairs, batch, H_D, 2 * HD_D, seq), F32), sds((n_pairs, r * H_D, 2 * HD_D), F32)]
        if bufs is not None:
            in_specs += [pl.BlockSpec(memory_space=pl.ANY)] * 2
            args += list(bufs)
            aliases = {len(args) - 2: 6, len(args) - 1: 7}
    kern = functools.partial(_odd_in_kernel, tm=tm, blk=blk, cache_layout=pair is not None, aliased=bool(aliases))
    return pl.pallas_call(
        kern,
        grid=(r // tm,),
        in_specs=in_specs,
        out_specs=[_rows(tm, D_CQK), pl.BlockSpec((D_CQK, tm), lambda i: (0, i)),
                   pl.BlockSpec((1, D_CQK, LANES), lambda i: (i, 0, 0)), _rows(tm, D_C), _rows(tm, D_C),
                   _rows(tm, D_D)] + kv_specs,
        out_shape=[sds((r, D_CQK), BF16), sds((D_CQK, r), BF16), sds((r // tm, D_CQK, LANES), F32),
                   sds((r, D_C), BF16), sds((r, D_C), F32), sds((r, D_D), BF16)] + kv_shapes,
        scratch_shapes=[pltpu.VMEM((D_CQK // LANES, tm, LANES), F32)],
        input_output_aliases=aliases,
        compiler_params=_params(),
        name="odd_in",
    )(*args)


def _lambda_value(lp_ref, lambda_init):
    lp = lp_ref[...]
    s1 = jnp.sum(lp[0:1] * lp[1:2], axis=-1, keepdims=True)
    s2 = jnp.sum(lp[2:3] * lp[3:4], axis=-1, keepdims=True)
    return jnp.exp(s1) - jnp.exp(s2) + lambda_init


def _diff_finish(acc, den, lam, nw, lambda_init):
    o = acc[0] / den[0] - lam * (acc[1] / den[1])
    return (_rms(o, nw) * (1.0 - lambda_init)).astype(BF16)


def _lane_tile(x, n):
    return x if n == 1 else jnp.concatenate([x] * n, axis=1)


def _diff_prompt_kernel(lp_ref, nw_ref, q_ref, k_ref, v_ref, o_ref, kbf, vbf, m_scr, acc_scr, s_a, s_b, bias,
                        *, tq, lambda_init):
    hd = pl.program_id(1)
    qi = pl.program_id(2)
    dv = 2 * HD_D
    seq = vbf.shape[0]

    @pl.when(qi == 0)
    def _():
        kbf[...] = k_ref[0, 0, 0].astype(BF16)
        vbf[:, :dv] = v_ref[0, pl.ds(hd, seq, stride=H_D), :].astype(BF16)
        vbf[:, dv:] = jnp.ones((seq, dv), BF16)

    tk = tq // 2

    @pl.when((pl.program_id(0) == 0) & (pl.program_id(1) == 0) & (qi == 0))
    def _():
        qc = lax.broadcasted_iota(jnp.int32, (tk, tk), 0) // CHUNK
        kc = lax.broadcasted_iota(jnp.int32, (tk, tk), 1) // CHUNK
        bias[...] = jnp.where(kc <= qc, 0.0, NEG_INF)

    m_scr[...] = jnp.full(m_scr.shape, NEG_INF, F32)
    acc_scr[...] = jnp.zeros(acc_scr.shape, F32)

    def scores(t, s_buf, row0=0):
        start = pl.multiple_of(t * tk, tk)
        for c in range(2):
            s_buf[c, row0:, :] = _dot(q_ref[row0:, c * HD_D:(c + 1) * HD_D],
                                      kbf[c * HD_D:(c + 1) * HD_D, pl.ds(start, tk)])

    def attend(t, s_buf, row0=0, rows=tq, masked=False):
        vs = vbf[pl.ds(pl.multiple_of(t * tk, tk), tk), :]
        for c in range(2):
            sc = s_buf[c, row0:row0 + rows, :]
            if masked:
                sc = sc + bias[...]
            m_old = m_scr[c, row0:row0 + rows, :]
            m_new = jnp.maximum(m_old, jnp.max(sc, axis=-1, keepdims=True))
            p = jnp.exp2(sc - _lane_tile(m_new, tk // LANES)).astype(BF16)
            alpha = _lane_tile(jnp.exp2(m_old - m_new), 2 * dv // LANES)
            acc_scr[c, row0:row0 + rows, :] = alpha * acc_scr[c, row0:row0 + rows, :] + _dot(p, vs)
            m_scr[c, row0:row0 + rows, :] = m_new

    scores(0, s_a)

    def pair(j, _):
        t = 2 * j
        scores(t + 1, s_b)
        attend(t, s_a)
        scores(t + 2, s_a)
        attend(t + 1, s_b)
        return 0

    lax.fori_loop(0, qi, pair, 0)
    scores(2 * qi + 1, s_b, row0=tk)
    attend(2 * qi, s_a, row0=0, rows=tk, masked=True)
    attend(2 * qi, s_a, row0=tk, rows=tk)
    attend(2 * qi + 1, s_b, row0=tk, rows=tk, masked=True)

    lam = _lambda_value(lp_ref, lambda_init)
    acc = [acc_scr[c] for c in range(2)]
    o_ref[...] = _diff_finish([a[:, :dv] for a in acc], [a[:, dv:] for a in acc], lam, nw_ref[...], lambda_init)


def _diff_prompt(lp, nw, q, kt, v, p, batch, seq, lambda_init, tq):
    r = q.shape[0]
    nq = seq // tq
    kern = functools.partial(_diff_prompt_kernel, tq=tq, lambda_init=lambda_init)
    k_spec = pl.BlockSpec((1, 1, 1, 2 * HD_D, seq), lambda b, h, i: (p, b, h, 0, 0))
    v_spec = pl.BlockSpec((1, seq * H_D, 2 * HD_D), lambda b, h, i: (p, b, 0))
    q_spec = pl.BlockSpec((tq, 2 * HD_D), lambda b, h, i: (b * nq + i, h))
    return pl.pallas_call(
        kern,
        grid=(batch, H_D, nq),
        in_specs=[pl.BlockSpec((4, HD_D), lambda b, h, i: (0, 0)), pl.BlockSpec((1, 2 * HD_D), lambda b, h, i: (0, 0)),
                  q_spec, k_spec, v_spec],
        out_specs=q_spec,
        out_shape=jax.ShapeDtypeStruct((r, D_D), BF16),
        scratch_shapes=[pltpu.VMEM((2 * HD_D, seq), BF16), pltpu.VMEM((seq, 4 * HD_D), BF16),
                        pltpu.VMEM((2, tq, LANES), F32), pltpu.VMEM((2, tq, 4 * HD_D), F32),
                        pltpu.VMEM((2, tq, tq // 2), F32), pltpu.VMEM((2, tq, tq // 2), F32),
                        pltpu.VMEM((tq // 2, tq // 2), F32)],
        compiler_params=_params(3),
        name="diff_prompt",
    )(lp, nw, q, kt, v)


def _diff_sample_kernel(lp_ref, nw_ref, q_ref, k_ref, v_ref, kc_ref, vc_ref, o_ref, *, l, past, lambda_init):
    pad = jnp.zeros((LANES - l, 2 * HD_D), BF16)
    valid = lax.broadcasted_iota(jnp.int32, (l, LANES), 1) < l
    lam = _lambda_value(lp_ref, lambda_init)
    for hd in range(H_D):
        lo, hi = hd * 2 * HD_D, (hd + 1) * 2 * HD_D
        knew = jnp.concatenate([k_ref[:, lo:hi].astype(BF16), pad], axis=0)
        vnew = jnp.concatenate([v_ref[:, lo:hi].astype(BF16), pad], axis=0)
        kct = kc_ref[0, 0, hd].astype(BF16)
        vcache = vc_ref[0, pl.ds(hd, past, stride=H_D), :].astype(BF16)
        acc, den = [], []
        for c in range(2):
            qc = q_ref[:, lo + c * HD_D:lo + (c + 1) * HD_D]
            s_old = _dot(qc, kct[c * HD_D:(c + 1) * HD_D, :])
            s_new = jnp.where(valid, _dot_nt(qc, knew[:, c * HD_D:(c + 1) * HD_D]), NEG_INF)
            m = jnp.maximum(jnp.max(s_old, axis=-1, keepdims=True), jnp.max(s_new, axis=-1, keepdims=True))
            p_old = jnp.exp2(s_old - m)
            p_new = jnp.exp2(s_new - m)
            den.append(jnp.sum(p_old, axis=-1, keepdims=True) + jnp.sum(p_new, axis=-1, keepdims=True))
            acc.append(_dot(p_old.astype(BF16), vcache) + _dot(p_new.astype(BF16), vnew))
        o_ref[:, lo:hi] = _diff_finish(acc, den, lam, nw_ref[...], lambda_init)


def _diff_sample(lp, nw, q, k, v, kct, vc, p, batch, l, lambda_init):
    r = q.shape[0]
    past = kct.shape[-1]
    kern = functools.partial(_diff_sample_kernel, l=l, past=past, lambda_init=lambda_init)
    cache_k_spec = pl.BlockSpec((1, 1, H_D, 2 * HD_D, past), lambda b: (p, b, 0, 0, 0))
    cache_v_spec = pl.BlockSpec((1, past * H_D, 2 * HD_D), lambda b: (p, b, 0))
    return pl.pallas_call(
        kern,
        grid=(batch,),
        in_specs=[_full((4, HD_D)), _full((1, 2 * HD_D)), _rows(l, D_D), _rows(l, D_D), _rows(l, D_D),
                  cache_k_spec, cache_v_spec],
        out_specs=_rows(l, D_D),
        out_shape=jax.ShapeDtypeStruct((r, D_D), BF16),
        compiler_params=_params(),
        name="diff_sample",
    )(lp, nw, q, k, v, kct, vc)


def _odd_mix_kernel(qd_ref, kd_ref, el_ref, cv_ref, cg_ref, yd_ref, x_ref, s0_ref, gw_ref, wo_ref,
                    o_ref, s_ref, state, og, ycat, *, tm, blk, tiles_per_seq, per_block_state):
    i = pl.program_id(0)
    if not per_block_state:
        @pl.when(i % tiles_per_seq == 0)
        def _():
            state[...] = jnp.zeros_like(state)

    per_group = LANES // blk
    rows_of = lax.broadcasted_iota(jnp.int32, (LANES, LANES), 0)
    cols_of = lax.broadcasted_iota(jnp.int32, (LANES, LANES), 1)
    q_row = lax.broadcasted_iota(jnp.int32, (blk, LANES), 0)
    q_col = lax.broadcasted_iota(jnp.int32, (blk, LANES), 1)
    low = [(h % 2) * DK_C for h in range(H_C)]
    own_rows = [(rows_of >= low[h]) & (rows_of < low[h] + DK_C) for h in range(H_C)]
    lanes = [slice(h // 2 * LANES, (h // 2 + 1) * LANES) for h in range(H_C)]
    vcols = [slice(h * DV_C, (h + 1) * DV_C) for h in range(H_C)]
    st = [None if per_block_state else state[h] for h in range(H_C)]
    for n in range(tm // blk):
        g0, c = (n // per_group) * LANES, n % per_group
        r0 = n * blk
        causal = (q_col >= c * blk) & (q_col - c * blk <= q_row)
        own_cols = (cols_of >= c * blk) & (cols_of < (c + 1) * blk)
        for h in range(H_C):
            if per_block_state:
                st[h] = jnp.where(own_rows[h], jnp.concatenate([s0_ref[n, h]] * 2, axis=0), 0.0)
            qd = qd_ref[r0:r0 + blk, lanes[h]]
            kt = jnp.where(own_rows[h], kd_ref[lanes[h], g0:g0 + LANES], jnp.zeros((), BF16))
            vg = cv_ref[g0:g0 + LANES, vcols[h]]
            att = jnp.where(causal, _dot(qd, kt), 0.0).astype(BF16)
            og[r0:r0 + blk, vcols[h]] = _dot(att, vg) + _dot(qd, st[h].astype(BF16))
            k_blk = jnp.where(own_cols, kt, jnp.zeros((), BF16))
            decay = jnp.where(own_rows[h], jnp.broadcast_to(el_ref[0, lanes[h], n:n + 1], (LANES, LANES)), 0.0)
            st[h] = decay * (st[h] + _dot(k_blk, vg))
            if per_block_state:
                s_ref[n, h] = st[h][low[h]:low[h] + DK_C]
    if not per_block_state:
        for h in range(H_C):
            state[h] = st[h]
            s_ref[0, h] = st[h][low[h]:low[h] + DK_C]

    gw = gw_ref[...]
    for h in range(H_C):
        lo, hi = h * DV_C, (h + 1) * DV_C
        g = cg_ref[:, lo:hi]
        y = _rms(og[:, lo:hi], gw) * (g * (1.0 / (1.0 + jnp.exp(-g))))
        ycat[:, lo:hi] = y.astype(BF16)
    ycat[:, D_C:] = yd_ref[...]
    o_ref[...] = x_ref[...] + _dot(ycat[...], wo_ref[...])


def _odd_mix(qd, kd, el, cv, cg, yd, x, s0, gw, wo, tm, blk, seq, per_block_state):
    assert tm % LANES == 0 and LANES % blk == 0 and tm // blk <= LANES
    r = x.shape[0]
    nb = tm // blk
    tiles_per_seq = max(seq // tm, 1)
    kern = functools.partial(_odd_mix_kernel, tm=tm, blk=blk, tiles_per_seq=tiles_per_seq,
                             per_block_state=per_block_state)
    if per_block_state:
        n_state = r // blk
        state_spec = pl.BlockSpec((nb, H_C, DK_C, DV_C), lambda i: (i, 0, 0, 0))
    else:
        n_state = r // seq
        state_spec = pl.BlockSpec((1, H_C, DK_C, DV_C), lambda i: (i // tiles_per_seq, 0, 0, 0))
    return pl.pallas_call(
        kern,
        grid=(r // tm,),
        in_specs=[_rows(tm, D_CQK), pl.BlockSpec((D_CQK, tm), lambda i: (0, i)),
                  pl.BlockSpec((1, D_CQK, LANES), lambda i: (i, 0, 0)), _rows(tm, D_C), _rows(tm, D_C),
                  _rows(tm, D_D), _rows(tm, D_MODEL), state_spec, _full((1, DV_C)), _full((D_C + D_D, D_MODEL))],
        out_specs=[_rows(tm, D_MODEL), state_spec],
        out_shape=[jax.ShapeDtypeStruct((r, D_MODEL), F32),
                   jax.ShapeDtypeStruct((n_state, H_C, DK_C, DV_C), F32)],
        scratch_shapes=[pltpu.VMEM((H_C, LANES, DV_C), F32), pltpu.VMEM((tm, D_C), F32),
                        pltpu.VMEM((tm, D_C + D_D), BF16)],
        compiler_params=_params(),
        name="odd_mix",
    )(qd, kd, el, cv, cg, yd, x, s0, gw, wo)


def _gelu_tanh(x):
    return 0.5 * x * (1.0 + jnp.tanh(math.sqrt(2.0 / math.pi) * (x + 0.044715 * (x * x * x))))


def _ffn_kernel(x_ref, nw_ref, wup_ref, cw_ref, cb_ref, wdn_ref, prev_ref, fnw_ref, o_ref, tail_ref, carry, act,
                *, tm, seg, tiles_per_seq, prev_from_input, final_norm):
    i = pl.program_id(0)
    x = x_ref[...]
    nseg = tm // seg
    row = lax.broadcasted_iota(jnp.int32, (seg, 1), 0)
    if not prev_from_input:
        @pl.when(i % tiles_per_seq == 0)
        def _():
            carry[...] = jnp.zeros_like(carry)

    def up(hh, c0):
        return (_dot(hh, wup_ref[0, :, c0:c0 + FFN_COLS]),
                _dot(hh, wup_ref[0, :, D_FF + c0:D_FF + c0 + FFN_COLS]))

    h = _rms(x, nw_ref[...]).astype(BF16)
    nxt = up(h, 0)
    for c0 in range(0, D_FF, FFN_COLS):
        c1 = c0 + FFN_COLS
        gate, val = nxt
        if c1 < D_FF:
            nxt = up(h, c1)
        w0, w1, w2 = cw_ref[0:1, c0:c1], cw_ref[1:2, c0:c1], cw_ref[2:3, c0:c1]
        convs = []
        for s in range(nseg):
            g = gate[s * seg:(s + 1) * seg]
            if prev_from_input:
                p0, p1 = prev_ref[s, 0:1, c0:c1], prev_ref[s, 1:2, c0:c1]
            else:
                p0 = carry[CONV_TAIL - 2:CONV_TAIL - 1, c0:c1]
                p1 = carry[CONV_TAIL - 1:CONV_TAIL, c0:c1]
            g1 = jnp.where(row == 0, p1, pltpu.roll(g, 1, 0))
            g2 = jnp.where(row == 0, p0, jnp.where(row == 1, p1, pltpu.roll(g, 2, 0)))
            convs.append(cb_ref[:, c0:c1] + g2 * w0 + g1 * w1 + g * w2)
            tail_ref[s, :, c0:c1] = g[seg - CONV_TAIL:]
        conv = convs[0] if nseg == 1 else jnp.concatenate(convs, axis=0)
        if not prev_from_input:
            carry[:, c0:c1] = gate[tm - CONV_TAIL:]
        act[:, c0:c1] = (_gelu_tanh(conv) * val).astype(BF16)
    if final_norm:
        o_ref[...] = _rms(x + _dot(act[...], wdn_ref[0]), fnw_ref[...])
    else:
        half = D_MODEL // 2
        for c0 in (0, half):
            o_ref[:, c0:c0 + half] = x[:, c0:c0 + half] + _dot(act[...], wdn_ref[0, :, c0:c0 + half])


def _ffn(x, nw, wup, cw, cb, wdn, prev, fnw, layer, tm, seg, seq, prev_from_input, final_norm):
    r = x.shape[0]
    nseg = tm // seg
    kern = functools.partial(_ffn_kernel, tm=tm, seg=seg, tiles_per_seq=max(seq // tm, 1),
                             prev_from_input=prev_from_input, final_norm=final_norm)
    prev_spec = pl.BlockSpec((nseg, CONV_W - 1, D_FF), lambda i: (i, 0, 0))
    tail_spec = pl.BlockSpec((nseg, CONV_TAIL, D_FF), lambda i: (i, 0, 0))
    wup_spec = pl.BlockSpec((1, D_MODEL, 2 * D_FF), lambda i: (layer, 0, 0), pipeline_mode=pl.Buffered(1))
    wdn_spec = pl.BlockSpec((1, D_FF, D_MODEL), lambda i: (layer, 0, 0), pipeline_mode=pl.Buffered(1))
    return pl.pallas_call(
        kern,
        grid=(r // tm,),
        in_specs=[_rows(tm, D_MODEL), _full((1, D_MODEL)), wup_spec, _full((CONV_W, D_FF)),
                  _full((1, D_FF)), wdn_spec, prev_spec, _full((1, D_MODEL))],
        out_specs=[_rows(tm, D_MODEL), tail_spec],
        out_shape=[jax.ShapeDtypeStruct((r, D_MODEL), F32),
                   jax.ShapeDtypeStruct((r // seg, CONV_TAIL, D_FF), F32)],
        scratch_shapes=[pltpu.VMEM((CONV_TAIL, D_FF), F32), pltpu.VMEM((tm, D_FF), BF16)],
        compiler_params=_params(),
        name="ffn",
    )(x, nw, wup, cw, cb, wdn, prev, fnw)


def _pad_odd_w_in(w):
    cr = w[:, ODD_CR0:ODD_CR0 + GATE_RANK]
    return jnp.concatenate([w[:, :ODD_CR0], cr, jnp.zeros((D_MODEL, LANES - GATE_RANK), w.dtype),
                            w[:, ODD_CR0 + GATE_RANK:]], axis=1).astype(BF16)


def _trunk(x3, pos, prompt, caches, weights):
    (norm_mix_w, norm_ffn_w, final_norm_w, even_w_in, pool_w, pool_scale, swa_sinks, even_w_out, odd_w_in,
     gla_w_gate2, gla_b_gate, gla_norm_w, diff_lambda, diff_norm_w, odd_w_out, ffn_w_up, ffn_conv_w, ffn_conv_b,
     ffn_w_down) = weights
    cache_pool, cache_swa_k, cache_swa_v, state_gla, cache_diff_k, cache_diff_v, cache_ffn_conv = caches
    b, l, _ = x3.shape
    r = b * l
    depth = norm_mix_w.shape[0]
    x = x3.reshape(r, D_MODEL)
    if prompt:
        tm = min(ROW_TILE, l)
        blk = min(CHUNK, l)
        tabs = _rope_tables(pos)
    else:
        tm = r
        blk = min(CHUNK, l)
        tabs = tuple(jnp.tile(t, (b, 1)) for t in _rope_tables(pos))
    pos0 = 0 if prompt else cache_diff_k.shape[2]

    pools, swa_ks, swa_vs, glas, diff_ks, diff_vs, ffns = [], [], [], [], [], [], []
    kv_p = None
    for layer in range(depth):
        p = layer // 2
        nw = norm_mix_w[layer].reshape(1, D_MODEL)
        if layer % 2 == 0:
            u, q, k, v, *tails = _even_in(x, nw, even_w_in[p].astype(BF16), tabs, tm, l, prompt)
            wp = pool_w[p].astype(BF16)
            ps = pool_scale[p].reshape(1, D_A)
            wo = even_w_out[p].astype(BF16)
            if prompt:
                x = _even_mix_prompt(swa_sinks[p], u, q, k, v, x, wp, ps, wo, l, tm)
                pools.append(u.reshape(b, l, D_A)[:, l - POOL_STATE:])
                swa_ks.append(tails[0].reshape(b, WINDOW, HKV_B, HEAD_DIM))
                swa_vs.append(tails[1].reshape(b, WINDOW, HKV_B, HEAD_DIM))
            else:
                rows_c = cache_swa_k.shape[2]
                pool_prev = jnp.pad(cache_pool[p], ((0, 0), (1, 0), (0, 0)))
                kc = cache_swa_k[p].reshape(b, rows_c, D_KV)
                vc = cache_swa_v[p].reshape(b, rows_c, D_KV)
                x = _even_mix_sample(swa_sinks[p], u, q, k, v, x, pool_prev, kc, vc, wp, ps, wo, l, pos0)
                pools.append(jnp.concatenate([cache_pool[p], u.reshape(b, l, D_A)], axis=1)[:, -POOL_STATE:])
                swa_ks.append(jnp.concatenate([cache_swa_k[p], k.reshape(b, l, HKV_B, HEAD_DIM)], axis=1)[:, -rows_c:])
                swa_vs.append(jnp.concatenate([cache_swa_v[p], v.reshape(b, l, HKV_B, HEAD_DIM)], axis=1)[:, -rows_c:])
        else:
            lambda_init = 0.8 - 0.6 * math.exp(-0.3 * layer)
            wg = jnp.pad(gla_w_gate2[p], ((0, LANES - GATE_RANK), (0, 0))).astype(BF16)
            bg = gla_b_gate[p].reshape(1, D_CQK)
            lp = diff_lambda[p]
            dnw = diff_norm_w[p].reshape(1, 2 * HD_D)
            w_in = _pad_odd_w_in(odd_w_in[p])
            if prompt:
                qd, kd, el, cv, cg, dq, *kv_p = _odd_in(x, nw, w_in, wg, bg, tabs, tm, blk,
                                                        pair=(p, depth // 2, b, l, kv_p))
                yd = _diff_prompt(lp, dnw, dq, kv_p[0], kv_p[1], p, b, l, lambda_init, min(ATTN_TILE, l))
                s0 = jnp.zeros((b, H_C, DK_C, DV_C), F32)
            else:
                qd, kd, el, cv, cg, dq, dk, dv = _odd_in(x, nw, w_in, wg, bg, tabs, tm, blk)
                past = cache_diff_k.shape[2]
                kct = jnp.transpose(cache_diff_k, (0, 1, 3, 4, 5, 2)).reshape(depth // 2, b, H_D, 2 * HD_D, past)
                vc = cache_diff_v.reshape(depth // 2, b * past * H_D, 2 * HD_D)
                yd = _diff_sample(lp, dnw, dq, dk, dv, kct, vc, p, b, l, lambda_init)
                s0 = state_gla[p]
                diff_ks.append(dk.reshape(b, l, H_D, 2, HD_D))
                diff_vs.append(dv.reshape(b, l, H_D, 2 * HD_D))
            x, s_new = _odd_mix(qd, kd, el, cv, cg, yd, x, s0, gla_norm_w[p].reshape(1, DV_C),
                                odd_w_out[p].astype(BF16), tm, blk, l, not prompt)
            glas.append(s_new)
        if prompt:
            tf = seg = min(FFN_ROW_TILE, l)
            prev = jnp.zeros((r // tf, CONV_W - 1, D_FF), F32)
        else:
            tf, seg = tm, l
            prev = cache_ffn_conv[layer]
        x, tails = _ffn(x, norm_ffn_w[layer].reshape(1, D_MODEL), ffn_w_up.astype(BF16), ffn_conv_w[layer],
                        ffn_conv_b[layer].reshape(1, D_FF), ffn_w_down.astype(BF16), prev,
                        final_norm_w.reshape(1, D_MODEL), layer, tf, seg, l, not prompt, layer == depth - 1)
        tails = tails.reshape(b, -1, CONV_TAIL, D_FF)
        ffns.append(tails[:, -1, CONV_TAIL - (CONV_W - 1):])
    if prompt:
        n_pairs = depth // 2
        diff_k = jnp.transpose(kv_p[0].reshape(n_pairs, b, H_D, 2, HD_D, l), (0, 1, 5, 2, 3, 4))
        diff_v = kv_p[1].reshape(n_pairs, b, l, H_D, 2 * HD_D)
    else:
        diff_k, diff_v = jnp.stack(diff_ks), jnp.stack(diff_vs)
    return (x.reshape(b, l, D_MODEL), jnp.stack(pools), jnp.stack(swa_ks), jnp.stack(swa_vs), jnp.stack(glas),
            diff_k, diff_v, jnp.stack(ffns))


def kernel(x_prompt, x_sample, cache_pool, cache_swa_k, cache_swa_v, state_gla, cache_diff_k, cache_diff_v, cache_ffn_conv, norm_mix_w, norm_ffn_w, final_norm_w, even_w_in, pool_w, pool_scale, swa_sinks, even_w_out, odd_w_in, gla_w_gate2, gla_b_gate, gla_norm_w, diff_lambda, diff_norm_w, odd_w_out, ffn_w_up, ffn_conv_w, ffn_conv_b, ffn_w_down):
    weights = (norm_mix_w, norm_ffn_w, final_norm_w, even_w_in, pool_w, pool_scale, swa_sinks, even_w_out, odd_w_in,
               gla_w_gate2, gla_b_gate, gla_norm_w, diff_lambda, diff_norm_w, odd_w_out, ffn_w_up, ffn_conv_w,
               ffn_conv_b, ffn_w_down)
    caches = (cache_pool, cache_swa_k, cache_swa_v, state_gla, cache_diff_k, cache_diff_v, cache_ffn_conv)
    past = cache_diff_k.shape[2]
    pos_p = jnp.arange(x_prompt.shape[1])
    pos_s = past + jnp.arange(x_sample.shape[1])
    outs_p = _trunk(x_prompt, pos_p, True, caches, weights)
    outs_s = _trunk(x_sample, pos_s, False, caches, weights)
    return (outs_p[0], outs_s[0]) + tuple(outs_p[1:]) + tuple(outs_s[1:])
```

```python
import functools
import math

import jax
import jax.numpy as jnp
from jax import lax
from jax.experimental import pallas as pl
from jax.experimental.pallas import tpu as pltpu

F32 = jnp.float32
BF16 = jnp.bfloat16

D_MODEL = 1024
CHUNK = 64
EPS = 1e-6
NEG_INF = -1e30
LOG2E = 1.0 / math.log(2.0)
ROPE_THETA = 500000.0
HEAD_DIM = 64
ROT_DIM = HEAD_DIM // 4
ROT_HALF = ROT_DIM // 2
LANES = 128

POOL_WINDOWS = (2, 4, 8, 16)
POOL_GROUP = D_MODEL // 8
D_A = len(POOL_WINDOWS) * POOL_GROUP
POOL_STATE = max(POOL_WINDOWS) - 1
POOL_HALO = POOL_STATE + 1

H_B = D_MODEL // 128
HKV_B = 2
GQA_B = H_B // HKV_B
WINDOW = 128
D_B = H_B * HEAD_DIM
D_KV = HKV_B * HEAD_DIM

H_C = 4
DK_C = 64
DV_C = 128
GATE_RANK = 16
GATE_NORM = 16.0
D_CQK = H_C * DK_C
D_C = H_C * DV_C

H_D = 4
HD_D = 64
D_D = H_D * 2 * HD_D

D_FF = 2816
CONV_W = 3
CONV_TAIL = 8

ROW_TILE = 1024
FFN_ROW_TILE = 1024
ATTN_TILE = 1024
FFN_COLS = 256
VMEM_LIMIT = 60 * 1024 * 1024


def _dot(a, b):
    return jnp.dot(a, b, preferred_element_type=F32)


def _dot_nt(a, b):
    return lax.dot_general(a, b, (((1,), (1,)), ((), ())), preferred_element_type=F32)


def _rms(x, w):
    return x * lax.rsqrt(jnp.mean(x * x, axis=-1, keepdims=True) + EPS) * w


def _rope(z, cos_t, sin_a, sin_b):
    outs = []
    for s in range(z.shape[1] // LANES):
        x = z[:, s * LANES:(s + 1) * LANES]
        outs.append(x * cos_t + pltpu.roll(x, LANES - ROT_HALF, 1) * sin_a + pltpu.roll(x, ROT_HALF, 1) * sin_b)
    return outs[0] if len(outs) == 1 else jnp.concatenate(outs, axis=-1)


def _rope_tables(pos):
    t = pos.shape[0]
    inv = jnp.exp(-math.log(ROPE_THETA) * jnp.arange(ROT_HALF, dtype=F32) * (2.0 / ROT_DIM))
    ang = pos.astype(F32)[:, None] * inv[None, :]
    cos, sin = jnp.cos(ang), jnp.sin(ang)
    rest = HEAD_DIM - ROT_DIM
    cos_t = jnp.concatenate([cos, cos, jnp.ones((t, rest), F32)], axis=-1)
    sin_a = jnp.concatenate([-sin, jnp.zeros((t, HEAD_DIM - ROT_HALF), F32)], axis=-1)
    sin_b = jnp.concatenate([jnp.zeros((t, ROT_HALF), F32), sin, jnp.zeros((t, rest), F32)], axis=-1)
    rep = LANES // HEAD_DIM
    return tuple(jnp.tile(a, (1, rep)) for a in (cos_t, sin_a, sin_b))


def _params(n_axes=1):
    return pltpu.CompilerParams(dimension_semantics=("arbitrary",) * n_axes, vmem_limit_bytes=VMEM_LIMIT)


def _full(shape):
    nd = len(shape)
    return pl.BlockSpec(shape, lambda *_: (0,) * nd, pipeline_mode=pl.Buffered(1))


def _layer(shape, index):
    nd = len(shape)
    return pl.BlockSpec((1,) + tuple(shape), lambda *_: (index,) + (0,) * nd, pipeline_mode=pl.Buffered(1))


def _rows(tm, cols):
    return pl.BlockSpec((tm, cols), lambda i: (i, 0))


def _even_in_kernel(x_ref, nw_ref, w_ref, cos_ref, sa_ref, sb_ref, u_ref, q_ref, k_ref, v_ref, *tails,
                    tm, prompt):
    h = _rms(x_ref[...], nw_ref[...]).astype(BF16)
    cos_t, sin_a, sin_b = cos_ref[...], sa_ref[...], sb_ref[...]
    q = _dot(h, w_ref[0, :, D_A:D_A + D_B])
    k = _dot(h, w_ref[0, :, D_A + D_B:D_A + D_B + D_KV])
    v = _dot(h, w_ref[0, :, D_A + D_B + D_KV:D_A + D_B + 2 * D_KV])
    q_ref[...] = (_rope(q, cos_t, sin_a, sin_b) * (HEAD_DIM ** -0.5)).astype(BF16)
    k = _rope(k, cos_t, sin_a, sin_b)
    if prompt:
        k_ref[...] = k.T.astype(BF16)
        v_ref[...] = v.astype(BF16)
        tails[0][0] = k[tm - WINDOW:]
        tails[1][0] = v[tm - WINDOW:]
    else:
        k_ref[...] = k
        v_ref[...] = v
    u_ref[...] = _dot(h, w_ref[0, :, 0:D_A])


def _even_in(x, nw, w, p, tabs, tm, seq, prompt):
    r = x.shape[0]
    t_tiles = tabs[0].shape[0] // tm
    tab_spec = pl.BlockSpec((tm, LANES), lambda i: (i % t_tiles, 0))
    n_in = D_A + D_B + 2 * D_KV
    sds = jax.ShapeDtypeStruct
    if prompt:
        tps = seq // tm
        tail_spec = pl.BlockSpec((1, WINDOW, D_KV), lambda i: (i // tps, 0, 0))
        kv_specs = [pl.BlockSpec((D_KV, tm), lambda i: (0, i)), _rows(tm, D_KV), tail_spec, tail_spec]
        kv_shapes = [sds((D_KV, r), BF16), sds((r, D_KV), BF16),
                     sds((r // seq, WINDOW, D_KV), F32), sds((r // seq, WINDOW, D_KV), F32)]
    else:
        kv_specs = [_rows(tm, D_KV), _rows(tm, D_KV)]
        kv_shapes = [sds((r, D_KV), F32), sds((r, D_KV), F32)]
    return pl.pallas_call(
        functools.partial(_even_in_kernel, tm=tm, prompt=prompt),
        grid=(r // tm,),
        in_specs=[_rows(tm, D_MODEL), _full((1, D_MODEL)), _layer((D_MODEL, n_in), p), tab_spec, tab_spec,
                  tab_spec],
        out_specs=[_rows(tm, D_A), _rows(tm, D_B)] + kv_specs,
        out_shape=[sds((r, D_A), F32), sds((r, D_B), BF16)] + kv_shapes,
        compiler_params=_params(),
        name="even_in",
    )(x, nw, w, *tabs)


def _pool_mix(ext, u, pos, wp_ref, ps, ycat_ref, row0, rows):
    for g, w in enumerate(POOL_WINDOWS):
        lo, hi = g * POOL_GROUP, (g + 1) * POOL_GROUP
        s = ext[:, lo:hi]
        sh = 1
        while sh < w:
            s = s + pltpu.roll(s, sh, 0)
            sh *= 2
        count = jnp.minimum(w, pos + 1).astype(F32)
        diff = s[POOL_HALO:] / count - u[:, lo:hi]
        ya = _dot(diff.astype(BF16), wp_ref[0, g]) * ps[:, lo:hi]
        ycat_ref[row0:row0 + rows, lo:hi] = ya.astype(BF16)


def _sink_column(sink_ref, hk, rows_per_head):
    n = GQA_B * rows_per_head
    grp = lax.broadcasted_iota(jnp.int32, (n, 1), 0) // rows_per_head
    col = jnp.full((n, 1), sink_ref[hk * GQA_B], F32)
    for g in range(1, GQA_B):
        col = jnp.where(grp == g, sink_ref[hk * GQA_B + g], col)
    return col


def _sink_attend(s, sink_col, vb):
    m = jnp.maximum(jnp.max(s, axis=-1, keepdims=True), sink_col)
    e = jnp.exp(s - m)
    den = jnp.sum(e, axis=-1, keepdims=True) + jnp.exp(sink_col - m)
    return _dot(e.astype(BF16), vb) / den


def _store_heads(ycat_ref, o, row0, rows, hk):
    for pair in range(GQA_B // 2):
        both = jnp.concatenate([o[(2 * pair) * rows:(2 * pair + 1) * rows],
                                o[(2 * pair + 1) * rows:(2 * pair + 2) * rows]], axis=-1)
        c0 = D_A + (hk * GQA_B + 2 * pair) * HEAD_DIM
        ycat_ref[row0:row0 + rows, c0:c0 + 2 * HEAD_DIM] = both.astype(BF16)


def _even_mix_prompt_kernel(sink_ref, u_ref, q_ref, k_ref, v_ref, x_ref, wp_ref, ps_ref, wo_ref, o_ref,
                            ucar, kcar, vcar, ycat, *, tm, tiles_per_seq):
    i = pl.program_id(0)
    ti = i % tiles_per_seq
    first_key_chunk = jnp.where(ti == 0, WINDOW // CHUNK, 0)

    @pl.when(ti == 0)
    def _():
        ucar[...] = jnp.zeros_like(ucar)
        kcar[...] = jnp.zeros_like(kcar)
        vcar[...] = jnp.zeros_like(vcar)

    u = u_ref[...]
    ext = jnp.concatenate([ucar[...], u], axis=0)
    pos = ti * tm + lax.broadcasted_iota(jnp.int32, (tm, POOL_GROUP), 0)
    _pool_mix(ext, u, pos, wp_ref, ps_ref[...], ycat, 0, tm)
    ucar[...] = u[tm - POOL_HALO:]

    kext = jnp.concatenate([kcar[...], k_ref[...]], axis=1)
    vext = jnp.concatenate([vcar[...], v_ref[...]], axis=0)
    kcar[...] = kext[:, tm:]
    vcar[...] = vext[tm:]
    qb = 2 * CHUNK
    kb = qb + WINDOW
    nrow = GQA_B * qb
    qc = (lax.broadcasted_iota(jnp.int32, (nrow, kb), 0) % qb) // CHUNK
    kc = lax.broadcasted_iota(jnp.int32, (nrow, kb), 1) // CHUNK
    band = (kc >= qc) & (kc <= qc + WINDOW // CHUNK)
    first = band & (kc >= first_key_chunk)
    grp = lax.broadcasted_iota(jnp.int32, (nrow, LANES), 0) // qb
    low_lanes = lax.broadcasted_iota(jnp.int32, (qb, LANES), 1) < HEAD_DIM
    ones = jnp.ones((WINDOW + tm, HEAD_DIM), BF16)
    zeros = jnp.zeros((HEAD_DIM, WINDOW + tm), BF16)
    order = [g for g in range(GQA_B) if g % 2 == 0] + [g for g in range(GQA_B) if g % 2 == 1]
    v_aug, sink_rep, k_even, k_odd = [], [], [], []
    for hk in range(HKV_B):
        v_aug.append(jnp.concatenate([vext[:, hk * HEAD_DIM:(hk + 1) * HEAD_DIM], ones], axis=-1))
        kt = kext[hk * HEAD_DIM:(hk + 1) * HEAD_DIM]
        k_even.append(jnp.concatenate([kt, zeros], axis=0))
        k_odd.append(jnp.concatenate([zeros, kt], axis=0))
        rep = jnp.full((nrow, LANES), sink_ref[hk * GQA_B + order[0]], F32)
        for r in range(1, GQA_B):
            rep = jnp.where(grp == r, sink_ref[hk * GQA_B + order[r]], rep)
        sink_rep.append(rep)

    def scores(a, hk):
        def tiles(gs):
            return jnp.concatenate(
                [q_ref[a * qb:(a + 1) * qb, (hk * GQA_B + g) // 2 * LANES:((hk * GQA_B + g) // 2 + 1) * LANES]
                 for g in gs], axis=0)
        half = GQA_B // 2
        return jnp.concatenate([_dot(tiles(order[:half]), k_even[hk][:, a * qb:a * qb + kb]),
                                _dot(tiles(order[half:]), k_odd[hk][:, a * qb:a * qb + kb])], axis=0)

    def softmax(a, hk, s):
        s = jnp.where(first if a == 0 else band, s, NEG_INF)
        m = jnp.maximum(jnp.max(s, axis=-1, keepdims=True), sink_rep[hk])
        return jnp.exp(s - _lane_tile(m, kb // LANES)).astype(BF16), jnp.exp(sink_rep[hk] - m)

    def finish(a, hk, e, sink_term):
        oa = _dot(e, v_aug[hk][a * qb:a * qb + kb])
        inv = 1.0 / (oa + sink_term)
        for pair in range(GQA_B // 2):
            lo = slice(pair * qb, (pair + 1) * qb)
            hi = slice((GQA_B // 2 + pair) * qb, (GQA_B // 2 + pair + 1) * qb)
            both = jnp.where(low_lanes, oa[lo] * pltpu.roll(inv[lo], HEAD_DIM, 1),
                             pltpu.roll(oa[hi], HEAD_DIM, 1) * inv[hi])
            c0 = D_A + (hk * GQA_B + 2 * pair) * HEAD_DIM
            ycat[a * qb:(a + 1) * qb, c0:c0 + 2 * HEAD_DIM] = both.astype(BF16)

    blocks = [(a, hk) for a in range(tm // qb) for hk in range(HKV_B)]
    nb = len(blocks)
    s_q = {0: scores(*blocks[0])}
    e_q = {}
    for n in range(nb + 2):
        if n + 1 < nb:
            s_q[n + 1] = scores(*blocks[n + 1])
        if n < nb:
            e_q[n] = softmax(*blocks[n], s_q.pop(n))
        if 1 <= n <= nb:
            finish(*blocks[n - 1], *e_q.pop(n - 1))
    o_ref[...] = x_ref[...] + _dot(ycat[...], wo_ref[0])


def _even_mix_prompt(sinks, u, q, kt, v, x, wp, ps, wo, p, seq, tm):
    r = x.shape[0]
    kern = functools.partial(_even_mix_prompt_kernel, tm=tm, tiles_per_seq=seq // tm)
    return pl.pallas_call(
        kern,
        grid=(r // tm,),
        in_specs=[pl.BlockSpec(memory_space=pltpu.SMEM),
                  _rows(tm, D_A), _rows(tm, D_B), pl.BlockSpec((D_KV, tm), lambda i: (0, i)), _rows(tm, D_KV),
                  _rows(tm, D_MODEL), _layer(wp.shape[1:], p), _full((1, D_A)), _layer((D_A + D_B, D_MODEL), p)],
        out_specs=_rows(tm, D_MODEL),
        out_shape=jax.ShapeDtypeStruct((r, D_MODEL), F32),
        scratch_shapes=[pltpu.VMEM((POOL_HALO, D_A), F32), pltpu.VMEM((D_KV, WINDOW), BF16),
                        pltpu.VMEM((WINDOW, D_KV), BF16), pltpu.VMEM((tm, D_A + D_B), BF16)],
        compiler_params=_params(),
        name="even_mix_prompt",
    )(sinks, u, q, kt, v, x, wp, ps, wo)


def _even_mix_sample_kernel(sink_ref, u_ref, q_ref, k_ref, v_ref, x_ref, pp_ref, kc_ref, vc_ref, wp_ref, ps_ref,
                            wo_ref, o_ref, ycat, *, l, pos0):
    u = u_ref[...]
    ext = jnp.concatenate([pp_ref[0], u], axis=0)
    pos = pos0 + lax.broadcasted_iota(jnp.int32, (l, 1), 0)
    _pool_mix(ext, u, pos, wp_ref, ps_ref[...], ycat, 0, l)

    pad = jnp.zeros((LANES - l, D_KV), BF16)
    kall = jnp.concatenate([kc_ref[0].astype(BF16), k_ref[...].astype(BF16), pad], axis=0)
    vall = jnp.concatenate([vc_ref[0].astype(BF16), v_ref[...].astype(BF16), pad], axis=0)
    nk = kall.shape[0]
    valid = lax.broadcasted_iota(jnp.int32, (GQA_B * l, nk), 1) < nk - (LANES - l)
    for hk in range(HKV_B):
        ks = kall[:, hk * HEAD_DIM:(hk + 1) * HEAD_DIM]
        vs = vall[:, hk * HEAD_DIM:(hk + 1) * HEAD_DIM]
        qs = jnp.concatenate(
            [q_ref[:, (hk * GQA_B + g) * HEAD_DIM:(hk * GQA_B + g + 1) * HEAD_DIM] for g in range(GQA_B)], axis=0)
        s = jnp.where(valid, _dot_nt(qs, ks), NEG_INF)
        o = _sink_attend(s, _sink_column(sink_ref, hk, l), vs)
        _store_heads(ycat, o, 0, l, hk)
    o_ref[...] = x_ref[...] + _dot(ycat[...], wo_ref[0])


def _even_mix_sample(sinks, u, q, k, v, x, pool_prev, kc, vc, wp, ps, wo, p, l, pos0):
    r = x.shape[0]
    rows_c = kc.shape[1]
    kern = functools.partial(_even_mix_sample_kernel, l=l, pos0=pos0)
    cache_spec = pl.BlockSpec((1, rows_c, D_KV), lambda i: (i, 0, 0))
    return pl.pallas_call(
        kern,
        grid=(r // l,),
        in_specs=[pl.BlockSpec(memory_space=pltpu.SMEM),
                  _rows(l, D_A), _rows(l, D_B), _rows(l, D_KV), _rows(l, D_KV), _rows(l, D_MODEL),
                  pl.BlockSpec((1, POOL_HALO, D_A), lambda i: (i, 0, 0)), cache_spec, cache_spec,
                  _layer(wp.shape[1:], p), _full((1, D_A)), _layer((D_A + D_B, D_MODEL), p)],
        out_specs=_rows(l, D_MODEL),
        out_shape=jax.ShapeDtypeStruct((r, D_MODEL), F32),
        scratch_shapes=[pltpu.VMEM((l, D_A + D_B), BF16)],
        compiler_params=_params(),
        name="even_mix_sample",
    )(sinks, u, q, k, v, x, pool_prev, kc, vc, wp, ps, wo)


ODD_CR0 = 2 * D_CQK + 2 * D_C
ODD_DQ0 = ODD_CR0 + LANES
ODD_IN_PADDED = ODD_DQ0 + 3 * D_D


def _odd_in_kernel(x_ref, nw_ref, w_ref, wg_ref, bg_ref, cos_ref, sa_ref, sb_ref, *rest, tm, blk, cache_layout,
                   aliased):
    if aliased:
        rest = rest[2:]
    qd_ref, kd_ref, el_ref, cv_ref, cg_ref, dq_ref, dk_ref, dv_ref, cum_scr = rest
    h = _rms(x_ref[...], nw_ref[...]).astype(BF16)
    cos_t, sin_a, sin_b = cos_ref[...], sa_ref[...], sb_ref[...]
    cr = _dot(h, w_ref[0, :, ODD_CR0:ODD_DQ0]).astype(BF16)
    gate = _dot(cr, wg_ref[0]) + bg_ref[...]
    cq = _dot(h, w_ref[0, :, 0:D_CQK])
    ck = _dot(h, w_ref[0, :, D_CQK:2 * D_CQK])
    dq = _dot(h, w_ref[0, :, ODD_DQ0:ODD_DQ0 + D_D])
    cum = (jnp.minimum(gate, 0.0) - jnp.log1p(jnp.exp(-jnp.abs(gate)))) * (1.0 / GATE_NORM)
    rowb = lax.broadcasted_iota(jnp.int32, (tm, 1), 0) % blk
    sh = 1
    while sh < blk:
        cum = cum + jnp.where(rowb >= sh, pltpu.roll(cum, sh, 0), 0.0)
        sh *= 2
    nb = tm // blk
    for s in range(D_CQK // LANES):
        cum_scr[s] = cum[:, s * LANES:(s + 1) * LANES]
        last = cum_scr[s, pl.ds(blk - 1, nb, stride=blk), :]
        last = jnp.concatenate([last, jnp.zeros((LANES - nb, LANES), F32)], axis=0)
        el_ref[0, s * LANES:(s + 1) * LANES, :] = jnp.exp(last.T)
    qd_ref[...] = (cq * (DK_C ** -0.5) * jnp.exp(cum)).astype(BF16)
    kd_ref[...] = (ck * jnp.exp(-cum)).T.astype(BF16)
    dk = _dot(h, w_ref[0, :, ODD_DQ0 + D_D:ODD_DQ0 + 2 * D_D])
    dq_ref[...] = (_rope(dq, cos_t, sin_a, sin_b) * (HD_D ** -0.5 * LOG2E)).astype(BF16)
    dv = _dot(h, w_ref[0, :, ODD_DQ0 + 2 * D_D:ODD_DQ0 + 3 * D_D])
    dk = _rope(dk, cos_t, sin_a, sin_b)
    cv_ref[...] = _dot(h, w_ref[0, :, 2 * D_CQK:2 * D_CQK + D_C]).astype(BF16)
    cg_ref[...] = _dot(h, w_ref[0, :, 2 * D_CQK + D_C:ODD_CR0])
    if cache_layout:
        for hd in range(H_D):
            lo, hi = hd * 2 * HD_D, (hd + 1) * 2 * HD_D
            dk_ref[0, 0, hd] = dk[:, lo:hi].T
            dv_ref[0, pl.ds(hd, tm, stride=H_D), :] = dv[:, lo:hi]
    else:
        dk_ref[...] = dk
        dv_ref[...] = dv


def _odd_in(x, nw, w, wg, wi, bg, tabs, tm, blk, pair=None):
    r = x.shape[0]
    t_tiles = tabs[0].shape[0] // tm
    tab_spec = pl.BlockSpec((tm, LANES), lambda i: (i % t_tiles, 0))
    nb = tm // blk
    sds = jax.ShapeDtypeStruct
    in_specs = [_rows(tm, D_MODEL), _full((1, D_MODEL)), _layer((D_MODEL, ODD_IN_PADDED), wi),
                _layer((LANES, D_CQK), wi), _full((1, D_CQK)), tab_spec, tab_spec, tab_spec]
    args = [x, nw, w, wg, bg, *tabs]
    aliases = {}
    if pair is None:
        kv_specs = [_rows(tm, D_D), _rows(tm, D_D)]
        kv_shapes = [sds((r, D_D), F32), sds((r, D_D), F32)]
    else:
        p, n_pairs, batch, seq, bufs = pair
        tps = seq // tm
        kv_specs = [pl.BlockSpec((1, 1, H_D, 2 * HD_D, tm), lambda i: (p, i // tps, 0, 0, i % tps)),
                    pl.BlockSpec((1, tm * H_D, 2 * HD_D), lambda i: (p, i, 0))]
        kv_shapes = [sds((n_pairs, batch, H_D, 2 * HD_D, seq), F32), sds((n_pairs, r * H_D, 2 * HD_D), F32)]
        if bufs is not None:
            in_specs += [pl.BlockSpec(memory_space=pl.ANY)] * 2
            args += list(bufs)
            aliases = {len(args) - 2: 6, len(args) - 1: 7}
    kern = functools.partial(_odd_in_kernel, tm=tm, blk=blk, cache_layout=pair is not None, aliased=bool(aliases))
    return pl.pallas_call(
        kern,
        grid=(r // tm,),
        in_specs=in_specs,
        out_specs=[_rows(tm, D_CQK), pl.BlockSpec((D_CQK, tm), lambda i: (0, i)),
                   pl.BlockSpec((1, D_CQK, LANES), lambda i: (i, 0, 0)), _rows(tm, D_C), _rows(tm, D_C),
                   _rows(tm, D_D)] + kv_specs,
        out_shape=[sds((r, D_CQK), BF16), sds((D_CQK, r), BF16), sds((r // tm, D_CQK, LANES), F32),
                   sds((r, D_C), BF16), sds((r, D_C), F32), sds((r, D_D), BF16)] + kv_shapes,
        scratch_shapes=[pltpu.VMEM((D_CQK // LANES, tm, LANES), F32)],
        input_output_aliases=aliases,
        compiler_params=_params(),
        name="odd_in",
    )(*args)


def _lambda_value(lp_ref, lambda_init):
    lp = lp_ref[...]
    s1 = jnp.sum(lp[0:1] * lp[1:2], axis=-1, keepdims=True)
    s2 = jnp.sum(lp[2:3] * lp[3:4], axis=-1, keepdims=True)
    return jnp.exp(s1) - jnp.exp(s2) + lambda_init


def _diff_finish(acc, den, lam, nw, lambda_init):
    o = acc[0] / den[0] - lam * (acc[1] / den[1])
    return (_rms(o, nw) * (1.0 - lambda_init)).astype(BF16)


def _lane_tile(x, n):
    return x if n == 1 else jnp.concatenate([x] * n, axis=1)


def _diff_prompt_kernel(lp_ref, nw_ref, q_ref, k_ref, v_ref, o_ref, kbf, vbf, m_scr, acc_scr, s_a, s_b, bias,
                        *, tq, lambda_init):
    hd = pl.program_id(1)
    qi = pl.program_id(2)
    dv = 2 * HD_D
    seq = vbf.shape[0]

    @pl.when(qi == 0)
    def _():
        kbf[...] = k_ref[0, 0, 0].astype(BF16)
        vbf[:, :dv] = v_ref[0, pl.ds(hd, seq, stride=H_D), :].astype(BF16)
        vbf[:, dv:] = jnp.ones((seq, dv), BF16)

    tk = tq // 2

    @pl.when((pl.program_id(0) == 0) & (pl.program_id(1) == 0) & (qi == 0))
    def _():
        qc = lax.broadcasted_iota(jnp.int32, (tk, tk), 0) // CHUNK
        kc = lax.broadcasted_iota(jnp.int32, (tk, tk), 1) // CHUNK
        bias[...] = jnp.where(kc <= qc, 0.0, NEG_INF)

    m_scr[...] = jnp.full(m_scr.shape, NEG_INF, F32)
    acc_scr[...] = jnp.zeros(acc_scr.shape, F32)

    def scores(t, s_buf, row0=0):
        start = pl.multiple_of(t * tk, tk)
        for c in range(2):
            s_buf[c, row0:, :] = _dot(q_ref[row0:, c * HD_D:(c + 1) * HD_D],
                                      kbf[c * HD_D:(c + 1) * HD_D, pl.ds(start, tk)])

    def attend(t, s_buf, row0=0, rows=tq, masked=False):
        vs = vbf[pl.ds(pl.multiple_of(t * tk, tk), tk), :]
        for c in range(2):
            sc = s_buf[c, row0:row0 + rows, :]
            if masked:
                sc = sc + bias[...]
            m_old = m_scr[c, row0:row0 + rows, :]
            m_new = jnp.maximum(m_old, jnp.max(sc, axis=-1, keepdims=True))
            p = jnp.exp2(sc - _lane_tile(m_new, tk // LANES)).astype(BF16)
            alpha = _lane_tile(jnp.exp2(m_old - m_new), 2 * dv // LANES)
            acc_scr[c, row0:row0 + rows, :] = alpha * acc_scr[c, row0:row0 + rows, :] + _dot(p, vs)
            m_scr[c, row0:row0 + rows, :] = m_new

    scores(0, s_a)

    def pair(j, _):
        t = 2 * j
        scores(t + 1, s_b)
        attend(t, s_a)
        scores(t + 2, s_a)
        attend(t + 1, s_b)
        return 0

    lax.fori_loop(0, qi, pair, 0)
    scores(2 * qi + 1, s_b, row0=tk)
    attend(2 * qi, s_a, row0=0, rows=tk, masked=True)
    attend(2 * qi, s_a, row0=tk, rows=tk)
    attend(2 * qi + 1, s_b, row0=tk, rows=tk, masked=True)

    lam = _lambda_value(lp_ref, lambda_init)
    acc = [acc_scr[c] for c in range(2)]
    o_ref[...] = _diff_finish([a[:, :dv] for a in acc], [a[:, dv:] for a in acc], lam, nw_ref[...], lambda_init)


def _diff_prompt(lp, nw, q, kt, v, p, batch, seq, lambda_init, tq):
    r = q.shape[0]
    nq = seq // tq
    kern = functools.partial(_diff_prompt_kernel, tq=tq, lambda_init=lambda_init)
    k_spec = pl.BlockSpec((1, 1, 1, 2 * HD_D, seq), lambda b, h, i: (p, b, h, 0, 0))
    v_spec = pl.BlockSpec((1, seq * H_D, 2 * HD_D), lambda b, h, i: (p, b, 0))
    q_spec = pl.BlockSpec((tq, 2 * HD_D), lambda b, h, i: (b * nq + i, h))
    return pl.pallas_call(
        kern,
        grid=(batch, H_D, nq),
        in_specs=[pl.BlockSpec((4, HD_D), lambda b, h, i: (0, 0)), pl.BlockSpec((1, 2 * HD_D), lambda b, h, i: (0, 0)),
                  q_spec, k_spec, v_spec],
        out_specs=q_spec,
        out_shape=jax.ShapeDtypeStruct((r, D_D), BF16),
        scratch_shapes=[pltpu.VMEM((2 * HD_D, seq), BF16), pltpu.VMEM((seq, 4 * HD_D), BF16),
                        pltpu.VMEM((2, tq, LANES), F32), pltpu.VMEM((2, tq, 4 * HD_D), F32),
                        pltpu.VMEM((2, tq, tq // 2), F32), pltpu.VMEM((2, tq, tq // 2), F32),
                        pltpu.VMEM((tq // 2, tq // 2), F32)],
        compiler_params=_params(3),
        name="diff_prompt",
    )(lp, nw, q, kt, v)


def _diff_sample_kernel(lp_ref, nw_ref, q_ref, k_ref, v_ref, kc_ref, vc_ref, o_ref, *, l, past, lambda_init):
    pad = jnp.zeros((LANES - l, 2 * HD_D), BF16)
    valid = lax.broadcasted_iota(jnp.int32, (l, LANES), 1) < l
    lam = _lambda_value(lp_ref, lambda_init)
    for hd in range(H_D):
        lo, hi = hd * 2 * HD_D, (hd + 1) * 2 * HD_D
        knew = jnp.concatenate([k_ref[:, lo:hi].astype(BF16), pad], axis=0)
        vnew = jnp.concatenate([v_ref[:, lo:hi].astype(BF16), pad], axis=0)
        kct = kc_ref[0, 0, hd].astype(BF16)
        vcache = vc_ref[0, pl.ds(hd, past, stride=H_D), :].astype(BF16)
        acc, den = [], []
        for c in range(2):
            qc = q_ref[:, lo + c * HD_D:lo + (c + 1) * HD_D]
            s_old = _dot(qc, kct[c * HD_D:(c + 1) * HD_D, :])
            s_new = jnp.where(valid, _dot_nt(qc, knew[:, c * HD_D:(c + 1) * HD_D]), NEG_INF)
            m = jnp.maximum(jnp.max(s_old, axis=-1, keepdims=True), jnp.max(s_new, axis=-1, keepdims=True))
            p_old = jnp.exp2(s_old - m)
            p_new = jnp.exp2(s_new - m)
            den.append(jnp.sum(p_old, axis=-1, keepdims=True) + jnp.sum(p_new, axis=-1, keepdims=True))
            acc.append(_dot(p_old.astype(BF16), vcache) + _dot(p_new.astype(BF16), vnew))
        o_ref[:, lo:hi] = _diff_finish(acc, den, lam, nw_ref[...], lambda_init)


def _diff_sample(lp, nw, q, k, v, kct, vc, p, batch, l, lambda_init):
    r = q.shape[0]
    past = kct.shape[-1]
    kern = functools.partial(_diff_sample_kernel, l=l, past=past, lambda_init=lambda_init)
    cache_k_spec = pl.BlockSpec((1, 1, H_D, 2 * HD_D, past), lambda b: (p, b, 0, 0, 0))
    cache_v_spec = pl.BlockSpec((1, past * H_D, 2 * HD_D), lambda b: (p, b, 0))
    return pl.pallas_call(
        kern,
        grid=(batch,),
        in_specs=[_full((4, HD_D)), _full((1, 2 * HD_D)), _rows(l, D_D), _rows(l, D_D), _rows(l, D_D),
                  cache_k_spec, cache_v_spec],
        out_specs=_rows(l, D_D),
        out_shape=jax.ShapeDtypeStruct((r, D_D), BF16),
        compiler_params=_params(),
        name="diff_sample",
    )(lp, nw, q, k, v, kct, vc)


def _odd_mix_kernel(qd_ref, kd_ref, el_ref, cv_ref, cg_ref, yd_ref, x_ref, s0_ref, gw_ref, wo_ref,
                    o_ref, s_ref, state, og, ycat, *, tm, blk, tiles_per_seq, per_block_state):
    i = pl.program_id(0)
    if not per_block_state:
        @pl.when(i % tiles_per_seq == 0)
        def _():
            state[...] = jnp.zeros_like(state)

    per_group = LANES // blk
    rows_of = lax.broadcasted_iota(jnp.int32, (LANES, LANES), 0)
    cols_of = lax.broadcasted_iota(jnp.int32, (LANES, LANES), 1)
    q_row = lax.broadcasted_iota(jnp.int32, (blk, LANES), 0)
    q_col = lax.broadcasted_iota(jnp.int32, (blk, LANES), 1)
    low = [(h % 2) * DK_C for h in range(H_C)]
    own_rows = [(rows_of >= low[h]) & (rows_of < low[h] + DK_C) for h in range(H_C)]
    lanes = [slice(h // 2 * LANES, (h // 2 + 1) * LANES) for h in range(H_C)]
    vcols = [slice(h * DV_C, (h + 1) * DV_C) for h in range(H_C)]
    st = [None if per_block_state else state[h] for h in range(H_C)]
    for n in range(tm // blk):
        g0, c = (n // per_group) * LANES, n % per_group
        r0 = n * blk
        causal = (q_col >= c * blk) & (q_col - c * blk <= q_row)
        own_cols = (cols_of >= c * blk) & (cols_of < (c + 1) * blk)
        for h in range(H_C):
            if per_block_state:
                st[h] = jnp.where(own_rows[h], jnp.concatenate([s0_ref[n, h]] * 2, axis=0), 0.0)
            qd = qd_ref[r0:r0 + blk, lanes[h]]
            kt = jnp.where(own_rows[h], kd_ref[lanes[h], g0:g0 + LANES], jnp.zeros((), BF16))
            vg = cv_ref[g0:g0 + LANES, vcols[h]]
            att = jnp.where(causal, _dot(qd, kt), 0.0).astype(BF16)
            og[r0:r0 + blk, vcols[h]] = _dot(att, vg) + _dot(qd, st[h].astype(BF16))
            k_blk = jnp.where(own_cols, kt, jnp.zeros((), BF16))
            decay = jnp.where(own_rows[h], jnp.broadcast_to(el_ref[0, lanes[h], n:n + 1], (LANES, LANES)), 0.0)
            st[h] = decay * (st[h] + _dot(k_blk, vg))
            if per_block_state:
                s_ref[n, h] = st[h][low[h]:low[h] + DK_C]
    if not per_block_state:
        for h in range(H_C):
            state[h] = st[h]
            s_ref[0, h] = st[h][low[h]:low[h] + DK_C]

    gw = gw_ref[...]
    for h in range(H_C):
        lo, hi = h * DV_C, (h + 1) * DV_C
        g = cg_ref[:, lo:hi]
        y = _rms(og[:, lo:hi], gw) * (g * (1.0 / (1.0 + jnp.exp(-g))))
        ycat[:, lo:hi] = y.astype(BF16)
    ycat[:, D_C:] = yd_ref[...]
    o_ref[...] = x_ref[...] + _dot(ycat[...], wo_ref[0])


def _odd_mix(qd, kd, el, cv, cg, yd, x, s0, gw, wo, p, tm, blk, seq, per_block_state):
    assert tm % LANES == 0 and LANES % blk == 0 and tm // blk <= LANES
    r = x.shape[0]
    nb = tm // blk
    tiles_per_seq = max(seq // tm, 1)
    kern = functools.partial(_odd_mix_kernel, tm=tm, blk=blk, tiles_per_seq=tiles_per_seq,
                             per_block_state=per_block_state)
    if per_block_state:
        n_state = r // blk
        state_spec = pl.BlockSpec((nb, H_C, DK_C, DV_C), lambda i: (i, 0, 0, 0))
    else:
        n_state = r // seq
        state_spec = pl.BlockSpec((1, H_C, DK_C, DV_C), lambda i: (i // tiles_per_seq, 0, 0, 0))
    return pl.pallas_call(
        kern,
        grid=(r // tm,),
        in_specs=[_rows(tm, D_CQK), pl.BlockSpec((D_CQK, tm), lambda i: (0, i)),
                  pl.BlockSpec((1, D_CQK, LANES), lambda i: (i, 0, 0)), _rows(tm, D_C), _rows(tm, D_C),
                  _rows(tm, D_D), _rows(tm, D_MODEL), state_spec, _full((1, DV_C)),
                  _layer((D_C + D_D, D_MODEL), p)],
        out_specs=[_rows(tm, D_MODEL), state_spec],
        out_shape=[jax.ShapeDtypeStruct((r, D_MODEL), F32),
                   jax.ShapeDtypeStruct((n_state, H_C, DK_C, DV_C), F32)],
        scratch_shapes=[pltpu.VMEM((H_C, LANES, DV_C), F32), pltpu.VMEM((tm, D_C), F32),
                        pltpu.VMEM((tm, D_C + D_D), BF16)],
        compiler_params=_params(),
        name="odd_mix",
    )(qd, kd, el, cv, cg, yd, x, s0, gw, wo)


def _gelu_tanh(x):
    return 0.5 * x * (1.0 + jnp.tanh(math.sqrt(2.0 / math.pi) * (x + 0.044715 * (x * x * x))))


def _ffn_kernel(x_ref, nw_ref, wup_ref, cw_ref, cb_ref, wdn_ref, prev_ref, fnw_ref, o_ref, tail_ref, carry, act,
                *, tm, seg, tiles_per_seq, prev_from_input, final_norm):
    i = pl.program_id(0)
    x = x_ref[...]
    nseg = tm // seg
    row = lax.broadcasted_iota(jnp.int32, (seg, 1), 0)
    if not prev_from_input:
        @pl.when(i % tiles_per_seq == 0)
        def _():
            carry[...] = jnp.zeros_like(carry)

    def up(hh, c0):
        return (_dot(hh, wup_ref[0, :, c0:c0 + FFN_COLS]),
                _dot(hh, wup_ref[0, :, D_FF + c0:D_FF + c0 + FFN_COLS]))

    h = _rms(x, nw_ref[...]).astype(BF16)
    nxt = up(h, 0)
    for c0 in range(0, D_FF, FFN_COLS):
        c1 = c0 + FFN_COLS
        gate, val = nxt
        if c1 < D_FF:
            nxt = up(h, c1)
        w0, w1, w2 = cw_ref[0:1, c0:c1], cw_ref[1:2, c0:c1], cw_ref[2:3, c0:c1]
        convs = []
        for s in range(nseg):
            g = gate[s * seg:(s + 1) * seg]
            if prev_from_input:
                p0, p1 = prev_ref[s, 0:1, c0:c1], prev_ref[s, 1:2, c0:c1]
            else:
                p0 = carry[CONV_TAIL - 2:CONV_TAIL - 1, c0:c1]
                p1 = carry[CONV_TAIL - 1:CONV_TAIL, c0:c1]
            g1 = jnp.where(row == 0, p1, pltpu.roll(g, 1, 0))
            g2 = jnp.where(row == 0, p0, jnp.where(row == 1, p1, pltpu.roll(g, 2, 0)))
            convs.append(cb_ref[:, c0:c1] + g2 * w0 + g1 * w1 + g * w2)
            tail_ref[s, :, c0:c1] = g[seg - CONV_TAIL:]
        conv = convs[0] if nseg == 1 else jnp.concatenate(convs, axis=0)
        if not prev_from_input:
            carry[:, c0:c1] = gate[tm - CONV_TAIL:]
        act[:, c0:c1] = (_gelu_tanh(conv) * val).astype(BF16)
    if final_norm:
        o_ref[...] = _rms(x + _dot(act[...], wdn_ref[0]), fnw_ref[...])
    else:
        half = D_MODEL // 2
        for c0 in (0, half):
            o_ref[:, c0:c0 + half] = x[:, c0:c0 + half] + _dot(act[...], wdn_ref[0, :, c0:c0 + half])


def _ffn(x, nw, wup, cw, cb, wdn, prev, fnw, layer, tm, seg, seq, prev_from_input, final_norm):
    r = x.shape[0]
    nseg = tm // seg
    kern = functools.partial(_ffn_kernel, tm=tm, seg=seg, tiles_per_seq=max(seq // tm, 1),
                             prev_from_input=prev_from_input, final_norm=final_norm)
    prev_spec = pl.BlockSpec((nseg, CONV_W - 1, D_FF), lambda i: (i, 0, 0))
    tail_spec = pl.BlockSpec((nseg, CONV_TAIL, D_FF), lambda i: (i, 0, 0))
    wup_spec = pl.BlockSpec((1, D_MODEL, 2 * D_FF), lambda i: (layer, 0, 0), pipeline_mode=pl.Buffered(1))
    wdn_spec = pl.BlockSpec((1, D_FF, D_MODEL), lambda i: (layer, 0, 0), pipeline_mode=pl.Buffered(1))
    return pl.pallas_call(
        kern,
        grid=(r // tm,),
        in_specs=[_rows(tm, D_MODEL), _full((1, D_MODEL)), wup_spec, _full((CONV_W, D_FF)),
                  _full((1, D_FF)), wdn_spec, prev_spec, _full((1, D_MODEL))],
        out_specs=[_rows(tm, D_MODEL), tail_spec],
        out_shape=[jax.ShapeDtypeStruct((r, D_MODEL), F32),
                   jax.ShapeDtypeStruct((r // seg, CONV_TAIL, D_FF), F32)],
        scratch_shapes=[pltpu.VMEM((CONV_TAIL, D_FF), F32), pltpu.VMEM((tm, D_FF), BF16)],
        compiler_params=_params(),
        name="ffn",
    )(x, nw, wup, cw, cb, wdn, prev, fnw)


def _pad_odd_w_in(w):
    cr = w[..., ODD_CR0:ODD_CR0 + GATE_RANK]
    pad = jnp.zeros(w.shape[:-1] + (LANES - GATE_RANK,), w.dtype)
    return jnp.concatenate([w[..., :ODD_CR0], cr, pad, w[..., ODD_CR0 + GATE_RANK:]], axis=-1).astype(BF16)


def _trunk(x3, pos, prompt, caches, weights):
    (norm_mix_w, norm_ffn_w, final_norm_w, even_w_in, pool_w, pool_scale, swa_sinks, even_w_out, odd_w_in,
     gla_w_gate2, gla_b_gate, gla_norm_w, diff_lambda, diff_norm_w, odd_w_out, ffn_w_up, ffn_conv_w, ffn_conv_b,
     ffn_w_down) = weights
    cache_pool, cache_swa_k, cache_swa_v, state_gla, cache_diff_k, cache_diff_v, cache_ffn_conv = caches
    b, l, _ = x3.shape
    r = b * l
    depth = norm_mix_w.shape[0]
    x = x3.reshape(r, D_MODEL)
    if prompt:
        tm = min(ROW_TILE, l)
        blk = min(CHUNK, l)
        tabs = _rope_tables(pos)
    else:
        tm = r
        blk = min(CHUNK, l)
        tabs = tuple(jnp.tile(t, (b, 1)) for t in _rope_tables(pos))
    pos0 = 0 if prompt else cache_diff_k.shape[2]

    pools, swa_ks, swa_vs, glas, diff_ks, diff_vs, ffns = [], [], [], [], [], [], []
    kv_p = None
    for layer in range(depth):
        p = layer // 2
        nw = norm_mix_w[layer].reshape(1, D_MODEL)
        if layer % 2 == 0:
            u, q, k, v, *tails = _even_in(x, nw, even_w_in.astype(BF16), p, tabs, tm, l, prompt)
            wp = pool_w.astype(BF16)
            ps = pool_scale[p].reshape(1, D_A)
            wo = even_w_out.astype(BF16)
            if prompt:
                x = _even_mix_prompt(swa_sinks[p], u, q, k, v, x, wp, ps, wo, p, l, tm)
                pools.append(u.reshape(b, l, D_A)[:, l - POOL_STATE:])
                swa_ks.append(tails[0].reshape(b, WINDOW, HKV_B, HEAD_DIM))
                swa_vs.append(tails[1].reshape(b, WINDOW, HKV_B, HEAD_DIM))
            else:
                rows_c = cache_swa_k.shape[2]
                pool_prev = jnp.pad(cache_pool[p], ((0, 0), (1, 0), (0, 0)))
                kc = cache_swa_k[p].reshape(b, rows_c, D_KV)
                vc = cache_swa_v[p].reshape(b, rows_c, D_KV)
                x = _even_mix_sample(swa_sinks[p], u, q, k, v, x, pool_prev, kc, vc, wp, ps, wo, p, l, pos0)
                pools.append(jnp.concatenate([cache_pool[p], u.reshape(b, l, D_A)], axis=1)[:, -POOL_STATE:])
                swa_ks.append(jnp.concatenate([cache_swa_k[p], k.reshape(b, l, HKV_B, HEAD_DIM)], axis=1)[:, -rows_c:])
                swa_vs.append(jnp.concatenate([cache_swa_v[p], v.reshape(b, l, HKV_B, HEAD_DIM)], axis=1)[:, -rows_c:])
        else:
            lambda_init = 0.8 - 0.6 * math.exp(-0.3 * layer)
            wg = jnp.pad(gla_w_gate2, ((0, 0), (0, LANES - GATE_RANK), (0, 0))).astype(BF16)
            bg = gla_b_gate[p].reshape(1, D_CQK)
            lp = diff_lambda[p]
            dnw = diff_norm_w[p].reshape(1, 2 * HD_D)
            w_in = _pad_odd_w_in(odd_w_in)
            if prompt:
                qd, kd, el, cv, cg, dq, *kv_p = _odd_in(x, nw, w_in, wg, p, bg, tabs, tm, blk,
                                                        pair=(p, depth // 2, b, l, kv_p))
                yd = _diff_prompt(lp, dnw, dq, kv_p[0], kv_p[1], p, b, l, lambda_init, min(ATTN_TILE, l))
                s0 = jnp.zeros((b, H_C, DK_C, DV_C), F32)
            else:
                qd, kd, el, cv, cg, dq, dk, dv = _odd_in(x, nw, w_in, wg, p, bg, tabs, tm, blk)
                past = cache_diff_k.shape[2]
                kct = jnp.transpose(cache_diff_k, (0, 1, 3, 4, 5, 2)).reshape(depth // 2, b, H_D, 2 * HD_D, past)
                vc = cache_diff_v.reshape(depth // 2, b * past * H_D, 2 * HD_D)
                yd = _diff_sample(lp, dnw, dq, dk, dv, kct, vc, p, b, l, lambda_init)
                s0 = state_gla[p]
                diff_ks.append(dk.reshape(b, l, H_D, 2, HD_D))
                diff_vs.append(dv.reshape(b, l, H_D, 2 * HD_D))
            x, s_new = _odd_mix(qd, kd, el, cv, cg, yd, x, s0, gla_norm_w[p].reshape(1, DV_C),
                                odd_w_out.astype(BF16), p, tm, blk, l, not prompt)
            glas.append(s_new)
        if prompt:
            tf = seg = min(FFN_ROW_TILE, l)
            prev = jnp.zeros((r // tf, CONV_W - 1, D_FF), F32)
        else:
            tf, seg = tm, l
            prev = cache_ffn_conv[layer]
        x, tails = _ffn(x, norm_ffn_w[layer].reshape(1, D_MODEL), ffn_w_up.astype(BF16), ffn_conv_w[layer],
                        ffn_conv_b[layer].reshape(1, D_FF), ffn_w_down.astype(BF16), prev,
                        final_norm_w.reshape(1, D_MODEL), layer, tf, seg, l, not prompt, layer == depth - 1)
        tails = tails.reshape(b, -1, CONV_TAIL, D_FF)
        ffns.append(tails[:, -1, CONV_TAIL - (CONV_W - 1):])
    if prompt:
        n_pairs = depth // 2
        diff_k = jnp.transpose(kv_p[0].reshape(n_pairs, b, H_D, 2, HD_D, l), (0, 1, 5, 2, 3, 4))
        diff_v = kv_p[1].reshape(n_pairs, b, l, H_D, 2 * HD_D)
    else:
        diff_k, diff_v = jnp.stack(diff_ks), jnp.stack(diff_vs)
    return (x.reshape(b, l, D_MODEL), jnp.stack(pools), jnp.stack(swa_ks), jnp.stack(swa_vs), jnp.stack(glas),
            diff_k, diff_v, jnp.stack(ffns))


def kernel(x_prompt, x_sample, cache_pool, cache_swa_k, cache_swa_v, state_gla, cache_diff_k, cache_diff_v, cache_ffn_conv, norm_mix_w, norm_ffn_w, final_norm_w, even_w_in, pool_w, pool_scale, swa_sinks, even_w_out, odd_w_in, gla_w_gate2, gla_b_gate, gla_norm_w, diff_lambda, diff_norm_w, odd_w_out, ffn_w_up, ffn_conv_w, ffn_conv_b, ffn_w_down):
    weights = (norm_mix_w, norm_ffn_w, final_norm_w, even_w_in, pool_w, pool_scale, swa_sinks, even_w_out, odd_w_in,
               gla_w_gate2, gla_b_gate, gla_norm_w, diff_lambda, diff_norm_w, odd_w_out, ffn_w_up, ffn_conv_w,
               ffn_conv_b, ffn_w_down)
    caches = (cache_pool, cache_swa_k, cache_swa_v, state_gla, cache_diff_k, cache_diff_v, cache_ffn_conv)
    past = cache_diff_k.shape[2]
    pos_p = jnp.arange(x_prompt.shape[1])
    pos_s = past + jnp.arange(x_sample.shape[1])
    outs_p = _trunk(x_prompt, pos_p, True, caches, weights)
    outs_s = _trunk(x_sample, pos_s, False, caches, weights)
    return (outs_p[0], outs_s[0]) + tuple(outs_p[1:]) + tuple(outs_s[1:])
```

```python
import functools
import math

import jax
import jax.numpy as jnp
from jax import lax
from jax.experimental import pallas as pl
from jax.experimental.pallas import tpu as pltpu

F32 = jnp.float32
BF16 = jnp.bfloat16

D_MODEL = 1024
CHUNK = 64
EPS = 1e-6
NEG_INF = -1e30
LOG2E = 1.0 / math.log(2.0)
ROPE_THETA = 500000.0
HEAD_DIM = 64
ROT_DIM = HEAD_DIM // 4
ROT_HALF = ROT_DIM // 2
LANES = 128

POOL_WINDOWS = (2, 4, 8, 16)
POOL_GROUP = D_MODEL // 8
D_A = len(POOL_WINDOWS) * POOL_GROUP
POOL_STATE = max(POOL_WINDOWS) - 1
POOL_HALO = POOL_STATE + 1

H_B = D_MODEL // 128
HKV_B = 2
GQA_B = H_B // HKV_B
WINDOW = 128
D_B = H_B * HEAD_DIM
D_KV = HKV_B * HEAD_DIM

H_C = 4
DK_C = 64
DV_C = 128
GATE_RANK = 16
GATE_NORM = 16.0
D_CQK = H_C * DK_C
D_C = H_C * DV_C

H_D = 4
HD_D = 64
D_D = H_D * 2 * HD_D

D_FF = 2816
CONV_W = 3
CONV_TAIL = 8

ROW_TILE = 1024
FFN_ROW_TILE = 1024
ATTN_TILE = 1024
FFN_COLS = 256
VMEM_LIMIT = 60 * 1024 * 1024


def _dot(a, b):
    return jnp.dot(a, b, preferred_element_type=F32)


def _dot_nt(a, b):
    return lax.dot_general(a, b, (((1,), (1,)), ((), ())), preferred_element_type=F32)


def _rms(x, w):
    return x * lax.rsqrt(jnp.mean(x * x, axis=-1, keepdims=True) + EPS) * w


def _rope(z, cos_t, sin_a, sin_b):
    outs = []
    for s in range(z.shape[1] // LANES):
        x = z[:, s * LANES:(s + 1) * LANES]
        outs.append(x * cos_t + pltpu.roll(x, LANES - ROT_HALF, 1) * sin_a + pltpu.roll(x, ROT_HALF, 1) * sin_b)
    return outs[0] if len(outs) == 1 else jnp.concatenate(outs, axis=-1)


def _rope_tables(pos):
    t = pos.shape[0]
    inv = jnp.exp(-math.log(ROPE_THETA) * jnp.arange(ROT_HALF, dtype=F32) * (2.0 / ROT_DIM))
    ang = pos.astype(F32)[:, None] * inv[None, :]
    cos, sin = jnp.cos(ang), jnp.sin(ang)
    rest = HEAD_DIM - ROT_DIM
    cos_t = jnp.concatenate([cos, cos, jnp.ones((t, rest), F32)], axis=-1)
    sin_a = jnp.concatenate([-sin, jnp.zeros((t, HEAD_DIM - ROT_HALF), F32)], axis=-1)
    sin_b = jnp.concatenate([jnp.zeros((t, ROT_HALF), F32), sin, jnp.zeros((t, rest), F32)], axis=-1)
    rep = LANES // HEAD_DIM
    return tuple(jnp.tile(a, (1, rep)) for a in (cos_t, sin_a, sin_b))


def _params(n_axes=1):
    return pltpu.CompilerParams(dimension_semantics=("arbitrary",) * n_axes, vmem_limit_bytes=VMEM_LIMIT)


def _full(shape):
    nd = len(shape)
    return pl.BlockSpec(shape, lambda *_: (0,) * nd, pipeline_mode=pl.Buffered(1))


def _layer(shape, index):
    nd = len(shape)
    return pl.BlockSpec((1,) + tuple(shape), lambda *_: (index,) + (0,) * nd, pipeline_mode=pl.Buffered(1))


def _rows(tm, cols):
    return pl.BlockSpec((tm, cols), lambda i: (i, 0))


def _even_in_kernel(x_ref, nw_ref, w_ref, cos_ref, sa_ref, sb_ref, u_ref, q_ref, k_ref, v_ref, *tails,
                    tm, prompt):
    h = _rms(x_ref[...], nw_ref[...]).astype(BF16)
    cos_t, sin_a, sin_b = cos_ref[...], sa_ref[...], sb_ref[...]
    q = _dot(h, w_ref[0, :, D_A:D_A + D_B])
    k = _dot(h, w_ref[0, :, D_A + D_B:D_A + D_B + D_KV])
    v = _dot(h, w_ref[0, :, D_A + D_B + D_KV:D_A + D_B + 2 * D_KV])
    q_ref[...] = (_rope(q, cos_t, sin_a, sin_b) * (HEAD_DIM ** -0.5 * LOG2E)).astype(BF16)
    k = _rope(k, cos_t, sin_a, sin_b)
    if prompt:
        k_ref[...] = k.T.astype(BF16)
        v_ref[...] = v.astype(BF16)
        tails[0][0] = k[tm - WINDOW:]
        tails[1][0] = v[tm - WINDOW:]
    else:
        k_ref[...] = k
        v_ref[...] = v
    u_ref[...] = _dot(h, w_ref[0, :, 0:D_A])


def _even_in(x, nw, w, p, tabs, tm, seq, prompt):
    r = x.shape[0]
    t_tiles = tabs[0].shape[0] // tm
    tab_spec = pl.BlockSpec((tm, LANES), lambda i: (i % t_tiles, 0))
    n_in = D_A + D_B + 2 * D_KV
    sds = jax.ShapeDtypeStruct
    if prompt:
        tps = seq // tm
        tail_spec = pl.BlockSpec((1, WINDOW, D_KV), lambda i: (i // tps, 0, 0))
        kv_specs = [pl.BlockSpec((D_KV, tm), lambda i: (0, i)), _rows(tm, D_KV), tail_spec, tail_spec]
        kv_shapes = [sds((D_KV, r), BF16), sds((r, D_KV), BF16),
                     sds((r // seq, WINDOW, D_KV), F32), sds((r // seq, WINDOW, D_KV), F32)]
    else:
        kv_specs = [_rows(tm, D_KV), _rows(tm, D_KV)]
        kv_shapes = [sds((r, D_KV), F32), sds((r, D_KV), F32)]
    return pl.pallas_call(
        functools.partial(_even_in_kernel, tm=tm, prompt=prompt),
        grid=(r // tm,),
        in_specs=[_rows(tm, D_MODEL), _full((1, D_MODEL)), _layer((D_MODEL, n_in), p), tab_spec, tab_spec,
                  tab_spec],
        out_specs=[_rows(tm, D_A), _rows(tm, D_B)] + kv_specs,
        out_shape=[sds((r, D_A), F32), sds((r, D_B), BF16)] + kv_shapes,
        compiler_params=_params(),
        name="even_in",
    )(x, nw, w, *tabs)


def _pool_mix(ext, u, pos, wp_ref, ps, ycat_ref, row0, rows):
    for g, w in enumerate(POOL_WINDOWS):
        lo, hi = g * POOL_GROUP, (g + 1) * POOL_GROUP
        s = ext[:, lo:hi]
        sh = 1
        while sh < w:
            s = s + pltpu.roll(s, sh, 0)
            sh *= 2
        count = jnp.minimum(w, pos + 1).astype(F32)
        diff = s[POOL_HALO:] / count - u[:, lo:hi]
        ya = _dot(diff.astype(BF16), wp_ref[0, g]) * ps[:, lo:hi]
        ycat_ref[row0:row0 + rows, lo:hi] = ya.astype(BF16)


def _sink_column(sink_ref, hk, rows_per_head):
    n = GQA_B * rows_per_head
    grp = lax.broadcasted_iota(jnp.int32, (n, 1), 0) // rows_per_head
    col = jnp.full((n, 1), sink_ref[hk * GQA_B] * LOG2E, F32)
    for g in range(1, GQA_B):
        col = jnp.where(grp == g, sink_ref[hk * GQA_B + g] * LOG2E, col)
    return col


def _sink_attend(s, sink_col, vb):
    m = jnp.maximum(jnp.max(s, axis=-1, keepdims=True), sink_col)
    e = jnp.exp2(s - m)
    den = jnp.sum(e, axis=-1, keepdims=True) + jnp.exp2(sink_col - m)
    return _dot(e.astype(BF16), vb) / den


def _store_heads(ycat_ref, o, row0, rows, hk):
    for pair in range(GQA_B // 2):
        both = jnp.concatenate([o[(2 * pair) * rows:(2 * pair + 1) * rows],
                                o[(2 * pair + 1) * rows:(2 * pair + 2) * rows]], axis=-1)
        c0 = D_A + (hk * GQA_B + 2 * pair) * HEAD_DIM
        ycat_ref[row0:row0 + rows, c0:c0 + 2 * HEAD_DIM] = both.astype(BF16)


def _even_mix_prompt_kernel(sink_ref, u_ref, q_ref, k_ref, v_ref, x_ref, wp_ref, ps_ref, wo_ref, o_ref,
                            ucar, kcar, vcar, ycat, *, tm, tiles_per_seq):
    i = pl.program_id(0)
    ti = i % tiles_per_seq
    first_key_chunk = jnp.where(ti == 0, WINDOW // CHUNK, 0)

    @pl.when(ti == 0)
    def _():
        ucar[...] = jnp.zeros_like(ucar)
        kcar[...] = jnp.zeros_like(kcar)
        vcar[...] = jnp.zeros_like(vcar)

    u = u_ref[...]
    ext = jnp.concatenate([ucar[...], u], axis=0)
    pos = ti * tm + lax.broadcasted_iota(jnp.int32, (tm, POOL_GROUP), 0)
    _pool_mix(ext, u, pos, wp_ref, ps_ref[...], ycat, 0, tm)
    ucar[...] = u[tm - POOL_HALO:]

    kext = jnp.concatenate([kcar[...], k_ref[...]], axis=1)
    vext = jnp.concatenate([vcar[...], v_ref[...]], axis=0)
    kcar[...] = kext[:, tm:]
    vcar[...] = vext[tm:]
    qb = 2 * CHUNK
    kb = qb + WINDOW
    nrow = GQA_B * qb
    qc = (lax.broadcasted_iota(jnp.int32, (nrow, kb), 0) % qb) // CHUNK
    kc = lax.broadcasted_iota(jnp.int32, (nrow, kb), 1) // CHUNK
    band = (kc >= qc) & (kc <= qc + WINDOW // CHUNK)
    first = band & (kc >= first_key_chunk)
    grp = lax.broadcasted_iota(jnp.int32, (nrow, LANES), 0) // qb
    low_lanes = lax.broadcasted_iota(jnp.int32, (qb, LANES), 1) < HEAD_DIM
    ones = jnp.ones((WINDOW + tm, HEAD_DIM), BF16)
    zeros = jnp.zeros((HEAD_DIM, WINDOW + tm), BF16)
    order = [g for g in range(GQA_B) if g % 2 == 0] + [g for g in range(GQA_B) if g % 2 == 1]
    v_aug, sink_rep, k_even, k_odd = [], [], [], []
    for hk in range(HKV_B):
        v_aug.append(jnp.concatenate([vext[:, hk * HEAD_DIM:(hk + 1) * HEAD_DIM], ones], axis=-1))
        kt = kext[hk * HEAD_DIM:(hk + 1) * HEAD_DIM]
        k_even.append(jnp.concatenate([kt, zeros], axis=0))
        k_odd.append(jnp.concatenate([zeros, kt], axis=0))
        rep = jnp.full((nrow, LANES), sink_ref[hk * GQA_B + order[0]] * LOG2E, F32)
        for r in range(1, GQA_B):
            rep = jnp.where(grp == r, sink_ref[hk * GQA_B + order[r]] * LOG2E, rep)
        sink_rep.append(rep)

    def scores(a, hk):
        def tiles(gs):
            return jnp.concatenate(
                [q_ref[a * qb:(a + 1) * qb, (hk * GQA_B + g) // 2 * LANES:((hk * GQA_B + g) // 2 + 1) * LANES]
                 for g in gs], axis=0)
        half = GQA_B // 2
        return jnp.concatenate([_dot(tiles(order[:half]), k_even[hk][:, a * qb:a * qb + kb]),
                                _dot(tiles(order[half:]), k_odd[hk][:, a * qb:a * qb + kb])], axis=0)

    def softmax(a, hk, s):
        s = jnp.where(first if a == 0 else band, s, NEG_INF)
        m = jnp.maximum(jnp.max(s, axis=-1, keepdims=True), sink_rep[hk])
        return jnp.exp2(s - _lane_tile(m, kb // LANES)).astype(BF16), jnp.exp2(sink_rep[hk] - m)

    def finish(a, hk, e, sink_term):
        oa = _dot(e, v_aug[hk][a * qb:a * qb + kb])
        inv = 1.0 / (oa + sink_term)
        for pair in range(GQA_B // 2):
            lo = slice(pair * qb, (pair + 1) * qb)
            hi = slice((GQA_B // 2 + pair) * qb, (GQA_B // 2 + pair + 1) * qb)
            both = jnp.where(low_lanes, oa[lo] * pltpu.roll(inv[lo], HEAD_DIM, 1),
                             pltpu.roll(oa[hi], HEAD_DIM, 1) * inv[hi])
            c0 = D_A + (hk * GQA_B + 2 * pair) * HEAD_DIM
            ycat[a * qb:(a + 1) * qb, c0:c0 + 2 * HEAD_DIM] = both.astype(BF16)

    blocks = [(a, hk) for a in range(tm // qb) for hk in range(HKV_B)]
    nb = len(blocks)
    s_q = {0: scores(*blocks[0])}
    e_q = {}
    for n in range(nb + 2):
        if n + 1 < nb:
            s_q[n + 1] = scores(*blocks[n + 1])
        if n < nb:
            e_q[n] = softmax(*blocks[n], s_q.pop(n))
        if 1 <= n <= nb:
            finish(*blocks[n - 1], *e_q.pop(n - 1))
    o_ref[...] = x_ref[...] + _dot(ycat[...], wo_ref[0])


def _even_mix_prompt(sinks, u, q, kt, v, x, wp, ps, wo, p, seq, tm):
    r = x.shape[0]
    kern = functools.partial(_even_mix_prompt_kernel, tm=tm, tiles_per_seq=seq // tm)
    return pl.pallas_call(
        kern,
        grid=(r // tm,),
        in_specs=[pl.BlockSpec(memory_space=pltpu.SMEM),
                  _rows(tm, D_A), _rows(tm, D_B), pl.BlockSpec((D_KV, tm), lambda i: (0, i)), _rows(tm, D_KV),
                  _rows(tm, D_MODEL), _layer(wp.shape[1:], p), _full((1, D_A)), _layer((D_A + D_B, D_MODEL), p)],
        out_specs=_rows(tm, D_MODEL),
        out_shape=jax.ShapeDtypeStruct((r, D_MODEL), F32),
        scratch_shapes=[pltpu.VMEM((POOL_HALO, D_A), F32), pltpu.VMEM((D_KV, WINDOW), BF16),
                        pltpu.VMEM((WINDOW, D_KV), BF16), pltpu.VMEM((tm, D_A + D_B), BF16)],
        compiler_params=_params(),
        name="even_mix_prompt",
    )(sinks, u, q, kt, v, x, wp, ps, wo)


def _even_mix_sample_kernel(sink_ref, u_ref, q_ref, k_ref, v_ref, x_ref, pp_ref, kc_ref, vc_ref, wp_ref, ps_ref,
                            wo_ref, o_ref, ycat, *, l, pos0):
    u = u_ref[...]
    ext = jnp.concatenate([pp_ref[0], u], axis=0)
    pos = pos0 + lax.broadcasted_iota(jnp.int32, (l, 1), 0)
    _pool_mix(ext, u, pos, wp_ref, ps_ref[...], ycat, 0, l)

    pad = jnp.zeros((LANES - l, D_KV), BF16)
    kall = jnp.concatenate([kc_ref[0].astype(BF16), k_ref[...].astype(BF16), pad], axis=0)
    vall = jnp.concatenate([vc_ref[0].astype(BF16), v_ref[...].astype(BF16), pad], axis=0)
    nk = kall.shape[0]
    valid = lax.broadcasted_iota(jnp.int32, (GQA_B * l, nk), 1) < nk - (LANES - l)
    for hk in range(HKV_B):
        ks = kall[:, hk * HEAD_DIM:(hk + 1) * HEAD_DIM]
        vs = vall[:, hk * HEAD_DIM:(hk + 1) * HEAD_DIM]
        qs = jnp.concatenate(
            [q_ref[:, (hk * GQA_B + g) * HEAD_DIM:(hk * GQA_B + g + 1) * HEAD_DIM] for g in range(GQA_B)], axis=0)
        s = jnp.where(valid, _dot_nt(qs, ks), NEG_INF)
        o = _sink_attend(s, _sink_column(sink_ref, hk, l), vs)
        _store_heads(ycat, o, 0, l, hk)
    o_ref[...] = x_ref[...] + _dot(ycat[...], wo_ref[0])


def _even_mix_sample(sinks, u, q, k, v, x, pool_prev, kc, vc, wp, ps, wo, p, l, pos0):
    r = x.shape[0]
    rows_c = kc.shape[1]
    kern = functools.partial(_even_mix_sample_kernel, l=l, pos0=pos0)
    cache_spec = pl.BlockSpec((1, rows_c, D_KV), lambda i: (i, 0, 0))
    return pl.pallas_call(
        kern,
        grid=(r // l,),
        in_specs=[pl.BlockSpec(memory_space=pltpu.SMEM),
                  _rows(l, D_A), _rows(l, D_B), _rows(l, D_KV), _rows(l, D_KV), _rows(l, D_MODEL),
                  pl.BlockSpec((1, POOL_HALO, D_A), lambda i: (i, 0, 0)), cache_spec, cache_spec,
                  _layer(wp.shape[1:], p), _full((1, D_A)), _layer((D_A + D_B, D_MODEL), p)],
        out_specs=_rows(l, D_MODEL),
        out_shape=jax.ShapeDtypeStruct((r, D_MODEL), F32),
        scratch_shapes=[pltpu.VMEM((l, D_A + D_B), BF16)],
        compiler_params=_params(),
        name="even_mix_sample",
    )(sinks, u, q, k, v, x, pool_prev, kc, vc, wp, ps, wo)


ODD_CR0 = 2 * D_CQK + 2 * D_C
ODD_DQ0 = ODD_CR0 + LANES
ODD_IN_PADDED = ODD_DQ0 + 3 * D_D


def _odd_in_kernel(x_ref, nw_ref, w_ref, wg_ref, bg_ref, cos_ref, sa_ref, sb_ref, *rest, tm, blk, cache_layout,
                   aliased):
    if aliased:
        rest = rest[2:]
    qd_ref, kd_ref, el_ref, cv_ref, cg_ref, dq_ref, dk_ref, dv_ref, cum_scr = rest
    h = _rms(x_ref[...], nw_ref[...]).astype(BF16)
    cos_t, sin_a, sin_b = cos_ref[...], sa_ref[...], sb_ref[...]
    cr = _dot(h, w_ref[0, :, ODD_CR0:ODD_DQ0]).astype(BF16)
    gate = _dot(cr, wg_ref[0]) + bg_ref[...]
    cq = _dot(h, w_ref[0, :, 0:D_CQK])
    ck = _dot(h, w_ref[0, :, D_CQK:2 * D_CQK])
    dq = _dot(h, w_ref[0, :, ODD_DQ0:ODD_DQ0 + D_D])
    cum = (jnp.minimum(gate, 0.0) - jnp.log1p(jnp.exp(-jnp.abs(gate)))) * (1.0 / GATE_NORM)
    rowb = lax.broadcasted_iota(jnp.int32, (tm, 1), 0) % blk
    sh = 1
    while sh < blk:
        cum = cum + jnp.where(rowb >= sh, pltpu.roll(cum, sh, 0), 0.0)
        sh *= 2
    nb = tm // blk
    for s in range(D_CQK // LANES):
        cum_scr[s] = cum[:, s * LANES:(s + 1) * LANES]
        last = cum_scr[s, pl.ds(blk - 1, nb, stride=blk), :]
        last = jnp.concatenate([last, jnp.zeros((LANES - nb, LANES), F32)], axis=0)
        el_ref[0, s * LANES:(s + 1) * LANES, :] = jnp.exp(last.T)
    qd_ref[...] = (cq * (DK_C ** -0.5) * jnp.exp(cum)).astype(BF16)
    kd_ref[...] = (ck * jnp.exp(-cum)).T.astype(BF16)
    dk = _dot(h, w_ref[0, :, ODD_DQ0 + D_D:ODD_DQ0 + 2 * D_D])
    dq_ref[...] = (_rope(dq, cos_t, sin_a, sin_b) * (HD_D ** -0.5 * LOG2E)).astype(BF16)
    dv = _dot(h, w_ref[0, :, ODD_DQ0 + 2 * D_D:ODD_DQ0 + 3 * D_D])
    dk = _rope(dk, cos_t, sin_a, sin_b)
    cv_ref[...] = _dot(h, w_ref[0, :, 2 * D_CQK:2 * D_CQK + D_C]).astype(BF16)
    cg_ref[...] = _dot(h, w_ref[0, :, 2 * D_CQK + D_C:ODD_CR0])
    if cache_layout:
        for hd in range(H_D):
            lo, hi = hd * 2 * HD_D, (hd + 1) * 2 * HD_D
            dk_ref[0, 0, hd] = dk[:, lo:hi].T
            dv_ref[0, pl.ds(hd, tm, stride=H_D), :] = dv[:, lo:hi]
    else:
        dk_ref[...] = dk
        dv_ref[...] = dv


def _odd_in(x, nw, w, wg, wi, bg, tabs, tm, blk, pair=None):
    r = x.shape[0]
    t_tiles = tabs[0].shape[0] // tm
    tab_spec = pl.BlockSpec((tm, LANES), lambda i: (i % t_tiles, 0))
    nb = tm // blk
    sds = jax.ShapeDtypeStruct
    in_specs = [_rows(tm, D_MODEL), _full((1, D_MODEL)), _layer((D_MODEL, ODD_IN_PADDED), wi),
                _layer((LANES, D_CQK), wi), _full((1, D_CQK)), tab_spec, tab_spec, tab_spec]
    args = [x, nw, w, wg, bg, *tabs]
    aliases = {}
    if pair is None:
        kv_specs = [_rows(tm, D_D), _rows(tm, D_D)]
        kv_shapes = [sds((r, D_D), F32), sds((r, D_D), F32)]
    else:
        p, n_pairs, batch, seq, bufs = pair
        tps = seq // tm
        kv_specs = [pl.BlockSpec((1, 1, H_D, 2 * HD_D, tm), lambda i: (p, i // tps, 0, 0, i % tps)),
                    pl.BlockSpec((1, tm * H_D, 2 * HD_D), lambda i: (p, i, 0))]
        kv_shapes = [sds((n_pairs, batch, H_D, 2 * HD_D, seq), F32), sds((n_pairs, r * H_D, 2 * HD_D), F32)]
        if bufs is not None:
            in_specs += [pl.BlockSpec(memory_space=pl.ANY)] * 2
            args += list(bufs)
            aliases = {len(args) - 2: 6, len(args) - 1: 7}
    kern = functools.partial(_odd_in_kernel, tm=tm, blk=blk, cache_layout=pair is not None, aliased=bool(aliases))
    return pl.pallas_call(
        kern,
        grid=(r // tm,),
        in_specs=in_specs,
        out_specs=[_rows(tm, D_CQK), pl.BlockSpec((D_CQK, tm), lambda i: (0, i)),
                   pl.BlockSpec((1, D_CQK, LANES), lambda i: (i, 0, 0)), _rows(tm, D_C), _rows(tm, D_C),
                   _rows(tm, D_D)] + kv_specs,
        out_shape=[sds((r, D_CQK), BF16), sds((D_CQK, r), BF16), sds((r // tm, D_CQK, LANES), F32),
                   sds((r, D_C), BF16), sds((r, D_C), F32), sds((r, D_D), BF16)] + kv_shapes,
        scratch_shapes=[pltpu.VMEM((D_CQK // LANES, tm, LANES), F32)],
        input_output_aliases=aliases,
        compiler_params=_params(),
        name="odd_in",
    )(*args)


def _lambda_value(lp_ref, lambda_init):
    lp = lp_ref[...]
    s1 = jnp.sum(lp[0:1] * lp[1:2], axis=-1, keepdims=True)
    s2 = jnp.sum(lp[2:3] * lp[3:4], axis=-1, keepdims=True)
    return jnp.exp(s1) - jnp.exp(s2) + lambda_init


def _diff_finish(acc, den, lam, nw, lambda_init):
    o = acc[0] / den[0] - lam * (acc[1] / den[1])
    return (_rms(o, nw) * (1.0 - lambda_init)).astype(BF16)


def _lane_tile(x, n):
    return x if n == 1 else jnp.concatenate([x] * n, axis=1)


def _diff_prompt_kernel(lp_ref, nw_ref, q_ref, k_ref, v_ref, o_ref, kbf, vbf, m_scr, acc_scr, s_a, s_b, bias,
                        *, tq, lambda_init):
    hd = pl.program_id(1)
    qi = pl.program_id(2)
    dv = 2 * HD_D
    seq = vbf.shape[0]

    @pl.when(qi == 0)
    def _():
        kbf[...] = k_ref[0, 0, 0].astype(BF16)
        vbf[:, :dv] = v_ref[0, pl.ds(hd, seq, stride=H_D), :].astype(BF16)
        vbf[:, dv:] = jnp.ones((seq, dv), BF16)

    tk = tq // 2

    @pl.when((pl.program_id(0) == 0) & (pl.program_id(1) == 0) & (qi == 0))
    def _():
        qc = lax.broadcasted_iota(jnp.int32, (tk, tk), 0) // CHUNK
        kc = lax.broadcasted_iota(jnp.int32, (tk, tk), 1) // CHUNK
        bias[...] = jnp.where(kc <= qc, 0.0, NEG_INF)

    m_scr[...] = jnp.full(m_scr.shape, NEG_INF, F32)
    acc_scr[...] = jnp.zeros(acc_scr.shape, F32)

    def scores(t, s_buf, row0=0):
        start = pl.multiple_of(t * tk, tk)
        for c in range(2):
            s_buf[c, row0:, :] = _dot(q_ref[row0:, c * HD_D:(c + 1) * HD_D],
                                      kbf[c * HD_D:(c + 1) * HD_D, pl.ds(start, tk)])

    def attend(t, s_buf, row0=0, rows=tq, masked=False):
        vs = vbf[pl.ds(pl.multiple_of(t * tk, tk), tk), :]
        for c in range(2):
            sc = s_buf[c, row0:row0 + rows, :]
            if masked:
                sc = sc + bias[...]
            m_old = m_scr[c, row0:row0 + rows, :]
            m_new = jnp.maximum(m_old, jnp.max(sc, axis=-1, keepdims=True))
            p = jnp.exp2(sc - _lane_tile(m_new, tk // LANES)).astype(BF16)
            alpha = _lane_tile(jnp.exp2(m_old - m_new), 2 * dv // LANES)
            acc_scr[c, row0:row0 + rows, :] = alpha * acc_scr[c, row0:row0 + rows, :] + _dot(p, vs)
            m_scr[c, row0:row0 + rows, :] = m_new

    scores(0, s_a)

    def pair(j, _):
        t = 2 * j
        scores(t + 1, s_b)
        attend(t, s_a)
        scores(t + 2, s_a)
        attend(t + 1, s_b)
        return 0

    lax.fori_loop(0, qi, pair, 0)
    scores(2 * qi + 1, s_b, row0=tk)
    attend(2 * qi, s_a, row0=0, rows=tk, masked=True)
    attend(2 * qi, s_a, row0=tk, rows=tk)
    attend(2 * qi + 1, s_b, row0=tk, rows=tk, masked=True)

    lam = _lambda_value(lp_ref, lambda_init)
    acc = [acc_scr[c] for c in range(2)]
    o_ref[...] = _diff_finish([a[:, :dv] for a in acc], [a[:, dv:] for a in acc], lam, nw_ref[...], lambda_init)


def _diff_prompt(lp, nw, q, kt, v, p, batch, seq, lambda_init, tq):
    r = q.shape[0]
    nq = seq // tq
    kern = functools.partial(_diff_prompt_kernel, tq=tq, lambda_init=lambda_init)
    k_spec = pl.BlockSpec((1, 1, 1, 2 * HD_D, seq), lambda b, h, i: (p, b, h, 0, 0))
    v_spec = pl.BlockSpec((1, seq * H_D, 2 * HD_D), lambda b, h, i: (p, b, 0))
    q_spec = pl.BlockSpec((tq, 2 * HD_D), lambda b, h, i: (b * nq + i, h))
    return pl.pallas_call(
        kern,
        grid=(batch, H_D, nq),
        in_specs=[pl.BlockSpec((4, HD_D), lambda b, h, i: (0, 0)), pl.BlockSpec((1, 2 * HD_D), lambda b, h, i: (0, 0)),
                  q_spec, k_spec, v_spec],
        out_specs=q_spec,
        out_shape=jax.ShapeDtypeStruct((r, D_D), BF16),
        scratch_shapes=[pltpu.VMEM((2 * HD_D, seq), BF16), pltpu.VMEM((seq, 4 * HD_D), BF16),
                        pltpu.VMEM((2, tq, LANES), F32), pltpu.VMEM((2, tq, 4 * HD_D), F32),
                        pltpu.VMEM((2, tq, tq // 2), F32), pltpu.VMEM((2, tq, tq // 2), F32),
                        pltpu.VMEM((tq // 2, tq // 2), F32)],
        compiler_params=_params(3),
        name="diff_prompt",
    )(lp, nw, q, kt, v)


def _diff_sample_kernel(lp_ref, nw_ref, q_ref, k_ref, v_ref, kc_ref, vc_ref, o_ref, *, l, past, lambda_init):
    pad = jnp.zeros((LANES - l, 2 * HD_D), BF16)
    valid = lax.broadcasted_iota(jnp.int32, (l, LANES), 1) < l
    lam = _lambda_value(lp_ref, lambda_init)
    for hd in range(H_D):
        lo, hi = hd * 2 * HD_D, (hd + 1) * 2 * HD_D
        knew = jnp.concatenate([k_ref[:, lo:hi].astype(BF16), pad], axis=0)
        vnew = jnp.concatenate([v_ref[:, lo:hi].astype(BF16), pad], axis=0)
        kct = kc_ref[0, 0, hd].astype(BF16)
        vcache = vc_ref[0, pl.ds(hd, past, stride=H_D), :].astype(BF16)
        acc, den = [], []
        for c in range(2):
            qc = q_ref[:, lo + c * HD_D:lo + (c + 1) * HD_D]
            s_old = _dot(qc, kct[c * HD_D:(c + 1) * HD_D, :])
            s_new = jnp.where(valid, _dot_nt(qc, knew[:, c * HD_D:(c + 1) * HD_D]), NEG_INF)
            m = jnp.maximum(jnp.max(s_old, axis=-1, keepdims=True), jnp.max(s_new, axis=-1, keepdims=True))
            p_old = jnp.exp2(s_old - m)
            p_new = jnp.exp2(s_new - m)
            den.append(jnp.sum(p_old, axis=-1, keepdims=True) + jnp.sum(p_new, axis=-1, keepdims=True))
            acc.append(_dot(p_old.astype(BF16), vcache) + _dot(p_new.astype(BF16), vnew))
        o_ref[:, lo:hi] = _diff_finish(acc, den, lam, nw_ref[...], lambda_init)


def _diff_sample(lp, nw, q, k, v, kct, vc, p, batch, l, lambda_init):
    r = q.shape[0]
    past = kct.shape[-1]
    kern = functools.partial(_diff_sample_kernel, l=l, past=past, lambda_init=lambda_init)
    cache_k_spec = pl.BlockSpec((1, 1, H_D, 2 * HD_D, past), lambda b: (p, b, 0, 0, 0))
    cache_v_spec = pl.BlockSpec((1, past * H_D, 2 * HD_D), lambda b: (p, b, 0))
    return pl.pallas_call(
        kern,
        grid=(batch,),
        in_specs=[_full((4, HD_D)), _full((1, 2 * HD_D)), _rows(l, D_D), _rows(l, D_D), _rows(l, D_D),
                  cache_k_spec, cache_v_spec],
        out_specs=_rows(l, D_D),
        out_shape=jax.ShapeDtypeStruct((r, D_D), BF16),
        compiler_params=_params(),
        name="diff_sample",
    )(lp, nw, q, k, v, kct, vc)


def _odd_mix_kernel(qd_ref, kd_ref, el_ref, cv_ref, cg_ref, yd_ref, x_ref, s0_ref, gw_ref, wo_ref,
                    o_ref, s_ref, state, og, ycat, *, tm, blk, tiles_per_seq, per_block_state):
    i = pl.program_id(0)
    if not per_block_state:
        @pl.when(i % tiles_per_seq == 0)
        def _():
            state[...] = jnp.zeros_like(state)

    per_group = LANES // blk
    rows_of = lax.broadcasted_iota(jnp.int32, (LANES, LANES), 0)
    cols_of = lax.broadcasted_iota(jnp.int32, (LANES, LANES), 1)
    q_row = lax.broadcasted_iota(jnp.int32, (blk, LANES), 0)
    q_col = lax.broadcasted_iota(jnp.int32, (blk, LANES), 1)
    low = [(h % 2) * DK_C for h in range(H_C)]
    own_rows = [(rows_of >= low[h]) & (rows_of < low[h] + DK_C) for h in range(H_C)]
    lanes = [slice(h // 2 * LANES, (h // 2 + 1) * LANES) for h in range(H_C)]
    vcols = [slice(h * DV_C, (h + 1) * DV_C) for h in range(H_C)]
    st = [None if per_block_state else state[h] for h in range(H_C)]
    for n in range(tm // blk):
        g0, c = (n // per_group) * LANES, n % per_group
        r0 = n * blk
        causal = (q_col >= c * blk) & (q_col - c * blk <= q_row)
        own_cols = (cols_of >= c * blk) & (cols_of < (c + 1) * blk)
        for h in range(H_C):
            if per_block_state:
                st[h] = jnp.where(own_rows[h], jnp.concatenate([s0_ref[n, h]] * 2, axis=0), 0.0)
            qd = qd_ref[r0:r0 + blk, lanes[h]]
            kt = jnp.where(own_rows[h], kd_ref[lanes[h], g0:g0 + LANES], jnp.zeros((), BF16))
            vg = cv_ref[g0:g0 + LANES, vcols[h]]
            att = jnp.where(causal, _dot(qd, kt), 0.0).astype(BF16)
            og[r0:r0 + blk, vcols[h]] = _dot(att, vg) + _dot(qd, st[h].astype(BF16))
            k_blk = jnp.where(own_cols, kt, jnp.zeros((), BF16))
            decay = jnp.where(own_rows[h], jnp.broadcast_to(el_ref[0, lanes[h], n:n + 1], (LANES, LANES)), 0.0)
            st[h] = decay * (st[h] + _dot(k_blk, vg))
            if per_block_state:
                s_ref[n, h] = st[h][low[h]:low[h] + DK_C]
    if not per_block_state:
        for h in range(H_C):
            state[h] = st[h]
            s_ref[0, h] = st[h][low[h]:low[h] + DK_C]

    gw = gw_ref[...]
    for h in range(H_C):
        lo, hi = h * DV_C, (h + 1) * DV_C
        g = cg_ref[:, lo:hi]
        y = _rms(og[:, lo:hi], gw) * (g * (1.0 / (1.0 + jnp.exp(-g))))
        ycat[:, lo:hi] = y.astype(BF16)
    ycat[:, D_C:] = yd_ref[...]
    o_ref[...] = x_ref[...] + _dot(ycat[...], wo_ref[0])


def _odd_mix(qd, kd, el, cv, cg, yd, x, s0, gw, wo, p, tm, blk, seq, per_block_state):
    assert tm % LANES == 0 and LANES % blk == 0 and tm // blk <= LANES
    r = x.shape[0]
    nb = tm // blk
    tiles_per_seq = max(seq // tm, 1)
    kern = functools.partial(_odd_mix_kernel, tm=tm, blk=blk, tiles_per_seq=tiles_per_seq,
                             per_block_state=per_block_state)
    if per_block_state:
        n_state = r // blk
        state_spec = pl.BlockSpec((nb, H_C, DK_C, DV_C), lambda i: (i, 0, 0, 0))
    else:
        n_state = r // seq
        state_spec = pl.BlockSpec((1, H_C, DK_C, DV_C), lambda i: (i // tiles_per_seq, 0, 0, 0))
    return pl.pallas_call(
        kern,
        grid=(r // tm,),
        in_specs=[_rows(tm, D_CQK), pl.BlockSpec((D_CQK, tm), lambda i: (0, i)),
                  pl.BlockSpec((1, D_CQK, LANES), lambda i: (i, 0, 0)), _rows(tm, D_C), _rows(tm, D_C),
                  _rows(tm, D_D), _rows(tm, D_MODEL), state_spec, _full((1, DV_C)),
                  _layer((D_C + D_D, D_MODEL), p)],
        out_specs=[_rows(tm, D_MODEL), state_spec],
        out_shape=[jax.ShapeDtypeStruct((r, D_MODEL), F32),
                   jax.ShapeDtypeStruct((n_state, H_C, DK_C, DV_C), F32)],
        scratch_shapes=[pltpu.VMEM((H_C, LANES, DV_C), F32), pltpu.VMEM((tm, D_C), F32),
                        pltpu.VMEM((tm, D_C + D_D), BF16)],
        compiler_params=_params(),
        name="odd_mix",
    )(qd, kd, el, cv, cg, yd, x, s0, gw, wo)


def _gelu_tanh(x):
    return 0.5 * x * (1.0 + jnp.tanh(math.sqrt(2.0 / math.pi) * (x + 0.044715 * (x * x * x))))


def _ffn_kernel(x_ref, nw_ref, wup_ref, cw_ref, cb_ref, wdn_ref, prev_ref, fnw_ref, o_ref, tail_ref, carry, act,
                *, tm, seg, tiles_per_seq, prev_from_input, final_norm):
    i = pl.program_id(0)
    x = x_ref[...]
    nseg = tm // seg
    row = lax.broadcasted_iota(jnp.int32, (seg, 1), 0)
    if not prev_from_input:
        @pl.when(i % tiles_per_seq == 0)
        def _():
            carry[...] = jnp.zeros_like(carry)

    def up(hh, c0):
        return (_dot(hh, wup_ref[0, :, c0:c0 + FFN_COLS]),
                _dot(hh, wup_ref[0, :, D_FF + c0:D_FF + c0 + FFN_COLS]))

    h = _rms(x, nw_ref[...]).astype(BF16)
    nxt = up(h, 0)
    for c0 in range(0, D_FF, FFN_COLS):
        c1 = c0 + FFN_COLS
        gate, val = nxt
        if c1 < D_FF:
            nxt = up(h, c1)
        w0, w1, w2 = cw_ref[0:1, c0:c1], cw_ref[1:2, c0:c1], cw_ref[2:3, c0:c1]
        convs = []
        for s in range(nseg):
            g = gate[s * seg:(s + 1) * seg]
            if prev_from_input:
                p0, p1 = prev_ref[s, 0:1, c0:c1], prev_ref[s, 1:2, c0:c1]
            else:
                p0 = carry[CONV_TAIL - 2:CONV_TAIL - 1, c0:c1]
                p1 = carry[CONV_TAIL - 1:CONV_TAIL, c0:c1]
            g1 = jnp.where(row == 0, p1, pltpu.roll(g, 1, 0))
            g2 = jnp.where(row == 0, p0, jnp.where(row == 1, p1, pltpu.roll(g, 2, 0)))
            convs.append(cb_ref[:, c0:c1] + g2 * w0 + g1 * w1 + g * w2)
            tail_ref[s, :, c0:c1] = g[seg - CONV_TAIL:]
        conv = convs[0] if nseg == 1 else jnp.concatenate(convs, axis=0)
        if not prev_from_input:
            carry[:, c0:c1] = gate[tm - CONV_TAIL:]
        act[:, c0:c1] = (_gelu_tanh(conv) * val).astype(BF16)
    if final_norm:
        o_ref[...] = _rms(x + _dot(act[...], wdn_ref[0]), fnw_ref[...])
    else:
        half = D_MODEL // 2
        for c0 in (0, half):
            o_ref[:, c0:c0 + half] = x[:, c0:c0 + half] + _dot(act[...], wdn_ref[0, :, c0:c0 + half])


def _ffn(x, nw, wup, cw, cb, wdn, prev, fnw, layer, tm, seg, seq, prev_from_input, final_norm):
    r = x.shape[0]
    nseg = tm // seg
    kern = functools.partial(_ffn_kernel, tm=tm, seg=seg, tiles_per_seq=max(seq // tm, 1),
                             prev_from_input=prev_from_input, final_norm=final_norm)
    prev_spec = pl.BlockSpec((nseg, CONV_W - 1, D_FF), lambda i: (i, 0, 0))
    tail_spec = pl.BlockSpec((nseg, CONV_TAIL, D_FF), lambda i: (i, 0, 0))
    wup_spec = pl.BlockSpec((1, D_MODEL, 2 * D_FF), lambda i: (layer, 0, 0), pipeline_mode=pl.Buffered(1))
    wdn_spec = pl.BlockSpec((1, D_FF, D_MODEL), lambda i: (layer, 0, 0), pipeline_mode=pl.Buffered(1))
    return pl.pallas_call(
        kern,
        grid=(r // tm,),
        in_specs=[_rows(tm, D_MODEL), _full((1, D_MODEL)), wup_spec, _full((CONV_W, D_FF)),
                  _full((1, D_FF)), wdn_spec, prev_spec, _full((1, D_MODEL))],
        out_specs=[_rows(tm, D_MODEL), tail_spec],
        out_shape=[jax.ShapeDtypeStruct((r, D_MODEL), F32),
                   jax.ShapeDtypeStruct((r // seg, CONV_TAIL, D_FF), F32)],
        scratch_shapes=[pltpu.VMEM((CONV_TAIL, D_FF), F32), pltpu.VMEM((tm, D_FF), BF16)],
        compiler_params=_params(),
        name="ffn",
    )(x, nw, wup, cw, cb, wdn, prev, fnw)


def _pad_odd_w_in(w):
    cr = w[..., ODD_CR0:ODD_CR0 + GATE_RANK]
    pad = jnp.zeros(w.shape[:-1] + (LANES - GATE_RANK,), w.dtype)
    return jnp.concatenate([w[..., :ODD_CR0], cr, pad, w[..., ODD_CR0 + GATE_RANK:]], axis=-1).astype(BF16)


def _trunk(x3, pos, prompt, caches, weights):
    (norm_mix_w, norm_ffn_w, final_norm_w, even_w_in, pool_w, pool_scale, swa_sinks, even_w_out, odd_w_in,
     gla_w_gate2, gla_b_gate, gla_norm_w, diff_lambda, diff_norm_w, odd_w_out, ffn_w_up, ffn_conv_w, ffn_conv_b,
     ffn_w_down) = weights
    cache_pool, cache_swa_k, cache_swa_v, state_gla, cache_diff_k, cache_diff_v, cache_ffn_conv = caches
    b, l, _ = x3.shape
    r = b * l
    depth = norm_mix_w.shape[0]
    x = x3.reshape(r, D_MODEL)
    if prompt:
        tm = min(ROW_TILE, l)
        blk = min(CHUNK, l)
        tabs = _rope_tables(pos)
    else:
        tm = r
        blk = min(CHUNK, l)
        tabs = tuple(jnp.tile(t, (b, 1)) for t in _rope_tables(pos))
    pos0 = 0 if prompt else cache_diff_k.shape[2]

    pools, swa_ks, swa_vs, glas, diff_ks, diff_vs, ffns = [], [], [], [], [], [], []
    kv_p = None
    for layer in range(depth):
        p = layer // 2
        nw = norm_mix_w[layer].reshape(1, D_MODEL)
        if layer % 2 == 0:
            u, q, k, v, *tails = _even_in(x, nw, even_w_in.astype(BF16), p, tabs, tm, l, prompt)
            wp = pool_w.astype(BF16)
            ps = pool_scale[p].reshape(1, D_A)
            wo = even_w_out.astype(BF16)
            if prompt:
                x = _even_mix_prompt(swa_sinks[p], u, q, k, v, x, wp, ps, wo, p, l, tm)
                pools.append(u.reshape(b, l, D_A)[:, l - POOL_STATE:])
                swa_ks.append(tails[0].reshape(b, WINDOW, HKV_B, HEAD_DIM))
                swa_vs.append(tails[1].reshape(b, WINDOW, HKV_B, HEAD_DIM))
            else:
                rows_c = cache_swa_k.shape[2]
                pool_prev = jnp.pad(cache_pool[p], ((0, 0), (1, 0), (0, 0)))
                kc = cache_swa_k[p].reshape(b, rows_c, D_KV)
                vc = cache_swa_v[p].reshape(b, rows_c, D_KV)
                x = _even_mix_sample(swa_sinks[p], u, q, k, v, x, pool_prev, kc, vc, wp, ps, wo, p, l, pos0)
                pools.append(jnp.concatenate([cache_pool[p], u.reshape(b, l, D_A)], axis=1)[:, -POOL_STATE:])
                swa_ks.append(jnp.concatenate([cache_swa_k[p], k.reshape(b, l, HKV_B, HEAD_DIM)], axis=1)[:, -rows_c:])
                swa_vs.append(jnp.concatenate([cache_swa_v[p], v.reshape(b, l, HKV_B, HEAD_DIM)], axis=1)[:, -rows_c:])
        else:
            lambda_init = 0.8 - 0.6 * math.exp(-0.3 * layer)
            wg = jnp.pad(gla_w_gate2, ((0, 0), (0, LANES - GATE_RANK), (0, 0))).astype(BF16)
            bg = gla_b_gate[p].reshape(1, D_CQK)
            lp = diff_lambda[p]
            dnw = diff_norm_w[p].reshape(1, 2 * HD_D)
            w_in = _pad_odd_w_in(odd_w_in)
            if prompt:
                qd, kd, el, cv, cg, dq, *kv_p = _odd_in(x, nw, w_in, wg, p, bg, tabs, tm, blk,
                                                        pair=(p, depth // 2, b, l, kv_p))
                yd = _diff_prompt(lp, dnw, dq, kv_p[0], kv_p[1], p, b, l, lambda_init, min(ATTN_TILE, l))
                s0 = jnp.zeros((b, H_C, DK_C, DV_C), F32)
            else:
                qd, kd, el, cv, cg, dq, dk, dv = _odd_in(x, nw, w_in, wg, p, bg, tabs, tm, blk)
                past = cache_diff_k.shape[2]
                kct = jnp.transpose(cache_diff_k, (0, 1, 3, 4, 5, 2)).reshape(depth // 2, b, H_D, 2 * HD_D, past)
                vc = cache_diff_v.reshape(depth // 2, b * past * H_D, 2 * HD_D)
                yd = _diff_sample(lp, dnw, dq, dk, dv, kct, vc, p, b, l, lambda_init)
                s0 = state_gla[p]
                diff_ks.append(dk.reshape(b, l, H_D, 2, HD_D))
                diff_vs.append(dv.reshape(b, l, H_D, 2 * HD_D))
            x, s_new = _odd_mix(qd, kd, el, cv, cg, yd, x, s0, gla_norm_w[p].reshape(1, DV_C),
                                odd_w_out.astype(BF16), p, tm, blk, l, not prompt)
            glas.append(s_new)
        if prompt:
            tf = seg = min(FFN_ROW_TILE, l)
            prev = jnp.zeros((r // tf, CONV_W - 1, D_FF), F32)
        else:
            tf, seg = tm, l
            prev = cache_ffn_conv[layer]
        x, tails = _ffn(x, norm_ffn_w[layer].reshape(1, D_MODEL), ffn_w_up.astype(BF16), ffn_conv_w[layer],
                        ffn_conv_b[layer].reshape(1, D_FF), ffn_w_down.astype(BF16), prev,
                        final_norm_w.reshape(1, D_MODEL), layer, tf, seg, l, not prompt, layer == depth - 1)
        tails = tails.reshape(b, -1, CONV_TAIL, D_FF)
        ffns.append(tails[:, -1, CONV_TAIL - (CONV_W - 1):])
    if prompt:
        n_pairs = depth // 2
        diff_k = jnp.transpose(kv_p[0].reshape(n_pairs, b, H_D, 2, HD_D, l), (0, 1, 5, 2, 3, 4))
        diff_v = kv_p[1].reshape(n_pairs, b, l, H_D, 2 * HD_D)
    else:
        diff_k, diff_v = jnp.stack(diff_ks), jnp.stack(diff_vs)
    return (x.reshape(b, l, D_MODEL), jnp.stack(pools), jnp.stack(swa_ks), jnp.stack(swa_vs), jnp.stack(glas),
            diff_k, diff_v, jnp.stack(ffns))


def kernel(x_prompt, x_sample, cache_pool, cache_swa_k, cache_swa_v, state_gla, cache_diff_k, cache_diff_v, cache_ffn_conv, norm_mix_w, norm_ffn_w, final_norm_w, even_w_in, pool_w, pool_scale, swa_sinks, even_w_out, odd_w_in, gla_w_gate2, gla_b_gate, gla_norm_w, diff_lambda, diff_norm_w, odd_w_out, ffn_w_up, ffn_conv_w, ffn_conv_b, ffn_w_down):
    weights = (norm_mix_w, norm_ffn_w, final_norm_w, even_w_in, pool_w, pool_scale, swa_sinks, even_w_out, odd_w_in,
               gla_w_gate2, gla_b_gate, gla_norm_w, diff_lambda, diff_norm_w, odd_w_out, ffn_w_up, ffn_conv_w,
               ffn_conv_b, ffn_w_down)
    caches = (cache_pool, cache_swa_k, cache_swa_v, state_gla, cache_diff_k, cache_diff_v, cache_ffn_conv)
    past = cache_diff_k.shape[2]
    pos_p = jnp.arange(x_prompt.shape[1])
    pos_s = past + jnp.arange(x_sample.shape[1])
    outs_p = _trunk(x_prompt, pos_p, True, caches, weights)
    outs_s = _trunk(x_sample, pos_s, False, caches, weights)
    return (outs_p[0], outs_s[0]) + tuple(outs_p[1:]) + tuple(outs_s[1:])
```

```python
import functools
import math

import jax
import jax.numpy as jnp
from jax import lax
from jax.experimental import pallas as pl
from jax.experimental.pallas import tpu as pltpu

F32 = jnp.float32
BF16 = jnp.bfloat16

D_MODEL = 1024
CHUNK = 64
EPS = 1e-6
NEG_INF = -1e30
LOG2E = 1.0 / math.log(2.0)
ROPE_THETA = 500000.0
HEAD_DIM = 64
ROT_DIM = HEAD_DIM // 4
ROT_HALF = ROT_DIM // 2
LANES = 128

POOL_WINDOWS = (2, 4, 8, 16)
POOL_GROUP = D_MODEL // 8
D_A = len(POOL_WINDOWS) * POOL_GROUP
POOL_STATE = max(POOL_WINDOWS) - 1
POOL_HALO = POOL_STATE + 1

H_B = D_MODEL // 128
HKV_B = 2
GQA_B = H_B // HKV_B
WINDOW = 128
D_B = H_B * HEAD_DIM
D_KV = HKV_B * HEAD_DIM

H_C = 4
DK_C = 64
DV_C = 128
GATE_RANK = 16
GATE_NORM = 16.0
D_CQK = H_C * DK_C
D_C = H_C * DV_C

H_D = 4
HD_D = 64
D_D = H_D * 2 * HD_D

D_FF = 2816
CONV_W = 3
CONV_TAIL = 8

ROW_TILE = 1024
FFN_ROW_TILE = 1024
ATTN_TILE = 1024
FFN_COLS = 256
VMEM_LIMIT = 60 * 1024 * 1024


def _dot(a, b):
    return jnp.dot(a, b, preferred_element_type=F32)


def _dot_nt(a, b):
    return lax.dot_general(a, b, (((1,), (1,)), ((), ())), preferred_element_type=F32)


def _rms(x, w):
    return x * lax.rsqrt(jnp.mean(x * x, axis=-1, keepdims=True) + EPS) * w


def _rope(z, cos_t, sin_a, sin_b):
    outs = []
    for s in range(z.shape[1] // LANES):
        x = z[:, s * LANES:(s + 1) * LANES]
        outs.append(x * cos_t + pltpu.roll(x, LANES - ROT_HALF, 1) * sin_a + pltpu.roll(x, ROT_HALF, 1) * sin_b)
    return outs[0] if len(outs) == 1 else jnp.concatenate(outs, axis=-1)


def _rope_tables(pos):
    t = pos.shape[0]
    inv = jnp.exp(-math.log(ROPE_THETA) * jnp.arange(ROT_HALF, dtype=F32) * (2.0 / ROT_DIM))
    ang = pos.astype(F32)[:, None] * inv[None, :]
    cos, sin = jnp.cos(ang), jnp.sin(ang)
    rest = HEAD_DIM - ROT_DIM
    cos_t = jnp.concatenate([cos, cos, jnp.ones((t, rest), F32)], axis=-1)
    sin_a = jnp.concatenate([-sin, jnp.zeros((t, HEAD_DIM - ROT_HALF), F32)], axis=-1)
    sin_b = jnp.concatenate([jnp.zeros((t, ROT_HALF), F32), sin, jnp.zeros((t, rest), F32)], axis=-1)
    rep = LANES // HEAD_DIM
    return tuple(jnp.tile(a, (1, rep)) for a in (cos_t, sin_a, sin_b))


def _params(n_axes=1):
    return pltpu.CompilerParams(dimension_semantics=("arbitrary",) * n_axes, vmem_limit_bytes=VMEM_LIMIT)


def _full(shape):
    nd = len(shape)
    return pl.BlockSpec(shape, lambda *_: (0,) * nd, pipeline_mode=pl.Buffered(1))


def _layer(shape, index):
    nd = len(shape)
    return pl.BlockSpec((1,) + tuple(shape), lambda *_: (index,) + (0,) * nd, pipeline_mode=pl.Buffered(1))


def _rows(tm, cols):
    return pl.BlockSpec((tm, cols), lambda i: (i, 0))


def _even_in_kernel(x_ref, nw_ref, w_ref, cos_ref, sa_ref, sb_ref, u_ref, q_ref, k_ref, v_ref, *tails,
                    tm, prompt):
    h = _rms(x_ref[...], nw_ref[...]).astype(BF16)
    cos_t, sin_a, sin_b = cos_ref[...], sa_ref[...], sb_ref[...]
    q = _dot(h, w_ref[0, :, D_A:D_A + D_B])
    k = _dot(h, w_ref[0, :, D_A + D_B:D_A + D_B + D_KV])
    v = _dot(h, w_ref[0, :, D_A + D_B + D_KV:D_A + D_B + 2 * D_KV])
    q_ref[...] = (_rope(q, cos_t, sin_a, sin_b) * (HEAD_DIM ** -0.5 * LOG2E)).astype(BF16)
    k = _rope(k, cos_t, sin_a, sin_b)
    if prompt:
        k_ref[...] = k.T.astype(BF16)
        v_ref[...] = v.astype(BF16)
        tails[0][0] = k[tm - WINDOW:]
        tails[1][0] = v[tm - WINDOW:]
    else:
        k_ref[...] = k
        v_ref[...] = v
    u_ref[...] = _dot(h, w_ref[0, :, 0:D_A])


def _even_in(x, nw, w, p, tabs, tm, seq, prompt):
    r = x.shape[0]
    t_tiles = tabs[0].shape[0] // tm
    tab_spec = pl.BlockSpec((tm, LANES), lambda i: (i % t_tiles, 0))
    n_in = D_A + D_B + 2 * D_KV
    sds = jax.ShapeDtypeStruct
    if prompt:
        tps = seq // tm
        tail_spec = pl.BlockSpec((1, WINDOW, D_KV), lambda i: (i // tps, 0, 0))
        kv_specs = [pl.BlockSpec((D_KV, tm), lambda i: (0, i)), _rows(tm, D_KV), tail_spec, tail_spec]
        kv_shapes = [sds((D_KV, r), BF16), sds((r, D_KV), BF16),
                     sds((r // seq, WINDOW, D_KV), F32), sds((r // seq, WINDOW, D_KV), F32)]
    else:
        kv_specs = [_rows(tm, D_KV), _rows(tm, D_KV)]
        kv_shapes = [sds((r, D_KV), F32), sds((r, D_KV), F32)]
    return pl.pallas_call(
        functools.partial(_even_in_kernel, tm=tm, prompt=prompt),
        grid=(r // tm,),
        in_specs=[_rows(tm, D_MODEL), _full((1, D_MODEL)), _layer((D_MODEL, n_in), p), tab_spec, tab_spec,
                  tab_spec],
        out_specs=[_rows(tm, D_A), _rows(tm, D_B)] + kv_specs,
        out_shape=[sds((r, D_A), F32), sds((r, D_B), BF16)] + kv_shapes,
        compiler_params=_params(),
        name="even_in",
    )(x, nw, w, *tabs)


def _pool_mix(ext, u, pos, wp_ref, ps, ycat_ref, row0, rows):
    for g, w in enumerate(POOL_WINDOWS):
        lo, hi = g * POOL_GROUP, (g + 1) * POOL_GROUP
        s = ext[:, lo:hi]
        sh = 1
        while sh < w:
            s = s + pltpu.roll(s, sh, 0)
            sh *= 2
        count = jnp.minimum(w, pos + 1).astype(F32)
        diff = s[POOL_HALO:] / count - u[:, lo:hi]
        ya = _dot(diff.astype(BF16), wp_ref[0, g]) * ps[:, lo:hi]
        ycat_ref[row0:row0 + rows, lo:hi] = ya.astype(BF16)


def _sink_column(sink_ref, hk, rows_per_head):
    n = GQA_B * rows_per_head
    grp = lax.broadcasted_iota(jnp.int32, (n, 1), 0) // rows_per_head
    col = jnp.full((n, 1), sink_ref[hk * GQA_B] * LOG2E, F32)
    for g in range(1, GQA_B):
        col = jnp.where(grp == g, sink_ref[hk * GQA_B + g] * LOG2E, col)
    return col


def _sink_attend(s, sink_col, vb):
    m = jnp.maximum(jnp.max(s, axis=-1, keepdims=True), sink_col)
    e = jnp.exp2(s - m)
    den = jnp.sum(e, axis=-1, keepdims=True) + jnp.exp2(sink_col - m)
    return _dot(e.astype(BF16), vb) / den


def _store_heads(ycat_ref, o, row0, rows, hk):
    for pair in range(GQA_B // 2):
        both = jnp.concatenate([o[(2 * pair) * rows:(2 * pair + 1) * rows],
                                o[(2 * pair + 1) * rows:(2 * pair + 2) * rows]], axis=-1)
        c0 = D_A + (hk * GQA_B + 2 * pair) * HEAD_DIM
        ycat_ref[row0:row0 + rows, c0:c0 + 2 * HEAD_DIM] = both.astype(BF16)


def _even_mix_prompt_kernel(sink_ref, u_ref, q_ref, k_ref, v_ref, x_ref, wp_ref, ps_ref, wo_ref, o_ref,
                            ucar, kcar, vcar, ycat, *, tm, tiles_per_seq):
    i = pl.program_id(0)
    ti = i % tiles_per_seq
    first_key_chunk = jnp.where(ti == 0, WINDOW // CHUNK, 0)

    @pl.when(ti == 0)
    def _():
        ucar[...] = jnp.zeros_like(ucar)
        kcar[...] = jnp.zeros_like(kcar)
        vcar[...] = jnp.zeros_like(vcar)

    u = u_ref[...]
    ext = jnp.concatenate([ucar[...], u], axis=0)
    pos = ti * tm + lax.broadcasted_iota(jnp.int32, (tm, POOL_GROUP), 0)
    _pool_mix(ext, u, pos, wp_ref, ps_ref[...], ycat, 0, tm)
    ucar[...] = u[tm - POOL_HALO:]

    kext = jnp.concatenate([kcar[...], k_ref[...]], axis=1)
    vext = jnp.concatenate([vcar[...], v_ref[...]], axis=0)
    kcar[...] = kext[:, tm:]
    vcar[...] = vext[tm:]
    qb = 2 * CHUNK
    kb = qb + WINDOW
    nrow = GQA_B * qb
    qc = (lax.broadcasted_iota(jnp.int32, (nrow, kb), 0) % qb) // CHUNK
    kc = lax.broadcasted_iota(jnp.int32, (nrow, kb), 1) // CHUNK
    band = (kc >= qc) & (kc <= qc + WINDOW // CHUNK)
    first = band & (kc >= first_key_chunk)
    grp = lax.broadcasted_iota(jnp.int32, (nrow, LANES), 0) // qb
    low_lanes = lax.broadcasted_iota(jnp.int32, (qb, LANES), 1) < HEAD_DIM
    ones = jnp.ones((WINDOW + tm, HEAD_DIM), BF16)
    zeros = jnp.zeros((HEAD_DIM, WINDOW + tm), BF16)
    order = [g for g in range(GQA_B) if g % 2 == 0] + [g for g in range(GQA_B) if g % 2 == 1]
    v_aug, sink_rep, k_even, k_odd = [], [], [], []
    for hk in range(HKV_B):
        v_aug.append(jnp.concatenate([vext[:, hk * HEAD_DIM:(hk + 1) * HEAD_DIM], ones], axis=-1))
        kt = kext[hk * HEAD_DIM:(hk + 1) * HEAD_DIM]
        k_even.append(jnp.concatenate([kt, zeros], axis=0))
        k_odd.append(jnp.concatenate([zeros, kt], axis=0))
        rep = jnp.full((nrow, LANES), sink_ref[hk * GQA_B + order[0]] * LOG2E, F32)
        for r in range(1, GQA_B):
            rep = jnp.where(grp == r, sink_ref[hk * GQA_B + order[r]] * LOG2E, rep)
        sink_rep.append(rep)

    def scores(a, hk):
        def tiles(gs):
            return jnp.concatenate(
                [q_ref[a * qb:(a + 1) * qb, (hk * GQA_B + g) // 2 * LANES:((hk * GQA_B + g) // 2 + 1) * LANES]
                 for g in gs], axis=0)
        half = GQA_B // 2
        return jnp.concatenate([_dot(tiles(order[:half]), k_even[hk][:, a * qb:a * qb + kb]),
                                _dot(tiles(order[half:]), k_odd[hk][:, a * qb:a * qb + kb])], axis=0)

    def softmax(a, hk, s):
        s = jnp.where(first if a == 0 else band, s, NEG_INF)
        m = jnp.maximum(jnp.max(s, axis=-1, keepdims=True), sink_rep[hk])
        return jnp.exp2(s - _lane_tile(m, kb // LANES)).astype(BF16), jnp.exp2(sink_rep[hk] - m)

    def finish(a, hk, e, sink_term):
        oa = _dot(e, v_aug[hk][a * qb:a * qb + kb])
        inv = 1.0 / (oa + sink_term)
        for pair in range(GQA_B // 2):
            lo = slice(pair * qb, (pair + 1) * qb)
            hi = slice((GQA_B // 2 + pair) * qb, (GQA_B // 2 + pair + 1) * qb)
            both = jnp.where(low_lanes, oa[lo] * pltpu.roll(inv[lo], HEAD_DIM, 1),
                             pltpu.roll(oa[hi], HEAD_DIM, 1) * inv[hi])
            c0 = D_A + (hk * GQA_B + 2 * pair) * HEAD_DIM
            ycat[a * qb:(a + 1) * qb, c0:c0 + 2 * HEAD_DIM] = both.astype(BF16)

    blocks = [(a, hk) for a in range(tm // qb) for hk in range(HKV_B)]
    nb = len(blocks)
    s_q = {0: scores(*blocks[0])}
    e_q = {}
    for n in range(nb + 2):
        if n + 1 < nb:
            s_q[n + 1] = scores(*blocks[n + 1])
        if n < nb:
            e_q[n] = softmax(*blocks[n], s_q.pop(n))
        if 1 <= n <= nb:
            finish(*blocks[n - 1], *e_q.pop(n - 1))
    o_ref[...] = x_ref[...] + _dot(ycat[...], wo_ref[0])


def _even_mix_prompt(sinks, u, q, kt, v, x, wp, ps, wo, p, seq, tm):
    r = x.shape[0]
    kern = functools.partial(_even_mix_prompt_kernel, tm=tm, tiles_per_seq=seq // tm)
    return pl.pallas_call(
        kern,
        grid=(r // tm,),
        in_specs=[pl.BlockSpec(memory_space=pltpu.SMEM),
                  _rows(tm, D_A), _rows(tm, D_B), pl.BlockSpec((D_KV, tm), lambda i: (0, i)), _rows(tm, D_KV),
                  _rows(tm, D_MODEL), _layer(wp.shape[1:], p), _full((1, D_A)), _layer((D_A + D_B, D_MODEL), p)],
        out_specs=_rows(tm, D_MODEL),
        out_shape=jax.ShapeDtypeStruct((r, D_MODEL), F32),
        scratch_shapes=[pltpu.VMEM((POOL_HALO, D_A), F32), pltpu.VMEM((D_KV, WINDOW), BF16),
                        pltpu.VMEM((WINDOW, D_KV), BF16), pltpu.VMEM((tm, D_A + D_B), BF16)],
        compiler_params=_params(),
        name="even_mix_prompt",
    )(sinks, u, q, kt, v, x, wp, ps, wo)


def _even_mix_sample_kernel(sink_ref, u_ref, q_ref, k_ref, v_ref, x_ref, pp_ref, kc_ref, vc_ref, wp_ref, ps_ref,
                            wo_ref, o_ref, ycat, *, l, pos0):
    u = u_ref[...]
    ext = jnp.concatenate([pp_ref[0], u], axis=0)
    pos = pos0 + lax.broadcasted_iota(jnp.int32, (l, 1), 0)
    _pool_mix(ext, u, pos, wp_ref, ps_ref[...], ycat, 0, l)

    pad = jnp.zeros((LANES - l, D_KV), BF16)
    kall = jnp.concatenate([kc_ref[0].astype(BF16), k_ref[...].astype(BF16), pad], axis=0)
    vall = jnp.concatenate([vc_ref[0].astype(BF16), v_ref[...].astype(BF16), pad], axis=0)
    nk = kall.shape[0]
    valid = lax.broadcasted_iota(jnp.int32, (GQA_B * l, nk), 1) < nk - (LANES - l)
    for hk in range(HKV_B):
        ks = kall[:, hk * HEAD_DIM:(hk + 1) * HEAD_DIM]
        vs = vall[:, hk * HEAD_DIM:(hk + 1) * HEAD_DIM]
        qs = jnp.concatenate(
            [q_ref[:, (hk * GQA_B + g) * HEAD_DIM:(hk * GQA_B + g + 1) * HEAD_DIM] for g in range(GQA_B)], axis=0)
        s = jnp.where(valid, _dot_nt(qs, ks), NEG_INF)
        o = _sink_attend(s, _sink_column(sink_ref, hk, l), vs)
        _store_heads(ycat, o, 0, l, hk)
    o_ref[...] = x_ref[...] + _dot(ycat[...], wo_ref[0])


def _even_mix_sample(sinks, u, q, k, v, x, pool_prev, kc, vc, wp, ps, wo, p, l, pos0):
    r = x.shape[0]
    rows_c = kc.shape[1]
    kern = functools.partial(_even_mix_sample_kernel, l=l, pos0=pos0)
    cache_spec = pl.BlockSpec((1, rows_c, D_KV), lambda i: (i, 0, 0))
    return pl.pallas_call(
        kern,
        grid=(r // l,),
        in_specs=[pl.BlockSpec(memory_space=pltpu.SMEM),
                  _rows(l, D_A), _rows(l, D_B), _rows(l, D_KV), _rows(l, D_KV), _rows(l, D_MODEL),
                  pl.BlockSpec((1, POOL_HALO, D_A), lambda i: (i, 0, 0)), cache_spec, cache_spec,
                  _layer(wp.shape[1:], p), _full((1, D_A)), _layer((D_A + D_B, D_MODEL), p)],
        out_specs=_rows(l, D_MODEL),
        out_shape=jax.ShapeDtypeStruct((r, D_MODEL), F32),
        scratch_shapes=[pltpu.VMEM((l, D_A + D_B), BF16)],
        compiler_params=_params(),
        name="even_mix_sample",
    )(sinks, u, q, k, v, x, pool_prev, kc, vc, wp, ps, wo)


ODD_CR0 = 2 * D_CQK + 2 * D_C
ODD_DQ0 = ODD_CR0 + LANES
ODD_IN_PADDED = ODD_DQ0 + 3 * D_D


def _odd_in_kernel(x_ref, nw_ref, w_ref, wg_ref, bg_ref, cos_ref, sa_ref, sb_ref, *rest, tm, blk, cache_layout,
                   aliased):
    if aliased:
        rest = rest[2:]
    qd_ref, kd_ref, el_ref, cv_ref, cg_ref, dq_ref, dk_ref, dv_ref, cum_scr = rest
    h = _rms(x_ref[...], nw_ref[...]).astype(BF16)
    cos_t, sin_a, sin_b = cos_ref[...], sa_ref[...], sb_ref[...]
    cr = _dot(h, w_ref[0, :, ODD_CR0:ODD_DQ0]).astype(BF16)
    gate = _dot(cr, wg_ref[0]) + bg_ref[...]
    cq = _dot(h, w_ref[0, :, 0:D_CQK])
    ck = _dot(h, w_ref[0, :, D_CQK:2 * D_CQK])
    dq = _dot(h, w_ref[0, :, ODD_DQ0:ODD_DQ0 + D_D])
    cum = (jnp.minimum(gate, 0.0) - jnp.log1p(jnp.exp(-jnp.abs(gate)))) * (1.0 / GATE_NORM)
    rowb = lax.broadcasted_iota(jnp.int32, (tm, 1), 0) % blk
    sh = 1
    while sh < blk:
        cum = cum + jnp.where(rowb >= sh, pltpu.roll(cum, sh, 0), 0.0)
        sh *= 2
    nb = tm // blk
    for s in range(D_CQK // LANES):
        cum_scr[s] = cum[:, s * LANES:(s + 1) * LANES]
        last = cum_scr[s, pl.ds(blk - 1, nb, stride=blk), :]
        last = jnp.concatenate([last, jnp.zeros((LANES - nb, LANES), F32)], axis=0)
        el_ref[0, s * LANES:(s + 1) * LANES, :] = jnp.exp(last.T)
    qd_ref[...] = (cq * (DK_C ** -0.5) * jnp.exp(cum)).astype(BF16)
    kd_ref[...] = (ck * jnp.exp(-cum)).T.astype(BF16)
    dk = _dot(h, w_ref[0, :, ODD_DQ0 + D_D:ODD_DQ0 + 2 * D_D])
    dq_ref[...] = (_rope(dq, cos_t, sin_a, sin_b) * (HD_D ** -0.5 * LOG2E)).astype(BF16)
    dv = _dot(h, w_ref[0, :, ODD_DQ0 + 2 * D_D:ODD_DQ0 + 3 * D_D])
    dk = _rope(dk, cos_t, sin_a, sin_b)
    cv_ref[...] = _dot(h, w_ref[0, :, 2 * D_CQK:2 * D_CQK + D_C]).astype(BF16)
    cg_ref[...] = _dot(h, w_ref[0, :, 2 * D_CQK + D_C:ODD_CR0])
    if cache_layout:
        for hd in range(H_D):
            lo, hi = hd * 2 * HD_D, (hd + 1) * 2 * HD_D
            dk_ref[0, 0, hd] = dk[:, lo:hi].T
            dv_ref[0, pl.ds(hd, tm, stride=H_D), :] = dv[:, lo:hi]
    else:
        dk_ref[...] = dk
        dv_ref[...] = dv


def _odd_in(x, nw, w, wg, wi, bg, tabs, tm, blk, pair=None):
    r = x.shape[0]
    t_tiles = tabs[0].shape[0] // tm
    tab_spec = pl.BlockSpec((tm, LANES), lambda i: (i % t_tiles, 0))
    nb = tm // blk
    sds = jax.ShapeDtypeStruct
    in_specs = [_rows(tm, D_MODEL), _full((1, D_MODEL)), _layer((D_MODEL, ODD_IN_PADDED), wi),
                _layer((LANES, D_CQK), wi), _full((1, D_CQK)), tab_spec, tab_spec, tab_spec]
    args = [x, nw, w, wg, bg, *tabs]
    aliases = {}
    if pair is None:
        kv_specs = [_rows(tm, D_D), _rows(tm, D_D)]
        kv_shapes = [sds((r, D_D), F32), sds((r, D_D), F32)]
    else:
        p, n_pairs, batch, seq, bufs = pair
        tps = seq // tm
        kv_specs = [pl.BlockSpec((1, 1, H_D, 2 * HD_D, tm), lambda i: (p, i // tps, 0, 0, i % tps)),
                    pl.BlockSpec((1, tm * H_D, 2 * HD_D), lambda i: (p, i, 0))]
        kv_shapes = [sds((n_pairs, batch, H_D, 2 * HD_D, seq), F32), sds((n_pairs, r * H_D, 2 * HD_D), F32)]
        if bufs is not None:
            in_specs += [pl.BlockSpec(memory_space=pl.ANY)] * 2
            args += list(bufs)
            aliases = {len(args) - 2: 6, len(args) - 1: 7}
    kern = functools.partial(_odd_in_kernel, tm=tm, blk=blk, cache_layout=pair is not None, aliased=bool(aliases))
    return pl.pallas_call(
        kern,
        grid=(r // tm,),
        in_specs=in_specs,
        out_specs=[_rows(tm, D_CQK), pl.BlockSpec((D_CQK, tm), lambda i: (0, i)),
                   pl.BlockSpec((1, D_CQK, LANES), lambda i: (i, 0, 0)), _rows(tm, D_C), _rows(tm, D_C),
                   _rows(tm, D_D)] + kv_specs,
        out_shape=[sds((r, D_CQK), BF16), sds((D_CQK, r), BF16), sds((r // tm, D_CQK, LANES), F32),
                   sds((r, D_C), BF16), sds((r, D_C), F32), sds((r, D_D), BF16)] + kv_shapes,
        scratch_shapes=[pltpu.VMEM((D_CQK // LANES, tm, LANES), F32)],
        input_output_aliases=aliases,
        compiler_params=_params(),
        name="odd_in",
    )(*args)


def _lambda_value(lp_ref, lambda_init):
    lp = lp_ref[...]
    s1 = jnp.sum(lp[0:1] * lp[1:2], axis=-1, keepdims=True)
    s2 = jnp.sum(lp[2:3] * lp[3:4], axis=-1, keepdims=True)
    return jnp.exp(s1) - jnp.exp(s2) + lambda_init


def _diff_finish(acc, den, lam, nw, lambda_init):
    o = acc[0] / den[0] - lam * (acc[1] / den[1])
    return (_rms(o, nw) * (1.0 - lambda_init)).astype(BF16)


def _lane_tile(x, n):
    return x if n == 1 else jnp.concatenate([x] * n, axis=1)


def _diff_prompt_kernel(lp_ref, nw_ref, q_ref, k_ref, v_ref, o_ref, kbf, vbf, m_scr, acc_scr, s_a, s_b, bias,
                        *, tq, lambda_init):
    hd = pl.program_id(1)
    dv = 2 * HD_D
    seq = vbf.shape[0]
    tk = tq // 2

    kbf[...] = k_ref[0, 0, 0].astype(BF16)
    vbf[:, :dv] = v_ref[0, pl.ds(hd, seq, stride=H_D), :].astype(BF16)
    vbf[:, dv:] = jnp.ones((seq, dv), BF16)

    @pl.when((pl.program_id(0) == 0) & (hd == 0))
    def _():
        qc = lax.broadcasted_iota(jnp.int32, (tk, tk), 0) // CHUNK
        kc = lax.broadcasted_iota(jnp.int32, (tk, tk), 1) // CHUNK
        bias[...] = jnp.where(kc <= qc, 0.0, NEG_INF)

    lam = _lambda_value(lp_ref, lambda_init)

    def scores(qi, t, s_buf, row0=0):
        start = pl.multiple_of(t * tk, tk)
        rows = pl.ds(pl.multiple_of(qi * tq + row0, tk), tq - row0)
        for c in range(2):
            s_buf[c, row0:, :] = _dot(q_ref[rows, c * HD_D:(c + 1) * HD_D],
                                      kbf[c * HD_D:(c + 1) * HD_D, pl.ds(start, tk)])

    def attend(t, s_buf, row0=0, rows=tq, masked=False):
        vs = vbf[pl.ds(pl.multiple_of(t * tk, tk), tk), :]
        for c in range(2):
            sc = s_buf[c, row0:row0 + rows, :]
            if masked:
                sc = sc + bias[...]
            m_old = m_scr[c, row0:row0 + rows, :]
            m_new = jnp.maximum(m_old, jnp.max(sc, axis=-1, keepdims=True))
            p = jnp.exp2(sc - _lane_tile(m_new, tk // LANES)).astype(BF16)
            alpha = _lane_tile(jnp.exp2(m_old - m_new), 2 * dv // LANES)
            acc_scr[c, row0:row0 + rows, :] = alpha * acc_scr[c, row0:row0 + rows, :] + _dot(p, vs)
            m_scr[c, row0:row0 + rows, :] = m_new

    def query_tile(qi, _):
        m_scr[...] = jnp.full(m_scr.shape, NEG_INF, F32)
        acc_scr[...] = jnp.zeros(acc_scr.shape, F32)
        scores(qi, 0, s_a)

        def pair(j, _):
            t = 2 * j
            scores(qi, t + 1, s_b)
            attend(t, s_a)
            scores(qi, t + 2, s_a)
            attend(t + 1, s_b)
            return 0

        lax.fori_loop(0, qi, pair, 0)
        scores(qi, 2 * qi + 1, s_b, row0=tk)
        attend(2 * qi, s_a, row0=0, rows=tk, masked=True)
        attend(2 * qi, s_a, row0=tk, rows=tk)
        attend(2 * qi + 1, s_b, row0=tk, rows=tk, masked=True)
        acc = [acc_scr[c] for c in range(2)]
        o_ref[pl.ds(pl.multiple_of(qi * tq, tq), tq), :] = _diff_finish(
            [a[:, :dv] for a in acc], [a[:, dv:] for a in acc], lam, nw_ref[...], lambda_init)
        return 0

    lax.fori_loop(0, seq // tq, query_tile, 0)


def _diff_prompt(lp, nw, q, kt, v, p, batch, seq, lambda_init, tq):
    r = q.shape[0]
    kern = functools.partial(_diff_prompt_kernel, tq=tq, lambda_init=lambda_init)
    k_spec = pl.BlockSpec((1, 1, 1, 2 * HD_D, seq), lambda b, h: (p, b, h, 0, 0))
    v_spec = pl.BlockSpec((1, seq * H_D, 2 * HD_D), lambda b, h: (p, b, 0))
    q_spec = pl.BlockSpec((seq, 2 * HD_D), lambda b, h: (b, h))
    return pl.pallas_call(
        kern,
        grid=(batch, H_D),
        in_specs=[_full((4, HD_D)), _full((1, 2 * HD_D)), q_spec, k_spec, v_spec],
        out_specs=q_spec,
        out_shape=jax.ShapeDtypeStruct((r, D_D), BF16),
        scratch_shapes=[pltpu.VMEM((2 * HD_D, seq), BF16), pltpu.VMEM((seq, 4 * HD_D), BF16),
                        pltpu.VMEM((2, tq, LANES), F32), pltpu.VMEM((2, tq, 4 * HD_D), F32),
                        pltpu.VMEM((2, tq, tq // 2), F32), pltpu.VMEM((2, tq, tq // 2), F32),
                        pltpu.VMEM((tq // 2, tq // 2), F32)],
        compiler_params=_params(2),
        name="diff_prompt",
    )(lp, nw, q, kt, v)


def _diff_sample_kernel(lp_ref, nw_ref, q_ref, k_ref, v_ref, kc_ref, vc_ref, o_ref, *, l, past, lambda_init):
    pad = jnp.zeros((LANES - l, 2 * HD_D), BF16)
    valid = lax.broadcasted_iota(jnp.int32, (l, LANES), 1) < l
    lam = _lambda_value(lp_ref, lambda_init)
    for hd in range(H_D):
        lo, hi = hd * 2 * HD_D, (hd + 1) * 2 * HD_D
        knew = jnp.concatenate([k_ref[:, lo:hi].astype(BF16), pad], axis=0)
        vnew = jnp.concatenate([v_ref[:, lo:hi].astype(BF16), pad], axis=0)
        kct = kc_ref[0, 0, hd].astype(BF16)
        vcache = vc_ref[0, pl.ds(hd, past, stride=H_D), :].astype(BF16)
        acc, den = [], []
        for c in range(2):
            qc = q_ref[:, lo + c * HD_D:lo + (c + 1) * HD_D]
            s_old = _dot(qc, kct[c * HD_D:(c + 1) * HD_D, :])
            s_new = jnp.where(valid, _dot_nt(qc, knew[:, c * HD_D:(c + 1) * HD_D]), NEG_INF)
            m = jnp.maximum(jnp.max(s_old, axis=-1, keepdims=True), jnp.max(s_new, axis=-1, keepdims=True))
            p_old = jnp.exp2(s_old - m)
            p_new = jnp.exp2(s_new - m)
            den.append(jnp.sum(p_old, axis=-1, keepdims=True) + jnp.sum(p_new, axis=-1, keepdims=True))
            acc.append(_dot(p_old.astype(BF16), vcache) + _dot(p_new.astype(BF16), vnew))
        o_ref[:, lo:hi] = _diff_finish(acc, den, lam, nw_ref[...], lambda_init)


def _diff_sample(lp, nw, q, k, v, kct, vc, p, batch, l, lambda_init):
    r = q.shape[0]
    past = kct.shape[-1]
    kern = functools.partial(_diff_sample_kernel, l=l, past=past, lambda_init=lambda_init)
    cache_k_spec = pl.BlockSpec((1, 1, H_D, 2 * HD_D, past), lambda b: (p, b, 0, 0, 0))
    cache_v_spec = pl.BlockSpec((1, past * H_D, 2 * HD_D), lambda b: (p, b, 0))
    return pl.pallas_call(
        kern,
        grid=(batch,),
        in_specs=[_full((4, HD_D)), _full((1, 2 * HD_D)), _rows(l, D_D), _rows(l, D_D), _rows(l, D_D),
                  cache_k_spec, cache_v_spec],
        out_specs=_rows(l, D_D),
        out_shape=jax.ShapeDtypeStruct((r, D_D), BF16),
        compiler_params=_params(),
        name="diff_sample",
    )(lp, nw, q, k, v, kct, vc)


def _odd_mix_kernel(qd_ref, kd_ref, el_ref, cv_ref, cg_ref, yd_ref, x_ref, s0_ref, gw_ref, wo_ref,
                    o_ref, s_ref, state, og, ycat, *, tm, blk, tiles_per_seq, per_block_state):
    i = pl.program_id(0)
    if not per_block_state:
        @pl.when(i % tiles_per_seq == 0)
        def _():
            state[...] = jnp.zeros_like(state)

    per_group = LANES // blk
    rows_of = lax.broadcasted_iota(jnp.int32, (LANES, LANES), 0)
    cols_of = lax.broadcasted_iota(jnp.int32, (LANES, LANES), 1)
    q_row = lax.broadcasted_iota(jnp.int32, (blk, LANES), 0)
    q_col = lax.broadcasted_iota(jnp.int32, (blk, LANES), 1)
    low = [(h % 2) * DK_C for h in range(H_C)]
    own_rows = [(rows_of >= low[h]) & (rows_of < low[h] + DK_C) for h in range(H_C)]
    lanes = [slice(h // 2 * LANES, (h // 2 + 1) * LANES) for h in range(H_C)]
    vcols = [slice(h * DV_C, (h + 1) * DV_C) for h in range(H_C)]
    st = [None if per_block_state else state[h] for h in range(H_C)]
    for n in range(tm // blk):
        g0, c = (n // per_group) * LANES, n % per_group
        r0 = n * blk
        causal = (q_col >= c * blk) & (q_col - c * blk <= q_row)
        own_cols = (cols_of >= c * blk) & (cols_of < (c + 1) * blk)
        for h in range(H_C):
            if per_block_state:
                st[h] = jnp.where(own_rows[h], jnp.concatenate([s0_ref[n, h]] * 2, axis=0), 0.0)
            qd = qd_ref[r0:r0 + blk, lanes[h]]
            kt = jnp.where(own_rows[h], kd_ref[lanes[h], g0:g0 + LANES], jnp.zeros((), BF16))
            vg = cv_ref[g0:g0 + LANES, vcols[h]]
            att = jnp.where(causal, _dot(qd, kt), 0.0).astype(BF16)
            og[r0:r0 + blk, vcols[h]] = _dot(att, vg) + _dot(qd, st[h].astype(BF16))
            k_blk = jnp.where(own_cols, kt, jnp.zeros((), BF16))
            decay = jnp.where(own_rows[h], jnp.broadcast_to(el_ref[0, lanes[h], n:n + 1], (LANES, LANES)), 0.0)
            st[h] = decay * (st[h] + _dot(k_blk, vg))
            if per_block_state:
                s_ref[n, h] = st[h][low[h]:low[h] + DK_C]
    if not per_block_state:
        for h in range(H_C):
            state[h] = st[h]
            s_ref[0, h] = st[h][low[h]:low[h] + DK_C]

    gw = gw_ref[...]
    for h in range(H_C):
        lo, hi = h * DV_C, (h + 1) * DV_C
        g = cg_ref[:, lo:hi]
        y = _rms(og[:, lo:hi], gw) * (g * (1.0 / (1.0 + jnp.exp(-g))))
        ycat[:, lo:hi] = y.astype(BF16)
    ycat[:, D_C:] = yd_ref[...]
    o_ref[...] = x_ref[...] + _dot(ycat[...], wo_ref[0])


def _odd_mix(qd, kd, el, cv, cg, yd, x, s0, gw, wo, p, tm, blk, seq, per_block_state):
    assert tm % LANES == 0 and LANES % blk == 0 and tm // blk <= LANES
    r = x.shape[0]
    nb = tm // blk
    tiles_per_seq = max(seq // tm, 1)
    kern = functools.partial(_odd_mix_kernel, tm=tm, blk=blk, tiles_per_seq=tiles_per_seq,
                             per_block_state=per_block_state)
    if per_block_state:
        n_state = r // blk
        state_spec = pl.BlockSpec((nb, H_C, DK_C, DV_C), lambda i: (i, 0, 0, 0))
    else:
        n_state = r // seq
        state_spec = pl.BlockSpec((1, H_C, DK_C, DV_C), lambda i: (i // tiles_per_seq, 0, 0, 0))
    return pl.pallas_call(
        kern,
        grid=(r // tm,),
        in_specs=[_rows(tm, D_CQK), pl.BlockSpec((D_CQK, tm), lambda i: (0, i)),
                  pl.BlockSpec((1, D_CQK, LANES), lambda i: (i, 0, 0)), _rows(tm, D_C), _rows(tm, D_C),
                  _rows(tm, D_D), _rows(tm, D_MODEL), state_spec, _full((1, DV_C)),
                  _layer((D_C + D_D, D_MODEL), p)],
        out_specs=[_rows(tm, D_MODEL), state_spec],
        out_shape=[jax.ShapeDtypeStruct((r, D_MODEL), F32),
                   jax.ShapeDtypeStruct((n_state, H_C, DK_C, DV_C), F32)],
        scratch_shapes=[pltpu.VMEM((H_C, LANES, DV_C), F32), pltpu.VMEM((tm, D_C), F32),
                        pltpu.VMEM((tm, D_C + D_D), BF16)],
        compiler_params=_params(),
        name="odd_mix",
    )(qd, kd, el, cv, cg, yd, x, s0, gw, wo)


def _gelu_tanh(x):
    return 0.5 * x * (1.0 + jnp.tanh(math.sqrt(2.0 / math.pi) * (x + 0.044715 * (x * x * x))))


def _ffn_kernel(x_ref, nw_ref, wup_ref, cw_ref, cb_ref, wdn_ref, prev_ref, fnw_ref, o_ref, tail_ref, carry, act,
                *, tm, seg, tiles_per_seq, prev_from_input, final_norm):
    i = pl.program_id(0)
    x = x_ref[...]
    nseg = tm // seg
    row = lax.broadcasted_iota(jnp.int32, (seg, 1), 0)
    if not prev_from_input:
        @pl.when(i % tiles_per_seq == 0)
        def _():
            carry[...] = jnp.zeros_like(carry)

    def up(hh, c0):
        return (_dot(hh, wup_ref[0, :, c0:c0 + FFN_COLS]),
                _dot(hh, wup_ref[0, :, D_FF + c0:D_FF + c0 + FFN_COLS]))

    h = _rms(x, nw_ref[...]).astype(BF16)
    nxt = up(h, 0)
    for c0 in range(0, D_FF, FFN_COLS):
        c1 = c0 + FFN_COLS
        gate, val = nxt
        if c1 < D_FF:
            nxt = up(h, c1)
        w0, w1, w2 = cw_ref[0:1, c0:c1], cw_ref[1:2, c0:c1], cw_ref[2:3, c0:c1]
        convs = []
        for s in range(nseg):
            g = gate[s * seg:(s + 1) * seg]
            if prev_from_input:
                p0, p1 = prev_ref[s, 0:1, c0:c1], prev_ref[s, 1:2, c0:c1]
            else:
                p0 = carry[CONV_TAIL - 2:CONV_TAIL - 1, c0:c1]
                p1 = carry[CONV_TAIL - 1:CONV_TAIL, c0:c1]
            g1 = jnp.where(row == 0, p1, pltpu.roll(g, 1, 0))
            g2 = jnp.where(row == 0, p0, jnp.where(row == 1, p1, pltpu.roll(g, 2, 0)))
            convs.append(cb_ref[:, c0:c1] + g2 * w0 + g1 * w1 + g * w2)
            tail_ref[s, :, c0:c1] = g[seg - CONV_TAIL:]
        conv = convs[0] if nseg == 1 else jnp.concatenate(convs, axis=0)
        if not prev_from_input:
            carry[:, c0:c1] = gate[tm - CONV_TAIL:]
        act[:, c0:c1] = (_gelu_tanh(conv) * val).astype(BF16)
    if final_norm:
        o_ref[...] = _rms(x + _dot(act[...], wdn_ref[0]), fnw_ref[...])
    else:
        half = D_MODEL // 2
        for c0 in (0, half):
            o_ref[:, c0:c0 + half] = x[:, c0:c0 + half] + _dot(act[...], wdn_ref[0, :, c0:c0 + half])


def _ffn(x, nw, wup, cw, cb, wdn, prev, fnw, layer, tm, seg, seq, prev_from_input, final_norm):
    r = x.shape[0]
    nseg = tm // seg
    kern = functools.partial(_ffn_kernel, tm=tm, seg=seg, tiles_per_seq=max(seq // tm, 1),
                             prev_from_input=prev_from_input, final_norm=final_norm)
    prev_spec = pl.BlockSpec((nseg, CONV_W - 1, D_FF), lambda i: (i, 0, 0))
    tail_spec = pl.BlockSpec((nseg, CONV_TAIL, D_FF), lambda i: (i, 0, 0))
    wup_spec = pl.BlockSpec((1, D_MODEL, 2 * D_FF), lambda i: (layer, 0, 0), pipeline_mode=pl.Buffered(1))
    wdn_spec = pl.BlockSpec((1, D_FF, D_MODEL), lambda i: (layer, 0, 0), pipeline_mode=pl.Buffered(1))
    return pl.pallas_call(
        kern,
        grid=(r // tm,),
        in_specs=[_rows(tm, D_MODEL), _full((1, D_MODEL)), wup_spec, _full((CONV_W, D_FF)),
                  _full((1, D_FF)), wdn_spec, prev_spec, _full((1, D_MODEL))],
        out_specs=[_rows(tm, D_MODEL), tail_spec],
        out_shape=[jax.ShapeDtypeStruct((r, D_MODEL), F32),
                   jax.ShapeDtypeStruct((r // seg, CONV_TAIL, D_FF), F32)],
        scratch_shapes=[pltpu.VMEM((CONV_TAIL, D_FF), F32), pltpu.VMEM((tm, D_FF), BF16)],
        compiler_params=_params(),
        name="ffn",
    )(x, nw, wup, cw, cb, wdn, prev, fnw)


def _pad_odd_w_in(w):
    cr = w[..., ODD_CR0:ODD_CR0 + GATE_RANK]
    pad = jnp.zeros(w.shape[:-1] + (LANES - GATE_RANK,), w.dtype)
    return jnp.concatenate([w[..., :ODD_CR0], cr, pad, w[..., ODD_CR0 + GATE_RANK:]], axis=-1).astype(BF16)


def _trunk(x3, pos, prompt, caches, weights):
    (norm_mix_w, norm_ffn_w, final_norm_w, even_w_in, pool_w, pool_scale, swa_sinks, even_w_out, odd_w_in,
     gla_w_gate2, gla_b_gate, gla_norm_w, diff_lambda, diff_norm_w, odd_w_out, ffn_w_up, ffn_conv_w, ffn_conv_b,
     ffn_w_down) = weights
    cache_pool, cache_swa_k, cache_swa_v, state_gla, cache_diff_k, cache_diff_v, cache_ffn_conv = caches
    b, l, _ = x3.shape
    r = b * l
    depth = norm_mix_w.shape[0]
    x = x3.reshape(r, D_MODEL)
    if prompt:
        tm = min(ROW_TILE, l)
        blk = min(CHUNK, l)
        tabs = _rope_tables(pos)
    else:
        tm = r
        blk = min(CHUNK, l)
        tabs = tuple(jnp.tile(t, (b, 1)) for t in _rope_tables(pos))
    pos0 = 0 if prompt else cache_diff_k.shape[2]

    pools, swa_ks, swa_vs, glas, diff_ks, diff_vs, ffns = [], [], [], [], [], [], []
    kv_p = None
    for layer in range(depth):
        p = layer // 2
        nw = norm_mix_w[layer].reshape(1, D_MODEL)
        if layer % 2 == 0:
            u, q, k, v, *tails = _even_in(x, nw, even_w_in.astype(BF16), p, tabs, tm, l, prompt)
            wp = pool_w.astype(BF16)
            ps = pool_scale[p].reshape(1, D_A)
            wo = even_w_out.astype(BF16)
            if prompt:
                x = _even_mix_prompt(swa_sinks[p], u, q, k, v, x, wp, ps, wo, p, l, tm)
                pools.append(u.reshape(b, l, D_A)[:, l - POOL_STATE:])
                swa_ks.append(tails[0].reshape(b, WINDOW, HKV_B, HEAD_DIM))
                swa_vs.append(tails[1].reshape(b, WINDOW, HKV_B, HEAD_DIM))
            else:
                rows_c = cache_swa_k.shape[2]
                pool_prev = jnp.pad(cache_pool[p], ((0, 0), (1, 0), (0, 0)))
                kc = cache_swa_k[p].reshape(b, rows_c, D_KV)
                vc = cache_swa_v[p].reshape(b, rows_c, D_KV)
                x = _even_mix_sample(swa_sinks[p], u, q, k, v, x, pool_prev, kc, vc, wp, ps, wo, p, l, pos0)
                pools.append(jnp.concatenate([cache_pool[p], u.reshape(b, l, D_A)], axis=1)[:, -POOL_STATE:])
                swa_ks.append(jnp.concatenate([cache_swa_k[p], k.reshape(b, l, HKV_B, HEAD_DIM)], axis=1)[:, -rows_c:])
                swa_vs.append(jnp.concatenate([cache_swa_v[p], v.reshape(b, l, HKV_B, HEAD_DIM)], axis=1)[:, -rows_c:])
        else:
            lambda_init = 0.8 - 0.6 * math.exp(-0.3 * layer)
            wg = jnp.pad(gla_w_gate2, ((0, 0), (0, LANES - GATE_RANK), (0, 0))).astype(BF16)
            bg = gla_b_gate[p].reshape(1, D_CQK)
            lp = diff_lambda[p]
            dnw = diff_norm_w[p].reshape(1, 2 * HD_D)
            w_in = _pad_odd_w_in(odd_w_in)
            if prompt:
                qd, kd, el, cv, cg, dq, *kv_p = _odd_in(x, nw, w_in, wg, p, bg, tabs, tm, blk,
                                                        pair=(p, depth // 2, b, l, kv_p))
                yd = _diff_prompt(lp, dnw, dq, kv_p[0], kv_p[1], p, b, l, lambda_init, min(ATTN_TILE, l))
                s0 = jnp.zeros((b, H_C, DK_C, DV_C), F32)
            else:
                qd, kd, el, cv, cg, dq, dk, dv = _odd_in(x, nw, w_in, wg, p, bg, tabs, tm, blk)
                past = cache_diff_k.shape[2]
                kct = jnp.transpose(cache_diff_k, (0, 1, 3, 4, 5, 2)).reshape(depth // 2, b, H_D, 2 * HD_D, past)
                vc = cache_diff_v.reshape(depth // 2, b * past * H_D, 2 * HD_D)
                yd = _diff_sample(lp, dnw, dq, dk, dv, kct, vc, p, b, l, lambda_init)
                s0 = state_gla[p]
                diff_ks.append(dk.reshape(b, l, H_D, 2, HD_D))
                diff_vs.append(dv.reshape(b, l, H_D, 2 * HD_D))
            x, s_new = _odd_mix(qd, kd, el, cv, cg, yd, x, s0, gla_norm_w[p].reshape(1, DV_C),
                                odd_w_out.astype(BF16), p, tm, blk, l, not prompt)
            glas.append(s_new)
        if prompt:
            tf = seg = min(FFN_ROW_TILE, l)
            prev = jnp.zeros((r // tf, CONV_W - 1, D_FF), F32)
        else:
            tf, seg = tm, l
            prev = cache_ffn_conv[layer]
        x, tails = _ffn(x, norm_ffn_w[layer].reshape(1, D_MODEL), ffn_w_up.astype(BF16), ffn_conv_w[layer],
                        ffn_conv_b[layer].reshape(1, D_FF), ffn_w_down.astype(BF16), prev,
                        final_norm_w.reshape(1, D_MODEL), layer, tf, seg, l, not prompt, layer == depth - 1)
        tails = tails.reshape(b, -1, CONV_TAIL, D_FF)
        ffns.append(tails[:, -1, CONV_TAIL - (CONV_W - 1):])
    if prompt:
        n_pairs = depth // 2
        diff_k = jnp.transpose(kv_p[0].reshape(n_pairs, b, H_D, 2, HD_D, l), (0, 1, 5, 2, 3, 4))
        diff_v = kv_p[1].reshape(n_pairs, b, l, H_D, 2 * HD_D)
    else:
        diff_k, diff_v = jnp.stack(diff_ks), jnp.stack(diff_vs)
    return (x.reshape(b, l, D_MODEL), jnp.stack(pools), jnp.stack(swa_ks), jnp.stack(swa_vs), jnp.stack(glas),
            diff_k, diff_v, jnp.stack(ffns))


def kernel(x_prompt, x_sample, cache_pool, cache_swa_k, cache_swa_v, state_gla, cache_diff_k, cache_diff_v, cache_ffn_conv, norm_mix_w, norm_ffn_w, final_norm_w, even_w_in, pool_w, pool_scale, swa_sinks, even_w_out, odd_w_in, gla_w_gate2, gla_b_gate, gla_norm_w, diff_lambda, diff_norm_w, odd_w_out, ffn_w_up, ffn_conv_w, ffn_conv_b, ffn_w_down):
    weights = (norm_mix_w, norm_ffn_w, final_norm_w, even_w_in, pool_w, pool_scale, swa_sinks, even_w_out, odd_w_in,
               gla_w_gate2, gla_b_gate, gla_norm_w, diff_lambda, diff_norm_w, odd_w_out, ffn_w_up, ffn_conv_w,
               ffn_conv_b, ffn_w_down)
    caches = (cache_pool, cache_swa_k, cache_swa_v, state_gla, cache_diff_k, cache_diff_v, cache_ffn_conv)
    past = cache_diff_k.shape[2]
    pos_p = jnp.arange(x_prompt.shape[1])
    pos_s = past + jnp.arange(x_sample.shape[1])
    outs_p = _trunk(x_prompt, pos_p, True, caches, weights)
    outs_s = _trunk(x_sample, pos_s, False, caches, weights)
    return (outs_p[0], outs_s[0]) + tuple(outs_p[1:]) + tuple(outs_s[1:])
```

```python
import functools
import math

import jax
import jax.numpy as jnp
from jax import lax
from jax.experimental import pallas as pl
from jax.experimental.pallas import tpu as pltpu

F32 = jnp.float32
BF16 = jnp.bfloat16

D_MODEL = 1024
CHUNK = 64
EPS = 1e-6
NEG_INF = -1e30
LOG2E = 1.0 / math.log(2.0)
ROPE_THETA = 500000.0
HEAD_DIM = 64
ROT_DIM = HEAD_DIM // 4
ROT_HALF = ROT_DIM // 2
LANES = 128

POOL_WINDOWS = (2, 4, 8, 16)
POOL_GROUP = D_MODEL // 8
D_A = len(POOL_WINDOWS) * POOL_GROUP
POOL_STATE = max(POOL_WINDOWS) - 1
POOL_HALO = POOL_STATE + 1

H_B = D_MODEL // 128
HKV_B = 2
GQA_B = H_B // HKV_B
WINDOW = 128
D_B = H_B * HEAD_DIM
D_KV = HKV_B * HEAD_DIM

H_C = 4
DK_C = 64
DV_C = 128
GATE_RANK = 16
GATE_NORM = 16.0
D_CQK = H_C * DK_C
D_C = H_C * DV_C

H_D = 4
HD_D = 64
D_D = H_D * 2 * HD_D

D_FF = 2816
CONV_W = 3
CONV_TAIL = 8

ROW_TILE = 1024
FFN_ROW_TILE = 1024
ATTN_TILE = 1024
FFN_COLS = 256
VMEM_LIMIT = 60 * 1024 * 1024


def _dot(a, b):
    return jnp.dot(a, b, preferred_element_type=F32)


def _dot_nt(a, b):
    return lax.dot_general(a, b, (((1,), (1,)), ((), ())), preferred_element_type=F32)


def _rms(x, w):
    return x * lax.rsqrt(jnp.mean(x * x, axis=-1, keepdims=True) + EPS) * w


def _rope(z, cos_t, sin_a, sin_b):
    outs = []
    for s in range(z.shape[1] // LANES):
        x = z[:, s * LANES:(s + 1) * LANES]
        outs.append(x * cos_t + pltpu.roll(x, LANES - ROT_HALF, 1) * sin_a + pltpu.roll(x, ROT_HALF, 1) * sin_b)
    return outs[0] if len(outs) == 1 else jnp.concatenate(outs, axis=-1)


def _rope_tables(pos):
    t = pos.shape[0]
    inv = jnp.exp(-math.log(ROPE_THETA) * jnp.arange(ROT_HALF, dtype=F32) * (2.0 / ROT_DIM))
    ang = pos.astype(F32)[:, None] * inv[None, :]
    cos, sin = jnp.cos(ang), jnp.sin(ang)
    rest = HEAD_DIM - ROT_DIM
    cos_t = jnp.concatenate([cos, cos, jnp.ones((t, rest), F32)], axis=-1)
    sin_a = jnp.concatenate([-sin, jnp.zeros((t, HEAD_DIM - ROT_HALF), F32)], axis=-1)
    sin_b = jnp.concatenate([jnp.zeros((t, ROT_HALF), F32), sin, jnp.zeros((t, rest), F32)], axis=-1)
    rep = LANES // HEAD_DIM
    return tuple(jnp.tile(a, (1, rep)) for a in (cos_t, sin_a, sin_b))


def _params(n_axes=1):
    return pltpu.CompilerParams(dimension_semantics=("arbitrary",) * n_axes, vmem_limit_bytes=VMEM_LIMIT)


def _full(shape):
    nd = len(shape)
    return pl.BlockSpec(shape, lambda *_: (0,) * nd, pipeline_mode=pl.Buffered(1))


def _layer(shape, index):
    nd = len(shape)
    return pl.BlockSpec((1,) + tuple(shape), lambda *_: (index,) + (0,) * nd, pipeline_mode=pl.Buffered(1))


def _rows(tm, cols):
    return pl.BlockSpec((tm, cols), lambda i: (i, 0))


def _even_in_kernel(x_ref, nw_ref, w_ref, cos_ref, sa_ref, sb_ref, u_ref, q_ref, k_ref, v_ref, *tails,
                    tm, prompt):
    h = _rms(x_ref[...], nw_ref[...]).astype(BF16)
    cos_t, sin_a, sin_b = cos_ref[...], sa_ref[...], sb_ref[...]
    q = _dot(h, w_ref[0, :, D_A:D_A + D_B])
    k = _dot(h, w_ref[0, :, D_A + D_B:D_A + D_B + D_KV])
    v = _dot(h, w_ref[0, :, D_A + D_B + D_KV:D_A + D_B + 2 * D_KV])
    q_ref[...] = (_rope(q, cos_t, sin_a, sin_b) * (HEAD_DIM ** -0.5 * LOG2E)).astype(BF16)
    k = _rope(k, cos_t, sin_a, sin_b)
    if prompt:
        k_ref[...] = k.T.astype(BF16)
        v_ref[...] = v.astype(BF16)
        tails[0][0] = k[tm - WINDOW:]
        tails[1][0] = v[tm - WINDOW:]
    else:
        k_ref[...] = k
        v_ref[...] = v
    u_ref[...] = _dot(h, w_ref[0, :, 0:D_A])


def _even_in(x, nw, w, p, tabs, tm, seq, prompt):
    r = x.shape[0]
    t_tiles = tabs[0].shape[0] // tm
    tab_spec = pl.BlockSpec((tm, LANES), lambda i: (i % t_tiles, 0))
    n_in = D_A + D_B + 2 * D_KV
    sds = jax.ShapeDtypeStruct
    if prompt:
        tps = seq // tm
        tail_spec = pl.BlockSpec((1, WINDOW, D_KV), lambda i: (i // tps, 0, 0))
        kv_specs = [pl.BlockSpec((D_KV, tm), lambda i: (0, i)), _rows(tm, D_KV), tail_spec, tail_spec]
        kv_shapes = [sds((D_KV, r), BF16), sds((r, D_KV), BF16),
                     sds((r // seq, WINDOW, D_KV), F32), sds((r // seq, WINDOW, D_KV), F32)]
    else:
        kv_specs = [_rows(tm, D_KV), _rows(tm, D_KV)]
        kv_shapes = [sds((r, D_KV), F32), sds((r, D_KV), F32)]
    return pl.pallas_call(
        functools.partial(_even_in_kernel, tm=tm, prompt=prompt),
        grid=(r // tm,),
        in_specs=[_rows(tm, D_MODEL), _full((1, D_MODEL)), _layer((D_MODEL, n_in), p), tab_spec, tab_spec,
                  tab_spec],
        out_specs=[_rows(tm, D_A), _rows(tm, D_B)] + kv_specs,
        out_shape=[sds((r, D_A), F32), sds((r, D_B), BF16)] + kv_shapes,
        compiler_params=_params(),
        name="even_in",
    )(x, nw, w, *tabs)


def _pool_mix(ext, u, pos, wp_ref, ps, ycat_ref, row0, rows):
    for g, w in enumerate(POOL_WINDOWS):
        lo, hi = g * POOL_GROUP, (g + 1) * POOL_GROUP
        s = ext[:, lo:hi]
        sh = 1
        while sh < w:
            s = s + pltpu.roll(s, sh, 0)
            sh *= 2
        count = jnp.minimum(w, pos + 1).astype(F32)
        diff = s[POOL_HALO:] / count - u[:, lo:hi]
        ya = _dot(diff.astype(BF16), wp_ref[0, g]) * ps[:, lo:hi]
        ycat_ref[row0:row0 + rows, lo:hi] = ya.astype(BF16)


def _sink_column(sink_ref, hk, rows_per_head):
    n = GQA_B * rows_per_head
    grp = lax.broadcasted_iota(jnp.int32, (n, 1), 0) // rows_per_head
    col = jnp.full((n, 1), sink_ref[hk * GQA_B] * LOG2E, F32)
    for g in range(1, GQA_B):
        col = jnp.where(grp == g, sink_ref[hk * GQA_B + g] * LOG2E, col)
    return col


def _sink_attend(s, sink_col, vb):
    m = jnp.maximum(jnp.max(s, axis=-1, keepdims=True), sink_col)
    e = jnp.exp2(s - m)
    den = jnp.sum(e, axis=-1, keepdims=True) + jnp.exp2(sink_col - m)
    return _dot(e.astype(BF16), vb) / den


def _store_heads(ycat_ref, o, row0, rows, hk):
    for pair in range(GQA_B // 2):
        both = jnp.concatenate([o[(2 * pair) * rows:(2 * pair + 1) * rows],
                                o[(2 * pair + 1) * rows:(2 * pair + 2) * rows]], axis=-1)
        c0 = D_A + (hk * GQA_B + 2 * pair) * HEAD_DIM
        ycat_ref[row0:row0 + rows, c0:c0 + 2 * HEAD_DIM] = both.astype(BF16)


def _even_mix_prompt_kernel(sink_ref, u_ref, q_ref, k_ref, v_ref, x_ref, wp_ref, ps_ref, wo_ref, o_ref,
                            ucar, kcar, vcar, ycat, *, tm, tiles_per_seq):
    i = pl.program_id(0)
    ti = i % tiles_per_seq
    first_key_chunk = jnp.where(ti == 0, WINDOW // CHUNK, 0)

    @pl.when(ti == 0)
    def _():
        ucar[...] = jnp.zeros_like(ucar)
        kcar[...] = jnp.zeros_like(kcar)
        vcar[...] = jnp.zeros_like(vcar)

    u = u_ref[...]
    ext = jnp.concatenate([ucar[...], u], axis=0)
    pos = ti * tm + lax.broadcasted_iota(jnp.int32, (tm, POOL_GROUP), 0)
    _pool_mix(ext, u, pos, wp_ref, ps_ref[...], ycat, 0, tm)
    ucar[...] = u[tm - POOL_HALO:]

    kext = jnp.concatenate([kcar[...], k_ref[...]], axis=1)
    vext = jnp.concatenate([vcar[...], v_ref[...]], axis=0)
    kcar[...] = kext[:, tm:]
    vcar[...] = vext[tm:]
    qb = 2 * CHUNK
    kb = qb + WINDOW
    nrow = GQA_B * qb
    qc = (lax.broadcasted_iota(jnp.int32, (nrow, kb), 0) % qb) // CHUNK
    kc = lax.broadcasted_iota(jnp.int32, (nrow, kb), 1) // CHUNK
    band = (kc >= qc) & (kc <= qc + WINDOW // CHUNK)
    first = band & (kc >= first_key_chunk)
    grp = lax.broadcasted_iota(jnp.int32, (nrow, LANES), 0) // qb
    low_lanes = lax.broadcasted_iota(jnp.int32, (qb, LANES), 1) < HEAD_DIM
    ones = jnp.ones((WINDOW + tm, HEAD_DIM), BF16)
    zeros = jnp.zeros((HEAD_DIM, WINDOW + tm), BF16)
    order = [g for g in range(GQA_B) if g % 2 == 0] + [g for g in range(GQA_B) if g % 2 == 1]
    v_aug, sink_rep, k_even, k_odd = [], [], [], []
    for hk in range(HKV_B):
        v_aug.append(jnp.concatenate([vext[:, hk * HEAD_DIM:(hk + 1) * HEAD_DIM], ones], axis=-1))
        kt = kext[hk * HEAD_DIM:(hk + 1) * HEAD_DIM]
        k_even.append(jnp.concatenate([kt, zeros], axis=0))
        k_odd.append(jnp.concatenate([zeros, kt], axis=0))
        rep = jnp.full((nrow, LANES), sink_ref[hk * GQA_B + order[0]] * LOG2E, F32)
        for r in range(1, GQA_B):
            rep = jnp.where(grp == r, sink_ref[hk * GQA_B + order[r]] * LOG2E, rep)
        sink_rep.append(rep)

    def scores(a, hk):
        def tiles(gs):
            return jnp.concatenate(
                [q_ref[a * qb:(a + 1) * qb, (hk * GQA_B + g) // 2 * LANES:((hk * GQA_B + g) // 2 + 1) * LANES]
                 for g in gs], axis=0)
        half = GQA_B // 2
        return jnp.concatenate([_dot(tiles(order[:half]), k_even[hk][:, a * qb:a * qb + kb]),
                                _dot(tiles(order[half:]), k_odd[hk][:, a * qb:a * qb + kb])], axis=0)

    def softmax(a, hk, s):
        s = jnp.where(first if a == 0 else band, s, NEG_INF)
        m = jnp.maximum(jnp.max(s, axis=-1, keepdims=True), sink_rep[hk])
        return jnp.exp2(s - _lane_tile(m, kb // LANES)).astype(BF16), jnp.exp2(sink_rep[hk] - m)

    def finish(a, hk, e, sink_term):
        oa = _dot(e, v_aug[hk][a * qb:a * qb + kb])
        inv = 1.0 / (oa + sink_term)
        for pair in range(GQA_B // 2):
            lo = slice(pair * qb, (pair + 1) * qb)
            hi = slice((GQA_B // 2 + pair) * qb, (GQA_B // 2 + pair + 1) * qb)
            both = jnp.where(low_lanes, oa[lo] * pltpu.roll(inv[lo], HEAD_DIM, 1),
                             pltpu.roll(oa[hi], HEAD_DIM, 1) * inv[hi])
            c0 = D_A + (hk * GQA_B + 2 * pair) * HEAD_DIM
            ycat[a * qb:(a + 1) * qb, c0:c0 + 2 * HEAD_DIM] = both.astype(BF16)

    blocks = [(a, hk) for a in range(tm // qb) for hk in range(HKV_B)]
    nb = len(blocks)
    s_q = {0: scores(*blocks[0])}
    e_q = {}
    for n in range(nb + 2):
        if n + 1 < nb:
            s_q[n + 1] = scores(*blocks[n + 1])
        if n < nb:
            e_q[n] = softmax(*blocks[n], s_q.pop(n))
        if 1 <= n <= nb:
            finish(*blocks[n - 1], *e_q.pop(n - 1))
    o_ref[...] = x_ref[...] + _dot(ycat[...], wo_ref[0])


def _even_mix_prompt(sinks, u, q, kt, v, x, wp, ps, wo, p, seq, tm):
    r = x.shape[0]
    kern = functools.partial(_even_mix_prompt_kernel, tm=tm, tiles_per_seq=seq // tm)
    return pl.pallas_call(
        kern,
        grid=(r // tm,),
        in_specs=[pl.BlockSpec(memory_space=pltpu.SMEM),
                  _rows(tm, D_A), _rows(tm, D_B), pl.BlockSpec((D_KV, tm), lambda i: (0, i)), _rows(tm, D_KV),
                  _rows(tm, D_MODEL), _layer(wp.shape[1:], p), _full((1, D_A)), _layer((D_A + D_B, D_MODEL), p)],
        out_specs=_rows(tm, D_MODEL),
        out_shape=jax.ShapeDtypeStruct((r, D_MODEL), F32),
        scratch_shapes=[pltpu.VMEM((POOL_HALO, D_A), F32), pltpu.VMEM((D_KV, WINDOW), BF16),
                        pltpu.VMEM((WINDOW, D_KV), BF16), pltpu.VMEM((tm, D_A + D_B), BF16)],
        compiler_params=_params(),
        name="even_mix_prompt",
    )(sinks, u, q, kt, v, x, wp, ps, wo)


def _even_mix_sample_kernel(sink_ref, u_ref, q_ref, k_ref, v_ref, x_ref, pp_ref, kc_ref, vc_ref, wp_ref, ps_ref,
                            wo_ref, o_ref, ycat, *, l, pos0):
    u = u_ref[...]
    ext = jnp.concatenate([pp_ref[0], u], axis=0)
    pos = pos0 + lax.broadcasted_iota(jnp.int32, (l, 1), 0)
    _pool_mix(ext, u, pos, wp_ref, ps_ref[...], ycat, 0, l)

    pad = jnp.zeros((LANES - l, D_KV), BF16)
    kall = jnp.concatenate([kc_ref[0].astype(BF16), k_ref[...].astype(BF16), pad], axis=0)
    vall = jnp.concatenate([vc_ref[0].astype(BF16), v_ref[...].astype(BF16), pad], axis=0)
    nk = kall.shape[0]
    valid = lax.broadcasted_iota(jnp.int32, (GQA_B * l, nk), 1) < nk - (LANES - l)
    for hk in range(HKV_B):
        ks = kall[:, hk * HEAD_DIM:(hk + 1) * HEAD_DIM]
        vs = vall[:, hk * HEAD_DIM:(hk + 1) * HEAD_DIM]
        qs = jnp.concatenate(
            [q_ref[:, (hk * GQA_B + g) * HEAD_DIM:(hk * GQA_B + g + 1) * HEAD_DIM] for g in range(GQA_B)], axis=0)
        s = jnp.where(valid, _dot_nt(qs, ks), NEG_INF)
        o = _sink_attend(s, _sink_column(sink_ref, hk, l), vs)
        _store_heads(ycat, o, 0, l, hk)
    o_ref[...] = x_ref[...] + _dot(ycat[...], wo_ref[0])


def _even_mix_sample(sinks, u, q, k, v, x, pool_prev, kc, vc, wp, ps, wo, p, l, pos0):
    r = x.shape[0]
    rows_c = kc.shape[1]
    kern = functools.partial(_even_mix_sample_kernel, l=l, pos0=pos0)
    cache_spec = pl.BlockSpec((1, rows_c, D_KV), lambda i: (i, 0, 0))
    return pl.pallas_call(
        kern,
        grid=(r // l,),
        in_specs=[pl.BlockSpec(memory_space=pltpu.SMEM),
                  _rows(l, D_A), _rows(l, D_B), _rows(l, D_KV), _rows(l, D_KV), _rows(l, D_MODEL),
                  pl.BlockSpec((1, POOL_HALO, D_A), lambda i: (i, 0, 0)), cache_spec, cache_spec,
                  _layer(wp.shape[1:], p), _full((1, D_A)), _layer((D_A + D_B, D_MODEL), p)],
        out_specs=_rows(l, D_MODEL),
        out_shape=jax.ShapeDtypeStruct((r, D_MODEL), F32),
        scratch_shapes=[pltpu.VMEM((l, D_A + D_B), BF16)],
        compiler_params=_params(),
        name="even_mix_sample",
    )(sinks, u, q, k, v, x, pool_prev, kc, vc, wp, ps, wo)


ODD_CR0 = 2 * D_CQK + 2 * D_C
ODD_DQ0 = ODD_CR0 + LANES
ODD_IN_PADDED = ODD_DQ0 + 3 * D_D


def _odd_in_kernel(x_ref, nw_ref, w_ref, wg_ref, bg_ref, cos_ref, sa_ref, sb_ref, *rest, tm, blk, cache_layout,
                   aliased):
    if aliased:
        rest = rest[2:]
    qd_ref, kd_ref, el_ref, cv_ref, cg_ref, dq_ref, dk_ref, dv_ref, cum_scr = rest
    h = _rms(x_ref[...], nw_ref[...]).astype(BF16)
    cos_t, sin_a, sin_b = cos_ref[...], sa_ref[...], sb_ref[...]
    cr = _dot(h, w_ref[0, :, ODD_CR0:ODD_DQ0]).astype(BF16)
    gate = _dot(cr, wg_ref[0]) + bg_ref[...]
    cq = _dot(h, w_ref[0, :, 0:D_CQK])
    ck = _dot(h, w_ref[0, :, D_CQK:2 * D_CQK])
    dq = _dot(h, w_ref[0, :, ODD_DQ0:ODD_DQ0 + D_D])
    cum = (jnp.minimum(gate, 0.0) - jnp.log1p(jnp.exp(-jnp.abs(gate)))) * (1.0 / GATE_NORM)
    rowb = lax.broadcasted_iota(jnp.int32, (tm, 1), 0) % blk
    sh = 1
    while sh < blk:
        cum = cum + jnp.where(rowb >= sh, pltpu.roll(cum, sh, 0), 0.0)
        sh *= 2
    nb = tm // blk
    for s in range(D_CQK // LANES):
        cum_scr[s] = cum[:, s * LANES:(s + 1) * LANES]
        last = cum_scr[s, pl.ds(blk - 1, nb, stride=blk), :]
        last = jnp.concatenate([last, jnp.zeros((LANES - nb, LANES), F32)], axis=0)
        el_ref[0, s * LANES:(s + 1) * LANES, :] = jnp.exp(last.T)
    qd_ref[...] = (cq * (DK_C ** -0.5) * jnp.exp(cum)).astype(BF16)
    kd_ref[...] = (ck * jnp.exp(-cum)).T.astype(BF16)
    dk = _dot(h, w_ref[0, :, ODD_DQ0 + D_D:ODD_DQ0 + 2 * D_D])
    dq_ref[...] = (_rope(dq, cos_t, sin_a, sin_b) * (HD_D ** -0.5 * LOG2E)).astype(BF16)
    dv = _dot(h, w_ref[0, :, ODD_DQ0 + 2 * D_D:ODD_DQ0 + 3 * D_D])
    dk = _rope(dk, cos_t, sin_a, sin_b)
    cv_ref[...] = _dot(h, w_ref[0, :, 2 * D_CQK:2 * D_CQK + D_C]).astype(BF16)
    cg_ref[...] = _dot(h, w_ref[0, :, 2 * D_CQK + D_C:ODD_CR0])
    if cache_layout:
        for hd in range(H_D):
            lo, hi = hd * 2 * HD_D, (hd + 1) * 2 * HD_D
            dk_ref[0, 0, hd] = dk[:, lo:hi].T
            dv_ref[0, pl.ds(hd, tm, stride=H_D), :] = dv[:, lo:hi]
    else:
        dk_ref[...] = dk
        dv_ref[...] = dv


def _odd_in(x, nw, w, wg, wi, bg, tabs, tm, blk, pair=None):
    r = x.shape[0]
    t_tiles = tabs[0].shape[0] // tm
    tab_spec = pl.BlockSpec((tm, LANES), lambda i: (i % t_tiles, 0))
    nb = tm // blk
    sds = jax.ShapeDtypeStruct
    in_specs = [_rows(tm, D_MODEL), _full((1, D_MODEL)), _layer((D_MODEL, ODD_IN_PADDED), wi),
                _layer((LANES, D_CQK), wi), _full((1, D_CQK)), tab_spec, tab_spec, tab_spec]
    args = [x, nw, w, wg, bg, *tabs]
    aliases = {}
    if pair is None:
        kv_specs = [_rows(tm, D_D), _rows(tm, D_D)]
        kv_shapes = [sds((r, D_D), F32), sds((r, D_D), F32)]
    else:
        p, n_pairs, batch, seq, bufs = pair
        tps = seq // tm
        kv_specs = [pl.BlockSpec((1, 1, H_D, 2 * HD_D, tm), lambda i: (p, i // tps, 0, 0, i % tps)),
                    pl.BlockSpec((1, tm * H_D, 2 * HD_D), lambda i: (p, i, 0))]
        kv_shapes = [sds((n_pairs, batch, H_D, 2 * HD_D, seq), F32), sds((n_pairs, r * H_D, 2 * HD_D), F32)]
        if bufs is not None:
            in_specs += [pl.BlockSpec(memory_space=pl.ANY)] * 2
            args += list(bufs)
            aliases = {len(args) - 2: 6, len(args) - 1: 7}
    kern = functools.partial(_odd_in_kernel, tm=tm, blk=blk, cache_layout=pair is not None, aliased=bool(aliases))
    return pl.pallas_call(
        kern,
        grid=(r // tm,),
        in_specs=in_specs,
        out_specs=[_rows(tm, D_CQK), pl.BlockSpec((D_CQK, tm), lambda i: (0, i)),
                   pl.BlockSpec((1, D_CQK, LANES), lambda i: (i, 0, 0)), _rows(tm, D_C), _rows(tm, D_C),
                   _rows(tm, D_D)] + kv_specs,
        out_shape=[sds((r, D_CQK), BF16), sds((D_CQK, r), BF16), sds((r // tm, D_CQK, LANES), F32),
                   sds((r, D_C), BF16), sds((r, D_C), F32), sds((r, D_D), BF16)] + kv_shapes,
        scratch_shapes=[pltpu.VMEM((D_CQK // LANES, tm, LANES), F32)],
        input_output_aliases=aliases,
        compiler_params=_params(),
        name="odd_in",
    )(*args)


def _lambda_value(lp_ref, lambda_init):
    lp = lp_ref[...]
    s1 = jnp.sum(lp[0:1] * lp[1:2], axis=-1, keepdims=True)
    s2 = jnp.sum(lp[2:3] * lp[3:4], axis=-1, keepdims=True)
    return jnp.exp(s1) - jnp.exp(s2) + lambda_init


def _diff_finish(acc, den, lam, nw, lambda_init):
    o = acc[0] / den[0] - lam * (acc[1] / den[1])
    return (_rms(o, nw) * (1.0 - lambda_init)).astype(BF16)


def _lane_tile(x, n):
    return x if n == 1 else jnp.concatenate([x] * n, axis=1)


def _diff_prompt_kernel(lp_ref, nw_ref, q_ref, k_ref, v_ref, o_ref, kbf, vbf, m_scr, acc_scr, s_a, s_b, bias,
                        *, tq, lambda_init):
    hd = pl.program_id(1)
    dv = 2 * HD_D
    seq = vbf.shape[0]
    tk = tq // 2

    kbf[...] = k_ref[0, 0, 0].astype(BF16)
    vbf[:, :dv] = v_ref[0, pl.ds(hd, seq, stride=H_D), :].astype(BF16)
    vbf[:, dv:] = jnp.ones((seq, dv), BF16)

    @pl.when((pl.program_id(0) == 0) & (hd == 0))
    def _():
        qc = lax.broadcasted_iota(jnp.int32, (tk, tk), 0) // CHUNK
        kc = lax.broadcasted_iota(jnp.int32, (tk, tk), 1) // CHUNK
        bias[...] = jnp.where(kc <= qc, 0.0, NEG_INF)

    lam = _lambda_value(lp_ref, lambda_init)

    def scores(qi, t, s_buf, row0=0):
        start = pl.multiple_of(t * tk, tk)
        rows = pl.ds(pl.multiple_of(qi * tq + row0, tk), tq - row0)
        for c in range(2):
            s_buf[c, row0:, :] = _dot(q_ref[rows, c * HD_D:(c + 1) * HD_D],
                                      kbf[c * HD_D:(c + 1) * HD_D, pl.ds(start, tk)])

    def attend(t, s_buf, row0=0, rows=tq, masked=False):
        vs = vbf[pl.ds(pl.multiple_of(t * tk, tk), tk), :]
        for c in range(2):
            sc = s_buf[c, row0:row0 + rows, :]
            if masked:
                sc = sc + bias[...]
            m_old = m_scr[c, row0:row0 + rows, :]
            m_new = jnp.maximum(m_old, jnp.max(sc, axis=-1, keepdims=True))
            p = jnp.exp2(sc - _lane_tile(m_new, tk // LANES)).astype(BF16)
            alpha = _lane_tile(jnp.exp2(m_old - m_new), 2 * dv // LANES)
            acc_scr[c, row0:row0 + rows, :] = alpha * acc_scr[c, row0:row0 + rows, :] + _dot(p, vs)
            m_scr[c, row0:row0 + rows, :] = m_new

    def query_tile(qi, _):
        m_scr[...] = jnp.full(m_scr.shape, NEG_INF, F32)
        acc_scr[...] = jnp.zeros(acc_scr.shape, F32)
        scores(qi, 0, s_a)

        def pair(j, _):
            t = 2 * j
            scores(qi, t + 1, s_b)
            attend(t, s_a)
            scores(qi, t + 2, s_a)
            attend(t + 1, s_b)
            return 0

        lax.fori_loop(0, qi, pair, 0)
        scores(qi, 2 * qi + 1, s_b, row0=tk)
        attend(2 * qi, s_a, row0=0, rows=tk, masked=True)
        attend(2 * qi, s_a, row0=tk, rows=tk)
        attend(2 * qi + 1, s_b, row0=tk, rows=tk, masked=True)
        acc = [acc_scr[c] for c in range(2)]
        o_ref[pl.ds(pl.multiple_of(qi * tq, tq), tq), :] = _diff_finish(
            [a[:, :dv] for a in acc], [a[:, dv:] for a in acc], lam, nw_ref[...], lambda_init)
        return 0

    lax.fori_loop(0, seq // tq, query_tile, 0)


def _diff_prompt(lp, nw, q, kt, v, p, batch, seq, lambda_init, tq):
    r = q.shape[0]
    kern = functools.partial(_diff_prompt_kernel, tq=tq, lambda_init=lambda_init)
    k_spec = pl.BlockSpec((1, 1, 1, 2 * HD_D, seq), lambda b, h: (p, b, h, 0, 0))
    v_spec = pl.BlockSpec((1, seq * H_D, 2 * HD_D), lambda b, h: (p, b, 0))
    q_spec = pl.BlockSpec((seq, 2 * HD_D), lambda b, h: (b, h))
    return pl.pallas_call(
        kern,
        grid=(batch, H_D),
        in_specs=[_full((4, HD_D)), _full((1, 2 * HD_D)), q_spec, k_spec, v_spec],
        out_specs=q_spec,
        out_shape=jax.ShapeDtypeStruct((r, D_D), BF16),
        scratch_shapes=[pltpu.VMEM((2 * HD_D, seq), BF16), pltpu.VMEM((seq, 4 * HD_D), BF16),
                        pltpu.VMEM((2, tq, LANES), F32), pltpu.VMEM((2, tq, 4 * HD_D), F32),
                        pltpu.VMEM((2, tq, tq // 2), F32), pltpu.VMEM((2, tq, tq // 2), F32),
                        pltpu.VMEM((tq // 2, tq // 2), F32)],
        compiler_params=_params(2),
        name="diff_prompt",
    )(lp, nw, q, kt, v)


def _diff_sample_kernel(lp_ref, nw_ref, q_ref, k_ref, v_ref, kc_ref, vc_ref, o_ref, *, l, past, lambda_init):
    pad = jnp.zeros((LANES - l, 2 * HD_D), BF16)
    valid = lax.broadcasted_iota(jnp.int32, (l, LANES), 1) < l
    lam = _lambda_value(lp_ref, lambda_init)
    for hd in range(H_D):
        lo, hi = hd * 2 * HD_D, (hd + 1) * 2 * HD_D
        knew = jnp.concatenate([k_ref[:, lo:hi].astype(BF16), pad], axis=0)
        vnew = jnp.concatenate([v_ref[:, lo:hi].astype(BF16), pad], axis=0)
        kct = kc_ref[0, 0, hd].astype(BF16)
        vcache = vc_ref[0, pl.ds(hd, past, stride=H_D), :].astype(BF16)
        acc, den = [], []
        for c in range(2):
            qc = q_ref[:, lo + c * HD_D:lo + (c + 1) * HD_D]
            s_old = _dot(qc, kct[c * HD_D:(c + 1) * HD_D, :])
            s_new = jnp.where(valid, _dot_nt(qc, knew[:, c * HD_D:(c + 1) * HD_D]), NEG_INF)
            m = jnp.maximum(jnp.max(s_old, axis=-1, keepdims=True), jnp.max(s_new, axis=-1, keepdims=True))
            p_old = jnp.exp2(s_old - m)
            p_new = jnp.exp2(s_new - m)
            den.append(jnp.sum(p_old, axis=-1, keepdims=True) + jnp.sum(p_new, axis=-1, keepdims=True))
            acc.append(_dot(p_old.astype(BF16), vcache) + _dot(p_new.astype(BF16), vnew))
        o_ref[:, lo:hi] = _diff_finish(acc, den, lam, nw_ref[...], lambda_init)


def _diff_sample(lp, nw, q, k, v, kct, vc, p, batch, l, lambda_init):
    r = q.shape[0]
    past = kct.shape[-1]
    kern = functools.partial(_diff_sample_kernel, l=l, past=past, lambda_init=lambda_init)
    cache_k_spec = pl.BlockSpec((1, 1, H_D, 2 * HD_D, past), lambda b: (p, b, 0, 0, 0))
    cache_v_spec = pl.BlockSpec((1, past * H_D, 2 * HD_D), lambda b: (p, b, 0))
    return pl.pallas_call(
        kern,
        grid=(batch,),
        in_specs=[_full((4, HD_D)), _full((1, 2 * HD_D)), _rows(l, D_D), _rows(l, D_D), _rows(l, D_D),
                  cache_k_spec, cache_v_spec],
        out_specs=_rows(l, D_D),
        out_shape=jax.ShapeDtypeStruct((r, D_D), BF16),
        compiler_params=_params(),
        name="diff_sample",
    )(lp, nw, q, k, v, kct, vc)


def _odd_mix_kernel(qd_ref, kd_ref, el_ref, cv_ref, cg_ref, yd_ref, x_ref, s0_ref, gw_ref, wo_ref,
                    o_ref, s_ref, state, og, ycat, *, tm, blk, tiles_per_seq, per_block_state):
    i = pl.program_id(0)
    if not per_block_state:
        @pl.when(i % tiles_per_seq == 0)
        def _():
            state[...] = jnp.zeros_like(state)

    per_group = LANES // blk
    rows_of = lax.broadcasted_iota(jnp.int32, (LANES, LANES), 0)
    cols_of = lax.broadcasted_iota(jnp.int32, (LANES, LANES), 1)
    q_row = lax.broadcasted_iota(jnp.int32, (blk, LANES), 0)
    q_col = lax.broadcasted_iota(jnp.int32, (blk, LANES), 1)
    low = [(h % 2) * DK_C for h in range(H_C)]
    own_rows = [(rows_of >= low[h]) & (rows_of < low[h] + DK_C) for h in range(H_C)]
    lanes = [slice(h // 2 * LANES, (h // 2 + 1) * LANES) for h in range(H_C)]
    vcols = [slice(h * DV_C, (h + 1) * DV_C) for h in range(H_C)]
    st = [None if per_block_state else state[h] for h in range(H_C)]
    for n in range(tm // blk):
        g0, c = (n // per_group) * LANES, n % per_group
        r0 = n * blk
        causal = (q_col >= c * blk) & (q_col - c * blk <= q_row)
        own_cols = (cols_of >= c * blk) & (cols_of < (c + 1) * blk)
        for h in range(H_C):
            if per_block_state:
                st[h] = jnp.where(own_rows[h], jnp.concatenate([s0_ref[n, h]] * 2, axis=0), 0.0)
            qd = qd_ref[r0:r0 + blk, lanes[h]]
            kt = jnp.where(own_rows[h], kd_ref[lanes[h], g0:g0 + LANES], jnp.zeros((), BF16))
            vg = cv_ref[g0:g0 + LANES, vcols[h]]
            att = jnp.where(causal, _dot(qd, kt), 0.0).astype(BF16)
            og[r0:r0 + blk, vcols[h]] = _dot(att, vg) + _dot(qd, st[h].astype(BF16))
            k_blk = jnp.where(own_cols, kt, jnp.zeros((), BF16))
            decay = jnp.where(own_rows[h], jnp.broadcast_to(el_ref[0, lanes[h], n:n + 1], (LANES, LANES)), 0.0)
            st[h] = decay * (st[h] + _dot(k_blk, vg))
            if per_block_state:
                s_ref[n, h] = st[h][low[h]:low[h] + DK_C]
    if not per_block_state:
        for h in range(H_C):
            state[h] = st[h]
            s_ref[0, h] = st[h][low[h]:low[h] + DK_C]

    gw = gw_ref[...]
    for h in range(H_C):
        lo, hi = h * DV_C, (h + 1) * DV_C
        g = cg_ref[:, lo:hi]
        y = _rms(og[:, lo:hi], gw) * (g * (1.0 / (1.0 + jnp.exp(-g))))
        ycat[:, lo:hi] = y.astype(BF16)
    ycat[:, D_C:] = yd_ref[...]
    o_ref[...] = x_ref[...] + _dot(ycat[...], wo_ref[0])


def _odd_mix(qd, kd, el, cv, cg, yd, x, s0, gw, wo, p, tm, blk, seq, per_block_state):
    assert tm % LANES == 0 and LANES % blk == 0 and tm // blk <= LANES
    r = x.shape[0]
    nb = tm // blk
    tiles_per_seq = max(seq // tm, 1)
    kern = functools.partial(_odd_mix_kernel, tm=tm, blk=blk, tiles_per_seq=tiles_per_seq,
                             per_block_state=per_block_state)
    if per_block_state:
        n_state = r // blk
        state_spec = pl.BlockSpec((nb, H_C, DK_C, DV_C), lambda i: (i, 0, 0, 0))
    else:
        n_state = r // seq
        state_spec = pl.BlockSpec((1, H_C, DK_C, DV_C), lambda i: (i // tiles_per_seq, 0, 0, 0))
    return pl.pallas_call(
        kern,
        grid=(r // tm,),
        in_specs=[_rows(tm, D_CQK), pl.BlockSpec((D_CQK, tm), lambda i: (0, i)),
                  pl.BlockSpec((1, D_CQK, LANES), lambda i: (i, 0, 0)), _rows(tm, D_C), _rows(tm, D_C),
                  _rows(tm, D_D), _rows(tm, D_MODEL), state_spec, _full((1, DV_C)),
                  _layer((D_C + D_D, D_MODEL), p)],
        out_specs=[_rows(tm, D_MODEL), state_spec],
        out_shape=[jax.ShapeDtypeStruct((r, D_MODEL), F32),
                   jax.ShapeDtypeStruct((n_state, H_C, DK_C, DV_C), F32)],
        scratch_shapes=[pltpu.VMEM((H_C, LANES, DV_C), F32), pltpu.VMEM((tm, D_C), F32),
                        pltpu.VMEM((tm, D_C + D_D), BF16)],
        compiler_params=_params(),
        name="odd_mix",
    )(qd, kd, el, cv, cg, yd, x, s0, gw, wo)


def _gelu_tanh(x):
    return 0.5 * x * (1.0 + jnp.tanh(math.sqrt(2.0 / math.pi) * (x + 0.044715 * (x * x * x))))


def _zero_after(v):
    t = v[:, :LANES]
    for c in range(1, v.shape[1] // LANES):
        t = t + v[:, c * LANES:(c + 1) * LANES]
    t = jnp.sum(t, axis=0, keepdims=True)
    bits = lax.shift_right_logical(lax.shift_right_logical(pltpu.bitcast(t, jnp.uint32), jnp.uint32(16)),
                                   jnp.uint32(16))
    return pltpu.bitcast(bits, F32)


def _ffn_kernel(x_ref, nw_ref, wup_ref, cw_ref, cb_ref, wdn_ref, prev_ref, fnw_ref, *rest,
                tm, seg, tiles_per_seq, prev_from_input, final_norm, norm_ahead):
    if norm_ahead:
        xn_ref, o_ref, tail_ref, carry, act, h_scr = rest
    else:
        o_ref, tail_ref, carry, act = rest
    i = pl.program_id(0)
    x = x_ref[...]
    nseg = tm // seg
    row = lax.broadcasted_iota(jnp.int32, (seg, 1), 0)
    if not prev_from_input:
        @pl.when(i % tiles_per_seq == 0)
        def _():
            carry[...] = jnp.zeros_like(carry)

    def up(hh, c0):
        return (_dot(hh, wup_ref[0, :, c0:c0 + FFN_COLS]),
                _dot(hh, wup_ref[0, :, D_FF + c0:D_FF + c0 + FFN_COLS]))

    if norm_ahead:
        @pl.when(i == 0)
        def _():
            h_scr[...] = _rms(x, nw_ref[...]).astype(BF16)
        h = h_scr[...]
    else:
        h = _rms(x, nw_ref[...]).astype(BF16)
    nxt = up(h, 0)
    for c0 in range(0, D_FF, FFN_COLS):
        c1 = c0 + FFN_COLS
        gate, val = nxt
        if c1 < D_FF:
            nxt = up(h, c1)
        w0, w1, w2 = cw_ref[0:1, c0:c1], cw_ref[1:2, c0:c1], cw_ref[2:3, c0:c1]
        convs = []
        for s in range(nseg):
            g = gate[s * seg:(s + 1) * seg]
            if prev_from_input:
                p0, p1 = prev_ref[s, 0:1, c0:c1], prev_ref[s, 1:2, c0:c1]
            else:
                p0 = carry[CONV_TAIL - 2:CONV_TAIL - 1, c0:c1]
                p1 = carry[CONV_TAIL - 1:CONV_TAIL, c0:c1]
            g1 = jnp.where(row == 0, p1, pltpu.roll(g, 1, 0))
            g2 = jnp.where(row == 0, p0, jnp.where(row == 1, p1, pltpu.roll(g, 2, 0)))
            convs.append(cb_ref[:, c0:c1] + g2 * w0 + g1 * w1 + g * w2)
            tail_ref[s, :, c0:c1] = g[seg - CONV_TAIL:]
        conv = convs[0] if nseg == 1 else jnp.concatenate(convs, axis=0)
        if not prev_from_input:
            carry[:, c0:c1] = gate[tm - CONV_TAIL:]
        act[:, c0:c1] = (_gelu_tanh(conv) * val).astype(BF16)
    zero = 0.0
    if norm_ahead:
        h_next = _rms(xn_ref[...], nw_ref[...])
        zero = _zero_after(h_next)
    if final_norm:
        y = x + _dot(act[...], wdn_ref[0])
        if norm_ahead:
            y = y + _lane_tile(zero, D_MODEL // LANES)
        o_ref[...] = _rms(y, fnw_ref[...])
    else:
        half = D_MODEL // 2
        for c0 in (0, half):
            y = x[:, c0:c0 + half] + _dot(act[...], wdn_ref[0, :, c0:c0 + half])
            if norm_ahead and c0 == 0:
                y = y + _lane_tile(zero, half // LANES)
            o_ref[:, c0:c0 + half] = y
    if norm_ahead:
        h_scr[...] = h_next.astype(BF16)


def _ffn(x, nw, wup, cw, cb, wdn, prev, fnw, layer, tm, seg, seq, prev_from_input, final_norm):
    r = x.shape[0]
    nseg = tm // seg
    n_tiles = r // tm
    norm_ahead = n_tiles > 1
    kern = functools.partial(_ffn_kernel, tm=tm, seg=seg, tiles_per_seq=max(seq // tm, 1),
                             prev_from_input=prev_from_input, final_norm=final_norm, norm_ahead=norm_ahead)
    prev_spec = pl.BlockSpec((nseg, CONV_W - 1, D_FF), lambda i: (i, 0, 0))
    tail_spec = pl.BlockSpec((nseg, CONV_TAIL, D_FF), lambda i: (i, 0, 0))
    wup_spec = pl.BlockSpec((1, D_MODEL, 2 * D_FF), lambda i: (layer, 0, 0), pipeline_mode=pl.Buffered(1))
    wdn_spec = pl.BlockSpec((1, D_FF, D_MODEL), lambda i: (layer, 0, 0), pipeline_mode=pl.Buffered(1))
    in_specs = [_rows(tm, D_MODEL), _full((1, D_MODEL)), wup_spec, _full((CONV_W, D_FF)),
                _full((1, D_FF)), wdn_spec, prev_spec, _full((1, D_MODEL))]
    args = [x, nw, wup, cw, cb, wdn, prev, fnw]
    scratch = [pltpu.VMEM((CONV_TAIL, D_FF), F32), pltpu.VMEM((tm, D_FF), BF16)]
    if norm_ahead:
        in_specs.append(pl.BlockSpec((tm, D_MODEL), lambda i: (jnp.minimum(i + 1, n_tiles - 1), 0)))
        args.append(x)
        scratch.append(pltpu.VMEM((tm, D_MODEL), BF16))
    return pl.pallas_call(
        kern,
        grid=(n_tiles,),
        in_specs=in_specs,
        out_specs=[_rows(tm, D_MODEL), tail_spec],
        out_shape=[jax.ShapeDtypeStruct((r, D_MODEL), F32),
                   jax.ShapeDtypeStruct((r // seg, CONV_TAIL, D_FF), F32)],
        scratch_shapes=scratch,
        compiler_params=_params(),
        name="ffn",
    )(*args)


def _pad_odd_w_in(w):
    cr = w[..., ODD_CR0:ODD_CR0 + GATE_RANK]
    pad = jnp.zeros(w.shape[:-1] + (LANES - GATE_RANK,), w.dtype)
    return jnp.concatenate([w[..., :ODD_CR0], cr, pad, w[..., ODD_CR0 + GATE_RANK:]], axis=-1).astype(BF16)


def _trunk(x3, pos, prompt, caches, weights):
    (norm_mix_w, norm_ffn_w, final_norm_w, even_w_in, pool_w, pool_scale, swa_sinks, even_w_out, odd_w_in,
     gla_w_gate2, gla_b_gate, gla_norm_w, diff_lambda, diff_norm_w, odd_w_out, ffn_w_up, ffn_conv_w, ffn_conv_b,
     ffn_w_down) = weights
    cache_pool, cache_swa_k, cache_swa_v, state_gla, cache_diff_k, cache_diff_v, cache_ffn_conv = caches
    b, l, _ = x3.shape
    r = b * l
    depth = norm_mix_w.shape[0]
    x = x3.reshape(r, D_MODEL)
    if prompt:
        tm = min(ROW_TILE, l)
        blk = min(CHUNK, l)
        tabs = _rope_tables(pos)
    else:
        tm = r
        blk = min(CHUNK, l)
        tabs = tuple(jnp.tile(t, (b, 1)) for t in _rope_tables(pos))
    pos0 = 0 if prompt else cache_diff_k.shape[2]

    pools, swa_ks, swa_vs, glas, diff_ks, diff_vs, ffns = [], [], [], [], [], [], []
    kv_p = None
    for layer in range(depth):
        p = layer // 2
        nw = norm_mix_w[layer].reshape(1, D_MODEL)
        if layer % 2 == 0:
            u, q, k, v, *tails = _even_in(x, nw, even_w_in.astype(BF16), p, tabs, tm, l, prompt)
            wp = pool_w.astype(BF16)
            ps = pool_scale[p].reshape(1, D_A)
            wo = even_w_out.astype(BF16)
            if prompt:
                x = _even_mix_prompt(swa_sinks[p], u, q, k, v, x, wp, ps, wo, p, l, tm)
                pools.append(u.reshape(b, l, D_A)[:, l - POOL_STATE:])
                swa_ks.append(tails[0].reshape(b, WINDOW, HKV_B, HEAD_DIM))
                swa_vs.append(tails[1].reshape(b, WINDOW, HKV_B, HEAD_DIM))
            else:
                rows_c = cache_swa_k.shape[2]
                pool_prev = jnp.pad(cache_pool[p], ((0, 0), (1, 0), (0, 0)))
                kc = cache_swa_k[p].reshape(b, rows_c, D_KV)
                vc = cache_swa_v[p].reshape(b, rows_c, D_KV)
                x = _even_mix_sample(swa_sinks[p], u, q, k, v, x, pool_prev, kc, vc, wp, ps, wo, p, l, pos0)
                pools.append(jnp.concatenate([cache_pool[p], u.reshape(b, l, D_A)], axis=1)[:, -POOL_STATE:])
                swa_ks.append(jnp.concatenate([cache_swa_k[p], k.reshape(b, l, HKV_B, HEAD_DIM)], axis=1)[:, -rows_c:])
                swa_vs.append(jnp.concatenate([cache_swa_v[p], v.reshape(b, l, HKV_B, HEAD_DIM)], axis=1)[:, -rows_c:])
        else:
            lambda_init = 0.8 - 0.6 * math.exp(-0.3 * layer)
            wg = jnp.pad(gla_w_gate2, ((0, 0), (0, LANES - GATE_RANK), (0, 0))).astype(BF16)
            bg = gla_b_gate[p].reshape(1, D_CQK)
            lp = diff_lambda[p]
            dnw = diff_norm_w[p].reshape(1, 2 * HD_D)
            w_in = _pad_odd_w_in(odd_w_in)
            if prompt:
                qd, kd, el, cv, cg, dq, *kv_p = _odd_in(x, nw, w_in, wg, p, bg, tabs, tm, blk,
                                                        pair=(p, depth // 2, b, l, kv_p))
                yd = _diff_prompt(lp, dnw, dq, kv_p[0], kv_p[1], p, b, l, lambda_init, min(ATTN_TILE, l))
                s0 = jnp.zeros((b, H_C, DK_C, DV_C), F32)
            else:
                qd, kd, el, cv, cg, dq, dk, dv = _odd_in(x, nw, w_in, wg, p, bg, tabs, tm, blk)
                past = cache_diff_k.shape[2]
                kct = jnp.transpose(cache_diff_k, (0, 1, 3, 4, 5, 2)).reshape(depth // 2, b, H_D, 2 * HD_D, past)
                vc = cache_diff_v.reshape(depth // 2, b * past * H_D, 2 * HD_D)
                yd = _diff_sample(lp, dnw, dq, dk, dv, kct, vc, p, b, l, lambda_init)
                s0 = state_gla[p]
                diff_ks.append(dk.reshape(b, l, H_D, 2, HD_D))
                diff_vs.append(dv.reshape(b, l, H_D, 2 * HD_D))
            x, s_new = _odd_mix(qd, kd, el, cv, cg, yd, x, s0, gla_norm_w[p].reshape(1, DV_C),
                                odd_w_out.astype(BF16), p, tm, blk, l, not prompt)
            glas.append(s_new)
        if prompt:
            tf = seg = min(FFN_ROW_TILE, l)
            prev = jnp.zeros((r // tf, CONV_W - 1, D_FF), F32)
        else:
            tf, seg = tm, l
            prev = cache_ffn_conv[layer]
        x, tails = _ffn(x, norm_ffn_w[layer].reshape(1, D_MODEL), ffn_w_up.astype(BF16), ffn_conv_w[layer],
                        ffn_conv_b[layer].reshape(1, D_FF), ffn_w_down.astype(BF16), prev,
                        final_norm_w.reshape(1, D_MODEL), layer, tf, seg, l, not prompt, layer == depth - 1)
        tails = tails.reshape(b, -1, CONV_TAIL, D_FF)
        ffns.append(tails[:, -1, CONV_TAIL - (CONV_W - 1):])
    if prompt:
        n_pairs = depth // 2
        diff_k = jnp.transpose(kv_p[0].reshape(n_pairs, b, H_D, 2, HD_D, l), (0, 1, 5, 2, 3, 4))
        diff_v = kv_p[1].reshape(n_pairs, b, l, H_D, 2 * HD_D)
    else:
        diff_k, diff_v = jnp.stack(diff_ks), jnp.stack(diff_vs)
    return (x.reshape(b, l, D_MODEL), jnp.stack(pools), jnp.stack(swa_ks), jnp.stack(swa_vs), jnp.stack(glas),
            diff_k, diff_v, jnp.stack(ffns))


def kernel(x_prompt, x_sample, cache_pool, cache_swa_k, cache_swa_v, state_gla, cache_diff_k, cache_diff_v, cache_ffn_conv, norm_mix_w, norm_ffn_w, final_norm_w, even_w_in, pool_w, pool_scale, swa_sinks, even_w_out, odd_w_in, gla_w_gate2, gla_b_gate, gla_norm_w, diff_lambda, diff_norm_w, odd_w_out, ffn_w_up, ffn_conv_w, ffn_conv_b, ffn_w_down):
    weights = (norm_mix_w, norm_ffn_w, final_norm_w, even_w_in, pool_w, pool_scale, swa_sinks, even_w_out, odd_w_in,
               gla_w_gate2, gla_b_gate, gla_norm_w, diff_lambda, diff_norm_w, odd_w_out, ffn_w_up, ffn_conv_w,
               ffn_conv_b, ffn_w_down)
    caches = (cache_pool, cache_swa_k, cache_swa_v, state_gla, cache_diff_k, cache_diff_v, cache_ffn_conv)
    past = cache_diff_k.shape[2]
    pos_p = jnp.arange(x_prompt.shape[1])
    pos_s = past + jnp.arange(x_sample.shape[1])
    outs_p = _trunk(x_prompt, pos_p, True, caches, weights)
    outs_s = _trunk(x_sample, pos_s, False, caches, weights)
    return (outs_p[0], outs_s[0]) + tuple(outs_p[1:]) + tuple(outs_s[1:])
```

```python
import functools
import math

import jax
import jax.numpy as jnp
from jax import lax
from jax.experimental import pallas as pl
from jax.experimental.pallas import tpu as pltpu

F32 = jnp.float32
BF16 = jnp.bfloat16

D_MODEL = 1024
CHUNK = 64
EPS = 1e-6
NEG_INF = -1e30
LOG2E = 1.0 / math.log(2.0)
ROPE_THETA = 500000.0
HEAD_DIM = 64
ROT_DIM = HEAD_DIM // 4
ROT_HALF = ROT_DIM // 2
LANES = 128

POOL_WINDOWS = (2, 4, 8, 16)
POOL_GROUP = D_MODEL // 8
D_A = len(POOL_WINDOWS) * POOL_GROUP
POOL_STATE = max(POOL_WINDOWS) - 1
POOL_HALO = POOL_STATE + 1

H_B = D_MODEL // 128
HKV_B = 2
GQA_B = H_B // HKV_B
WINDOW = 128
D_B = H_B * HEAD_DIM
D_KV = HKV_B * HEAD_DIM

H_C = 4
DK_C = 64
DV_C = 128
GATE_RANK = 16
GATE_NORM = 16.0
D_CQK = H_C * DK_C
D_C = H_C * DV_C

H_D = 4
HD_D = 64
D_D = H_D * 2 * HD_D

D_FF = 2816
CONV_W = 3
CONV_TAIL = 8

ROW_TILE = 1024
FFN_ROW_TILE = 1024
ATTN_TILE = 1024
FFN_COLS = 256
VMEM_LIMIT = 60 * 1024 * 1024


def _dot(a, b):
    return jnp.dot(a, b, preferred_element_type=F32)


def _dot_nt(a, b):
    return lax.dot_general(a, b, (((1,), (1,)), ((), ())), preferred_element_type=F32)


def _rms(x, w):
    return x * lax.rsqrt(jnp.mean(x * x, axis=-1, keepdims=True) + EPS) * w


def _rope(z, cos_t, sin_a, sin_b):
    outs = []
    for s in range(z.shape[1] // LANES):
        x = z[:, s * LANES:(s + 1) * LANES]
        outs.append(x * cos_t + pltpu.roll(x, LANES - ROT_HALF, 1) * sin_a + pltpu.roll(x, ROT_HALF, 1) * sin_b)
    return outs[0] if len(outs) == 1 else jnp.concatenate(outs, axis=-1)


def _rope_tables(pos):
    t = pos.shape[0]
    inv = jnp.exp(-math.log(ROPE_THETA) * jnp.arange(ROT_HALF, dtype=F32) * (2.0 / ROT_DIM))
    ang = pos.astype(F32)[:, None] * inv[None, :]
    cos, sin = jnp.cos(ang), jnp.sin(ang)
    rest = HEAD_DIM - ROT_DIM
    cos_t = jnp.concatenate([cos, cos, jnp.ones((t, rest), F32)], axis=-1)
    sin_a = jnp.concatenate([-sin, jnp.zeros((t, HEAD_DIM - ROT_HALF), F32)], axis=-1)
    sin_b = jnp.concatenate([jnp.zeros((t, ROT_HALF), F32), sin, jnp.zeros((t, rest), F32)], axis=-1)
    rep = LANES // HEAD_DIM
    return tuple(jnp.tile(a, (1, rep)) for a in (cos_t, sin_a, sin_b))


def _params(n_axes=1):
    return pltpu.CompilerParams(dimension_semantics=("arbitrary",) * n_axes, vmem_limit_bytes=VMEM_LIMIT)


def _full(shape):
    nd = len(shape)
    return pl.BlockSpec(shape, lambda *_: (0,) * nd, pipeline_mode=pl.Buffered(1))


def _layer(shape, index):
    nd = len(shape)
    return pl.BlockSpec((1,) + tuple(shape), lambda *_: (index,) + (0,) * nd, pipeline_mode=pl.Buffered(1))


def _rows(tm, cols):
    return pl.BlockSpec((tm, cols), lambda i: (i, 0))


def _even_in_kernel(x_ref, nw_ref, w_ref, cos_ref, sa_ref, sb_ref, u_ref, q_ref, k_ref, v_ref, *tails,
                    tm, prompt):
    h = _rms(x_ref[...], nw_ref[...]).astype(BF16)
    cos_t, sin_a, sin_b = cos_ref[...], sa_ref[...], sb_ref[...]
    q = _dot(h, w_ref[0, :, D_A:D_A + D_B])
    k = _dot(h, w_ref[0, :, D_A + D_B:D_A + D_B + D_KV])
    v = _dot(h, w_ref[0, :, D_A + D_B + D_KV:D_A + D_B + 2 * D_KV])
    q_ref[...] = (_rope(q, cos_t, sin_a, sin_b) * (HEAD_DIM ** -0.5 * LOG2E)).astype(BF16)
    k = _rope(k, cos_t, sin_a, sin_b)
    if prompt:
        k_ref[...] = k.T.astype(BF16)
        v_ref[...] = v.astype(BF16)
        tails[0][0] = k[tm - WINDOW:]
        tails[1][0] = v[tm - WINDOW:]
    else:
        k_ref[...] = k
        v_ref[...] = v
    u_ref[...] = _dot(h, w_ref[0, :, 0:D_A])


def _even_in(x, nw, w, p, tabs, tm, seq, prompt):
    r = x.shape[0]
    t_tiles = tabs[0].shape[0] // tm
    tab_spec = pl.BlockSpec((tm, LANES), lambda i: (i % t_tiles, 0))
    n_in = D_A + D_B + 2 * D_KV
    sds = jax.ShapeDtypeStruct
    if prompt:
        tps = seq // tm
        tail_spec = pl.BlockSpec((1, WINDOW, D_KV), lambda i: (i // tps, 0, 0))
        kv_specs = [pl.BlockSpec((D_KV, tm), lambda i: (0, i)), _rows(tm, D_KV), tail_spec, tail_spec]
        kv_shapes = [sds((D_KV, r), BF16), sds((r, D_KV), BF16),
                     sds((r // seq, WINDOW, D_KV), F32), sds((r // seq, WINDOW, D_KV), F32)]
    else:
        kv_specs = [_rows(tm, D_KV), _rows(tm, D_KV)]
        kv_shapes = [sds((r, D_KV), F32), sds((r, D_KV), F32)]
    return pl.pallas_call(
        functools.partial(_even_in_kernel, tm=tm, prompt=prompt),
        grid=(r // tm,),
        in_specs=[_rows(tm, D_MODEL), _full((1, D_MODEL)), _layer((D_MODEL, n_in), p), tab_spec, tab_spec,
                  tab_spec],
        out_specs=[_rows(tm, D_A), _rows(tm, D_B)] + kv_specs,
        out_shape=[sds((r, D_A), F32), sds((r, D_B), BF16)] + kv_shapes,
        compiler_params=_params(),
        name="even_in",
    )(x, nw, w, *tabs)


def _pool_mix(ext, u, pos, wp_ref, ps, ycat_ref, row0, rows):
    for g, w in enumerate(POOL_WINDOWS):
        lo, hi = g * POOL_GROUP, (g + 1) * POOL_GROUP
        s = ext[:, lo:hi]
        sh = 1
        while sh < w:
            s = s + pltpu.roll(s, sh, 0)
            sh *= 2
        count = jnp.minimum(w, pos + 1).astype(F32)
        diff = s[POOL_HALO:] / count - u[:, lo:hi]
        ya = _dot(diff.astype(BF16), wp_ref[0, g]) * ps[:, lo:hi]
        ycat_ref[row0:row0 + rows, lo:hi] = ya.astype(BF16)


def _sink_column(sink_ref, hk, rows_per_head):
    n = GQA_B * rows_per_head
    grp = lax.broadcasted_iota(jnp.int32, (n, 1), 0) // rows_per_head
    col = jnp.full((n, 1), sink_ref[hk * GQA_B] * LOG2E, F32)
    for g in range(1, GQA_B):
        col = jnp.where(grp == g, sink_ref[hk * GQA_B + g] * LOG2E, col)
    return col


def _sink_attend(s, sink_col, vb):
    m = jnp.maximum(jnp.max(s, axis=-1, keepdims=True), sink_col)
    e = jnp.exp2(s - m)
    den = jnp.sum(e, axis=-1, keepdims=True) + jnp.exp2(sink_col - m)
    return _dot(e.astype(BF16), vb) / den


def _store_heads(ycat_ref, o, row0, rows, hk):
    for pair in range(GQA_B // 2):
        both = jnp.concatenate([o[(2 * pair) * rows:(2 * pair + 1) * rows],
                                o[(2 * pair + 1) * rows:(2 * pair + 2) * rows]], axis=-1)
        c0 = D_A + (hk * GQA_B + 2 * pair) * HEAD_DIM
        ycat_ref[row0:row0 + rows, c0:c0 + 2 * HEAD_DIM] = both.astype(BF16)


def _even_mix_prompt_kernel(sink_ref, u_ref, q_ref, k_ref, v_ref, x_ref, wp_ref, ps_ref, wo_ref, o_ref,
                            ucar, kcar, vcar, ycat, *, tm, tiles_per_seq):
    i = pl.program_id(0)
    ti = i % tiles_per_seq
    first_key_chunk = jnp.where(ti == 0, WINDOW // CHUNK, 0)

    @pl.when(ti == 0)
    def _():
        ucar[...] = jnp.zeros_like(ucar)
        kcar[...] = jnp.zeros_like(kcar)
        vcar[...] = jnp.zeros_like(vcar)

    u = u_ref[...]
    ext = jnp.concatenate([ucar[...], u], axis=0)
    pos = ti * tm + lax.broadcasted_iota(jnp.int32, (tm, POOL_GROUP), 0)
    _pool_mix(ext, u, pos, wp_ref, ps_ref[...], ycat, 0, tm)
    ucar[...] = u[tm - POOL_HALO:]

    kext = jnp.concatenate([kcar[...], k_ref[...]], axis=1)
    vext = jnp.concatenate([vcar[...], v_ref[...]], axis=0)
    kcar[...] = kext[:, tm:]
    vcar[...] = vext[tm:]
    qb = 2 * CHUNK
    kb = qb + WINDOW
    nrow = GQA_B * qb
    qc = (lax.broadcasted_iota(jnp.int32, (nrow, kb), 0) % qb) // CHUNK
    kc = lax.broadcasted_iota(jnp.int32, (nrow, kb), 1) // CHUNK
    band = (kc >= qc) & (kc <= qc + WINDOW // CHUNK)
    first = band & (kc >= first_key_chunk)
    grp = lax.broadcasted_iota(jnp.int32, (nrow, LANES), 0) // qb
    low_lanes = lax.broadcasted_iota(jnp.int32, (qb, LANES), 1) < HEAD_DIM
    ones = jnp.ones((WINDOW + tm, HEAD_DIM), BF16)
    zeros = jnp.zeros((HEAD_DIM, WINDOW + tm), BF16)
    order = [g for g in range(GQA_B) if g % 2 == 0] + [g for g in range(GQA_B) if g % 2 == 1]
    v_aug, sink_rep, k_even, k_odd = [], [], [], []
    for hk in range(HKV_B):
        v_aug.append(jnp.concatenate([vext[:, hk * HEAD_DIM:(hk + 1) * HEAD_DIM], ones], axis=-1))
        kt = kext[hk * HEAD_DIM:(hk + 1) * HEAD_DIM]
        k_even.append(jnp.concatenate([kt, zeros], axis=0))
        k_odd.append(jnp.concatenate([zeros, kt], axis=0))
        rep = jnp.full((nrow, LANES), sink_ref[hk * GQA_B + order[0]] * LOG2E, F32)
        for r in range(1, GQA_B):
            rep = jnp.where(grp == r, sink_ref[hk * GQA_B + order[r]] * LOG2E, rep)
        sink_rep.append(rep)

    def scores(a, hk):
        def tiles(gs):
            return jnp.concatenate(
                [q_ref[a * qb:(a + 1) * qb, (hk * GQA_B + g) // 2 * LANES:((hk * GQA_B + g) // 2 + 1) * LANES]
                 for g in gs], axis=0)
        half = GQA_B // 2
        return jnp.concatenate([_dot(tiles(order[:half]), k_even[hk][:, a * qb:a * qb + kb]),
                                _dot(tiles(order[half:]), k_odd[hk][:, a * qb:a * qb + kb])], axis=0)

    def softmax(a, hk, s):
        s = jnp.where(first if a == 0 else band, s, NEG_INF)
        m = jnp.maximum(jnp.max(s, axis=-1, keepdims=True), sink_rep[hk])
        return jnp.exp2(s - _lane_tile(m, kb // LANES)).astype(BF16), jnp.exp2(sink_rep[hk] - m)

    def finish(a, hk, e, sink_term):
        oa = _dot(e, v_aug[hk][a * qb:a * qb + kb])
        inv = 1.0 / (oa + sink_term)
        for pair in range(GQA_B // 2):
            lo = slice(pair * qb, (pair + 1) * qb)
            hi = slice((GQA_B // 2 + pair) * qb, (GQA_B // 2 + pair + 1) * qb)
            both = jnp.where(low_lanes, oa[lo] * pltpu.roll(inv[lo], HEAD_DIM, 1),
                             pltpu.roll(oa[hi], HEAD_DIM, 1) * inv[hi])
            c0 = D_A + (hk * GQA_B + 2 * pair) * HEAD_DIM
            ycat[a * qb:(a + 1) * qb, c0:c0 + 2 * HEAD_DIM] = both.astype(BF16)

    blocks = [(a, hk) for a in range(tm // qb) for hk in range(HKV_B)]
    nb = len(blocks)
    s_q = {0: scores(*blocks[0])}
    e_q = {}
    for n in range(nb + 2):
        if n + 1 < nb:
            s_q[n + 1] = scores(*blocks[n + 1])
        if n < nb:
            e_q[n] = softmax(*blocks[n], s_q.pop(n))
        if 1 <= n <= nb:
            finish(*blocks[n - 1], *e_q.pop(n - 1))
    o_ref[...] = x_ref[...] + _dot(ycat[...], wo_ref[0])


def _even_mix_prompt(sinks, u, q, kt, v, x, wp, ps, wo, p, seq, tm):
    r = x.shape[0]
    kern = functools.partial(_even_mix_prompt_kernel, tm=tm, tiles_per_seq=seq // tm)
    return pl.pallas_call(
        kern,
        grid=(r // tm,),
        in_specs=[pl.BlockSpec(memory_space=pltpu.SMEM),
                  _rows(tm, D_A), _rows(tm, D_B), pl.BlockSpec((D_KV, tm), lambda i: (0, i)), _rows(tm, D_KV),
                  _rows(tm, D_MODEL), _layer(wp.shape[1:], p), _full((1, D_A)), _layer((D_A + D_B, D_MODEL), p)],
        out_specs=_rows(tm, D_MODEL),
        out_shape=jax.ShapeDtypeStruct((r, D_MODEL), F32),
        scratch_shapes=[pltpu.VMEM((POOL_HALO, D_A), F32), pltpu.VMEM((D_KV, WINDOW), BF16),
                        pltpu.VMEM((WINDOW, D_KV), BF16), pltpu.VMEM((tm, D_A + D_B), BF16)],
        compiler_params=_params(),
        name="even_mix_prompt",
    )(sinks, u, q, kt, v, x, wp, ps, wo)


def _even_mix_sample_kernel(sink_ref, u_ref, q_ref, k_ref, v_ref, x_ref, pp_ref, kc_ref, vc_ref, wp_ref, ps_ref,
                            wo_ref, o_ref, ycat, *, l, pos0):
    u = u_ref[...]
    ext = jnp.concatenate([pp_ref[0], u], axis=0)
    pos = pos0 + lax.broadcasted_iota(jnp.int32, (l, 1), 0)
    _pool_mix(ext, u, pos, wp_ref, ps_ref[...], ycat, 0, l)

    pad = jnp.zeros((LANES - l, D_KV), BF16)
    kall = jnp.concatenate([kc_ref[0].astype(BF16), k_ref[...].astype(BF16), pad], axis=0)
    vall = jnp.concatenate([vc_ref[0].astype(BF16), v_ref[...].astype(BF16), pad], axis=0)
    nk = kall.shape[0]
    valid = lax.broadcasted_iota(jnp.int32, (GQA_B * l, nk), 1) < nk - (LANES - l)
    for hk in range(HKV_B):
        ks = kall[:, hk * HEAD_DIM:(hk + 1) * HEAD_DIM]
        vs = vall[:, hk * HEAD_DIM:(hk + 1) * HEAD_DIM]
        qs = jnp.concatenate(
            [q_ref[:, (hk * GQA_B + g) * HEAD_DIM:(hk * GQA_B + g + 1) * HEAD_DIM] for g in range(GQA_B)], axis=0)
        s = jnp.where(valid, _dot_nt(qs, ks), NEG_INF)
        o = _sink_attend(s, _sink_column(sink_ref, hk, l), vs)
        _store_heads(ycat, o, 0, l, hk)
    o_ref[...] = x_ref[...] + _dot(ycat[...], wo_ref[0])


def _even_mix_sample(sinks, u, q, k, v, x, pool_prev, kc, vc, wp, ps, wo, p, l, pos0):
    r = x.shape[0]
    rows_c = kc.shape[1]
    kern = functools.partial(_even_mix_sample_kernel, l=l, pos0=pos0)
    cache_spec = pl.BlockSpec((1, rows_c, D_KV), lambda i: (i, 0, 0))
    return pl.pallas_call(
        kern,
        grid=(r // l,),
        in_specs=[pl.BlockSpec(memory_space=pltpu.SMEM),
                  _rows(l, D_A), _rows(l, D_B), _rows(l, D_KV), _rows(l, D_KV), _rows(l, D_MODEL),
                  pl.BlockSpec((1, POOL_HALO, D_A), lambda i: (i, 0, 0)), cache_spec, cache_spec,
                  _layer(wp.shape[1:], p), _full((1, D_A)), _layer((D_A + D_B, D_MODEL), p)],
        out_specs=_rows(l, D_MODEL),
        out_shape=jax.ShapeDtypeStruct((r, D_MODEL), F32),
        scratch_shapes=[pltpu.VMEM((l, D_A + D_B), BF16)],
        compiler_params=_params(),
        name="even_mix_sample",
    )(sinks, u, q, k, v, x, pool_prev, kc, vc, wp, ps, wo)


ODD_CR0 = 2 * D_CQK + 2 * D_C
ODD_DQ0 = ODD_CR0 + LANES
ODD_IN_PADDED = ODD_DQ0 + 3 * D_D


def _odd_in_kernel(x_ref, nw_ref, w_ref, wg_ref, bg_ref, cos_ref, sa_ref, sb_ref, *rest, tm, blk, cache_layout,
                   aliased):
    if aliased:
        rest = rest[2:]
    qd_ref, kd_ref, el_ref, cv_ref, cg_ref, dq_ref, dk_ref, dv_ref, cum_scr = rest
    h = _rms(x_ref[...], nw_ref[...]).astype(BF16)
    cos_t, sin_a, sin_b = cos_ref[...], sa_ref[...], sb_ref[...]
    cr = _dot(h, w_ref[0, :, ODD_CR0:ODD_DQ0]).astype(BF16)
    gate = _dot(cr, wg_ref[0]) + bg_ref[...]
    cq = _dot(h, w_ref[0, :, 0:D_CQK])
    ck = _dot(h, w_ref[0, :, D_CQK:2 * D_CQK])
    dq = _dot(h, w_ref[0, :, ODD_DQ0:ODD_DQ0 + D_D])
    cum = (jnp.minimum(gate, 0.0) - jnp.log1p(jnp.exp(-jnp.abs(gate)))) * (1.0 / GATE_NORM)
    rowb = lax.broadcasted_iota(jnp.int32, (tm, 1), 0) % blk
    sh = 1
    while sh < blk:
        cum = cum + jnp.where(rowb >= sh, pltpu.roll(cum, sh, 0), 0.0)
        sh *= 2
    nb = tm // blk
    for s in range(D_CQK // LANES):
        cum_scr[s] = cum[:, s * LANES:(s + 1) * LANES]
        last = cum_scr[s, pl.ds(blk - 1, nb, stride=blk), :]
        last = jnp.concatenate([last, jnp.zeros((LANES - nb, LANES), F32)], axis=0)
        el_ref[0, s * LANES:(s + 1) * LANES, :] = jnp.exp(last.T)
    qd_ref[...] = (cq * (DK_C ** -0.5) * jnp.exp(cum)).astype(BF16)
    kd_ref[...] = (ck * jnp.exp(-cum)).T.astype(BF16)
    dk = _dot(h, w_ref[0, :, ODD_DQ0 + D_D:ODD_DQ0 + 2 * D_D])
    dq_ref[...] = (_rope(dq, cos_t, sin_a, sin_b) * (HD_D ** -0.5 * LOG2E)).astype(BF16)
    dv = _dot(h, w_ref[0, :, ODD_DQ0 + 2 * D_D:ODD_DQ0 + 3 * D_D])
    dk = _rope(dk, cos_t, sin_a, sin_b)
    cv_ref[...] = _dot(h, w_ref[0, :, 2 * D_CQK:2 * D_CQK + D_C]).astype(BF16)
    cg_ref[...] = _dot(h, w_ref[0, :, 2 * D_CQK + D_C:ODD_CR0])
    if cache_layout:
        for hd in range(H_D):
            lo, hi = hd * 2 * HD_D, (hd + 1) * 2 * HD_D
            dk_ref[0, 0, hd] = dk[:, lo:hi].T
            dv_ref[0, pl.ds(hd, tm, stride=H_D), :] = dv[:, lo:hi]
    else:
        dk_ref[...] = dk
        dv_ref[...] = dv


def _odd_in(x, nw, w, wg, wi, bg, tabs, tm, blk, pair=None):
    r = x.shape[0]
    t_tiles = tabs[0].shape[0] // tm
    tab_spec = pl.BlockSpec((tm, LANES), lambda i: (i % t_tiles, 0))
    nb = tm // blk
    sds = jax.ShapeDtypeStruct
    in_specs = [_rows(tm, D_MODEL), _full((1, D_MODEL)), _layer((D_MODEL, ODD_IN_PADDED), wi),
                _layer((LANES, D_CQK), wi), _full((1, D_CQK)), tab_spec, tab_spec, tab_spec]
    args = [x, nw, w, wg, bg, *tabs]
    aliases = {}
    if pair is None:
        kv_specs = [_rows(tm, D_D), _rows(tm, D_D)]
        kv_shapes = [sds((r, D_D), F32), sds((r, D_D), F32)]
    else:
        p, n_pairs, batch, seq, bufs = pair
        tps = seq // tm
        kv_specs = [pl.BlockSpec((1, 1, H_D, 2 * HD_D, tm), lambda i: (p, i // tps, 0, 0, i % tps)),
                    pl.BlockSpec((1, tm * H_D, 2 * HD_D), lambda i: (p, i, 0))]
        kv_shapes = [sds((n_pairs, batch, H_D, 2 * HD_D, seq), F32), sds((n_pairs, r * H_D, 2 * HD_D), F32)]
        if bufs is not None:
            in_specs += [pl.BlockSpec(memory_space=pl.ANY)] * 2
            args += list(bufs)
            aliases = {len(args) - 2: 6, len(args) - 1: 7}
    kern = functools.partial(_odd_in_kernel, tm=tm, blk=blk, cache_layout=pair is not None, aliased=bool(aliases))
    return pl.pallas_call(
        kern,
        grid=(r // tm,),
        in_specs=in_specs,
        out_specs=[_rows(tm, D_CQK), pl.BlockSpec((D_CQK, tm), lambda i: (0, i)),
                   pl.BlockSpec((1, D_CQK, LANES), lambda i: (i, 0, 0)), _rows(tm, D_C), _rows(tm, D_C),
                   _rows(tm, D_D)] + kv_specs,
        out_shape=[sds((r, D_CQK), BF16), sds((D_CQK, r), BF16), sds((r // tm, D_CQK, LANES), F32),
                   sds((r, D_C), BF16), sds((r, D_C), F32), sds((r, D_D), BF16)] + kv_shapes,
        scratch_shapes=[pltpu.VMEM((D_CQK // LANES, tm, LANES), F32)],
        input_output_aliases=aliases,
        compiler_params=_params(),
        name="odd_in",
    )(*args)


def _lambda_value(lp_ref, lambda_init):
    lp = lp_ref[...]
    s1 = jnp.sum(lp[0:1] * lp[1:2], axis=-1, keepdims=True)
    s2 = jnp.sum(lp[2:3] * lp[3:4], axis=-1, keepdims=True)
    return jnp.exp(s1) - jnp.exp(s2) + lambda_init


def _diff_finish(acc, den, lam, nw, lambda_init):
    o = acc[0] / den[0] - lam * (acc[1] / den[1])
    return (_rms(o, nw) * (1.0 - lambda_init)).astype(BF16)


def _lane_tile(x, n):
    return x if n == 1 else jnp.concatenate([x] * n, axis=1)


def _diff_prompt_kernel(lp_ref, nw_ref, q_ref, k_ref, v_ref, o_ref, kbf, vbf, m_scr, acc_scr, s_a, s_b, bias,
                        *, tq, lambda_init):
    hd = pl.program_id(1)
    dv = 2 * HD_D
    seq = vbf.shape[0]
    tk = tq // 2

    kbf[...] = k_ref[0, 0, 0].astype(BF16)
    vbf[:, :dv] = v_ref[0, pl.ds(hd, seq, stride=H_D), :].astype(BF16)
    vbf[:, dv:] = jnp.ones((seq, dv), BF16)

    @pl.when((pl.program_id(0) == 0) & (hd == 0))
    def _():
        qc = lax.broadcasted_iota(jnp.int32, (tk, tk), 0) // CHUNK
        kc = lax.broadcasted_iota(jnp.int32, (tk, tk), 1) // CHUNK
        bias[...] = jnp.where(kc <= qc, 0.0, NEG_INF)

    lam = _lambda_value(lp_ref, lambda_init)

    def scores(qi, t, s_buf, row0=0):
        start = pl.multiple_of(t * tk, tk)
        rows = pl.ds(pl.multiple_of(qi * tq + row0, tk), tq - row0)
        for c in range(2):
            s_buf[c, row0:, :] = _dot(q_ref[rows, c * HD_D:(c + 1) * HD_D],
                                      kbf[c * HD_D:(c + 1) * HD_D, pl.ds(start, tk)])

    def attend(t, s_buf, row0=0, rows=tq, masked=False):
        vs = vbf[pl.ds(pl.multiple_of(t * tk, tk), tk), :]
        for c in range(2):
            sc = s_buf[c, row0:row0 + rows, :]
            if masked:
                sc = sc + bias[...]
            m_old = m_scr[c, row0:row0 + rows, :]
            m_new = jnp.maximum(m_old, jnp.max(sc, axis=-1, keepdims=True))
            p = jnp.exp2(sc - _lane_tile(m_new, tk // LANES)).astype(BF16)
            alpha = _lane_tile(jnp.exp2(m_old - m_new), 2 * dv // LANES)
            acc_scr[c, row0:row0 + rows, :] = alpha * acc_scr[c, row0:row0 + rows, :] + _dot(p, vs)
            m_scr[c, row0:row0 + rows, :] = m_new

    def query_tile(qi, _):
        m_scr[...] = jnp.full(m_scr.shape, NEG_INF, F32)
        acc_scr[...] = jnp.zeros(acc_scr.shape, F32)
        scores(qi, 0, s_a)

        def pair(j, _):
            t = 2 * j
            scores(qi, t + 1, s_b)
            vs = vbf[pl.ds(pl.multiple_of(t * tk, 2 * tk), 2 * tk), :]
            for c in range(2):
                sa, sb = s_a[c], s_b[c]
                m_old = m_scr[c]
                m_new = jnp.maximum(m_old, jnp.maximum(jnp.max(sa, axis=-1, keepdims=True),
                                                       jnp.max(sb, axis=-1, keepdims=True)))
                m_rep = _lane_tile(m_new, tk // LANES)
                p = jnp.concatenate([jnp.exp2(sa - m_rep).astype(BF16), jnp.exp2(sb - m_rep).astype(BF16)], axis=1)
                alpha = _lane_tile(jnp.exp2(m_old - m_new), 2 * dv // LANES)
                acc_scr[c] = alpha * acc_scr[c] + _dot(p, vs)
                m_scr[c] = m_new
            scores(qi, t + 2, s_a)
            return 0

        lax.fori_loop(0, qi, pair, 0)
        scores(qi, 2 * qi + 1, s_b, row0=tk)
        attend(2 * qi, s_a, row0=0, rows=tk, masked=True)
        attend(2 * qi, s_a, row0=tk, rows=tk)
        attend(2 * qi + 1, s_b, row0=tk, rows=tk, masked=True)
        acc = [acc_scr[c] for c in range(2)]
        o_ref[pl.ds(pl.multiple_of(qi * tq, tq), tq), :] = _diff_finish(
            [a[:, :dv] for a in acc], [a[:, dv:] for a in acc], lam, nw_ref[...], lambda_init)
        return 0

    lax.fori_loop(0, seq // tq, query_tile, 0)


def _diff_prompt(lp, nw, q, kt, v, p, batch, seq, lambda_init, tq):
    r = q.shape[0]
    kern = functools.partial(_diff_prompt_kernel, tq=tq, lambda_init=lambda_init)
    k_spec = pl.BlockSpec((1, 1, 1, 2 * HD_D, seq), lambda b, h: (p, b, h, 0, 0))
    v_spec = pl.BlockSpec((1, seq * H_D, 2 * HD_D), lambda b, h: (p, b, 0))
    q_spec = pl.BlockSpec((seq, 2 * HD_D), lambda b, h: (b, h))
    return pl.pallas_call(
        kern,
        grid=(batch, H_D),
        in_specs=[_full((4, HD_D)), _full((1, 2 * HD_D)), q_spec, k_spec, v_spec],
        out_specs=q_spec,
        out_shape=jax.ShapeDtypeStruct((r, D_D), BF16),
        scratch_shapes=[pltpu.VMEM((2 * HD_D, seq), BF16), pltpu.VMEM((seq, 4 * HD_D), BF16),
                        pltpu.VMEM((2, tq, LANES), F32), pltpu.VMEM((2, tq, 4 * HD_D), F32),
                        pltpu.VMEM((2, tq, tq // 2), F32), pltpu.VMEM((2, tq, tq // 2), F32),
                        pltpu.VMEM((tq // 2, tq // 2), F32)],
        compiler_params=_params(2),
        name="diff_prompt",
    )(lp, nw, q, kt, v)


def _diff_sample_kernel(lp_ref, nw_ref, q_ref, k_ref, v_ref, kc_ref, vc_ref, o_ref, *, l, past, lambda_init):
    pad = jnp.zeros((LANES - l, 2 * HD_D), BF16)
    valid = lax.broadcasted_iota(jnp.int32, (l, LANES), 1) < l
    lam = _lambda_value(lp_ref, lambda_init)
    for hd in range(H_D):
        lo, hi = hd * 2 * HD_D, (hd + 1) * 2 * HD_D
        knew = jnp.concatenate([k_ref[:, lo:hi].astype(BF16), pad], axis=0)
        vnew = jnp.concatenate([v_ref[:, lo:hi].astype(BF16), pad], axis=0)
        kct = kc_ref[0, 0, hd].astype(BF16)
        vcache = vc_ref[0, pl.ds(hd, past, stride=H_D), :].astype(BF16)
        acc, den = [], []
        for c in range(2):
            qc = q_ref[:, lo + c * HD_D:lo + (c + 1) * HD_D]
            s_old = _dot(qc, kct[c * HD_D:(c + 1) * HD_D, :])
            s_new = jnp.where(valid, _dot_nt(qc, knew[:, c * HD_D:(c + 1) * HD_D]), NEG_INF)
            m = jnp.maximum(jnp.max(s_old, axis=-1, keepdims=True), jnp.max(s_new, axis=-1, keepdims=True))
            p_old = jnp.exp2(s_old - m)
            p_new = jnp.exp2(s_new - m)
            den.append(jnp.sum(p_old, axis=-1, keepdims=True) + jnp.sum(p_new, axis=-1, keepdims=True))
            acc.append(_dot(p_old.astype(BF16), vcache) + _dot(p_new.astype(BF16), vnew))
        o_ref[:, lo:hi] = _diff_finish(acc, den, lam, nw_ref[...], lambda_init)


def _diff_sample(lp, nw, q, k, v, kct, vc, p, batch, l, lambda_init):
    r = q.shape[0]
    past = kct.shape[-1]
    kern = functools.partial(_diff_sample_kernel, l=l, past=past, lambda_init=lambda_init)
    cache_k_spec = pl.BlockSpec((1, 1, H_D, 2 * HD_D, past), lambda b: (p, b, 0, 0, 0))
    cache_v_spec = pl.BlockSpec((1, past * H_D, 2 * HD_D), lambda b: (p, b, 0))
    return pl.pallas_call(
        kern,
        grid=(batch,),
        in_specs=[_full((4, HD_D)), _full((1, 2 * HD_D)), _rows(l, D_D), _rows(l, D_D), _rows(l, D_D),
                  cache_k_spec, cache_v_spec],
        out_specs=_rows(l, D_D),
        out_shape=jax.ShapeDtypeStruct((r, D_D), BF16),
        compiler_params=_params(),
        name="diff_sample",
    )(lp, nw, q, k, v, kct, vc)


def _odd_mix_kernel(qd_ref, kd_ref, el_ref, cv_ref, cg_ref, yd_ref, x_ref, s0_ref, gw_ref, wo_ref,
                    o_ref, s_ref, state, og, ycat, *, tm, blk, tiles_per_seq, per_block_state):
    i = pl.program_id(0)
    if not per_block_state:
        @pl.when(i % tiles_per_seq == 0)
        def _():
            state[...] = jnp.zeros_like(state)

    per_group = LANES // blk
    rows_of = lax.broadcasted_iota(jnp.int32, (LANES, LANES), 0)
    cols_of = lax.broadcasted_iota(jnp.int32, (LANES, LANES), 1)
    q_row = lax.broadcasted_iota(jnp.int32, (blk, LANES), 0)
    q_col = lax.broadcasted_iota(jnp.int32, (blk, LANES), 1)
    low = [(h % 2) * DK_C for h in range(H_C)]
    own_rows = [(rows_of >= low[h]) & (rows_of < low[h] + DK_C) for h in range(H_C)]
    lanes = [slice(h // 2 * LANES, (h // 2 + 1) * LANES) for h in range(H_C)]
    vcols = [slice(h * DV_C, (h + 1) * DV_C) for h in range(H_C)]
    st = [None if per_block_state else state[h] for h in range(H_C)]
    for n in range(tm // blk):
        g0, c = (n // per_group) * LANES, n % per_group
        r0 = n * blk
        causal = (q_col >= c * blk) & (q_col - c * blk <= q_row)
        own_cols = (cols_of >= c * blk) & (cols_of < (c + 1) * blk)
        for h in range(H_C):
            if per_block_state:
                st[h] = jnp.where(own_rows[h], jnp.concatenate([s0_ref[n, h]] * 2, axis=0), 0.0)
            qd = qd_ref[r0:r0 + blk, lanes[h]]
            kt = jnp.where(own_rows[h], kd_ref[lanes[h], g0:g0 + LANES], jnp.zeros((), BF16))
            vg = cv_ref[g0:g0 + LANES, vcols[h]]
            att = jnp.where(causal, _dot(qd, kt), 0.0).astype(BF16)
            og[r0:r0 + blk, vcols[h]] = _dot(att, vg) + _dot(qd, st[h].astype(BF16))
            k_blk = jnp.where(own_cols, kt, jnp.zeros((), BF16))
            decay = jnp.where(own_rows[h], jnp.broadcast_to(el_ref[0, lanes[h], n:n + 1], (LANES, LANES)), 0.0)
            st[h] = decay * (st[h] + _dot(k_blk, vg))
            if per_block_state:
                s_ref[n, h] = st[h][low[h]:low[h] + DK_C]
    if not per_block_state:
        for h in range(H_C):
            state[h] = st[h]
            s_ref[0, h] = st[h][low[h]:low[h] + DK_C]

    gw = gw_ref[...]
    for h in range(H_C):
        lo, hi = h * DV_C, (h + 1) * DV_C
        g = cg_ref[:, lo:hi]
        y = _rms(og[:, lo:hi], gw) * (g * (1.0 / (1.0 + jnp.exp(-g))))
        ycat[:, lo:hi] = y.astype(BF16)
    ycat[:, D_C:] = yd_ref[...]
    o_ref[...] = x_ref[...] + _dot(ycat[...], wo_ref[0])


def _odd_mix(qd, kd, el, cv, cg, yd, x, s0, gw, wo, p, tm, blk, seq, per_block_state):
    assert tm % LANES == 0 and LANES % blk == 0 and tm // blk <= LANES
    r = x.shape[0]
    nb = tm // blk
    tiles_per_seq = max(seq // tm, 1)
    kern = functools.partial(_odd_mix_kernel, tm=tm, blk=blk, tiles_per_seq=tiles_per_seq,
                             per_block_state=per_block_state)
    if per_block_state:
        n_state = r // blk
        state_spec = pl.BlockSpec((nb, H_C, DK_C, DV_C), lambda i: (i, 0, 0, 0))
    else:
        n_state = r // seq
        state_spec = pl.BlockSpec((1, H_C, DK_C, DV_C), lambda i: (i // tiles_per_seq, 0, 0, 0))
    return pl.pallas_call(
        kern,
        grid=(r // tm,),
        in_specs=[_rows(tm, D_CQK), pl.BlockSpec((D_CQK, tm), lambda i: (0, i)),
                  pl.BlockSpec((1, D_CQK, LANES), lambda i: (i, 0, 0)), _rows(tm, D_C), _rows(tm, D_C),
                  _rows(tm, D_D), _rows(tm, D_MODEL), state_spec, _full((1, DV_C)),
                  _layer((D_C + D_D, D_MODEL), p)],
        out_specs=[_rows(tm, D_MODEL), state_spec],
        out_shape=[jax.ShapeDtypeStruct((r, D_MODEL), F32),
                   jax.ShapeDtypeStruct((n_state, H_C, DK_C, DV_C), F32)],
        scratch_shapes=[pltpu.VMEM((H_C, LANES, DV_C), F32), pltpu.VMEM((tm, D_C), F32),
                        pltpu.VMEM((tm, D_C + D_D), BF16)],
        compiler_params=_params(),
        name="odd_mix",
    )(qd, kd, el, cv, cg, yd, x, s0, gw, wo)


def _gelu_tanh(x):
    return 0.5 * x * (1.0 + jnp.tanh(math.sqrt(2.0 / math.pi) * (x + 0.044715 * (x * x * x))))


def _ffn_kernel(x_ref, nw_ref, wup_ref, cw_ref, cb_ref, wdn_ref, prev_ref, fnw_ref, o_ref, tail_ref, carry, act,
                *, tm, seg, tiles_per_seq, prev_from_input, final_norm):
    i = pl.program_id(0)
    x = x_ref[...]
    nseg = tm // seg
    row = lax.broadcasted_iota(jnp.int32, (seg, 1), 0)
    if not prev_from_input:
        @pl.when(i % tiles_per_seq == 0)
        def _():
            carry[...] = jnp.zeros_like(carry)

    def up(hh, c0):
        return (_dot(hh, wup_ref[0, :, c0:c0 + FFN_COLS]),
                _dot(hh, wup_ref[0, :, D_FF + c0:D_FF + c0 + FFN_COLS]))

    h = _rms(x, nw_ref[...]).astype(BF16)
    nxt = up(h, 0)
    for c0 in range(0, D_FF, FFN_COLS):
        c1 = c0 + FFN_COLS
        gate, val = nxt
        if c1 < D_FF:
            nxt = up(h, c1)
        w0, w1, w2 = cw_ref[0:1, c0:c1], cw_ref[1:2, c0:c1], cw_ref[2:3, c0:c1]
        convs = []
        for s in range(nseg):
            g = gate[s * seg:(s + 1) * seg]
            if prev_from_input:
                p0, p1 = prev_ref[s, 0:1, c0:c1], prev_ref[s, 1:2, c0:c1]
            else:
                p0 = carry[CONV_TAIL - 2:CONV_TAIL - 1, c0:c1]
                p1 = carry[CONV_TAIL - 1:CONV_TAIL, c0:c1]
            g1 = jnp.where(row == 0, p1, pltpu.roll(g, 1, 0))
            g2 = jnp.where(row == 0, p0, jnp.where(row == 1, p1, pltpu.roll(g, 2, 0)))
            convs.append(cb_ref[:, c0:c1] + g2 * w0 + g1 * w1 + g * w2)
            tail_ref[s, :, c0:c1] = g[seg - CONV_TAIL:]
        conv = convs[0] if nseg == 1 else jnp.concatenate(convs, axis=0)
        if not prev_from_input:
            carry[:, c0:c1] = gate[tm - CONV_TAIL:]
        act[:, c0:c1] = (_gelu_tanh(conv) * val).astype(BF16)
    if final_norm:
        o_ref[...] = _rms(x + _dot(act[...], wdn_ref[0]), fnw_ref[...])
    else:
        half = D_MODEL // 2
        for c0 in (0, half):
            o_ref[:, c0:c0 + half] = x[:, c0:c0 + half] + _dot(act[...], wdn_ref[0, :, c0:c0 + half])


def _ffn(x, nw, wup, cw, cb, wdn, prev, fnw, layer, tm, seg, seq, prev_from_input, final_norm):
    r = x.shape[0]
    nseg = tm // seg
    kern = functools.partial(_ffn_kernel, tm=tm, seg=seg, tiles_per_seq=max(seq // tm, 1),
                             prev_from_input=prev_from_input, final_norm=final_norm)
    prev_spec = pl.BlockSpec((nseg, CONV_W - 1, D_FF), lambda i: (i, 0, 0))
    tail_spec = pl.BlockSpec((nseg, CONV_TAIL, D_FF), lambda i: (i, 0, 0))
    wup_spec = pl.BlockSpec((1, D_MODEL, 2 * D_FF), lambda i: (layer, 0, 0), pipeline_mode=pl.Buffered(1))
    wdn_spec = pl.BlockSpec((1, D_FF, D_MODEL), lambda i: (layer, 0, 0), pipeline_mode=pl.Buffered(1))
    return pl.pallas_call(
        kern,
        grid=(r // tm,),
        in_specs=[_rows(tm, D_MODEL), _full((1, D_MODEL)), wup_spec, _full((CONV_W, D_FF)),
                  _full((1, D_FF)), wdn_spec, prev_spec, _full((1, D_MODEL))],
        out_specs=[_rows(tm, D_MODEL), tail_spec],
        out_shape=[jax.ShapeDtypeStruct((r, D_MODEL), F32),
                   jax.ShapeDtypeStruct((r // seg, CONV_TAIL, D_FF), F32)],
        scratch_shapes=[pltpu.VMEM((CONV_TAIL, D_FF), F32), pltpu.VMEM((tm, D_FF), BF16)],
        compiler_params=_params(),
        name="ffn",
    )(x, nw, wup, cw, cb, wdn, prev, fnw)


def _pad_odd_w_in(w):
    cr = w[..., ODD_CR0:ODD_CR0 + GATE_RANK]
    pad = jnp.zeros(w.shape[:-1] + (LANES - GATE_RANK,), w.dtype)
    return jnp.concatenate([w[..., :ODD_CR0], cr, pad, w[..., ODD_CR0 + GATE_RANK:]], axis=-1).astype(BF16)


def _trunk(x3, pos, prompt, caches, weights):
    (norm_mix_w, norm_ffn_w, final_norm_w, even_w_in, pool_w, pool_scale, swa_sinks, even_w_out, odd_w_in,
     gla_w_gate2, gla_b_gate, gla_norm_w, diff_lambda, diff_norm_w, odd_w_out, ffn_w_up, ffn_conv_w, ffn_conv_b,
     ffn_w_down) = weights
    cache_pool, cache_swa_k, cache_swa_v, state_gla, cache_diff_k, cache_diff_v, cache_ffn_conv = caches
    b, l, _ = x3.shape
    r = b * l
    depth = norm_mix_w.shape[0]
    x = x3.reshape(r, D_MODEL)
    if prompt:
        tm = min(ROW_TILE, l)
        blk = min(CHUNK, l)
        tabs = _rope_tables(pos)
    else:
        tm = r
        blk = min(CHUNK, l)
        tabs = tuple(jnp.tile(t, (b, 1)) for t in _rope_tables(pos))
    pos0 = 0 if prompt else cache_diff_k.shape[2]

    pools, swa_ks, swa_vs, glas, diff_ks, diff_vs, ffns = [], [], [], [], [], [], []
    kv_p = None
    for layer in range(depth):
        p = layer // 2
        nw = norm_mix_w[layer].reshape(1, D_MODEL)
        if layer % 2 == 0:
            u, q, k, v, *tails = _even_in(x, nw, even_w_in.astype(BF16), p, tabs, tm, l, prompt)
            wp = pool_w.astype(BF16)
            ps = pool_scale[p].reshape(1, D_A)
            wo = even_w_out.astype(BF16)
            if prompt:
                x = _even_mix_prompt(swa_sinks[p], u, q, k, v, x, wp, ps, wo, p, l, tm)
                pools.append(u.reshape(b, l, D_A)[:, l - POOL_STATE:])
                swa_ks.append(tails[0].reshape(b, WINDOW, HKV_B, HEAD_DIM))
                swa_vs.append(tails[1].reshape(b, WINDOW, HKV_B, HEAD_DIM))
            else:
                rows_c = cache_swa_k.shape[2]
                pool_prev = jnp.pad(cache_pool[p], ((0, 0), (1, 0), (0, 0)))
                kc = cache_swa_k[p].reshape(b, rows_c, D_KV)
                vc = cache_swa_v[p].reshape(b, rows_c, D_KV)
                x = _even_mix_sample(swa_sinks[p], u, q, k, v, x, pool_prev, kc, vc, wp, ps, wo, p, l, pos0)
                pools.append(jnp.concatenate([cache_pool[p], u.reshape(b, l, D_A)], axis=1)[:, -POOL_STATE:])
                swa_ks.append(jnp.concatenate([cache_swa_k[p], k.reshape(b, l, HKV_B, HEAD_DIM)], axis=1)[:, -rows_c:])
                swa_vs.append(jnp.concatenate([cache_swa_v[p], v.reshape(b, l, HKV_B, HEAD_DIM)], axis=1)[:, -rows_c:])
        else:
            lambda_init = 0.8 - 0.6 * math.exp(-0.3 * layer)
            wg = jnp.pad(gla_w_gate2, ((0, 0), (0, LANES - GATE_RANK), (0, 0))).astype(BF16)
            bg = gla_b_gate[p].reshape(1, D_CQK)
            lp = diff_lambda[p]
            dnw = diff_norm_w[p].reshape(1, 2 * HD_D)
            w_in = _pad_odd_w_in(odd_w_in)
            if prompt:
                qd, kd, el, cv, cg, dq, *kv_p = _odd_in(x, nw, w_in, wg, p, bg, tabs, tm, blk,
                                                        pair=(p, depth // 2, b, l, kv_p))
                yd = _diff_prompt(lp, dnw, dq, kv_p[0], kv_p[1], p, b, l, lambda_init, min(ATTN_TILE, l))
                s0 = jnp.zeros((b, H_C, DK_C, DV_C), F32)
            else:
                qd, kd, el, cv, cg, dq, dk, dv = _odd_in(x, nw, w_in, wg, p, bg, tabs, tm, blk)
                past = cache_diff_k.shape[2]
                kct = jnp.transpose(cache_diff_k, (0, 1, 3, 4, 5, 2)).reshape(depth // 2, b, H_D, 2 * HD_D, past)
                vc = cache_diff_v.reshape(depth // 2, b * past * H_D, 2 * HD_D)
                yd = _diff_sample(lp, dnw, dq, dk, dv, kct, vc, p, b, l, lambda_init)
                s0 = state_gla[p]
                diff_ks.append(dk.reshape(b, l, H_D, 2, HD_D))
                diff_vs.append(dv.reshape(b, l, H_D, 2 * HD_D))
            x, s_new = _odd_mix(qd, kd, el, cv, cg, yd, x, s0, gla_norm_w[p].reshape(1, DV_C),
                                odd_w_out.astype(BF16), p, tm, blk, l, not prompt)
            glas.append(s_new)
        if prompt:
            tf = seg = min(FFN_ROW_TILE, l)
            prev = jnp.zeros((r // tf, CONV_W - 1, D_FF), F32)
        else:
            tf, seg = tm, l
            prev = cache_ffn_conv[layer]
        x, tails = _ffn(x, norm_ffn_w[layer].reshape(1, D_MODEL), ffn_w_up.astype(BF16), ffn_conv_w[layer],
                        ffn_conv_b[layer].reshape(1, D_FF), ffn_w_down.astype(BF16), prev,
                        final_norm_w.reshape(1, D_MODEL), layer, tf, seg, l, not prompt, layer == depth - 1)
        tails = tails.reshape(b, -1, CONV_TAIL, D_FF)
        ffns.append(tails[:, -1, CONV_TAIL - (CONV_W - 1):])
    if prompt:
        n_pairs = depth // 2
        diff_k = jnp.transpose(kv_p[0].reshape(n_pairs, b, H_D, 2, HD_D, l), (0, 1, 5, 2, 3, 4))
        diff_v = kv_p[1].reshape(n_pairs, b, l, H_D, 2 * HD_D)
    else:
        diff_k, diff_v = jnp.stack(diff_ks), jnp.stack(diff_vs)
    return (x.reshape(b, l, D_MODEL), jnp.stack(pools), jnp.stack(swa_ks), jnp.stack(swa_vs), jnp.stack(glas),
            diff_k, diff_v, jnp.stack(ffns))


def kernel(x_prompt, x_sample, cache_pool, cache_swa_k, cache_swa_v, state_gla, cache_diff_k, cache_diff_v, cache_ffn_conv, norm_mix_w, norm_ffn_w, final_norm_w, even_w_in, pool_w, pool_scale, swa_sinks, even_w_out, odd_w_in, gla_w_gate2, gla_b_gate, gla_norm_w, diff_lambda, diff_norm_w, odd_w_out, ffn_w_up, ffn_conv_w, ffn_conv_b, ffn_w_down):
    weights = (norm_mix_w, norm_ffn_w, final_norm_w, even_w_in, pool_w, pool_scale, swa_sinks, even_w_out, odd_w_in,
               gla_w_gate2, gla_b_gate, gla_norm_w, diff_lambda, diff_norm_w, odd_w_out, ffn_w_up, ffn_conv_w,
               ffn_conv_b, ffn_w_down)
    caches = (cache_pool, cache_swa_k, cache_swa_v, state_gla, cache_diff_k, cache_diff_v, cache_ffn_conv)
    past = cache_diff_k.shape[2]
    pos_p = jnp.arange(x_prompt.shape[1])
    pos_s = past + jnp.arange(x_sample.shape[1])
    outs_p = _trunk(x_prompt, pos_p, True, caches, weights)
    outs_s = _trunk(x_sample, pos_s, False, caches, weights)
    return (outs_p[0], outs_s[0]) + tuple(outs_p[1:]) + tuple(outs_s[1:])
```
